```python
import math
import jax, jax.numpy as jnp
from jax import lax
import numpy as np

D_MODEL = 1024
BATCH = 4
SEQ = 8192
DEPTH = 2

D_MIX = D_MODEL
HEAD_DIM = 64
D_POOL = D_MIX // 4
D_CONV = D_MIX // 4
D_NSA = D_MIX // 2
N_POOL_GROUPS = 4
POOL_GROUP = D_POOL // N_POOL_GROUPS
POOL_WINDOWS = (2, 4, 8, 16)
CONV_WIDTH = 3
N_HEADS = D_NSA // HEAD_DIM
N_KV = 2
GROUP = N_HEADS // N_KV
D_KV = N_KV * HEAD_DIM
CMP_BLOCK = 32
CMP_STRIDE = 16
CMP_HIDDEN = 128
SLC_BLOCK = 64
N_SELECT = 16
N_LOCAL = 2
WINDOW = 512
Q_BLOCK = 128
N_BUCKETS = 32
MAX_DISTANCE = 128
ALPHA = (2 * DEPTH) ** 0.25
BETA = (8 * DEPTH) ** -0.25
LN_EPS = 1e-5
NEG = -1e30
FORCED = 1e9
PROJ_SIZES = (D_POOL, D_POOL,
              D_CONV, D_CONV, D_CONV, D_CONV,
              D_NSA, D_KV, D_KV, D_KV, D_KV, D_KV, D_KV, 3 * N_HEADS, D_NSA)
D_PROJ = sum(PROJ_SIZES)

kernel_name = 'hymba_pool_conv_nsa_deepnorm'


def layer_norm(x, g, b):
    xf = x.astype(jnp.float32)
    mu = jnp.mean(xf, axis=-1, keepdims=True)
    var = jnp.mean(jnp.square(xf - mu), axis=-1, keepdims=True)
    return ((xf - mu) * lax.rsqrt(var + LN_EPS) * g + b).astype(x.dtype)


def rel_bucket(dist):
    n = jnp.maximum(dist, 0)
    max_exact = N_BUCKETS // 2
    nf = jnp.maximum(n, 1).astype(jnp.float32)
    large = max_exact + (jnp.log(nf / max_exact) / math.log(MAX_DISTANCE / max_exact)
                         * (N_BUCKETS - max_exact)).astype(jnp.int32)
    large = jnp.minimum(large, N_BUCKETS - 1)
    return jnp.where(n < max_exact, n, large)


def pool_mixer(v, w_grp, scale):
    B_, T, _ = v.shape
    vf = v.astype(jnp.float32)
    cs = jnp.pad(jnp.cumsum(vf, axis=1), ((0, 0), (1, 0), (0, 0)))
    pos = jnp.arange(T)
    outs = []
    for gi, w in enumerate(POOL_WINDOWS):
        sl = slice(gi * POOL_GROUP, (gi + 1) * POOL_GROUP)
        start = jnp.maximum(pos + 1 - w, 0)
        cnt = (pos + 1 - start).astype(jnp.float32)[None, :, None]
        outs.append((cs[:, 1:, sl] - cs[:, start, sl]) / cnt - vf[..., sl])
    p = jnp.stack(outs, axis=2)
    y = jnp.einsum('btgc,gcd->btgd', p, w_grp).reshape(B_, T, D_POOL)
    return (y * scale).astype(v.dtype)


def short_conv_mixer(b, c, xin, conv_w):
    u = c * xin
    T = u.shape[1]
    up = jnp.pad(u, ((0, 0), (CONV_WIDTH - 1, 0), (0, 0)))
    y = sum(conv_w[k] * up[:, k:k + T] for k in range(CONV_WIDTH))
    return b * y


def compress(kv, pe, w1, w2):
    B_, T = kv.shape[:2]
    n_cmp = (T - CMP_BLOCK) // CMP_STRIDE + 1
    idx = jnp.arange(n_cmp)[:, None] * CMP_STRIDE + jnp.arange(CMP_BLOCK)[None, :]
    blk = kv[:, idx] + pe[None, None, :, None, :]
    blk = jnp.moveaxis(blk, 3, 2).reshape(B_, n_cmp, N_KV, CMP_BLOCK * HEAD_DIM)
    out = jax.nn.silu(blk @ w1) @ w2
    return out.transpose(0, 2, 1, 3)


def nsa_mixer(q, k_cmp, v_cmp, k_slc, v_slc, k_win, v_win, gates,
              pe_k, w1_k, w2_k, pe_v, w1_v, w2_v, rel_bias):
    B_, T = q.shape[:2]
    f32 = jnp.float32
    n_cmp = (T - CMP_BLOCK) // CMP_STRIDE + 1
    n_slc = T // SLC_BLOCK
    n_qb = T // Q_BLOCK
    k_sel = min(N_SELECT, n_slc)
    heads = lambda a: a.reshape(B_, T, N_KV, HEAD_DIM)
    kc = compress(heads(k_cmp), pe_k, w1_k, w2_k).astype(f32)
    vc = compress(heads(v_cmp), pe_v, w1_v, w2_v).astype(f32)
    ksb = heads(k_slc).transpose(0, 2, 1, 3).reshape(B_, N_KV, n_slc, SLC_BLOCK, HEAD_DIM)
    vsb = heads(v_slc).transpose(0, 2, 1, 3).reshape(B_, N_KV, n_slc, SLC_BLOCK, HEAD_DIM)
    pad = ((0, 0), (0, 0), (WINDOW, 0), (0, 0))
    kwp = jnp.pad(heads(k_win).transpose(0, 2, 1, 3), pad)
    vwp = jnp.pad(heads(v_win).transpose(0, 2, 1, 3), pad)
    qbl = q.reshape(B_, n_qb, Q_BLOCK, N_KV, GROUP, HEAD_DIM).transpose(1, 0, 3, 4, 2, 5)
    gbl = gates.reshape(B_, n_qb, Q_BLOCK, N_KV, GROUP, 3).transpose(1, 0, 3, 4, 2, 5)
    rel_gr = rel_bias.T.reshape(N_KV, GROUP, N_BUCKETS).astype(f32)
    cstart = np.arange(n_cmp)[:, None] * CMP_STRIDE
    sstart = np.arange(n_slc)[None, :] * SLC_BLOCK
    overlap = np.clip(np.minimum(cstart + CMP_BLOCK, sstart + SLC_BLOCK) - np.maximum(cstart, sstart), 0, None)
    ov = jnp.asarray(overlap / CMP_STRIDE, dtype=f32)
    cmp_end = jnp.arange(n_cmp) * CMP_STRIDE + CMP_BLOCK - 1
    slc_idx = jnp.arange(n_slc)
    bi = jnp.arange(B_)[:, None, None, None]
    gi = jnp.arange(N_KV)[None, :, None, None]
    g5 = jnp.arange(N_KV)[None, :, None, None, None]
    r5 = jnp.arange(GROUP)[None, None, :, None, None]
    scale = HEAD_DIM ** -0.5

    def masked_softmax(s, m):
        return jnp.where(m, jax.nn.softmax(jnp.where(m, s, NEG), axis=-1), 0.0)

    def block(args):
        qb, gb, qi = args
        t = qi * Q_BLOCK + jnp.arange(Q_BLOCK)
        qf = qb.astype(f32) * scale
        d1 = t[:, None] - cmp_end[None, :]
        m1 = d1 >= 0
        s1 = jnp.einsum('bgrqd,bgnd->bgrqn', qf, kc) + rel_gr[:, :, rel_bucket(d1)]
        p1 = masked_softmax(s1, m1)
        o1 = jnp.einsum('bgrqn,bgnd->bgrqd', p1, vc)
        imp = jnp.einsum('bgrqn,nj->bgqj', p1, ov)
        back = (t // SLC_BLOCK)[:, None] - slc_idx[None, :]
        forced = (slc_idx[None, :] == 0) | ((back >= 0) & (back < N_LOCAL))
        score = jnp.where(forced, FORCED, jnp.where(back >= 0, imp, NEG))
        _, sel = lax.top_k(score, k_sel)
        kg = ksb[bi, gi, sel].reshape(B_, N_KV, Q_BLOCK, k_sel * SLC_BLOCK, HEAD_DIM).astype(f32)
        vg = vsb[bi, gi, sel].reshape(B_, N_KV, Q_BLOCK, k_sel * SLC_BLOCK, HEAD_DIM).astype(f32)
        kpos = (sel[..., None] * SLC_BLOCK + jnp.arange(SLC_BLOCK)).reshape(B_, N_KV, Q_BLOCK, -1)
        d2 = t[None, None, :, None] - kpos
        m2 = (d2 >= 0)[:, :, None]
        s2 = jnp.einsum('bgrqd,bgqkd->bgrqk', qf, kg) + rel_gr[g5, r5, rel_bucket(d2)[:, :, None]]
        o2 = jnp.einsum('bgrqk,bgqkd->bgrqd', masked_softmax(s2, m2), vg)
        start = qi * Q_BLOCK
        kw = lax.dynamic_slice_in_dim(kwp, start, Q_BLOCK + WINDOW, axis=2).astype(f32)
        vw = lax.dynamic_slice_in_dim(vwp, start, Q_BLOCK + WINDOW, axis=2).astype(f32)
        kpos3 = start - WINDOW + jnp.arange(Q_BLOCK + WINDOW)
        d3 = t[:, None] - kpos3[None, :]
        m3 = (d3 >= 0) & (d3 < WINDOW) & (kpos3[None, :] >= 0)
        s3 = jnp.einsum('bgrqd,bgkd->bgrqk', qf, kw) + rel_gr[:, :, rel_bucket(d3)]
        o3 = jnp.einsum('bgrqk,bgkd->bgrqd', masked_softmax(s3, m3), vw)
        g = jax.nn.sigmoid(gb.astype(f32))
        return g[..., 0:1] * o1 + g[..., 1:2] * o2 + g[..., 2:3] * o3

    out = lax.map(block, (qbl, gbl, jnp.arange(n_qb)))
    return out.transpose(1, 0, 4, 2, 3, 5).reshape(B_, T, D_NSA).astype(q.dtype)


def hybrid_layer(x, w_in, w_out, pool_w, pool_scale, conv_w,
                 pe_k, w1_k, w2_k, pe_v, w1_v, w2_v, rel_bias, ln_g, ln_b):
    proj = x @ w_in
    split_points = np.cumsum(PROJ_SIZES)[:-1].tolist()
    (pool_v, pool_z, conv_b, conv_c, conv_x, conv_z,
     q, k_cmp, v_cmp, k_slc, v_slc, k_win, v_win, gates, nsa_z) = jnp.split(proj, split_points, axis=-1)
    y_pool = jax.nn.silu(pool_z) * pool_mixer(pool_v, pool_w, pool_scale)
    y_conv = jax.nn.silu(conv_z) * short_conv_mixer(conv_b, conv_c, conv_x, conv_w)
    y_nsa = jax.nn.silu(nsa_z) * nsa_mixer(q, k_cmp, v_cmp, k_slc, v_slc, k_win, v_win, gates,
                                           pe_k, w1_k, w2_k, pe_v, w1_v, w2_v, rel_bias)
    y = jnp.concatenate([y_pool, y_conv, y_nsa], axis=-1) @ w_out
    return layer_norm(ALPHA * x + y, ln_g, ln_b)


def setup_inputs(seed: int = 0) -> dict:
    key = jax.random.key(seed)
    ks = jax.random.split(key, 16)
    f = jnp.float32
    nrm = lambda k, s: jax.random.normal(k, s, f)
    return {
        'x': nrm(ks[0], (BATCH, SEQ, D_MODEL)),
        'w_in': nrm(ks[1], (DEPTH, D_MODEL, D_PROJ)) * D_MODEL ** -0.5,
        'w_out': nrm(ks[2], (DEPTH, D_MIX, D_MODEL)) * (D_MIX ** -0.5 * BETA),
        'pool_w': nrm(ks[3], (DEPTH, N_POOL_GROUPS, POOL_GROUP, POOL_GROUP)) * POOL_GROUP ** -0.5,
        'pool_scale': 1.0 + 0.1 * nrm(ks[4], (DEPTH, D_POOL)),
        'conv_w': nrm(ks[5], (DEPTH, CONV_WIDTH, D_CONV)) * CONV_WIDTH ** -0.5,
        'cmp_pe_k': 0.1 * nrm(ks[6], (DEPTH, CMP_BLOCK, HEAD_DIM)),
        'cmp_w1_k': nrm(ks[7], (DEPTH, CMP_BLOCK * HEAD_DIM, CMP_HIDDEN)) * (CMP_BLOCK * HEAD_DIM) ** -0.5,
        'cmp_w2_k': nrm(ks[8], (DEPTH, CMP_HIDDEN, HEAD_DIM)) * (2.0 * CMP_HIDDEN ** -0.5),
        'cmp_pe_v': 0.1 * nrm(ks[9], (DEPTH, CMP_BLOCK, HEAD_DIM)),
        'cmp_w1_v': nrm(ks[10], (DEPTH, CMP_BLOCK * HEAD_DIM, CMP_HIDDEN)) * (CMP_BLOCK * HEAD_DIM) ** -0.5,
        'cmp_w2_v': nrm(ks[11], (DEPTH, CMP_HIDDEN, HEAD_DIM)) * (2.0 * CMP_HIDDEN ** -0.5),
        'rel_bias': 0.5 * nrm(ks[12], (N_BUCKETS, N_HEADS)),
        'ln_g': 1.0 + 0.02 * nrm(ks[13], (DEPTH, D_MODEL)),
        'ln_b': 0.02 * nrm(ks[14], (DEPTH, D_MODEL)),
    }


def reference(x, w_in, w_out, pool_w, pool_scale, conv_w, cmp_pe_k, cmp_w1_k, cmp_w2_k,
              cmp_pe_v, cmp_w1_v, cmp_w2_v, rel_bias, ln_g, ln_b):
    h = x
    for l in range(DEPTH):
        h = hybrid_layer(h, w_in[l], w_out[l], pool_w[l], pool_scale[l], conv_w[l],
                         cmp_pe_k[l], cmp_w1_k[l], cmp_w2_k[l],
                         cmp_pe_v[l], cmp_w1_v[l], cmp_w2_v[l],
                         rel_bias, ln_g[l], ln_b[l])
    return h
```

```python
import functools
import math

import numpy as np
import jax
import jax.numpy as jnp
from jax import lax
from jax.experimental import pallas as pl
from jax.experimental.pallas import tpu as pltpu

f32 = jnp.float32
bf16 = jnp.bfloat16

D_MODEL = 1024
D_POOL = 256
D_CONV = 256
D_NSA = 512
HEAD_DIM = 64
N_HEADS = 8
N_KV = 2
GROUP = 4
D_KV = 128
POOL_GROUP = 64
POOL_WINDOWS = (2, 4, 8, 16)
CONV_WIDTH = 3
CMP_BLOCK = 32
CMP_STRIDE = 16
CMP_HIDDEN = 128
SLC_BLOCK = 64
N_SELECT = 16
N_LOCAL = 2
WINDOW = 512
N_BUCKETS = 32
MAX_DISTANCE = 128
LN_EPS = 1e-5
FORCED = 1e9
NEG = -1e30

LANES = 128
TQ = 128
R = N_HEADS * TQ
NSB = 128
TT = 512
HALO = 16
KT_FAR = 512
MIX_W = 2 * D_POOL + 4 * D_CONV
VMEM_LIMIT = 56 * 1024 * 1024

_NT = (((1,), (1,)), ((), ()))


def _dot(a, b):
    return jnp.dot(a, b, preferred_element_type=f32)


def _dot_nt(a, b):
    return lax.dot_general(a, b, _NT, preferred_element_type=f32)


def _sigmoid(x):
    return 1.0 / (1.0 + jnp.exp(-x))


_C_MIX = (0, MIX_W)
_C_Q = (_C_MIX[1], _C_MIX[1] + D_NSA)
_C_Z = (_C_Q[1], _C_Q[1] + D_NSA)
_C_CMP = (_C_Z[1], _C_Z[1] + 2 * D_KV)
_C_SLC = (_C_CMP[1], _C_CMP[1] + 2 * D_KV)
_C_WIN = (_C_SLC[1], _C_SLC[1] + 2 * D_KV)
_C_G = (_C_WIN[1], _C_WIN[1] + LANES)
W_ALL = _C_G[1]


def _inproj_kernel(x_ref, w_ref, mix_ref, q_ref, z_ref, cmp_ref, slc_ref, win_ref, g_ref):
    x = x_ref[...].astype(bf16)

    def proj(c):
        return _dot(x, w_ref[:, c[0]:c[1]])

    mix_ref[...] = proj(_C_MIX)
    q_ref[...] = (proj(_C_Q) * (HEAD_DIM ** -0.5)).astype(bf16)
    z_ref[...] = proj(_C_Z).astype(bf16)
    cmp_ref[...] = proj(_C_CMP).astype(bf16)
    slc_ref[...] = proj(_C_SLC).astype(bf16)
    win_ref[...] = proj(_C_WIN).astype(bf16)
    g_ref[...] = proj(_C_G)


def _inproj(x2, w_all):
    bt = x2.shape[0]
    row = lambda i: (i, 0)
    outs = [(MIX_W, f32), (D_NSA, bf16), (D_NSA, bf16), (2 * D_KV, bf16), (2 * D_KV, bf16),
            (2 * D_KV, bf16), (LANES, f32)]
    return pl.pallas_call(
        _inproj_kernel,
        grid=(bt // TT,),
        in_specs=[pl.BlockSpec((TT, D_MODEL), row),
                  pl.BlockSpec((D_MODEL, W_ALL), lambda i: (0, 0))],
        out_specs=[pl.BlockSpec((TT, w), row) for w, _ in outs],
        out_shape=[jax.ShapeDtypeStruct((bt, w), dt) for w, dt in outs],
        compiler_params=pltpu.CompilerParams(dimension_semantics=("arbitrary",),
                                             vmem_limit_bytes=VMEM_LIMIT),
        name="inproj",
    )(x2, w_all)


def _mixer_kernel(mix_ref, halo_ref, pw_ref, ps_ref, cw_ref, out_ref):
    i = pl.program_id(1)
    n = TT + HALO
    halo = jnp.where(i > 0, halo_ref[...], 0.0)
    ext = jnp.concatenate([halo, mix_ref[...]], axis=0)

    e = ext[:, 0:D_POOL]
    s2 = e + pltpu.roll(e, 1, axis=0)
    s4 = s2 + pltpu.roll(s2, 2, axis=0)
    s8 = s4 + pltpu.roll(s4, 4, axis=0)
    s16 = s8 + pltpu.roll(s8, 8, axis=0)
    lane = lax.broadcasted_iota(jnp.int32, (TT, D_POOL), 1)
    grp = lane // POOL_GROUP
    wsum = jnp.where(grp == 0, s2[HALO:], jnp.where(grp == 1, s4[HALO:], jnp.where(grp == 2, s8[HALO:], s16[HALO:])))
    win = jnp.left_shift(2, grp)
    pos = i * TT + lax.broadcasted_iota(jnp.int32, (TT, D_POOL), 0)
    cnt = jnp.minimum(pos + 1, win).astype(f32)
    v = e[HALO:]
    pooled = wsum / cnt - v
    y_pool = _dot(pooled.astype(bf16), pw_ref[...]) * ps_ref[...]
    zp = ext[HALO:, D_POOL:2 * D_POOL]
    y_pool = y_pool * (zp * _sigmoid(zp))

    o = 2 * D_POOL
    cb = ext[HALO:, o:o + D_CONV]
    u = ext[:, o + D_CONV:o + 2 * D_CONV] * ext[:, o + 2 * D_CONV:o + 3 * D_CONV]
    zc = ext[HALO:, o + 3 * D_CONV:o + 4 * D_CONV]
    conv = cw_ref[CONV_WIDTH - 1:CONV_WIDTH, :] * u[HALO:]
    for k in range(CONV_WIDTH - 1):
        conv = conv + cw_ref[k:k + 1, :] * pltpu.roll(u, CONV_WIDTH - 1 - k, axis=0)[HALO:]
    y_conv = cb * conv * (zc * _sigmoid(zc))

    out_ref[:, 0:D_POOL] = y_pool.astype(bf16)
    out_ref[:, D_POOL:D_POOL + D_CONV] = y_conv.astype(bf16)
    del n


def _mixers(mix, pw_bd, pool_scale, conv_w, B, T):
    nt = T // TT
    hb = TT // HALO
    return pl.pallas_call(
        _mixer_kernel,
        grid=(B, nt),
        in_specs=[pl.BlockSpec((TT, MIX_W), lambda b, i: (b * nt + i, 0)),
                  pl.BlockSpec((HALO, MIX_W), lambda b, i: (jnp.maximum((b * nt + i) * hb - 1, 0), 0)),
                  pl.BlockSpec((D_POOL, D_POOL), lambda b, i: (0, 0)),
                  pl.BlockSpec((1, D_POOL), lambda b, i: (0, 0)),
                  pl.BlockSpec((CONV_WIDTH, D_CONV), lambda b, i: (0, 0))],
        out_specs=pl.BlockSpec((TT, D_POOL + D_CONV), lambda b, i: (b * nt + i, 0)),
        out_shape=jax.ShapeDtypeStruct((B * T, D_POOL + D_CONV), bf16),
        compiler_params=pltpu.CompilerParams(dimension_semantics=("arbitrary", "arbitrary"),
                                             vmem_limit_bytes=VMEM_LIMIT),
        name="mixers",
    )(mix, mix, pw_bd, pool_scale, conv_w)


CH = 4 * CMP_HIDDEN


def _compress_kernel(r_ref, pe_ref, w1_ref, w2_ref, w2t_ref, out_ref, outT_ref):
    nc = r_ref.shape[0]
    zz = _dot(r_ref[...], w1_ref[...])
    pb = _dot(pe_ref[...], w1_ref[...])
    z0 = zz[:, 0:CH] + pb[0:1, 0:CH]
    z1 = zz[:, CH:2 * CH] + pb[1:2, CH:2 * CH]
    h = z0 + pltpu.roll(z1, nc - 1, axis=0)
    h = (h * _sigmoid(h)).astype(bf16)
    out_ref[...] = _dot(h, w2_ref[...]).astype(bf16)
    outT_ref[...] = _dot_nt(w2t_ref[...], h).astype(bf16)


def _compress(r3, pe_r, w1_big, w2_big, w2t_big):
    B, nc, rw = r3.shape
    const = lambda b: (0, 0)
    return pl.pallas_call(
        _compress_kernel,
        grid=(B,),
        in_specs=[pl.BlockSpec((None, nc, rw), lambda b: (b, 0, 0)),
                  pl.BlockSpec((8, rw), const),
                  pl.BlockSpec((rw, 2 * CH), const),
                  pl.BlockSpec((CH, 2 * D_KV), const),
                  pl.BlockSpec((2 * D_KV, CH), const)],
        out_specs=[pl.BlockSpec((None, nc, 2 * D_KV), lambda b: (b, 0, 0)),
                   pl.BlockSpec((None, 2 * D_KV, nc), lambda b: (b, 0, 0))],
        out_shape=[jax.ShapeDtypeStruct((B, nc, 2 * D_KV), bf16),
                   jax.ShapeDtypeStruct((B, 2 * D_KV, nc), bf16)],
        compiler_params=pltpu.CompilerParams(dimension_semantics=("arbitrary",),
                                             vmem_limit_bytes=VMEM_LIMIT),
        name="compress",
    )(r3, pe_r, w1_big, w2_big, w2t_big)


def _select_penalty(imp_t, t0):
    shape = (NSB, TQ)
    jblk = lax.broadcasted_iota(jnp.int32, shape, 0)
    tq = lax.broadcasted_iota(jnp.int32, shape, 1)
    back = jnp.right_shift(t0 + tq, int(math.log2(SLC_BLOCK))) - jblk
    causal = back >= 0
    forced = jnp.logical_or(jblk == 0, jnp.logical_and(causal, back < N_LOCAL))
    score = jnp.where(forced, FORCED, jnp.where(causal, imp_t, NEG))
    jf = jblk.astype(f32)
    pen = jnp.full(shape, NEG, f32)
    for _ in range(N_SELECT):
        best = jnp.max(score, axis=0, keepdims=True)
        first = jnp.min(jnp.where(score == best, jf, float(NSB)), axis=0, keepdims=True)
        pick = jf == first
        pen = jnp.where(pick, 0.0, pen)
        score = jnp.where(pick, -jnp.inf, score)
    return pen


def _nsa_kernel(q_ref, z_ref, g_ref, cmp_ref, cmpT_ref, slc_ref, win_ref, vsT_ref, vwT_ref,
                e_ref, ovT_ref, tab_ref, acmp_ref, out_ref, m_ref, l_ref, acc_ref):
    qi = pl.program_id(1)
    t0 = qi * TQ
    nc = cmp_ref.shape[0]

    lane = lax.broadcasted_iota(jnp.int32, (TQ, LANES), 1)
    low = lane < HEAD_DIM
    blocks = []
    for h in range(N_HEADS):
        blk = q_ref[:, LANES * (h % GROUP):LANES * (h % GROUP + 1)].astype(f32)
        blocks.append((jnp.where(low, blk, 0.0) if h < GROUP else jnp.where(low, 0.0, blk)).astype(bf16))
    lhs8 = jnp.concatenate(blocks, axis=0)

    n_i = lax.broadcasted_iota(jnp.int32, (nc, LANES), 0)
    j_i = lax.broadcasted_iota(jnp.int32, (nc, LANES), 1)
    nstart = t0 // CMP_STRIDE - 8
    in_window = jnp.logical_and(j_i < 32, n_i == nstart + jnp.where(j_i < 16, j_i, j_i - 16))
    future = jnp.logical_and(j_i == 32, n_i >= nstart + 15)
    place_b = jnp.where(jnp.logical_or(in_window, future), 1.0, 0.0).astype(bf16)
    kc_ext = jnp.concatenate([cmp_ref[:, 0:D_KV], place_b], axis=1)
    q_cmp = jnp.concatenate([lhs8, acmp_ref[...]], axis=1)
    s1 = _dot_nt(kc_ext, q_cmp)
    m1 = jnp.max(s1, axis=0, keepdims=True)
    p1 = jnp.exp(s1 - m1)
    l1 = jnp.sum(p1, axis=0, keepdims=True)
    tcol = t0 + lax.broadcasted_iota(jnp.int32, (1, R), 1) % TQ
    p1 = p1 * jnp.where(tcol >= CMP_BLOCK - 1, 1.0 / l1, 0.0)
    o1 = _dot(cmpT_ref[D_KV:2 * D_KV, :], p1.astype(bf16))

    pens = []
    for g in range(N_KV):
        ps = p1[:, g * GROUP * TQ:g * GROUP * TQ + TQ]
        for r in range(1, GROUP):
            c0 = (g * GROUP + r) * TQ
            ps = ps + p1[:, c0:c0 + TQ]
        hi = ps.astype(bf16)
        lo = (ps - hi.astype(f32)).astype(bf16)
        imp_t = _dot(ovT_ref[...], hi) + _dot(ovT_ref[...], lo)
        pen = _select_penalty(imp_t, t0).T.astype(bf16)
        pens += [pen] * GROUP
    q_slc = jnp.concatenate([lhs8, jnp.concatenate(pens, axis=0)], axis=1)

    m_ref[...] = jnp.full((1, R), NEG, f32)
    l_ref[...] = jnp.zeros((1, R), f32)
    acc_ref[...] = jnp.zeros((LANES, R), f32)

    def slc_tile(kbase, kt, tab):
        kbase = pl.multiple_of(kbase, LANES)
        k_ext = jnp.concatenate([slc_ref[pl.ds(kbase, kt), 0:D_KV], e_ref[pl.ds(kbase, kt), :]], axis=1)
        s = _dot_nt(k_ext, q_slc)
        if tab is not None:
            s = s + tab
        m_prev = m_ref[...]
        m_new = jnp.maximum(m_prev, jnp.max(s, axis=0, keepdims=True))
        alpha = jnp.exp(m_prev - m_new)
        p = jnp.exp(s - m_new)
        l_ref[...] = alpha * l_ref[...] + jnp.sum(p, axis=0, keepdims=True)
        acc_ref[...] = acc_ref[...] * alpha + _dot(vsT_ref[:, pl.ds(kbase, kt)], p.astype(bf16))
        m_ref[...] = m_new

    n_far = jnp.maximum(qi - 1, 0)
    per = KT_FAR // TQ
    n_big = n_far // per

    def far_big(i, c):
        slc_tile(i * KT_FAR, KT_FAR, None)
        return c

    def far_small(i, c):
        slc_tile(n_big * KT_FAR + i * TQ, TQ, None)
        return c

    lax.fori_loop(0, n_big, far_big, 0)
    lax.fori_loop(0, n_far - n_big * per, far_small, 0)

    @pl.when(qi > 0)
    def _():
        slc_tile(t0 - TQ, TQ, tab_ref[1])

    slc_tile(t0, TQ, tab_ref[0])
    inv2 = 1.0 / l_ref[...]

    n_win = WINDOW // TQ + 1
    s3 = []
    bases = []
    for j in range(n_win):
        base = t0 - WINDOW + TQ * j
        basec = pl.multiple_of(jnp.maximum(base, 0), LANES)
        s = _dot_nt(win_ref[pl.ds(basec, TQ), 0:D_KV], lhs8)
        if j == 0:
            s = s + tab_ref[2]
        elif j == n_win - 2:
            s = s + tab_ref[1]
        elif j == n_win - 1:
            s = s + tab_ref[0]
        if j < n_win - 1:
            s = s + jnp.where(base >= 0, 0.0, NEG)
        s3.append(s)
        bases.append(basec)
    m3 = jnp.max(s3[0], axis=0, keepdims=True)
    for s in s3[1:]:
        m3 = jnp.maximum(m3, jnp.max(s, axis=0, keepdims=True))
    l3 = jnp.zeros((1, R), f32)
    o3 = jnp.zeros((LANES, R), f32)
    for s, basec in zip(s3, bases):
        p = jnp.exp(s - m3)
        l3 = l3 + jnp.sum(p, axis=0, keepdims=True)
        o3 = o3 + _dot(vwT_ref[:, pl.ds(basec, TQ)], p.astype(bf16))
    inv3 = 1.0 / l3

    sig_t = _sigmoid(g_ref[...]).T
    sub = lax.broadcasted_iota(jnp.int32, (LANES, TQ), 0)
    outs = []
    for h in range(N_HEADS):
        c = slice(h * TQ, (h + 1) * TQ)
        g1 = sig_t[3 * h:3 * h + 1, :]
        g2 = sig_t[3 * h + 1:3 * h + 2, :] * inv2[:, c]
        g3 = sig_t[3 * h + 2:3 * h + 3, :] * inv3[:, c]
        outs.append(g1 * o1[:, c] + g2 * acc_ref[:, c] + g3 * o3[:, c])
    for p in range(GROUP):
        y = jnp.where(sub < HEAD_DIM, outs[p], outs[p + GROUP]).T
        zb = z_ref[:, p * LANES:(p + 1) * LANES].astype(f32)
        out_ref[:, p * LANES:(p + 1) * LANES] = (y * zb * _sigmoid(zb)).astype(bf16)


def _nsa(q, z, g, cmp, cmpT, slc, win, vsT, vwT, e_all, ovT, tabs, acmp, B, T):
    nq = T // TQ
    nc = T // CMP_STRIDE
    tile = lambda b, i: (b * nq + i, 0)
    const2 = lambda b, i: (0, 0)
    return pl.pallas_call(
        _nsa_kernel,
        grid=(B, nq),
        in_specs=[pl.BlockSpec((TQ, D_NSA), tile),
                  pl.BlockSpec((TQ, D_NSA), tile),
                  pl.BlockSpec((TQ, LANES), tile),
                  pl.BlockSpec((None, nc, 2 * D_KV), lambda b, i: (b, 0, 0)),
                  pl.BlockSpec((None, 2 * D_KV, nc), lambda b, i: (b, 0, 0)),
                  pl.BlockSpec((T, 2 * D_KV), lambda b, i: (b, 0)),
                  pl.BlockSpec((T, 2 * D_KV), lambda b, i: (b, 0)),
                  pl.BlockSpec((None, D_KV, T), lambda b, i: (b, 0, 0)),
                  pl.BlockSpec((None, D_KV, T), lambda b, i: (b, 0, 0)),
                  pl.BlockSpec((T, NSB), const2),
                  pl.BlockSpec((NSB, nc), const2),
                  pl.BlockSpec((3, TQ, R), lambda b, i: (0, 0, 0)),
                  pl.BlockSpec((R, LANES), const2)],
        out_specs=pl.BlockSpec((TQ, D_NSA), tile),
        out_shape=jax.ShapeDtypeStruct((B * T, D_NSA), bf16),
        scratch_shapes=[pltpu.VMEM((1, R), f32), pltpu.VMEM((1, R), f32), pltpu.VMEM((LANES, R), f32)],
        compiler_params=pltpu.CompilerParams(dimension_semantics=("arbitrary", "arbitrary"),
                                             vmem_limit_bytes=VMEM_LIMIT),
        name="nsa",
    )(q, z, g, cmp, cmpT, slc, win, vsT, vwT, e_all, ovT, tabs, acmp)


def _outproj_kernel(alpha, ymix_ref, ynsa_ref, x_ref, w_ref, g_ref, b_ref, out_ref):
    y = jnp.concatenate([ymix_ref[...], ynsa_ref[...]], axis=1)
    r = alpha * x_ref[...] + _dot(y, w_ref[...])
    mu = jnp.mean(r, axis=-1, keepdims=True)
    d = r - mu
    var = jnp.mean(d * d, axis=-1, keepdims=True)
    out_ref[...] = d * lax.rsqrt(var + LN_EPS) * g_ref[...] + b_ref[...]


def _outproj(ymix, ynsa, x2, w_out, ln_g, ln_b, alpha):
    bt = x2.shape[0]
    row = lambda i: (i, 0)
    const = lambda i: (0, 0)
    return pl.pallas_call(
        functools.partial(_outproj_kernel, alpha),
        grid=(bt // TT,),
        in_specs=[pl.BlockSpec((TT, D_POOL + D_CONV), row),
                  pl.BlockSpec((TT, D_NSA), row),
                  pl.BlockSpec((TT, D_MODEL), row),
                  pl.BlockSpec((D_MODEL, D_MODEL), const),
                  pl.BlockSpec((1, D_MODEL), const),
                  pl.BlockSpec((1, D_MODEL), const)],
        out_specs=pl.BlockSpec((TT, D_MODEL), row),
        out_shape=jax.ShapeDtypeStruct((bt, D_MODEL), f32),
        compiler_params=pltpu.CompilerParams(dimension_semantics=("arbitrary",),
                                             vmem_limit_bytes=VMEM_LIMIT),
        name="outproj",
    )(ymix, ynsa, x2, w_out, ln_g, ln_b)


def _bucket_np(d):
    d = np.asarray(d)
    max_exact = N_BUCKETS // 2
    nf = np.maximum(d, 1).astype(np.float32)
    large = max_exact + (np.log(nf / np.float32(max_exact)) / np.float32(math.log(MAX_DISTANCE / max_exact))
                         * np.float32(N_BUCKETS - max_exact)).astype(np.int32)
    large = np.minimum(large, N_BUCKETS - 1)
    return np.where(d < max_exact, d, large)


_FAR_DIST = 113
assert _bucket_np(np.arange(_FAR_DIST, 4 * WINDOW)).min() == N_BUCKETS - 1

_PAIR_PERM = np.concatenate([np.concatenate([np.arange(HEAD_DIM) + HEAD_DIM * p,
                                             np.arange(HEAD_DIM) + HEAD_DIM * (p + GROUP)]) for p in range(GROUP)])


def _bias_tables(rel_bias):
    d = np.arange(2 * TQ)
    tabp = (rel_bias[_bucket_np(d), :] - rel_bias[N_BUCKETS - 1:N_BUCKETS, :]).T
    sl = np.arange(TQ)[:, None]
    tl = np.arange(TQ)[None, :]

    def toeplitz(dist, ok):
        vals = tabp[:, np.clip(dist, 0, 2 * TQ - 1)]
        vals = jnp.where(jnp.asarray(ok)[None], vals, NEG)
        return vals.transpose(1, 0, 2).reshape(TQ, R)

    diag = toeplitz(tl - sl, tl - sl >= 0)
    prev = toeplitz(tl - sl + TQ, np.ones((TQ, TQ), bool))
    edge = jnp.asarray(np.tile(np.where(sl > tl, 0.0, NEG).astype(np.float32), (1, N_HEADS)))
    tabs = jnp.stack([diag, prev, edge]).astype(f32)

    j = np.arange(16)[None, :]
    dc = np.arange(TQ)[:, None] - CMP_STRIDE * (j - 8) - (CMP_BLOCK - 1)
    okc = jnp.asarray(dc >= 0)[None]
    vc = jnp.where(okc, tabp[:, np.clip(dc, 0, 2 * TQ - 1)], NEG)
    hi = vc.astype(bf16)
    lo = jnp.where(okc, vc - hi.astype(f32), 0.0).astype(bf16)
    fut = jnp.full((N_HEADS, TQ, 1), NEG, bf16)
    pad = jnp.zeros((N_HEADS, TQ, LANES - 33), bf16)
    acmp = jnp.concatenate([hi, lo, fut, pad], axis=-1).reshape(R, LANES)
    return tabs, acmp


def _static_tables(T):
    nc = T // CMP_STRIDE
    ns = T // SLC_BLOCK
    cstart = np.arange(nc)[None, :] * CMP_STRIDE
    sstart = np.arange(NSB)[:, None] * SLC_BLOCK
    ov = np.clip(np.minimum(cstart + CMP_BLOCK, sstart + SLC_BLOCK) - np.maximum(cstart, sstart), 0, None) / CMP_STRIDE
    ov[ns:, :] = 0
    ov[:, nc - 1] = 0
    e_all = (np.arange(T)[:, None] // SLC_BLOCK == np.arange(NSB)[None, :])
    return jnp.asarray(ov, bf16), jnp.asarray(e_all, bf16)


def _layer_weights(w_in, w_out, pool_w, cmp_pe_k, cmp_w1_k, cmp_w2_k, cmp_pe_v, cmp_w1_v, cmp_w2_v):
    sizes = (D_POOL, D_POOL, D_CONV, D_CONV, D_CONV, D_CONV, D_NSA, D_KV, D_KV, D_KV, D_KV, D_KV, D_KV,
             3 * N_HEADS, D_NSA)
    offs = np.cumsum((0,) + sizes)
    col = lambda i: w_in[:, offs[i]:offs[i + 1]]
    wg = jnp.pad(col(13), ((0, 0), (0, LANES - 3 * N_HEADS)))
    w_all = jnp.concatenate([w_in[:, 0:MIX_W], col(6)[:, _PAIR_PERM], col(14)[:, _PAIR_PERM],
                             col(7), col(8), col(9), col(10), col(11), col(12), wg], axis=1).astype(bf16)
    w_out_p = jnp.concatenate([w_out[0:D_POOL + D_CONV], w_out[D_POOL + D_CONV:][_PAIR_PERM]], axis=0).astype(bf16)

    pw_bd = jnp.zeros((D_POOL, D_POOL), f32)
    for gi in range(len(POOL_WINDOWS)):
        s = slice(gi * POOL_GROUP, (gi + 1) * POOL_GROUP)
        pw_bd = pw_bd.at[s, s].set(pool_w[gi])

    half = CMP_BLOCK // 2
    w1_big = jnp.zeros((half, 2, N_KV, HEAD_DIM, 2, 2, N_KV, CMP_HIDDEN), f32)
    pe_r = jnp.zeros((8, half, 2, N_KV, HEAD_DIM), f32)
    w2_big = jnp.zeros((2, N_KV, CMP_HIDDEN, 2, N_KV, HEAD_DIM), f32)
    for kv, (pe, w1, w2) in enumerate(((cmp_pe_k, cmp_w1_k, cmp_w2_k), (cmp_pe_v, cmp_w1_v, cmp_w2_v))):
        w1r = w1.reshape(2, half, HEAD_DIM, CMP_HIDDEN)
        for g in range(N_KV):
            for a in range(2):
                w1_big = w1_big.at[:, kv, g, :, a, kv, g, :].set(w1r[a])
                pe_r = pe_r.at[a, :, kv, g, :].set(pe[a * half:(a + 1) * half])
            w2_big = w2_big.at[kv, g, :, kv, g, :].set(w2)
    rw = half * 2 * D_KV
    w1_big = w1_big.reshape(rw, 2 * CH).astype(bf16)
    pe_r = pe_r.reshape(8, rw).astype(bf16)
    w2_big = w2_big.reshape(CH, 2 * D_KV).astype(bf16)
    return w_all, w_out_p, pw_bd.astype(bf16), w1_big, pe_r, w2_big, w2_big.T


def _layer(x2, B, T, alpha, w_in, w_out, pool_w, pool_scale, conv_w, pe_k, w1_k, w2_k, pe_v, w1_v, w2_v,
           ln_g, ln_b, tabs, acmp, ovT, e_all):
    w_all, w_out_p, pw_bd, w1_big, pe_r, w2_big, w2t_big = _layer_weights(
        w_in, w_out, pool_w, pe_k, w1_k, w2_k, pe_v, w1_v, w2_v)
    mix, q, z, cmp_in, slc, win, g = _inproj(x2, w_all)
    ymix = _mixers(mix, pw_bd, pool_scale.reshape(1, D_POOL), conv_w, B, T)
    nc = T // CMP_STRIDE
    cmp, cmpT = _compress(cmp_in.reshape(B, nc, CMP_STRIDE * 2 * D_KV), pe_r, w1_big, w2_big, w2t_big)
    vsT = slc[:, D_KV:].reshape(B, T, D_KV).transpose(0, 2, 1)
    vwT = win[:, D_KV:].reshape(B, T, D_KV).transpose(0, 2, 1)
    ynsa = _nsa(q, z, g, cmp, cmpT, slc, win, vsT, vwT, e_all, ovT, tabs, acmp, B, T)
    return _outproj(ymix, ynsa, x2, w_out_p, ln_g.reshape(1, D_MODEL), ln_b.reshape(1, D_MODEL), alpha)


def kernel(x, w_in, w_out, pool_w, pool_scale, conv_w, cmp_pe_k, cmp_w1_k, cmp_w2_k, cmp_pe_v, cmp_w1_v, cmp_w2_v,
           rel_bias, ln_g, ln_b):
    B, T, D = x.shape
    depth = w_in.shape[0]
    assert D == D_MODEL and T % TT == 0 and T // SLC_BLOCK <= NSB and T // SLC_BLOCK >= N_SELECT
    alpha = (2 * depth) ** 0.25
    tabs, acmp = _bias_tables(rel_bias)
    ovT, e_all = _static_tables(T)
    h = x.reshape(B * T, D)
    for l in range(depth):
        h = _layer(h, B, T, alpha, w_in[l], w_out[l], pool_w[l], pool_scale[l], conv_w[l],
                   cmp_pe_k[l], cmp_w1_k[l], cmp_w2_k[l], cmp_pe_v[l], cmp_w1_v[l], cmp_w2_v[l],
                   ln_g[l], ln_b[l], tabs, acmp, ovT, e_all)
    return h.reshape(B, T, D)
```

```python
import functools
import math

import numpy as np
import jax
import jax.numpy as jnp
from jax import lax
from jax.experimental import pallas as pl
from jax.experimental.pallas import tpu as pltpu

f32 = jnp.float32
bf16 = jnp.bfloat16

D_MODEL = 1024
D_POOL = 256
D_CONV = 256
D_NSA = 512
HEAD_DIM = 64
N_HEADS = 8
N_KV = 2
GROUP = 4
D_KV = 128
POOL_GROUP = 64
POOL_WINDOWS = (2, 4, 8, 16)
CONV_WIDTH = 3
CMP_BLOCK = 32
CMP_STRIDE = 16
CMP_HIDDEN = 128
SLC_BLOCK = 64
N_SELECT = 16
N_LOCAL = 2
WINDOW = 512
N_BUCKETS = 32
MAX_DISTANCE = 128
LN_EPS = 1e-5
FORCED = 1e9
NEG = -1e30

LANES = 128
TQ = 128
R = N_HEADS * TQ
NSB = 128
TT = 512
HALO = 16
KT = 512
KPAD = KT
N_WIN = WINDOW + TQ
LOG2E = math.log2(math.e)
_PAD_FLAG_COL = 32
MIX_W = 2 * D_POOL + 4 * D_CONV
VMEM_LIMIT = 56 * 1024 * 1024

_NT = (((1,), (1,)), ((), ()))


def _dot(a, b):
    return jnp.dot(a, b, preferred_element_type=f32)


def _dot_nt(a, b):
    return lax.dot_general(a, b, _NT, preferred_element_type=f32)


def _sigmoid(x):
    return 1.0 / (1.0 + jnp.exp(-x))


_C_MIX = (0, MIX_W)
_C_Q = (_C_MIX[1], _C_MIX[1] + D_NSA)
_C_Z = (_C_Q[1], _C_Q[1] + D_NSA)
_C_CMP = (_C_Z[1], _C_Z[1] + 2 * D_KV)
_C_K = (_C_CMP[1], _C_CMP[1] + 2 * D_KV)
_C_V = (_C_K[1], _C_K[1] + 2 * D_KV)
_C_G = (_C_V[1], _C_V[1] + LANES)
W_ALL = _C_G[1]


def _inproj_kernel(x_ref, w_ref, mix_ref, q_ref, z_ref, cmp_ref, k_ref, v_ref, g_ref):
    x = x_ref[...].astype(bf16)

    def proj(c):
        return _dot(x, w_ref[:, c[0]:c[1]])

    mix_ref[...] = proj(_C_MIX)
    q_ref[...] = (proj(_C_Q) * (HEAD_DIM ** -0.5 * LOG2E)).astype(bf16)
    z_ref[...] = proj(_C_Z).astype(bf16)
    cmp_ref[...] = proj(_C_CMP).astype(bf16)
    k_ref[...] = proj(_C_K).astype(bf16)
    v_ref[...] = proj(_C_V).astype(bf16)
    g_ref[...] = proj(_C_G)


def _inproj(x2, w_all):
    bt = x2.shape[0]
    row = lambda i: (i, 0)
    outs = [(MIX_W, f32), (D_NSA, bf16), (D_NSA, bf16), (2 * D_KV, bf16), (2 * D_KV, bf16),
            (2 * D_KV, bf16), (LANES, f32)]
    return pl.pallas_call(
        _inproj_kernel,
        grid=(bt // TT,),
        in_specs=[pl.BlockSpec((TT, D_MODEL), row),
                  pl.BlockSpec((D_MODEL, W_ALL), lambda i: (0, 0))],
        out_specs=[pl.BlockSpec((TT, w), row) for w, _ in outs],
        out_shape=[jax.ShapeDtypeStruct((bt, w), dt) for w, dt in outs],
        compiler_params=pltpu.CompilerParams(dimension_semantics=("arbitrary",),
                                             vmem_limit_bytes=VMEM_LIMIT),
        name="inproj",
    )(x2, w_all)


def _mixer_kernel(mix_ref, halo_ref, pw_ref, ps_ref, cw_ref, out_ref):
    i = pl.program_id(1)
    n = TT + HALO
    halo = jnp.where(i > 0, halo_ref[...], 0.0)
    ext = jnp.concatenate([halo, mix_ref[...]], axis=0)

    e = ext[:, 0:D_POOL]
    s2 = e + pltpu.roll(e, 1, axis=0)
    s4 = s2 + pltpu.roll(s2, 2, axis=0)
    s8 = s4 + pltpu.roll(s4, 4, axis=0)
    s16 = s8 + pltpu.roll(s8, 8, axis=0)
    lane = lax.broadcasted_iota(jnp.int32, (TT, D_POOL), 1)
    grp = lane // POOL_GROUP
    wsum = jnp.where(grp == 0, s2[HALO:], jnp.where(grp == 1, s4[HALO:], jnp.where(grp == 2, s8[HALO:], s16[HALO:])))
    win = jnp.left_shift(2, grp)
    pos = i * TT + lax.broadcasted_iota(jnp.int32, (TT, D_POOL), 0)
    cnt = jnp.minimum(pos + 1, win).astype(f32)
    v = e[HALO:]
    pooled = wsum / cnt - v
    y_pool = _dot(pooled.astype(bf16), pw_ref[...]) * ps_ref[...]
    zp = ext[HALO:, D_POOL:2 * D_POOL]
    y_pool = y_pool * (zp * _sigmoid(zp))

    o = 2 * D_POOL
    cb = ext[HALO:, o:o + D_CONV]
    u = ext[:, o + D_CONV:o + 2 * D_CONV] * ext[:, o + 2 * D_CONV:o + 3 * D_CONV]
    zc = ext[HALO:, o + 3 * D_CONV:o + 4 * D_CONV]
    conv = cw_ref[CONV_WIDTH - 1:CONV_WIDTH, :] * u[HALO:]
    for k in range(CONV_WIDTH - 1):
        conv = conv + cw_ref[k:k + 1, :] * pltpu.roll(u, CONV_WIDTH - 1 - k, axis=0)[HALO:]
    y_conv = cb * conv * (zc * _sigmoid(zc))

    out_ref[:, 0:D_POOL] = y_pool.astype(bf16)
    out_ref[:, D_POOL:D_POOL + D_CONV] = y_conv.astype(bf16)
    del n


def _mixers(mix, pw_bd, pool_scale, conv_w, B, T):
    nt = T // TT
    hb = TT // HALO
    return pl.pallas_call(
        _mixer_kernel,
        grid=(B, nt),
        in_specs=[pl.BlockSpec((TT, MIX_W), lambda b, i: (b * nt + i, 0)),
                  pl.BlockSpec((HALO, MIX_W), lambda b, i: (jnp.maximum((b * nt + i) * hb - 1, 0), 0)),
                  pl.BlockSpec((D_POOL, D_POOL), lambda b, i: (0, 0)),
                  pl.BlockSpec((1, D_POOL), lambda b, i: (0, 0)),
                  pl.BlockSpec((CONV_WIDTH, D_CONV), lambda b, i: (0, 0))],
        out_specs=pl.BlockSpec((TT, D_POOL + D_CONV), lambda b, i: (b * nt + i, 0)),
        out_shape=jax.ShapeDtypeStruct((B * T, D_POOL + D_CONV), bf16),
        compiler_params=pltpu.CompilerParams(dimension_semantics=("arbitrary", "arbitrary"),
                                             vmem_limit_bytes=VMEM_LIMIT),
        name="mixers",
    )(mix, mix, pw_bd, pool_scale, conv_w)


CH = 4 * CMP_HIDDEN


def _compress_kernel(r_ref, pe_ref, w1_ref, w2_ref, w2t_ref, out_ref, outT_ref):
    nc = r_ref.shape[0]
    zz = _dot(r_ref[...], w1_ref[...])
    pb = _dot(pe_ref[...], w1_ref[...])
    z0 = zz[:, 0:CH] + pb[0:1, 0:CH]
    z1 = zz[:, CH:2 * CH] + pb[1:2, CH:2 * CH]
    h = z0 + pltpu.roll(z1, nc - 1, axis=0)
    h = (h * _sigmoid(h)).astype(bf16)
    out_ref[...] = _dot(h, w2_ref[...]).astype(bf16)
    outT_ref[...] = _dot_nt(w2t_ref[...], h).astype(bf16)


def _compress(r3, pe_r, w1_big, w2_big, w2t_big):
    B, nc, rw = r3.shape
    const = lambda b: (0, 0)
    return pl.pallas_call(
        _compress_kernel,
        grid=(B,),
        in_specs=[pl.BlockSpec((None, nc, rw), lambda b: (b, 0, 0)),
                  pl.BlockSpec((8, rw), const),
                  pl.BlockSpec((rw, 2 * CH), const),
                  pl.BlockSpec((CH, 2 * D_KV), const),
                  pl.BlockSpec((2 * D_KV, CH), const)],
        out_specs=[pl.BlockSpec((None, nc, 2 * D_KV), lambda b: (b, 0, 0)),
                   pl.BlockSpec((None, 2 * D_KV, nc), lambda b: (b, 0, 0))],
        out_shape=[jax.ShapeDtypeStruct((B, nc, 2 * D_KV), bf16),
                   jax.ShapeDtypeStruct((B, 2 * D_KV, nc), bf16)],
        compiler_params=pltpu.CompilerParams(dimension_semantics=("arbitrary",),
                                             vmem_limit_bytes=VMEM_LIMIT),
        name="compress",
    )(r3, pe_r, w1_big, w2_big, w2t_big)


def _select_penalty(imp_t, t0):
    shape = (NSB, N_KV * TQ)
    jblk = lax.broadcasted_iota(jnp.int32, shape, 0)
    tq = lax.broadcasted_iota(jnp.int32, shape, 1) % TQ
    back = jnp.right_shift(t0 + tq, int(math.log2(SLC_BLOCK))) - jblk
    causal = back >= 0
    forced = jnp.logical_or(jblk == 0, jnp.logical_and(causal, back < N_LOCAL))
    score = jnp.where(forced, FORCED, jnp.where(causal, imp_t, NEG))
    jf = jblk.astype(f32)
    pen = jnp.full(shape, NEG, f32)
    for _ in range(N_SELECT):
        best = jnp.max(score, axis=0, keepdims=True)
        first = jnp.min(jnp.where(score == best, jf, float(NSB)), axis=0, keepdims=True)
        pick = jf == first
        pen = jnp.where(pick, 0.0, pen)
        score = jnp.where(pick, -jnp.inf, score)
    return pen


def _nsa_kernel(q_ref, z_ref, g_ref, cmp_ref, cmpT_ref, ks_ref, kw_ref, vsT_ref, vwT_ref,
                e_ref, ovT_ref, tab_ref, acmp_ref, out_ref,
                m_ref, l_ref, acc_ref, sa_ref, sb_ref, mxa_ref, mxb_ref):
    qi = pl.program_id(1)
    t0 = qi * TQ
    nc = cmp_ref.shape[0]

    lane = lax.broadcasted_iota(jnp.int32, (TQ, LANES), 1)
    low = lane < HEAD_DIM
    blocks = []
    for h in range(N_HEADS):
        blk = q_ref[:, LANES * (h % GROUP):LANES * (h % GROUP + 1)].astype(f32)
        blocks.append((jnp.where(low, blk, 0.0) if h < GROUP else jnp.where(low, 0.0, blk)).astype(bf16))
    lhs8 = jnp.concatenate(blocks, axis=0)

    n_i = lax.broadcasted_iota(jnp.int32, (nc, LANES), 0)
    j_i = lax.broadcasted_iota(jnp.int32, (nc, LANES), 1)
    nstart = t0 // CMP_STRIDE - 8
    in_window = jnp.logical_and(j_i < 32, n_i == nstart + jnp.where(j_i < 16, j_i, j_i - 16))
    future = jnp.logical_and(j_i == 32, n_i >= nstart + 15)
    place_b = jnp.where(jnp.logical_or(in_window, future), 1.0, 0.0).astype(bf16)
    kc_ext = jnp.concatenate([cmp_ref[:, 0:D_KV], place_b], axis=1)
    q_cmp = jnp.concatenate([lhs8, acmp_ref[...]], axis=1)
    s1 = _dot_nt(kc_ext, q_cmp)
    m1 = jnp.max(s1, axis=0, keepdims=True)
    p1 = jnp.exp2(s1 - m1)
    l1 = jnp.sum(p1, axis=0, keepdims=True)
    tcol = t0 + lax.broadcasted_iota(jnp.int32, (1, R), 1) % TQ
    p1 = p1 * jnp.where(tcol >= CMP_BLOCK - 1, 1.0 / l1, 0.0)
    o1 = _dot(cmpT_ref[D_KV:2 * D_KV, :], p1.astype(bf16))

    w0 = pl.multiple_of(t0, LANES)
    s3 = _dot_nt(kw_ref[pl.ds(w0, N_WIN), :], q_cmp)
    s3 = jnp.concatenate([s3[0:TQ] + tab_ref[0:TQ], s3[TQ:WINDOW - TQ], s3[WINDOW - TQ:] + tab_ref[TQ:3 * TQ]], axis=0)
    m3 = jnp.max(s3, axis=0, keepdims=True)
    p3 = jnp.exp2(s3 - m3)
    inv3 = 1.0 / jnp.sum(p3, axis=0, keepdims=True)
    o3 = _dot(vwT_ref[:, pl.ds(w0, N_WIN)], p3.astype(bf16))

    sums = []
    for g in range(N_KV):
        ps = p1[:, g * GROUP * TQ:g * GROUP * TQ + TQ]
        for r in range(1, GROUP):
            c0 = (g * GROUP + r) * TQ
            ps = ps + p1[:, c0:c0 + TQ]
        sums.append(ps)
    ps = jnp.concatenate(sums, axis=1)
    hi = ps.astype(bf16)
    lo = (ps - hi.astype(f32)).astype(bf16)
    imp_t = _dot(ovT_ref[...], hi) + _dot(ovT_ref[...], lo)
    pen_t = _select_penalty(imp_t, t0)
    pens = []
    for g in range(N_KV):
        pens += [pen_t[:, g * TQ:(g + 1) * TQ].T.astype(bf16)] * GROUP
    q_slc = jnp.concatenate([lhs8, jnp.concatenate(pens, axis=0)], axis=1)

    m_ref[...] = jnp.full((1, R), NEG, f32)
    l_ref[...] = jnp.zeros((1, R), f32)
    acc_ref[...] = jnp.zeros((LANES, R), f32)
    n_far = (qi + KT // TQ) // (KT // TQ) - 1
    first = (qi + 1) * TQ - n_far * KT

    def tile_row(i):
        return pl.multiple_of(first + i * KT, LANES)

    def produce(i, s_ref, mx_ref):
        r0 = tile_row(i)
        k_ext = jnp.concatenate([ks_ref[pl.ds(r0, KT), :], e_ref[pl.ds(r0, KT), :]], axis=1)
        s = _dot_nt(k_ext, q_slc)
        s_ref[...] = s
        mx_ref[...] = jnp.max(s, axis=0, keepdims=True)

    def consume(i, s_ref, mx_ref, last):
        m_prev = m_ref[...]
        if last:
            s = jnp.concatenate([s_ref[0:KT - 2 * TQ], s_ref[KT - 2 * TQ:KT] + tab_ref[TQ:3 * TQ]], axis=0)
            m_new = jnp.maximum(m_prev, jnp.max(s, axis=0, keepdims=True))
        else:
            s = s_ref[...]
            m_new = jnp.maximum(m_prev, mx_ref[...])
        alpha = jnp.exp2(m_prev - m_new)
        p = jnp.exp2(s - m_new)
        l_ref[...] = alpha * l_ref[...] + jnp.sum(p, axis=0, keepdims=True)
        acc_ref[...] = acc_ref[...] * alpha + _dot(vsT_ref[:, pl.ds(tile_row(i), KT)], p.astype(bf16))
        m_ref[...] = m_new

    produce(0, sa_ref, mxa_ref)

    def pair(j, c):
        produce(2 * j + 1, sb_ref, mxb_ref)
        consume(2 * j, sa_ref, mxa_ref, False)
        produce(2 * j + 2, sa_ref, mxa_ref)
        consume(2 * j + 1, sb_ref, mxb_ref, False)
        return c

    lax.fori_loop(0, n_far // 2, pair, 0)

    @pl.when(n_far % 2 == 0)
    def _():
        consume(n_far, sa_ref, mxa_ref, True)

    @pl.when(n_far % 2 == 1)
    def _():
        produce(n_far, sb_ref, mxb_ref)
        consume(n_far - 1, sa_ref, mxa_ref, False)
        consume(n_far, sb_ref, mxb_ref, True)

    inv2 = 1.0 / l_ref[...]

    sig_t = _sigmoid(g_ref[...]).T
    sub = lax.broadcasted_iota(jnp.int32, (LANES, TQ), 0)
    outs = []
    for h in range(N_HEADS):
        c = slice(h * TQ, (h + 1) * TQ)
        g1 = sig_t[3 * h:3 * h + 1, :]
        g2 = sig_t[3 * h + 1:3 * h + 2, :] * inv2[:, c]
        g3 = sig_t[3 * h + 2:3 * h + 3, :] * inv3[:, c]
        outs.append(g1 * o1[:, c] + g2 * acc_ref[:, c] + g3 * o3[:, c])
    for p in range(GROUP):
        y = jnp.where(sub < HEAD_DIM, outs[p], outs[p + GROUP]).T
        zb = z_ref[:, p * LANES:(p + 1) * LANES].astype(f32)
        out_ref[:, p * LANES:(p + 1) * LANES] = (y * zb * _sigmoid(zb)).astype(bf16)


def _nsa(q, z, g, cmp, cmpT, ks, kw, vsT, vwT, e_pad, ovT, tabs, acmp, B, T):
    nq = T // TQ
    nc = T // CMP_STRIDE
    tp = T + KPAD
    tile = lambda b, i: (b * nq + i, 0)
    const2 = lambda b, i: (0, 0)
    batch3 = lambda b, i: (b, 0, 0)
    return pl.pallas_call(
        _nsa_kernel,
        grid=(B, nq),
        in_specs=[pl.BlockSpec((TQ, D_NSA), tile),
                  pl.BlockSpec((TQ, D_NSA), tile),
                  pl.BlockSpec((TQ, LANES), tile),
                  pl.BlockSpec((None, nc, 2 * D_KV), batch3),
                  pl.BlockSpec((None, 2 * D_KV, nc), batch3),
                  pl.BlockSpec((None, tp, D_KV), batch3),
                  pl.BlockSpec((None, tp, 2 * D_KV), batch3),
                  pl.BlockSpec((None, D_KV, tp), batch3),
                  pl.BlockSpec((None, D_KV, tp), batch3),
                  pl.BlockSpec((tp, NSB), const2),
                  pl.BlockSpec((NSB, nc), const2),
                  pl.BlockSpec((3 * TQ, R), const2),
                  pl.BlockSpec((R, LANES), const2)],
        out_specs=pl.BlockSpec((TQ, D_NSA), tile),
        out_shape=jax.ShapeDtypeStruct((B * T, D_NSA), bf16),
        scratch_shapes=[pltpu.VMEM((1, R), f32), pltpu.VMEM((1, R), f32), pltpu.VMEM((LANES, R), f32),
                        pltpu.VMEM((KT, R), f32), pltpu.VMEM((KT, R), f32),
                        pltpu.VMEM((1, R), f32), pltpu.VMEM((1, R), f32)],
        compiler_params=pltpu.CompilerParams(dimension_semantics=("arbitrary", "arbitrary"),
                                             vmem_limit_bytes=VMEM_LIMIT),
        name="nsa",
    )(q, z, g, cmp, cmpT, ks, kw, vsT, vwT, e_pad, ovT, tabs, acmp)


def _outproj_kernel(alpha, ymix_ref, ynsa_ref, x_ref, w_ref, g_ref, b_ref, out_ref):
    y = jnp.concatenate([ymix_ref[...], ynsa_ref[...]], axis=1)
    r = alpha * x_ref[...] + _dot(y, w_ref[...])
    mu = jnp.mean(r, axis=-1, keepdims=True)
    d = r - mu
    var = jnp.mean(d * d, axis=-1, keepdims=True)
    out_ref[...] = d * lax.rsqrt(var + LN_EPS) * g_ref[...] + b_ref[...]


def _outproj(ymix, ynsa, x2, w_out, ln_g, ln_b, alpha):
    bt = x2.shape[0]
    row = lambda i: (i, 0)
    const = lambda i: (0, 0)
    return pl.pallas_call(
        functools.partial(_outproj_kernel, alpha),
        grid=(bt // TT,),
        in_specs=[pl.BlockSpec((TT, D_POOL + D_CONV), row),
                  pl.BlockSpec((TT, D_NSA), row),
                  pl.BlockSpec((TT, D_MODEL), row),
                  pl.BlockSpec((D_MODEL, D_MODEL), const),
                  pl.BlockSpec((1, D_MODEL), const),
                  pl.BlockSpec((1, D_MODEL), const)],
        out_specs=pl.BlockSpec((TT, D_MODEL), row),
        out_shape=jax.ShapeDtypeStruct((bt, D_MODEL), f32),
        compiler_params=pltpu.CompilerParams(dimension_semantics=("arbitrary",),
                                             vmem_limit_bytes=VMEM_LIMIT),
        name="outproj",
    )(ymix, ynsa, x2, w_out, ln_g, ln_b)


def _bucket_np(d):
    d = np.asarray(d)
    max_exact = N_BUCKETS // 2
    nf = np.maximum(d, 1).astype(np.float32)
    large = max_exact + (np.log(nf / np.float32(max_exact)) / np.float32(math.log(MAX_DISTANCE / max_exact))
                         * np.float32(N_BUCKETS - max_exact)).astype(np.int32)
    large = np.minimum(large, N_BUCKETS - 1)
    return np.where(d < max_exact, d, large)


_FAR_DIST = 113
assert _bucket_np(np.arange(_FAR_DIST, 4 * WINDOW)).min() == N_BUCKETS - 1

_PAIR_PERM = np.concatenate([np.concatenate([np.arange(HEAD_DIM) + HEAD_DIM * p,
                                             np.arange(HEAD_DIM) + HEAD_DIM * (p + GROUP)]) for p in range(GROUP)])


def _bias_tables(rel_bias):
    d = np.arange(2 * TQ)
    tabp = ((rel_bias[_bucket_np(d), :] - rel_bias[N_BUCKETS - 1:N_BUCKETS, :]) * LOG2E).T
    sl = np.arange(TQ)[:, None]
    tl = np.arange(TQ)[None, :]

    def toeplitz(dist, ok):
        vals = tabp[:, np.clip(dist, 0, 2 * TQ - 1)]
        vals = jnp.where(jnp.asarray(ok)[None], vals, NEG)
        return vals.transpose(1, 0, 2).reshape(TQ, R)

    diag = toeplitz(tl - sl, tl - sl >= 0)
    prev = toeplitz(tl - sl + TQ, np.ones((TQ, TQ), bool))
    edge = jnp.asarray(np.tile(np.where(sl > tl, 0.0, NEG).astype(np.float32), (1, N_HEADS)))
    tabs = jnp.concatenate([edge, prev, diag], axis=0).astype(f32)

    j = np.arange(16)[None, :]
    dc = np.arange(TQ)[:, None] - CMP_STRIDE * (j - 8) - (CMP_BLOCK - 1)
    okc = jnp.asarray(dc >= 0)[None]
    vc = jnp.where(okc, tabp[:, np.clip(dc, 0, 2 * TQ - 1)], NEG)
    hi = vc.astype(bf16)
    lo = jnp.where(okc, vc - hi.astype(f32), 0.0).astype(bf16)
    fut = jnp.full((N_HEADS, TQ, 1), NEG, bf16)
    pad = jnp.zeros((N_HEADS, TQ, LANES - _PAD_FLAG_COL - 1), bf16)
    acmp = jnp.concatenate([hi, lo, fut, pad], axis=-1).reshape(R, LANES)
    return tabs, acmp


def _static_tables(T):
    nc = T // CMP_STRIDE
    ns = T // SLC_BLOCK
    cstart = np.arange(nc)[None, :] * CMP_STRIDE
    sstart = np.arange(NSB)[:, None] * SLC_BLOCK
    ov = np.clip(np.minimum(cstart + CMP_BLOCK, sstart + SLC_BLOCK) - np.maximum(cstart, sstart), 0, None) / CMP_STRIDE
    ov[ns:, :] = 0
    ov[:, nc - 1] = 0
    e_pad = np.concatenate([np.ones((KPAD, NSB), bool),
                            np.arange(T)[:, None] // SLC_BLOCK == np.arange(NSB)[None, :]], axis=0)
    return jnp.asarray(ov, bf16), jnp.asarray(e_pad, bf16)


def _layer_weights(w_in, w_out, pool_w, cmp_pe_k, cmp_w1_k, cmp_w2_k, cmp_pe_v, cmp_w1_v, cmp_w2_v):
    sizes = (D_POOL, D_POOL, D_CONV, D_CONV, D_CONV, D_CONV, D_NSA, D_KV, D_KV, D_KV, D_KV, D_KV, D_KV,
             3 * N_HEADS, D_NSA)
    offs = np.cumsum((0,) + sizes)
    col = lambda i: w_in[:, offs[i]:offs[i + 1]]
    wg = jnp.pad(col(13), ((0, 0), (0, LANES - 3 * N_HEADS)))
    w_all = jnp.concatenate([w_in[:, 0:MIX_W], col(6)[:, _PAIR_PERM], col(14)[:, _PAIR_PERM],
                             col(7), col(8), col(9), col(11), col(10), col(12), wg], axis=1).astype(bf16)
    w_out_p = jnp.concatenate([w_out[0:D_POOL + D_CONV], w_out[D_POOL + D_CONV:][_PAIR_PERM]], axis=0).astype(bf16)

    pw_bd = jnp.zeros((D_POOL, D_POOL), f32)
    for gi in range(len(POOL_WINDOWS)):
        s = slice(gi * POOL_GROUP, (gi + 1) * POOL_GROUP)
        pw_bd = pw_bd.at[s, s].set(pool_w[gi])

    half = CMP_BLOCK // 2
    w1_big = jnp.zeros((half, 2, N_KV, HEAD_DIM, 2, 2, N_KV, CMP_HIDDEN), f32)
    pe_r = jnp.zeros((8, half, 2, N_KV, HEAD_DIM), f32)
    w2_big = jnp.zeros((2, N_KV, CMP_HIDDEN, 2, N_KV, HEAD_DIM), f32)
    for kv, (pe, w1, w2) in enumerate(((cmp_pe_k, cmp_w1_k, cmp_w2_k), (cmp_pe_v, cmp_w1_v, cmp_w2_v))):
        w1r = w1.reshape(2, half, HEAD_DIM, CMP_HIDDEN)
        for g in range(N_KV):
            for a in range(2):
                w1_big = w1_big.at[:, kv, g, :, a, kv, g, :].set(w1r[a])
                pe_r = pe_r.at[a, :, kv, g, :].set(pe[a * half:(a + 1) * half])
            w2_big = w2_big.at[kv, g, :, kv, g, :].set(w2)
    rw = half * 2 * D_KV
    w1_big = w1_big.reshape(rw, 2 * CH).astype(bf16)
    pe_r = pe_r.reshape(8, rw).astype(bf16)
    w2_big = w2_big.reshape(CH, 2 * D_KV).astype(bf16)
    return w_all, w_out_p, pw_bd.astype(bf16), w1_big, pe_r, w2_big, w2_big.T


def _layer(x2, B, T, alpha, w_in, w_out, pool_w, pool_scale, conv_w, pe_k, w1_k, w2_k, pe_v, w1_v, w2_v,
           ln_g, ln_b, tabs, acmp, ovT, e_pad):
    w_all, w_out_p, pw_bd, w1_big, pe_r, w2_big, w2t_big = _layer_weights(
        w_in, w_out, pool_w, pe_k, w1_k, w2_k, pe_v, w1_v, w2_v)
    mix, q, z, cmp_in, kk, vv, g = _inproj(x2, w_all)
    ymix = _mixers(mix, pw_bd, pool_scale.reshape(1, D_POOL), conv_w, B, T)
    nc = T // CMP_STRIDE
    cmp, cmpT = _compress(cmp_in.reshape(B, nc, CMP_STRIDE * 2 * D_KV), pe_r, w1_big, w2_big, w2t_big)
    kk = kk.reshape(B, T, 2 * D_KV)
    vv = vv.reshape(B, T, 2 * D_KV)
    front = ((0, 0), (KPAD, 0), (0, 0))
    ks = jnp.pad(kk[:, :, :D_KV], front)
    flag = jnp.zeros((B, T + KPAD, D_KV), bf16).at[:, :KPAD, _PAD_FLAG_COL].set(1.0)
    kw = jnp.concatenate([jnp.pad(kk[:, :, D_KV:], front), flag], axis=2)
    vsT = jnp.pad(vv[:, :, :D_KV], front).transpose(0, 2, 1)
    vwT = jnp.pad(vv[:, :, D_KV:], front).transpose(0, 2, 1)
    ynsa = _nsa(q, z, g, cmp, cmpT, ks, kw, vsT, vwT, e_pad, ovT, tabs, acmp, B, T)
    return _outproj(ymix, ynsa, x2, w_out_p, ln_g.reshape(1, D_MODEL), ln_b.reshape(1, D_MODEL), alpha)


def kernel(x, w_in, w_out, pool_w, pool_scale, conv_w, cmp_pe_k, cmp_w1_k, cmp_w2_k, cmp_pe_v, cmp_w1_v, cmp_w2_v,
           rel_bias, ln_g, ln_b):
    B, T, D = x.shape
    depth = w_in.shape[0]
    assert D == D_MODEL and T % TT == 0 and T // SLC_BLOCK <= NSB and T // SLC_BLOCK >= N_SELECT
    alpha = (2 * depth) ** 0.25
    tabs, acmp = _bias_tables(rel_bias)
    ovT, e_pad = _static_tables(T)
    h = x.reshape(B * T, D)
    for l in range(depth):
        h = _layer(h, B, T, alpha, w_in[l], w_out[l], pool_w[l], pool_scale[l], conv_w[l],
                   cmp_pe_k[l], cmp_w1_k[l], cmp_w2_k[l], cmp_pe_v[l], cmp_w1_v[l], cmp_w2_v[l],
                   ln_g[l], ln_b[l], tabs, acmp, ovT, e_pad)
    return h.reshape(B, T, D)
```

```python
import functools
import math

import numpy as np
import jax
import jax.numpy as jnp
from jax import lax
from jax.experimental import pallas as pl
from jax.experimental.pallas import tpu as pltpu

f32 = jnp.float32
bf16 = jnp.bfloat16

D_MODEL = 1024
D_POOL = 256
D_CONV = 256
D_NSA = 512
HEAD_DIM = 64
N_HEADS = 8
N_KV = 2
GROUP = 4
D_KV = 128
POOL_GROUP = 64
POOL_WINDOWS = (2, 4, 8, 16)
CONV_WIDTH = 3
CMP_BLOCK = 32
CMP_STRIDE = 16
CMP_HIDDEN = 128
SLC_BLOCK = 64
N_SELECT = 16
N_LOCAL = 2
WINDOW = 512
N_BUCKETS = 32
MAX_DISTANCE = 128
LN_EPS = 1e-5
FORCED = 1e9
NEG = -1e30

LANES = 128
TQ = 128
R = N_HEADS * TQ
NSB = 128
TT = 512
HALO = 16
KT = 512
KPAD = KT
N_WIN = WINDOW + TQ
LOG2E = math.log2(math.e)
_PAD_FLAG_COL = 32
MIX_W = 2 * D_POOL + 4 * D_CONV
VMEM_LIMIT = 56 * 1024 * 1024

_NT = (((1,), (1,)), ((), ()))


def _dot(a, b):
    return jnp.dot(a, b, preferred_element_type=f32)


def _dot_nt(a, b):
    return lax.dot_general(a, b, _NT, preferred_element_type=f32)


def _sigmoid(x):
    return 1.0 / (1.0 + jnp.exp(-x))


_C_MIX = (0, MIX_W)
_C_Q = (_C_MIX[1], _C_MIX[1] + D_NSA)
_C_Z = (_C_Q[1], _C_Q[1] + D_NSA)
_C_CMP = (_C_Z[1], _C_Z[1] + 2 * D_KV)
_C_K = (_C_CMP[1], _C_CMP[1] + 2 * D_KV)
_C_G = (_C_K[1], _C_K[1] + LANES)
W_ALL = _C_G[1]
CMP_ROWS = TT // CMP_STRIDE


def _local_mixers(ext, i, pw_ref, ps_ref, cw_ref):
    e = ext[:, 0:D_POOL]
    s2 = e + pltpu.roll(e, 1, axis=0)
    s4 = s2 + pltpu.roll(s2, 2, axis=0)
    s8 = s4 + pltpu.roll(s4, 4, axis=0)
    s16 = s8 + pltpu.roll(s8, 8, axis=0)
    lane = lax.broadcasted_iota(jnp.int32, (TT, D_POOL), 1)
    grp = lane // POOL_GROUP
    wsum = jnp.where(grp == 0, s2[HALO:], jnp.where(grp == 1, s4[HALO:], jnp.where(grp == 2, s8[HALO:], s16[HALO:])))
    win = jnp.left_shift(2, grp)
    pos = i * TT + lax.broadcasted_iota(jnp.int32, (TT, D_POOL), 0)
    cnt = jnp.minimum(pos + 1, win).astype(f32)
    v = e[HALO:]
    pooled = wsum / cnt - v
    y_pool = _dot(pooled.astype(bf16), pw_ref[...]) * ps_ref[...]
    zp = ext[HALO:, D_POOL:2 * D_POOL]
    y_pool = y_pool * (zp * _sigmoid(zp))

    o = 2 * D_POOL
    cb = ext[HALO:, o:o + D_CONV]
    u = ext[:, o + D_CONV:o + 2 * D_CONV] * ext[:, o + 2 * D_CONV:o + 3 * D_CONV]
    zc = ext[HALO:, o + 3 * D_CONV:o + 4 * D_CONV]
    conv = cw_ref[CONV_WIDTH - 1:CONV_WIDTH, :] * u[HALO:]
    for k in range(CONV_WIDTH - 1):
        conv = conv + cw_ref[k:k + 1, :] * pltpu.roll(u, CONV_WIDTH - 1 - k, axis=0)[HALO:]
    y_conv = cb * conv * (zc * _sigmoid(zc))
    return y_pool, y_conv


def _inproj_kernel(x_ref, w_ref, wvt_ref, pw_ref, ps_ref, cw_ref, ks_in, kw_in, vs_in, vw_in,
                   ymix_ref, q_ref, z_ref, cmp_ref, g_ref, ks_ref, kw_ref, vsT_ref, vwT_ref,
                   halo_ref, cscr_ref):
    del ks_in, kw_in, vs_in, vw_in
    i = pl.program_id(1)
    x = x_ref[...].astype(bf16)

    def proj(c):
        return _dot(x, w_ref[:, c[0]:c[1]])

    mix = proj(_C_MIX)
    halo = jnp.where(i > 0, halo_ref[...], 0.0)
    y_pool, y_conv = _local_mixers(jnp.concatenate([halo, mix], axis=0), i, pw_ref, ps_ref, cw_ref)
    halo_ref[...] = mix[TT - HALO:]
    ymix_ref[:, 0:D_POOL] = y_pool.astype(bf16)
    ymix_ref[:, D_POOL:D_POOL + D_CONV] = y_conv.astype(bf16)

    q_ref[...] = (proj(_C_Q) * (HEAD_DIM ** -0.5 * LOG2E)).astype(bf16)
    z_ref[...] = proj(_C_Z).astype(bf16)
    g_ref[...] = proj(_C_G)
    kk = proj(_C_K).astype(bf16)
    ks_ref[...] = kk[:, 0:D_KV]
    kw_ref[:, 0:D_KV] = kk[:, D_KV:2 * D_KV]
    kw_ref[:, D_KV:2 * D_KV] = jnp.zeros((TT, D_KV), bf16)
    vt = _dot_nt(wvt_ref[...], x)
    vsT_ref[...] = vt[0:D_KV].astype(bf16)
    vwT_ref[...] = vt[D_KV:2 * D_KV].astype(bf16)

    kvc = proj(_C_CMP)
    for c in range(2):
        cscr_ref[c] = kvc[:, c * LANES:(c + 1) * LANES]
        for l in range(CMP_STRIDE):
            cmp_ref[l, :, c * LANES:(c + 1) * LANES] = (
                cscr_ref[c, pl.ds(l, CMP_ROWS, stride=CMP_STRIDE), :].astype(bf16))


def _inproj(x2, w_all, wvt, pw_bd, pool_scale, conv_w, kpads, layer, B, T):
    nt = T // TT
    bt = B * T
    row = lambda b, i: (b * nt + i, 0)
    wsel = lambda b, i: (layer, 0, 0)
    any_spec = pl.BlockSpec(memory_space=pl.ANY)
    tp = T + KPAD
    out_shape = [jax.ShapeDtypeStruct((bt, D_POOL + D_CONV), bf16),
                 jax.ShapeDtypeStruct((bt, D_NSA), bf16),
                 jax.ShapeDtypeStruct((bt, D_NSA), bf16),
                 jax.ShapeDtypeStruct((CMP_STRIDE, bt // CMP_STRIDE, 2 * D_KV), bf16),
                 jax.ShapeDtypeStruct((bt, LANES), f32),
                 jax.ShapeDtypeStruct((B, tp, D_KV), bf16),
                 jax.ShapeDtypeStruct((B, tp, 2 * D_KV), bf16),
                 jax.ShapeDtypeStruct((B, D_KV, tp), bf16),
                 jax.ShapeDtypeStruct((B, D_KV, tp), bf16)]
    out_specs = [pl.BlockSpec((TT, D_POOL + D_CONV), row),
                 pl.BlockSpec((TT, D_NSA), row),
                 pl.BlockSpec((TT, D_NSA), row),
                 pl.BlockSpec((CMP_STRIDE, CMP_ROWS, 2 * D_KV), lambda b, i: (0, b * nt + i, 0)),
                 pl.BlockSpec((TT, LANES), row),
                 pl.BlockSpec((None, TT, D_KV), lambda b, i: (b, i + KPAD // TT, 0)),
                 pl.BlockSpec((None, TT, 2 * D_KV), lambda b, i: (b, i + KPAD // TT, 0)),
                 pl.BlockSpec((None, D_KV, TT), lambda b, i: (b, 0, i + KPAD // TT)),
                 pl.BlockSpec((None, D_KV, TT), lambda b, i: (b, 0, i + KPAD // TT))]
    return pl.pallas_call(
        _inproj_kernel,
        grid=(B, nt),
        in_specs=[pl.BlockSpec((TT, D_MODEL), row),
                  pl.BlockSpec((None, D_MODEL, W_ALL), wsel),
                  pl.BlockSpec((None, 2 * D_KV, D_MODEL), wsel),
                  pl.BlockSpec((None, D_POOL, D_POOL), wsel),
                  pl.BlockSpec((None, 1, D_POOL), wsel),
                  pl.BlockSpec((None, CONV_WIDTH, D_CONV), wsel),
                  any_spec, any_spec, any_spec, any_spec],
        out_specs=out_specs,
        out_shape=out_shape,
        input_output_aliases={6: 5, 7: 6, 8: 7, 9: 8},
        scratch_shapes=[pltpu.VMEM((HALO, MIX_W), f32), pltpu.VMEM((2, TT, LANES), f32)],
        compiler_params=pltpu.CompilerParams(dimension_semantics=("arbitrary", "arbitrary"),
                                             vmem_limit_bytes=VMEM_LIMIT),
        name="inproj",
    )(x2, w_all, wvt, pw_bd, pool_scale, conv_w, *kpads)


CH = 4 * CMP_HIDDEN


def _compress_kernel(r_ref, pe_ref, w1_ref, w2_ref, w2t_ref, out_ref, outT_ref):
    nc = r_ref.shape[1]
    rows = jnp.concatenate([r_ref[l] for l in range(CMP_STRIDE)], axis=1)
    zz = _dot(rows, w1_ref[...])
    pb = _dot(pe_ref[...], w1_ref[...])
    z0 = zz[:, 0:CH] + pb[0:1, 0:CH]
    z1 = zz[:, CH:2 * CH] + pb[1:2, CH:2 * CH]
    h = z0 + pltpu.roll(z1, nc - 1, axis=0)
    h = (h * _sigmoid(h)).astype(bf16)
    out_ref[...] = _dot(h, w2_ref[...]).astype(bf16)
    outT_ref[...] = _dot_nt(w2t_ref[...], h).astype(bf16)


def _compress(cmp_l, pe_r, w1_big, w2_big, w2t_big, layer, B):
    nc = cmp_l.shape[1] // B
    rw = CMP_STRIDE * 2 * D_KV
    wsel = lambda b: (layer, 0, 0)
    return pl.pallas_call(
        _compress_kernel,
        grid=(B,),
        in_specs=[pl.BlockSpec((CMP_STRIDE, nc, 2 * D_KV), lambda b: (0, b, 0)),
                  pl.BlockSpec((None, 8, rw), wsel),
                  pl.BlockSpec((None, rw, 2 * CH), wsel),
                  pl.BlockSpec((None, CH, 2 * D_KV), wsel),
                  pl.BlockSpec((None, 2 * D_KV, CH), wsel)],
        out_specs=[pl.BlockSpec((None, nc, 2 * D_KV), lambda b: (b, 0, 0)),
                   pl.BlockSpec((None, 2 * D_KV, nc), lambda b: (b, 0, 0))],
        out_shape=[jax.ShapeDtypeStruct((B, nc, 2 * D_KV), bf16),
                   jax.ShapeDtypeStruct((B, 2 * D_KV, nc), bf16)],
        compiler_params=pltpu.CompilerParams(dimension_semantics=("arbitrary",),
                                             vmem_limit_bytes=VMEM_LIMIT),
        name="compress",
    )(cmp_l, pe_r, w1_big, w2_big, w2t_big)


def _select_penalty(imp_t, t0):
    shape = (NSB, N_KV * TQ)
    jblk = lax.broadcasted_iota(jnp.int32, shape, 0)
    tq = lax.broadcasted_iota(jnp.int32, shape, 1) % TQ
    back = jnp.right_shift(t0 + tq, int(math.log2(SLC_BLOCK))) - jblk
    causal = back >= 0
    forced = jnp.logical_or(jblk == 0, jnp.logical_and(causal, back < N_LOCAL))
    score = jnp.where(forced, -jnp.inf, jnp.where(causal, imp_t, NEG))
    jf = jblk.astype(f32)
    pen = jnp.where(forced, 0.0, NEG)
    for _ in range(N_SELECT - N_LOCAL - 1):
        best = jnp.max(score, axis=0, keepdims=True)
        first = jnp.min(jnp.where(score == best, jf, float(NSB)), axis=0, keepdims=True)
        pick = jf == first
        pen = jnp.where(pick, 0.0, pen)
        score = jnp.where(pick, -jnp.inf, score)
    return pen


def _nsa_kernel(q_ref, z_ref, g_ref, cmp_ref, cmpT_ref, ks_ref, kw_ref, vsT_ref, vwT_ref,
                e_ref, ovT_ref, tab_ref, acmp_ref, out_ref,
                m_ref, l_ref, acc_ref, sa_ref, sb_ref, mxa_ref, mxb_ref):
    qi = pl.program_id(1)
    t0 = qi * TQ
    nc = cmp_ref.shape[0]

    lane = lax.broadcasted_iota(jnp.int32, (TQ, LANES), 1)
    low = lane < HEAD_DIM
    blocks = []
    for h in range(N_HEADS):
        blk = q_ref[:, LANES * (h % GROUP):LANES * (h % GROUP + 1)].astype(f32)
        blocks.append((jnp.where(low, blk, 0.0) if h < GROUP else jnp.where(low, 0.0, blk)).astype(bf16))
    lhs8 = jnp.concatenate(blocks, axis=0)

    n_i = lax.broadcasted_iota(jnp.int32, (nc, LANES), 0)
    j_i = lax.broadcasted_iota(jnp.int32, (nc, LANES), 1)
    nstart = t0 // CMP_STRIDE - 8
    in_window = jnp.logical_and(j_i < 32, n_i == nstart + jnp.where(j_i < 16, j_i, j_i - 16))
    future = jnp.logical_and(j_i == 32, n_i >= nstart + 15)
    place_b = jnp.where(jnp.logical_or(in_window, future), 1.0, 0.0).astype(bf16)
    kc_ext = jnp.concatenate([cmp_ref[:, 0:D_KV], place_b], axis=1)
    q_cmp = jnp.concatenate([lhs8, acmp_ref[...]], axis=1)
    s1 = _dot_nt(kc_ext, q_cmp)
    m1 = jnp.max(s1, axis=0, keepdims=True)
    p1 = jnp.exp2(s1 - m1)
    l1 = jnp.sum(p1, axis=0, keepdims=True)
    tcol = t0 + lax.broadcasted_iota(jnp.int32, (1, R), 1) % TQ
    p1 = p1 * jnp.where(tcol >= CMP_BLOCK - 1, 1.0 / l1, 0.0)
    o1 = _dot(cmpT_ref[D_KV:2 * D_KV, :], p1.astype(bf16))

    w0 = pl.multiple_of(t0, LANES)
    s3 = _dot_nt(kw_ref[pl.ds(w0, N_WIN), :], q_cmp)
    s3 = jnp.concatenate([s3[0:TQ] + tab_ref[0:TQ], s3[TQ:WINDOW - TQ], s3[WINDOW - TQ:] + tab_ref[TQ:3 * TQ]], axis=0)
    m3 = jnp.max(s3, axis=0, keepdims=True)
    p3 = jnp.exp2(s3 - m3)
    inv3 = 1.0 / jnp.sum(p3, axis=0, keepdims=True)
    o3 = _dot(vwT_ref[:, pl.ds(w0, N_WIN)], p3.astype(bf16))

    sums = []
    for g in range(N_KV):
        ps = p1[:, g * GROUP * TQ:g * GROUP * TQ + TQ]
        for r in range(1, GROUP):
            c0 = (g * GROUP + r) * TQ
            ps = ps + p1[:, c0:c0 + TQ]
        sums.append(ps)
    ps = jnp.concatenate(sums, axis=1)
    hi = ps.astype(bf16)
    lo = (ps - hi.astype(f32)).astype(bf16)
    imp_t = _dot(ovT_ref[...], hi) + _dot(ovT_ref[...], lo)
    pen_t = _select_penalty(imp_t, t0)
    pens = []
    for g in range(N_KV):
        pens += [pen_t[:, g * TQ:(g + 1) * TQ].T.astype(bf16)] * GROUP
    q_slc = jnp.concatenate([lhs8, jnp.concatenate(pens, axis=0)], axis=1)

    m_ref[...] = jnp.full((1, R), NEG, f32)
    l_ref[...] = jnp.zeros((1, R), f32)
    acc_ref[...] = jnp.zeros((LANES, R), f32)
    n_far = (qi + KT // TQ) // (KT // TQ) - 1
    first = (qi + 1) * TQ - n_far * KT

    def tile_row(i):
        return pl.multiple_of(first + i * KT, LANES)

    def produce(i, s_ref, mx_ref):
        r0 = tile_row(i)
        k_ext = jnp.concatenate([ks_ref[pl.ds(r0, KT), :], e_ref[pl.ds(r0, KT), :]], axis=1)
        s = _dot_nt(k_ext, q_slc)
        s_ref[...] = s
        mx_ref[...] = jnp.max(s, axis=0, keepdims=True)

    def consume(i, s_ref, mx_ref, last):
        m_prev = m_ref[...]
        if last:
            s = jnp.concatenate([s_ref[0:KT - 2 * TQ], s_ref[KT - 2 * TQ:KT] + tab_ref[TQ:3 * TQ]], axis=0)
            m_new = jnp.maximum(m_prev, jnp.max(s, axis=0, keepdims=True))
        else:
            s = s_ref[...]
            m_new = jnp.maximum(m_prev, mx_ref[...])
        alpha = jnp.exp2(m_prev - m_new)
        p = jnp.exp2(s - m_new)
        l_ref[...] = alpha * l_ref[...] + jnp.sum(p, axis=0, keepdims=True)
        acc_ref[...] = acc_ref[...] * alpha + _dot(vsT_ref[:, pl.ds(tile_row(i), KT)], p.astype(bf16))
        m_ref[...] = m_new

    produce(0, sa_ref, mxa_ref)

    def pair(j, c):
        produce(2 * j + 1, sb_ref, mxb_ref)
        consume(2 * j, sa_ref, mxa_ref, False)
        produce(2 * j + 2, sa_ref, mxa_ref)
        consume(2 * j + 1, sb_ref, mxb_ref, False)
        return c

    lax.fori_loop(0, n_far // 2, pair, 0)

    @pl.when(n_far % 2 == 0)
    def _():
        consume(n_far, sa_ref, mxa_ref, True)

    @pl.when(n_far % 2 == 1)
    def _():
        produce(n_far, sb_ref, mxb_ref)
        consume(n_far - 1, sa_ref, mxa_ref, False)
        consume(n_far, sb_ref, mxb_ref, True)

    inv2 = 1.0 / l_ref[...]

    sig_t = _sigmoid(g_ref[...]).T
    sub = lax.broadcasted_iota(jnp.int32, (LANES, TQ), 0)
    outs = []
    for h in range(N_HEADS):
        c = slice(h * TQ, (h + 1) * TQ)
        g1 = sig_t[3 * h:3 * h + 1, :]
        g2 = sig_t[3 * h + 1:3 * h + 2, :] * inv2[:, c]
        g3 = sig_t[3 * h + 2:3 * h + 3, :] * inv3[:, c]
        outs.append(g1 * o1[:, c] + g2 * acc_ref[:, c] + g3 * o3[:, c])
    for p in range(GROUP):
        y = jnp.where(sub < HEAD_DIM, outs[p], outs[p + GROUP]).T
        zb = z_ref[:, p * LANES:(p + 1) * LANES].astype(f32)
        out_ref[:, p * LANES:(p + 1) * LANES] = (y * zb * _sigmoid(zb)).astype(bf16)


def _nsa(q, z, g, cmp, cmpT, ks, kw, vsT, vwT, e_pad, ovT, tabs, acmp, B, T):
    nq = T // TQ
    nc = T // CMP_STRIDE
    tp = T + KPAD
    tile = lambda b, i: (b * nq + i, 0)
    const2 = lambda b, i: (0, 0)
    batch3 = lambda b, i: (b, 0, 0)
    return pl.pallas_call(
        _nsa_kernel,
        grid=(B, nq),
        in_specs=[pl.BlockSpec((TQ, D_NSA), tile),
                  pl.BlockSpec((TQ, D_NSA), tile),
                  pl.BlockSpec((TQ, LANES), tile),
                  pl.BlockSpec((None, nc, 2 * D_KV), batch3),
                  pl.BlockSpec((None, 2 * D_KV, nc), batch3),
                  pl.BlockSpec((None, tp, D_KV), batch3),
                  pl.BlockSpec((None, tp, 2 * D_KV), batch3),
                  pl.BlockSpec((None, D_KV, tp), batch3),
                  pl.BlockSpec((None, D_KV, tp), batch3),
                  pl.BlockSpec((tp, NSB), const2),
                  pl.BlockSpec((NSB, nc), const2),
                  pl.BlockSpec((3 * TQ, R), const2),
                  pl.BlockSpec((R, LANES), const2)],
        out_specs=pl.BlockSpec((TQ, D_NSA), tile),
        out_shape=jax.ShapeDtypeStruct((B * T, D_NSA), bf16),
        scratch_shapes=[pltpu.VMEM((1, R), f32), pltpu.VMEM((1, R), f32), pltpu.VMEM((LANES, R), f32),
                        pltpu.VMEM((KT, R), f32), pltpu.VMEM((KT, R), f32),
                        pltpu.VMEM((1, R), f32), pltpu.VMEM((1, R), f32)],
        compiler_params=pltpu.CompilerParams(dimension_semantics=("arbitrary", "arbitrary"),
                                             vmem_limit_bytes=VMEM_LIMIT),
        name="nsa",
    )(q, z, g, cmp, cmpT, ks, kw, vsT, vwT, e_pad, ovT, tabs, acmp)


def _outproj_kernel(alpha, ymix_ref, ynsa_ref, x_ref, w_ref, g_ref, b_ref, out_ref):
    y = jnp.concatenate([ymix_ref[...], ynsa_ref[...]], axis=1)
    r = alpha * x_ref[...] + _dot(y, w_ref[...])
    mu = jnp.mean(r, axis=-1, keepdims=True)
    d = r - mu
    var = jnp.mean(d * d, axis=-1, keepdims=True)
    out_ref[...] = d * lax.rsqrt(var + LN_EPS) * g_ref[...] + b_ref[...]


def _outproj(ymix, ynsa, x2, w_out, ln_g, ln_b, alpha, layer):
    bt = x2.shape[0]
    row = lambda i: (i, 0)
    wsel = lambda i: (layer, 0, 0)
    return pl.pallas_call(
        functools.partial(_outproj_kernel, alpha),
        grid=(bt // TT,),
        in_specs=[pl.BlockSpec((TT, D_POOL + D_CONV), row),
                  pl.BlockSpec((TT, D_NSA), row),
                  pl.BlockSpec((TT, D_MODEL), row),
                  pl.BlockSpec((None, D_MODEL, D_MODEL), wsel),
                  pl.BlockSpec((None, 1, D_MODEL), wsel),
                  pl.BlockSpec((None, 1, D_MODEL), wsel)],
        out_specs=pl.BlockSpec((TT, D_MODEL), row),
        out_shape=jax.ShapeDtypeStruct((bt, D_MODEL), f32),
        compiler_params=pltpu.CompilerParams(dimension_semantics=("arbitrary",),
                                             vmem_limit_bytes=VMEM_LIMIT),
        name="outproj",
    )(ymix, ynsa, x2, w_out, ln_g, ln_b)


def _bucket_np(d):
    d = np.asarray(d)
    max_exact = N_BUCKETS // 2
    nf = np.maximum(d, 1).astype(np.float32)
    large = max_exact + (np.log(nf / np.float32(max_exact)) / np.float32(math.log(MAX_DISTANCE / max_exact))
                         * np.float32(N_BUCKETS - max_exact)).astype(np.int32)
    large = np.minimum(large, N_BUCKETS - 1)
    return np.where(d < max_exact, d, large)


_FAR_DIST = 113
assert _bucket_np(np.arange(_FAR_DIST, 4 * WINDOW)).min() == N_BUCKETS - 1

_PAIR_PERM = np.concatenate([np.concatenate([np.arange(HEAD_DIM) + HEAD_DIM * p,
                                             np.arange(HEAD_DIM) + HEAD_DIM * (p + GROUP)]) for p in range(GROUP)])


def _bias_tables(rel_bias):
    d = np.arange(2 * TQ)
    tabp = ((rel_bias[_bucket_np(d), :] - rel_bias[N_BUCKETS - 1:N_BUCKETS, :]) * LOG2E).T
    sl = np.arange(TQ)[:, None]
    tl = np.arange(TQ)[None, :]

    def toeplitz(dist, ok):
        vals = tabp[:, np.clip(dist, 0, 2 * TQ - 1)]
        vals = jnp.where(jnp.asarray(ok)[None], vals, NEG)
        return vals.transpose(1, 0, 2).reshape(TQ, R)

    diag = toeplitz(tl - sl, tl - sl >= 0)
    prev = toeplitz(tl - sl + TQ, np.ones((TQ, TQ), bool))
    edge = jnp.asarray(np.tile(np.where(sl > tl, 0.0, NEG).astype(np.float32), (1, N_HEADS)))
    tabs = jnp.concatenate([edge, prev, diag], axis=0).astype(f32)

    j = np.arange(16)[None, :]
    dc = np.arange(TQ)[:, None] - CMP_STRIDE * (j - 8) - (CMP_BLOCK - 1)
    okc = jnp.asarray(dc >= 0)[None]
    vc = jnp.where(okc, tabp[:, np.clip(dc, 0, 2 * TQ - 1)], NEG)
    hi = vc.astype(bf16)
    lo = jnp.where(okc, vc - hi.astype(f32), 0.0).astype(bf16)
    fut = jnp.full((N_HEADS, TQ, 1), NEG, bf16)
    pad = jnp.zeros((N_HEADS, TQ, LANES - _PAD_FLAG_COL - 1), bf16)
    acmp = jnp.concatenate([hi, lo, fut, pad], axis=-1).reshape(R, LANES)
    return tabs, acmp


def _static_tables(T):
    nc = T // CMP_STRIDE
    ns = T // SLC_BLOCK
    cstart = np.arange(nc)[None, :] * CMP_STRIDE
    sstart = np.arange(NSB)[:, None] * SLC_BLOCK
    ov = np.clip(np.minimum(cstart + CMP_BLOCK, sstart + SLC_BLOCK) - np.maximum(cstart, sstart), 0, None) / CMP_STRIDE
    ov[ns:, :] = 0
    ov[:, nc - 1] = 0
    e_pad = np.concatenate([np.ones((KPAD, NSB), bool),
                            np.arange(T)[:, None] // SLC_BLOCK == np.arange(NSB)[None, :]], axis=0)
    return jnp.asarray(ov, bf16), jnp.asarray(e_pad, bf16)


def _pair_order(w, axis):
    shp = w.shape
    w = w.reshape(shp[:axis] + (N_KV, GROUP, HEAD_DIM) + shp[axis + 1:])
    return jnp.swapaxes(w, axis, axis + 1).reshape(shp)


def _prep_weights(w_in, w_out, pool_w, pe_k, w1_k, w2_k, pe_v, w1_v, w2_v):
    depth = w_in.shape[0]
    sizes = (D_POOL, D_POOL, D_CONV, D_CONV, D_CONV, D_CONV, D_NSA, D_KV, D_KV, D_KV, D_KV, D_KV, D_KV,
             3 * N_HEADS, D_NSA)
    offs = np.cumsum((0,) + sizes)
    w_in = w_in.astype(bf16)
    col = lambda i: w_in[:, :, offs[i]:offs[i + 1]]
    wg = jnp.pad(col(13), ((0, 0), (0, 0), (0, LANES - 3 * N_HEADS)))
    w_all = jnp.concatenate([w_in[:, :, 0:MIX_W], _pair_order(col(6), 2), _pair_order(col(14), 2),
                             col(7), col(8), col(9), col(11), wg], axis=2)
    wvt = jnp.swapaxes(jnp.concatenate([col(10), col(12)], axis=2), 1, 2)
    w_out = w_out.astype(bf16)
    nm = D_POOL + D_CONV
    w_out_p = jnp.concatenate([w_out[:, 0:nm], _pair_order(w_out[:, nm:], 1)], axis=1)

    eye_g = jnp.eye(len(POOL_WINDOWS), dtype=bf16)
    pw_bd = jnp.einsum('zgcd,gh->zgchd', pool_w.astype(bf16), eye_g).reshape(depth, D_POOL, D_POOL)

    half = CMP_BLOCK // 2
    eye2 = jnp.eye(2, dtype=bf16)
    w1s = jnp.stack([w1_k, w1_v], axis=1).astype(bf16).reshape(depth, 2, 2, half, HEAD_DIM, CMP_HIDDEN)
    w1_big = jnp.einsum('zkaldh,kK,gG->zlkgdaKGh', w1s, eye2, eye2).reshape(depth, half * 2 * D_KV, 2 * CH)
    pes = jnp.stack([pe_k, pe_v], axis=1).astype(bf16).reshape(depth, 2, 2, half, HEAD_DIM)
    pe_r = jnp.broadcast_to(pes.transpose(0, 2, 3, 1, 4)[:, :, :, :, None, :],
                            (depth, 2, half, 2, N_KV, HEAD_DIM)).reshape(depth, 2, half * 2 * D_KV)
    pe_r = jnp.pad(pe_r, ((0, 0), (0, 6), (0, 0)))
    w2s = jnp.stack([w2_k, w2_v], axis=1).astype(bf16)
    w2_big = jnp.einsum('zkhd,kK,gG->zkghKGd', w2s, eye2, eye2).reshape(depth, CH, 2 * D_KV)
    return w_all, wvt, w_out_p, pw_bd, w1_big, pe_r, w2_big, jnp.swapaxes(w2_big, 1, 2)


def _padded_kv_init(B, T):
    tp = T + KPAD
    flag = np.zeros((1, tp, 2 * D_KV), np.float32)
    flag[:, :KPAD, D_KV + _PAD_FLAG_COL] = 1.0
    return (jnp.zeros((B, tp, D_KV), bf16), jnp.broadcast_to(jnp.asarray(flag, bf16), (B, tp, 2 * D_KV)),
            jnp.zeros((B, D_KV, tp), bf16), jnp.zeros((B, D_KV, tp), bf16))


def kernel(x, w_in, w_out, pool_w, pool_scale, conv_w, cmp_pe_k, cmp_w1_k, cmp_w2_k, cmp_pe_v, cmp_w1_v, cmp_w2_v,
           rel_bias, ln_g, ln_b):
    B, T, D = x.shape
    depth = w_in.shape[0]
    assert D == D_MODEL and T % TT == 0 and T // SLC_BLOCK <= NSB and T // SLC_BLOCK >= N_SELECT
    assert KPAD % TT == 0 and N_SELECT > N_LOCAL
    alpha = (2 * depth) ** 0.25
    tabs, acmp = _bias_tables(rel_bias)
    ovT, e_pad = _static_tables(T)
    w_all, wvt, w_out_p, pw_bd, w1_big, pe_r, w2_big, w2t_big = _prep_weights(
        w_in, w_out, pool_w, cmp_pe_k, cmp_w1_k, cmp_w2_k, cmp_pe_v, cmp_w1_v, cmp_w2_v)
    pool_scale = pool_scale.reshape(depth, 1, D_POOL)
    ln_g = ln_g.reshape(depth, 1, D_MODEL)
    ln_b = ln_b.reshape(depth, 1, D_MODEL)
    h = x.reshape(B * T, D)
    for l in range(depth):
        ymix, q, z, cmp_l, g, ks, kw, vsT, vwT = _inproj(h, w_all, wvt, pw_bd, pool_scale, conv_w,
                                                         _padded_kv_init(B, T), l, B, T)
        cmp, cmpT = _compress(cmp_l, pe_r, w1_big, w2_big, w2t_big, l, B)
        ynsa = _nsa(q, z, g, cmp, cmpT, ks, kw, vsT, vwT, e_pad, ovT, tabs, acmp, B, T)
        h = _outproj(ymix, ynsa, h, w_out_p, ln_g, ln_b, alpha, l)
    return h.reshape(B, T, D)
```

```python
import functools
import math

import numpy as np
import jax
import jax.numpy as jnp
from jax import lax
from jax.experimental import pallas as pl
from jax.experimental.pallas import tpu as pltpu

f32 = jnp.float32
bf16 = jnp.bfloat16

D_MODEL = 1024
D_POOL = 256
D_CONV = 256
D_NSA = 512
HEAD_DIM = 64
N_HEADS = 8
N_KV = 2
GROUP = 4
D_KV = 128
POOL_GROUP = 64
POOL_WINDOWS = (2, 4, 8, 16)
CONV_WIDTH = 3
CMP_BLOCK = 32
CMP_STRIDE = 16
CMP_HIDDEN = 128
SLC_BLOCK = 64
N_SELECT = 16
N_LOCAL = 2
WINDOW = 512
N_BUCKETS = 32
MAX_DISTANCE = 128
LN_EPS = 1e-5
FORCED = 1e9
NEG = -1e30

LANES = 128
TQ = 128
R = N_HEADS * TQ
NSB = 128
TT = 512
HALO = 16
KT = 512
KPAD = KT
N_WIN = WINDOW + TQ
LOG2E = math.log2(math.e)
_PAD_FLAG_COL = 32
MIX_W = 2 * D_POOL + 4 * D_CONV
VMEM_LIMIT = 56 * 1024 * 1024

_NT = (((1,), (1,)), ((), ()))


def _dot(a, b):
    return jnp.dot(a, b, preferred_element_type=f32)


def _dot_nt(a, b):
    return lax.dot_general(a, b, _NT, preferred_element_type=f32)


def _sigmoid(x):
    return 1.0 / (1.0 + jnp.exp(-x))


_C_MIX = (0, MIX_W)
_C_Q = (_C_MIX[1], _C_MIX[1] + D_NSA)
_C_Z = (_C_Q[1], _C_Q[1] + D_NSA)
_C_CMP = (_C_Z[1], _C_Z[1] + 2 * D_KV)
_C_K = (_C_CMP[1], _C_CMP[1] + 2 * D_KV)
_C_G = (_C_K[1], _C_K[1] + LANES)
W_ALL = _C_G[1]
CMP_ROWS = TT // CMP_STRIDE


def _local_mixers(ext, i, pw_ref, ps_ref, cw_ref):
    e = ext[:, 0:D_POOL]
    s2 = e + pltpu.roll(e, 1, axis=0)
    s4 = s2 + pltpu.roll(s2, 2, axis=0)
    s8 = s4 + pltpu.roll(s4, 4, axis=0)
    s16 = s8 + pltpu.roll(s8, 8, axis=0)
    lane = lax.broadcasted_iota(jnp.int32, (TT, D_POOL), 1)
    grp = lane // POOL_GROUP
    wsum = jnp.where(grp == 0, s2[HALO:], jnp.where(grp == 1, s4[HALO:], jnp.where(grp == 2, s8[HALO:], s16[HALO:])))
    win = jnp.left_shift(2, grp)
    pos = i * TT + lax.broadcasted_iota(jnp.int32, (TT, D_POOL), 0)
    cnt = jnp.minimum(pos + 1, win).astype(f32)
    v = e[HALO:]
    pooled = wsum / cnt - v
    y_pool = _dot(pooled.astype(bf16), pw_ref[...]) * ps_ref[...]
    zp = ext[HALO:, D_POOL:2 * D_POOL]
    y_pool = y_pool * (zp * _sigmoid(zp))

    o = 2 * D_POOL
    cb = ext[HALO:, o:o + D_CONV]
    u = ext[:, o + D_CONV:o + 2 * D_CONV] * ext[:, o + 2 * D_CONV:o + 3 * D_CONV]
    zc = ext[HALO:, o + 3 * D_CONV:o + 4 * D_CONV]
    conv = cw_ref[CONV_WIDTH - 1:CONV_WIDTH, :] * u[HALO:]
    for k in range(CONV_WIDTH - 1):
        conv = conv + cw_ref[k:k + 1, :] * pltpu.roll(u, CONV_WIDTH - 1 - k, axis=0)[HALO:]
    y_conv = cb * conv * (zc * _sigmoid(zc))
    return y_pool, y_conv


def _inproj_kernel(x_ref, w_ref, wvt_ref, pw_ref, ps_ref, cw_ref, ks_in, kw_in, vs_in, vw_in,
                   ymix_ref, q_ref, z_ref, cmp_ref, g_ref, ks_ref, kw_ref, vsT_ref, vwT_ref,
                   halo_ref, cscr_ref):
    del ks_in, kw_in, vs_in, vw_in
    i = pl.program_id(1)
    x = x_ref[...].astype(bf16)

    def proj(c):
        return _dot(x, w_ref[:, c[0]:c[1]])

    mix = proj(_C_MIX)
    halo = jnp.where(i > 0, halo_ref[...], 0.0)
    y_pool, y_conv = _local_mixers(jnp.concatenate([halo, mix], axis=0), i, pw_ref, ps_ref, cw_ref)
    halo_ref[...] = mix[TT - HALO:]
    ymix_ref[:, 0:D_POOL] = y_pool.astype(bf16)
    ymix_ref[:, D_POOL:D_POOL + D_CONV] = y_conv.astype(bf16)

    q_ref[...] = (proj(_C_Q) * (HEAD_DIM ** -0.5 * LOG2E)).astype(bf16)
    z_ref[...] = proj(_C_Z).astype(bf16)
    g_ref[...] = proj(_C_G)
    kk = proj(_C_K).astype(bf16)
    ks_ref[...] = kk[:, 0:D_KV]
    kw_ref[:, 0:D_KV] = kk[:, D_KV:2 * D_KV]
    kw_ref[:, D_KV:2 * D_KV] = jnp.zeros((TT, D_KV), bf16)
    vt = _dot_nt(wvt_ref[...], x)
    vsT_ref[...] = vt[0:D_KV].astype(bf16)
    vwT_ref[...] = vt[D_KV:2 * D_KV].astype(bf16)

    kvc = proj(_C_CMP)
    for c in range(2):
        cscr_ref[c] = kvc[:, c * LANES:(c + 1) * LANES]
        for l in range(CMP_STRIDE):
            cmp_ref[l, :, c * LANES:(c + 1) * LANES] = (
                cscr_ref[c, pl.ds(l, CMP_ROWS, stride=CMP_STRIDE), :].astype(bf16))


def _inproj(x2, w_all, wvt, pw_bd, pool_scale, conv_w, kpads, layer, B, T):
    nt = T // TT
    bt = B * T
    row = lambda b, i: (b * nt + i, 0)
    wsel = lambda b, i: (layer, 0, 0)
    any_spec = pl.BlockSpec(memory_space=pl.ANY)
    tp = T + KPAD
    out_shape = [jax.ShapeDtypeStruct((bt, D_POOL + D_CONV), bf16),
                 jax.ShapeDtypeStruct((bt, D_NSA), bf16),
                 jax.ShapeDtypeStruct((bt, D_NSA), bf16),
                 jax.ShapeDtypeStruct((CMP_STRIDE, bt // CMP_STRIDE, 2 * D_KV), bf16),
                 jax.ShapeDtypeStruct((bt, LANES), f32),
                 jax.ShapeDtypeStruct((B, tp, D_KV), bf16),
                 jax.ShapeDtypeStruct((B, tp, 2 * D_KV), bf16),
                 jax.ShapeDtypeStruct((B, D_KV, tp), bf16),
                 jax.ShapeDtypeStruct((B, D_KV, tp), bf16)]
    out_specs = [pl.BlockSpec((TT, D_POOL + D_CONV), row),
                 pl.BlockSpec((TT, D_NSA), row),
                 pl.BlockSpec((TT, D_NSA), row),
                 pl.BlockSpec((CMP_STRIDE, CMP_ROWS, 2 * D_KV), lambda b, i: (0, b * nt + i, 0)),
                 pl.BlockSpec((TT, LANES), row),
                 pl.BlockSpec((None, TT, D_KV), lambda b, i: (b, i + KPAD // TT, 0)),
                 pl.BlockSpec((None, TT, 2 * D_KV), lambda b, i: (b, i + KPAD // TT, 0)),
                 pl.BlockSpec((None, D_KV, TT), lambda b, i: (b, 0, i + KPAD // TT)),
                 pl.BlockSpec((None, D_KV, TT), lambda b, i: (b, 0, i + KPAD // TT))]
    return pl.pallas_call(
        _inproj_kernel,
        grid=(B, nt),
        in_specs=[pl.BlockSpec((TT, D_MODEL), row),
                  pl.BlockSpec((None, D_MODEL, W_ALL), wsel),
                  pl.BlockSpec((None, 2 * D_KV, D_MODEL), wsel),
                  pl.BlockSpec((None, D_POOL, D_POOL), wsel),
                  pl.BlockSpec((None, 1, D_POOL), wsel),
                  pl.BlockSpec((None, CONV_WIDTH, D_CONV), wsel),
                  any_spec, any_spec, any_spec, any_spec],
        out_specs=out_specs,
        out_shape=out_shape,
        input_output_aliases={6: 5, 7: 6, 8: 7, 9: 8},
        scratch_shapes=[pltpu.VMEM((HALO, MIX_W), f32), pltpu.VMEM((2, TT, LANES), f32)],
        compiler_params=pltpu.CompilerParams(dimension_semantics=("arbitrary", "arbitrary"),
                                             vmem_limit_bytes=VMEM_LIMIT),
        name="inproj",
    )(x2, w_all, wvt, pw_bd, pool_scale, conv_w, *kpads)


CH = 4 * CMP_HIDDEN


def _compress_kernel(r_ref, pe_ref, w1_ref, w2_ref, w2t_ref, out_ref, outT_ref):
    nc = r_ref.shape[1]
    rows = jnp.concatenate([r_ref[l] for l in range(CMP_STRIDE)], axis=1)
    zz = _dot(rows, w1_ref[...])
    pb = _dot(pe_ref[...], w1_ref[...])
    z0 = zz[:, 0:CH] + pb[0:1, 0:CH]
    z1 = zz[:, CH:2 * CH] + pb[1:2, CH:2 * CH]
    h = z0 + pltpu.roll(z1, nc - 1, axis=0)
    h = (h * _sigmoid(h)).astype(bf16)
    out_ref[...] = _dot(h, w2_ref[...]).astype(bf16)
    outT_ref[...] = _dot_nt(w2t_ref[...], h).astype(bf16)


def _compress(cmp_l, pe_r, w1_big, w2_big, w2t_big, layer, B):
    nc = cmp_l.shape[1] // B
    rw = CMP_STRIDE * 2 * D_KV
    wsel = lambda b: (layer, 0, 0)
    return pl.pallas_call(
        _compress_kernel,
        grid=(B,),
        in_specs=[pl.BlockSpec((CMP_STRIDE, nc, 2 * D_KV), lambda b: (0, b, 0)),
                  pl.BlockSpec((None, 8, rw), wsel),
                  pl.BlockSpec((None, rw, 2 * CH), wsel),
                  pl.BlockSpec((None, CH, 2 * D_KV), wsel),
                  pl.BlockSpec((None, 2 * D_KV, CH), wsel)],
        out_specs=[pl.BlockSpec((None, nc, 2 * D_KV), lambda b: (b, 0, 0)),
                   pl.BlockSpec((None, 2 * D_KV, nc), lambda b: (b, 0, 0))],
        out_shape=[jax.ShapeDtypeStruct((B, nc, 2 * D_KV), bf16),
                   jax.ShapeDtypeStruct((B, 2 * D_KV, nc), bf16)],
        compiler_params=pltpu.CompilerParams(dimension_semantics=("arbitrary",),
                                             vmem_limit_bytes=VMEM_LIMIT),
        name="compress",
    )(cmp_l, pe_r, w1_big, w2_big, w2t_big)


def _select_penalty(imp_t, t0):
    shape = (NSB, N_KV * TQ)
    jblk = lax.broadcasted_iota(jnp.int32, shape, 0)
    tq = lax.broadcasted_iota(jnp.int32, shape, 1) % TQ
    back = jnp.right_shift(t0 + tq, int(math.log2(SLC_BLOCK))) - jblk
    causal = back >= 0
    forced = jnp.logical_or(jblk == 0, jnp.logical_and(causal, back < N_LOCAL))
    score = jnp.where(forced, -jnp.inf, jnp.where(causal, imp_t, NEG))
    jf = jblk.astype(f32)
    pen = jnp.where(forced, 0.0, NEG)
    for _ in range(N_SELECT - N_LOCAL - 1):
        best = jnp.max(score, axis=0, keepdims=True)
        first = jnp.min(jnp.where(score == best, jf, float(NSB)), axis=0, keepdims=True)
        pick = jf == first
        pen = jnp.where(pick, 0.0, pen)
        score = jnp.where(pick, -jnp.inf, score)
    return pen


def _query_rows(q_ref):
    lane = lax.broadcasted_iota(jnp.int32, (TQ, LANES), 1)
    low = lane < HEAD_DIM
    blocks = []
    for h in range(N_HEADS):
        blk = q_ref[:, LANES * (h % GROUP):LANES * (h % GROUP + 1)].astype(f32)
        blocks.append((jnp.where(low, blk, 0.0) if h < GROUP else jnp.where(low, 0.0, blk)).astype(bf16))
    return jnp.concatenate(blocks, axis=0)


def _merge_head_pairs(outs):
    sub = lax.broadcasted_iota(jnp.int32, (LANES, TQ), 0)
    return [jnp.where(sub < HEAD_DIM, outs[p], outs[p + GROUP]).T for p in range(GROUP)]


def _cmp_scores(q_cmp, cmp_ref, t0):
    nc = cmp_ref.shape[0]
    n_i = lax.broadcasted_iota(jnp.int32, (nc, LANES), 0)
    j_i = lax.broadcasted_iota(jnp.int32, (nc, LANES), 1)
    nstart = t0 // CMP_STRIDE - 8
    in_window = jnp.logical_and(j_i < 32, n_i == nstart + jnp.where(j_i < 16, j_i, j_i - 16))
    future = jnp.logical_and(j_i == 32, n_i >= nstart + 15)
    place_b = jnp.where(jnp.logical_or(in_window, future), 1.0, 0.0).astype(bf16)
    kc_ext = jnp.concatenate([cmp_ref[:, 0:D_KV], place_b], axis=1)
    return _dot_nt(kc_ext, q_cmp)


def _cmp_finish(s1, g_ref, cmpT_ref, ovT_ref, pen_ref, y1_ref, t0):
    m1 = jnp.max(s1, axis=0, keepdims=True)
    p1 = jnp.exp2(s1 - m1)
    l1 = jnp.sum(p1, axis=0, keepdims=True)
    tcol = t0 + lax.broadcasted_iota(jnp.int32, (1, R), 1) % TQ
    p1 = p1 * jnp.where(tcol >= CMP_BLOCK - 1, 1.0 / l1, 0.0)
    o1 = _dot(cmpT_ref[D_KV:2 * D_KV, :], p1.astype(bf16))

    sig_t = _sigmoid(g_ref[...]).T
    outs = [sig_t[3 * h:3 * h + 1, :] * o1[:, h * TQ:(h + 1) * TQ] for h in range(N_HEADS)]
    for p, y in enumerate(_merge_head_pairs(outs)):
        y1_ref[:, p * LANES:(p + 1) * LANES] = y.astype(bf16)

    sums = []
    for g in range(N_KV):
        ps = p1[:, g * GROUP * TQ:g * GROUP * TQ + TQ]
        for r in range(1, GROUP):
            c0 = (g * GROUP + r) * TQ
            ps = ps + p1[:, c0:c0 + TQ]
        sums.append(ps)
    ps = jnp.concatenate(sums, axis=1)
    hi = ps.astype(bf16)
    lo = (ps - hi.astype(f32)).astype(bf16)
    imp_t = _dot(ovT_ref[...], hi) + _dot(ovT_ref[...], lo)
    pen_t = _select_penalty(imp_t, t0)
    for g in range(N_KV):
        pen_ref[:, g * NSB:(g + 1) * NSB] = pen_t[:, g * TQ:(g + 1) * TQ].T.astype(bf16)


def _nsa_kernel(q_ref, qn_ref, z_ref, g_ref, gn_ref, cmp_ref, cmpT_ref, ovT_ref, ks_ref, kw_ref, vsT_ref, vwT_ref,
                e_ref, tab_ref, acmp_ref, out_ref,
                pen_ref, y1_ref, m_ref, l_ref, acc_ref, sa_ref, sb_ref, mxa_ref, mxb_ref):
    qi = pl.program_id(1)
    nq = pl.num_programs(1)
    t0 = qi * TQ

    @pl.when(qi == 0)
    def _():
        q_cmp0 = jnp.concatenate([_query_rows(q_ref), acmp_ref[...]], axis=1)
        _cmp_finish(_cmp_scores(q_cmp0, cmp_ref, t0), g_ref, cmpT_ref, ovT_ref, pen_ref, y1_ref, t0)

    lhs8 = _query_rows(q_ref)
    q_cmp = jnp.concatenate([lhs8, acmp_ref[...]], axis=1)
    pens = [pen_ref[:, g * NSB:(g + 1) * NSB] for g in range(N_KV) for _ in range(GROUP)]
    q_slc = jnp.concatenate([lhs8, jnp.concatenate(pens, axis=0)], axis=1)
    y1 = y1_ref[...]

    m_ref[...] = jnp.full((1, R), NEG, f32)
    l_ref[...] = jnp.zeros((1, R), f32)
    acc_ref[...] = jnp.zeros((LANES, R), f32)
    n_far = (qi + KT // TQ) // (KT // TQ) - 1
    first = (qi + 1) * TQ - n_far * KT

    def tile_row(i):
        return pl.multiple_of(first + i * KT, LANES)

    def produce(i, s_ref, mx_ref):
        r0 = tile_row(i)
        k_ext = jnp.concatenate([ks_ref[pl.ds(r0, KT), :], e_ref[pl.ds(r0, KT), :]], axis=1)
        s = _dot_nt(k_ext, q_slc)
        s_ref[...] = s
        mx_ref[...] = jnp.max(s, axis=0, keepdims=True)

    def consume(i, s_ref, mx_ref, last):
        m_prev = m_ref[...]
        if last:
            s = jnp.concatenate([s_ref[0:KT - 2 * TQ], s_ref[KT - 2 * TQ:KT] + tab_ref[TQ:3 * TQ]], axis=0)
            m_new = jnp.maximum(m_prev, jnp.max(s, axis=0, keepdims=True))
        else:
            s = s_ref[...]
            m_new = jnp.maximum(m_prev, mx_ref[...])
        alpha = jnp.exp2(m_prev - m_new)
        p = jnp.exp2(s - m_new)
        l_ref[...] = alpha * l_ref[...] + jnp.sum(p, axis=0, keepdims=True)
        acc_ref[...] = acc_ref[...] * alpha + _dot(vsT_ref[:, pl.ds(tile_row(i), KT)], p.astype(bf16))
        m_ref[...] = m_new

    tn = jnp.minimum(qi + 1, nq - 1) * TQ
    s1 = _cmp_scores(jnp.concatenate([_query_rows(qn_ref), acmp_ref[...]], axis=1), cmp_ref, tn)
    w0 = pl.multiple_of(t0, LANES)
    s3 = _dot_nt(kw_ref[pl.ds(w0, N_WIN), :], q_cmp)

    _cmp_finish(s1, gn_ref, cmpT_ref, ovT_ref, pen_ref, y1_ref, tn)
    produce(0, sa_ref, mxa_ref)
    s3 = jnp.concatenate([s3[0:TQ] + tab_ref[0:TQ], s3[TQ:WINDOW - TQ], s3[WINDOW - TQ:] + tab_ref[TQ:3 * TQ]], axis=0)
    m3 = jnp.max(s3, axis=0, keepdims=True)
    p3 = jnp.exp2(s3 - m3)
    inv3 = 1.0 / jnp.sum(p3, axis=0, keepdims=True)
    o3 = _dot(vwT_ref[:, pl.ds(w0, N_WIN)], p3.astype(bf16))

    def pair(j, c):
        produce(2 * j + 1, sb_ref, mxb_ref)
        consume(2 * j, sa_ref, mxa_ref, False)
        produce(2 * j + 2, sa_ref, mxa_ref)
        consume(2 * j + 1, sb_ref, mxb_ref, False)
        return c

    lax.fori_loop(0, n_far // 2, pair, 0)

    @pl.when(n_far % 2 == 0)
    def _():
        consume(n_far, sa_ref, mxa_ref, True)

    @pl.when(n_far % 2 == 1)
    def _():
        produce(n_far, sb_ref, mxb_ref)
        consume(n_far - 1, sa_ref, mxa_ref, False)
        consume(n_far, sb_ref, mxb_ref, True)

    inv2 = 1.0 / l_ref[...]

    sig_t = _sigmoid(g_ref[...]).T
    outs = []
    for h in range(N_HEADS):
        c = slice(h * TQ, (h + 1) * TQ)
        g2 = sig_t[3 * h + 1:3 * h + 2, :] * inv2[:, c]
        g3 = sig_t[3 * h + 2:3 * h + 3, :] * inv3[:, c]
        outs.append(g2 * acc_ref[:, c] + g3 * o3[:, c])
    for p, y in enumerate(_merge_head_pairs(outs)):
        c = slice(p * LANES, (p + 1) * LANES)
        zb = z_ref[:, c].astype(f32)
        out_ref[:, c] = ((y + y1[:, c].astype(f32)) * zb * _sigmoid(zb)).astype(bf16)


def _nsa(q, z, g, cmp, cmpT, ks, kw, vsT, vwT, e_pad, ovT, tabs, acmp, B, T):
    nq = T // TQ
    nc = T // CMP_STRIDE
    tp = T + KPAD
    tile = lambda b, i: (b * nq + i, 0)
    nxt = lambda b, i: (b * nq + jnp.minimum(i + 1, nq - 1), 0)
    const2 = lambda b, i: (0, 0)
    batch3 = lambda b, i: (b, 0, 0)
    return pl.pallas_call(
        _nsa_kernel,
        grid=(B, nq),
        in_specs=[pl.BlockSpec((TQ, D_NSA), tile),
                  pl.BlockSpec((TQ, D_NSA), nxt),
                  pl.BlockSpec((TQ, D_NSA), tile),
                  pl.BlockSpec((TQ, LANES), tile),
                  pl.BlockSpec((TQ, LANES), nxt),
                  pl.BlockSpec((None, nc, 2 * D_KV), batch3),
                  pl.BlockSpec((None, 2 * D_KV, nc), batch3),
                  pl.BlockSpec((NSB, nc), const2),
                  pl.BlockSpec((None, tp, D_KV), batch3),
                  pl.BlockSpec((None, tp, 2 * D_KV), batch3),
                  pl.BlockSpec((None, D_KV, tp), batch3),
                  pl.BlockSpec((None, D_KV, tp), batch3),
                  pl.BlockSpec((tp, NSB), const2),
                  pl.BlockSpec((3 * TQ, R), const2),
                  pl.BlockSpec((R, LANES), const2)],
        out_specs=pl.BlockSpec((TQ, D_NSA), tile),
        out_shape=jax.ShapeDtypeStruct((B * T, D_NSA), bf16),
        scratch_shapes=[pltpu.VMEM((TQ, N_KV * NSB), bf16), pltpu.VMEM((TQ, D_NSA), bf16),
                        pltpu.VMEM((1, R), f32), pltpu.VMEM((1, R), f32), pltpu.VMEM((LANES, R), f32),
                        pltpu.VMEM((KT, R), f32), pltpu.VMEM((KT, R), f32),
                        pltpu.VMEM((1, R), f32), pltpu.VMEM((1, R), f32)],
        compiler_params=pltpu.CompilerParams(dimension_semantics=("arbitrary", "arbitrary"),
                                             vmem_limit_bytes=VMEM_LIMIT),
        name="nsa",
    )(q, q, z, g, g, cmp, cmpT, ovT, ks, kw, vsT, vwT, e_pad, tabs, acmp)


def _outproj_kernel(alpha, ymix_ref, ynsa_ref, x_ref, w_ref, g_ref, b_ref, out_ref):
    y = jnp.concatenate([ymix_ref[...], ynsa_ref[...]], axis=1)
    r = alpha * x_ref[...] + _dot(y, w_ref[...])
    mu = jnp.mean(r, axis=-1, keepdims=True)
    d = r - mu
    var = jnp.mean(d * d, axis=-1, keepdims=True)
    out_ref[...] = d * lax.rsqrt(var + LN_EPS) * g_ref[...] + b_ref[...]


def _outproj(ymix, ynsa, x2, w_out, ln_g, ln_b, alpha, layer):
    bt = x2.shape[0]
    row = lambda i: (i, 0)
    wsel = lambda i: (layer, 0, 0)
    return pl.pallas_call(
        functools.partial(_outproj_kernel, alpha),
        grid=(bt // TT,),
        in_specs=[pl.BlockSpec((TT, D_POOL + D_CONV), row),
                  pl.BlockSpec((TT, D_NSA), row),
                  pl.BlockSpec((TT, D_MODEL), row),
                  pl.BlockSpec((None, D_MODEL, D_MODEL), wsel),
                  pl.BlockSpec((None, 1, D_MODEL), wsel),
                  pl.BlockSpec((None, 1, D_MODEL), wsel)],
        out_specs=pl.BlockSpec((TT, D_MODEL), row),
        out_shape=jax.ShapeDtypeStruct((bt, D_MODEL), f32),
        compiler_params=pltpu.CompilerParams(dimension_semantics=("arbitrary",),
                                             vmem_limit_bytes=VMEM_LIMIT),
        name="outproj",
    )(ymix, ynsa, x2, w_out, ln_g, ln_b)


def _bucket_np(d):
    d = np.asarray(d)
    max_exact = N_BUCKETS // 2
    nf = np.maximum(d, 1).astype(np.float32)
    large = max_exact + (np.log(nf / np.float32(max_exact)) / np.float32(math.log(MAX_DISTANCE / max_exact))
                         * np.float32(N_BUCKETS - max_exact)).astype(np.int32)
    large = np.minimum(large, N_BUCKETS - 1)
    return np.where(d < max_exact, d, large)


_FAR_DIST = 113
assert _bucket_np(np.arange(_FAR_DIST, 4 * WINDOW)).min() == N_BUCKETS - 1

_PAIR_PERM = np.concatenate([np.concatenate([np.arange(HEAD_DIM) + HEAD_DIM * p,
                                             np.arange(HEAD_DIM) + HEAD_DIM * (p + GROUP)]) for p in range(GROUP)])


def _bias_tables(rel_bias):
    nd = 2 * TQ
    onehot = jnp.asarray(np.eye(N_BUCKETS, dtype=np.float32)[_bucket_np(np.arange(nd))])
    tabp = ((jnp.dot(onehot, rel_bias, precision=lax.Precision.HIGHEST)
             - rel_bias[N_BUCKETS - 1:N_BUCKETS, :]) * LOG2E).T
    sl = np.arange(TQ)[:, None]
    tl = np.arange(TQ)[None, :]
    neg = lambda n: jnp.full((N_HEADS, n), NEG, f32)

    def skew(u, rows):
        period = u.shape[1]
        return jnp.tile(u, (1, rows))[:, :rows * (period - 1)].reshape(N_HEADS, rows, period - 1)

    def tile_layout(t):
        return t.transpose(1, 0, 2).reshape(TQ, R)

    diag = tile_layout(skew(jnp.concatenate([tabp[:, 0:TQ], neg(TQ)], axis=1), TQ)[:, :, :TQ])
    prev = tile_layout(skew(jnp.concatenate([tabp[:, TQ:nd], tabp[:, 0:TQ]], axis=1), TQ)[:, :, :TQ])
    edge = jnp.asarray(np.tile(np.where(sl > tl, 0.0, NEG).astype(np.float32), (1, N_HEADS)))
    tabs = jnp.concatenate([edge, prev, diag], axis=0).astype(f32)

    j = np.arange(16)[None, :]
    off = 8 * CMP_STRIDE - (CMP_BLOCK - 1)
    dc = np.arange(TQ)[:, None] - CMP_STRIDE * j + off
    okc = jnp.asarray(dc >= 0)[None]
    period = 4 * TQ
    u = jnp.concatenate([tabp[:, off:nd], jnp.zeros((N_HEADS, off), f32), neg(period - nd - off), tabp[:, 0:off]], axis=1)
    vc = skew(u, nd)[:, 0:16 * CMP_STRIDE:CMP_STRIDE, 0:TQ].transpose(0, 2, 1)
    vc = jnp.where(okc, vc, NEG)
    hi = vc.astype(bf16)
    lo = jnp.where(okc, vc - hi.astype(f32), 0.0).astype(bf16)
    fut = jnp.full((N_HEADS, TQ, 1), NEG, bf16)
    pad = jnp.zeros((N_HEADS, TQ, LANES - _PAD_FLAG_COL - 1), bf16)
    acmp = jnp.concatenate([hi, lo, fut, pad], axis=-1).reshape(R, LANES)
    return tabs, acmp


def _static_tables(T):
    nc = T // CMP_STRIDE
    ns = T // SLC_BLOCK
    cstart = np.arange(nc)[None, :] * CMP_STRIDE
    sstart = np.arange(NSB)[:, None] * SLC_BLOCK
    ov = np.clip(np.minimum(cstart + CMP_BLOCK, sstart + SLC_BLOCK) - np.maximum(cstart, sstart), 0, None) / CMP_STRIDE
    ov[ns:, :] = 0
    ov[:, nc - 1] = 0
    e_pad = np.concatenate([np.ones((KPAD, NSB), bool),
                            np.arange(T)[:, None] // SLC_BLOCK == np.arange(NSB)[None, :]], axis=0)
    return jnp.asarray(ov, bf16), jnp.asarray(e_pad, bf16)


def _pair_order(w, axis):
    shp = w.shape
    w = w.reshape(shp[:axis] + (N_KV, GROUP, HEAD_DIM) + shp[axis + 1:])
    return jnp.swapaxes(w, axis, axis + 1).reshape(shp)


def _prep_weights(w_in, w_out, pool_w, pe_k, w1_k, w2_k, pe_v, w1_v, w2_v):
    depth = w_in.shape[0]
    sizes = (D_POOL, D_POOL, D_CONV, D_CONV, D_CONV, D_CONV, D_NSA, D_KV, D_KV, D_KV, D_KV, D_KV, D_KV,
             3 * N_HEADS, D_NSA)
    offs = np.cumsum((0,) + sizes)
    w_in = w_in.astype(bf16)
    col = lambda i: w_in[:, :, offs[i]:offs[i + 1]]
    wg = jnp.pad(col(13), ((0, 0), (0, 0), (0, LANES - 3 * N_HEADS)))
    w_all = jnp.concatenate([w_in[:, :, 0:MIX_W], _pair_order(col(6), 2), _pair_order(col(14), 2),
                             col(7), col(8), col(9), col(11), wg], axis=2)
    wvt = jnp.swapaxes(jnp.concatenate([col(10), col(12)], axis=2), 1, 2)
    w_out = w_out.astype(bf16)
    nm = D_POOL + D_CONV
    w_out_p = jnp.concatenate([w_out[:, 0:nm], _pair_order(w_out[:, nm:], 1)], axis=1)

    eye_g = jnp.eye(len(POOL_WINDOWS), dtype=bf16)
    pw_bd = jnp.einsum('zgcd,gh->zgchd', pool_w.astype(bf16), eye_g).reshape(depth, D_POOL, D_POOL)

    half = CMP_BLOCK // 2
    eye2 = jnp.eye(2, dtype=bf16)
    w1s = jnp.stack([w1_k, w1_v], axis=1).astype(bf16).reshape(depth, 2, 2, half, HEAD_DIM, CMP_HIDDEN)
    w1_big = jnp.einsum('zkaldh,kK,gG->zlkgdaKGh', w1s, eye2, eye2).reshape(depth, half * 2 * D_KV, 2 * CH)
    pes = jnp.stack([pe_k, pe_v], axis=1).astype(bf16).reshape(depth, 2, 2, half, HEAD_DIM)
    pe_r = jnp.broadcast_to(pes.transpose(0, 2, 3, 1, 4)[:, :, :, :, None, :],
                            (depth, 2, half, 2, N_KV, HEAD_DIM)).reshape(depth, 2, half * 2 * D_KV)
    pe_r = jnp.pad(pe_r, ((0, 0), (0, 6), (0, 0)))
    w2s = jnp.stack([w2_k, w2_v], axis=1).astype(bf16)
    w2_big = jnp.einsum('zkhd,kK,gG->zkghKGd', w2s, eye2, eye2).reshape(depth, CH, 2 * D_KV)
    return w_all, wvt, w_out_p, pw_bd, w1_big, pe_r, w2_big, jnp.swapaxes(w2_big, 1, 2)


def _padded_kv_init(B, T):
    tp = T + KPAD
    flag = np.zeros((1, tp, 2 * D_KV), np.float32)
    flag[:, :KPAD, D_KV + _PAD_FLAG_COL] = 1.0
    return (jnp.zeros((B, tp, D_KV), bf16), jnp.broadcast_to(jnp.asarray(flag, bf16), (B, tp, 2 * D_KV)),
            jnp.zeros((B, D_KV, tp), bf16), jnp.zeros((B, D_KV, tp), bf16))


def kernel(x, w_in, w_out, pool_w, pool_scale, conv_w, cmp_pe_k, cmp_w1_k, cmp_w2_k, cmp_pe_v, cmp_w1_v, cmp_w2_v,
           rel_bias, ln_g, ln_b):
    B, T, D = x.shape
    depth = w_in.shape[0]
    assert D == D_MODEL and T % TT == 0 and T // SLC_BLOCK <= NSB and T // SLC_BLOCK >= N_SELECT
    assert KPAD % TT == 0 and N_SELECT > N_LOCAL
    alpha = (2 * depth) ** 0.25
    tabs, acmp = _bias_tables(rel_bias)
    ovT, e_pad = _static_tables(T)
    w_all, wvt, w_out_p, pw_bd, w1_big, pe_r, w2_big, w2t_big = _prep_weights(
        w_in, w_out, pool_w, cmp_pe_k, cmp_w1_k, cmp_w2_k, cmp_pe_v, cmp_w1_v, cmp_w2_v)
    pool_scale = pool_scale.reshape(depth, 1, D_POOL)
    ln_g = ln_g.reshape(depth, 1, D_MODEL)
    ln_b = ln_b.reshape(depth, 1, D_MODEL)
    h = x.reshape(B * T, D)
    for l in range(depth):
        ymix, q, z, cmp_l, g, ks, kw, vsT, vwT = _inproj(h, w_all, wvt, pw_bd, pool_scale, conv_w,
                                                         _padded_kv_init(B, T), l, B, T)
        cmp, cmpT = _compress(cmp_l, pe_r, w1_big, w2_big, w2t_big, l, B)
        ynsa = _nsa(q, z, g, cmp, cmpT, ks, kw, vsT, vwT, e_pad, ovT, tabs, acmp, B, T)
        h = _outproj(ymix, ynsa, h, w_out_p, ln_g, ln_b, alpha, l)
    return h.reshape(B, T, D)
```

```python
import functools
import math

import numpy as np
import jax
import jax.numpy as jnp
from jax import lax
from jax.experimental import pallas as pl
from jax.experimental.pallas import tpu as pltpu

f32 = jnp.float32
bf16 = jnp.bfloat16

D_MODEL = 1024
D_POOL = 256
D_CONV = 256
D_NSA = 512
HEAD_DIM = 64
N_HEADS = 8
N_KV = 2
GROUP = 4
D_KV = 128
POOL_GROUP = 64
POOL_WINDOWS = (2, 4, 8, 16)
CONV_WIDTH = 3
CMP_BLOCK = 32
CMP_STRIDE = 16
CMP_HIDDEN = 128
SLC_BLOCK = 64
N_SELECT = 16
N_LOCAL = 2
WINDOW = 512
N_BUCKETS = 32
MAX_DISTANCE = 128
LN_EPS = 1e-5
FORCED = 1e9
NEG = -1e30

LANES = 128
TQ = 128
R = N_HEADS * TQ
NSB = 128
TT = 512
HALO = 16
KT = 512
KPAD = KT
N_WIN = WINDOW + TQ
LOG2E = math.log2(math.e)
_PAD_FLAG_COL = 32
MIX_W = 2 * D_POOL + 4 * D_CONV
VMEM_LIMIT = 56 * 1024 * 1024

_NT = (((1,), (1,)), ((), ()))


def _dot(a, b):
    return jnp.dot(a, b, preferred_element_type=f32)


def _dot_nt(a, b):
    return lax.dot_general(a, b, _NT, preferred_element_type=f32)


def _sigmoid(x):
    return 1.0 / (1.0 + jnp.exp(-x))


_C_MIX = (0, MIX_W)
_C_Q = (_C_MIX[1], _C_MIX[1] + D_NSA)
_C_Z = (_C_Q[1], _C_Q[1] + D_NSA)
_C_CMP = (_C_Z[1], _C_Z[1] + 2 * D_KV)
_C_K = (_C_CMP[1], _C_CMP[1] + 2 * D_KV)
_C_G = (_C_K[1], _C_K[1] + LANES)
W_ALL = _C_G[1]
CMP_ROWS = TT // CMP_STRIDE


def _local_mixers(ext, i, pw_ref, ps_ref, cw_ref):
    e = ext[:, 0:D_POOL]
    s2 = e + pltpu.roll(e, 1, axis=0)
    s4 = s2 + pltpu.roll(s2, 2, axis=0)
    s8 = s4 + pltpu.roll(s4, 4, axis=0)
    s16 = s8 + pltpu.roll(s8, 8, axis=0)
    lane = lax.broadcasted_iota(jnp.int32, (TT, D_POOL), 1)
    grp = lane // POOL_GROUP
    wsum = jnp.where(grp == 0, s2[HALO:], jnp.where(grp == 1, s4[HALO:], jnp.where(grp == 2, s8[HALO:], s16[HALO:])))
    win = jnp.left_shift(2, grp)
    pos = i * TT + lax.broadcasted_iota(jnp.int32, (TT, D_POOL), 0)
    cnt = jnp.minimum(pos + 1, win).astype(f32)
    v = e[HALO:]
    pooled = wsum / cnt - v
    y_pool = _dot(pooled.astype(bf16), pw_ref[...]) * ps_ref[...]
    zp = ext[HALO:, D_POOL:2 * D_POOL]
    y_pool = y_pool * (zp * _sigmoid(zp))

    o = 2 * D_POOL
    cb = ext[HALO:, o:o + D_CONV]
    u = ext[:, o + D_CONV:o + 2 * D_CONV] * ext[:, o + 2 * D_CONV:o + 3 * D_CONV]
    zc = ext[HALO:, o + 3 * D_CONV:o + 4 * D_CONV]
    conv = cw_ref[CONV_WIDTH - 1:CONV_WIDTH, :] * u[HALO:]
    for k in range(CONV_WIDTH - 1):
        conv = conv + cw_ref[k:k + 1, :] * pltpu.roll(u, CONV_WIDTH - 1 - k, axis=0)[HALO:]
    y_conv = cb * conv * (zc * _sigmoid(zc))
    return y_pool, y_conv


def _inproj_kernel(x_ref, w_ref, wvt_ref, pw_ref, ps_ref, cw_ref, ks_in, kw_in, vs_in, vw_in,
                   ymix_ref, q_ref, z_ref, cmp_ref, g_ref, ks_ref, kw_ref, vsT_ref, vwT_ref,
                   halo_ref, cscr_ref):
    del ks_in, kw_in, vs_in, vw_in
    i = pl.program_id(1)
    x = x_ref[...].astype(bf16)

    def proj(c):
        return _dot(x, w_ref[:, c[0]:c[1]])

    mix = proj(_C_MIX)
    halo = jnp.where(i > 0, halo_ref[...], 0.0)
    y_pool, y_conv = _local_mixers(jnp.concatenate([halo, mix], axis=0), i, pw_ref, ps_ref, cw_ref)
    halo_ref[...] = mix[TT - HALO:]
    ymix_ref[:, 0:D_POOL] = y_pool.astype(bf16)
    ymix_ref[:, D_POOL:D_POOL + D_CONV] = y_conv.astype(bf16)

    q_ref[...] = (proj(_C_Q) * (HEAD_DIM ** -0.5 * LOG2E)).astype(bf16)
    z_ref[...] = proj(_C_Z).astype(bf16)
    g_ref[...] = proj(_C_G)
    kk = proj(_C_K).astype(bf16)
    ks_ref[...] = kk[:, 0:D_KV]
    kw_ref[:, 0:D_KV] = kk[:, D_KV:2 * D_KV]
    kw_ref[:, D_KV:2 * D_KV] = jnp.zeros((TT, D_KV), bf16)
    vt = _dot_nt(wvt_ref[...], x)
    vsT_ref[...] = vt[0:D_KV].astype(bf16)
    vwT_ref[...] = vt[D_KV:2 * D_KV].astype(bf16)

    kvc = proj(_C_CMP)
    for c in range(2):
        cscr_ref[c] = kvc[:, c * LANES:(c + 1) * LANES]
        for l in range(CMP_STRIDE):
            cmp_ref[l, :, c * LANES:(c + 1) * LANES] = (
                cscr_ref[c, pl.ds(l, CMP_ROWS, stride=CMP_STRIDE), :].astype(bf16))


def _inproj(x2, w_all, wvt, pw_bd, pool_scale, conv_w, kpads, layer, B, T):
    nt = T // TT
    bt = B * T
    row = lambda b, i: (b * nt + i, 0)
    wsel = lambda b, i: (layer, 0, 0)
    any_spec = pl.BlockSpec(memory_space=pl.ANY)
    tp = T + KPAD
    out_shape = [jax.ShapeDtypeStruct((bt, D_POOL + D_CONV), bf16),
                 jax.ShapeDtypeStruct((bt, D_NSA), bf16),
                 jax.ShapeDtypeStruct((bt, D_NSA), bf16),
                 jax.ShapeDtypeStruct((CMP_STRIDE, bt // CMP_STRIDE, 2 * D_KV), bf16),
                 jax.ShapeDtypeStruct((bt, LANES), f32),
                 jax.ShapeDtypeStruct((B, tp, D_KV), bf16),
                 jax.ShapeDtypeStruct((B, tp, 2 * D_KV), bf16),
                 jax.ShapeDtypeStruct((B, D_KV, tp), bf16),
                 jax.ShapeDtypeStruct((B, D_KV, tp), bf16)]
    out_specs = [pl.BlockSpec((TT, D_POOL + D_CONV), row),
                 pl.BlockSpec((TT, D_NSA), row),
                 pl.BlockSpec((TT, D_NSA), row),
                 pl.BlockSpec((CMP_STRIDE, CMP_ROWS, 2 * D_KV), lambda b, i: (0, b * nt + i, 0)),
                 pl.BlockSpec((TT, LANES), row),
                 pl.BlockSpec((None, TT, D_KV), lambda b, i: (b, i + KPAD // TT, 0)),
                 pl.BlockSpec((None, TT, 2 * D_KV), lambda b, i: (b, i + KPAD // TT, 0)),
                 pl.BlockSpec((None, D_KV, TT), lambda b, i: (b, 0, i + KPAD // TT)),
                 pl.BlockSpec((None, D_KV, TT), lambda b, i: (b, 0, i + KPAD // TT))]
    return pl.pallas_call(
        _inproj_kernel,
        grid=(B, nt),
        in_specs=[pl.BlockSpec((TT, D_MODEL), row),
                  pl.BlockSpec((None, D_MODEL, W_ALL), wsel),
                  pl.BlockSpec((None, 2 * D_KV, D_MODEL), wsel),
                  pl.BlockSpec((None, D_POOL, D_POOL), wsel),
                  pl.BlockSpec((None, 1, D_POOL), wsel),
                  pl.BlockSpec((None, CONV_WIDTH, D_CONV), wsel),
                  any_spec, any_spec, any_spec, any_spec],
        out_specs=out_specs,
        out_shape=out_shape,
        input_output_aliases={6: 5, 7: 6, 8: 7, 9: 8},
        scratch_shapes=[pltpu.VMEM((HALO, MIX_W), f32), pltpu.VMEM((2, TT, LANES), f32)],
        compiler_params=pltpu.CompilerParams(dimension_semantics=("arbitrary", "arbitrary"),
                                             vmem_limit_bytes=VMEM_LIMIT),
        name="inproj",
    )(x2, w_all, wvt, pw_bd, pool_scale, conv_w, *kpads)


CH = 4 * CMP_HIDDEN


def _compress_kernel(r_ref, pe_ref, w1_ref, w2_ref, w2t_ref, out_ref, outT_ref):
    nc = r_ref.shape[1]
    rows = jnp.concatenate([r_ref[l] for l in range(CMP_STRIDE)], axis=1)
    zz = _dot(rows, w1_ref[...])
    pb = _dot(pe_ref[...], w1_ref[...])
    z0 = zz[:, 0:CH] + pb[0:1, 0:CH]
    z1 = zz[:, CH:2 * CH] + pb[1:2, CH:2 * CH]
    h = z0 + pltpu.roll(z1, nc - 1, axis=0)
    h = (h * _sigmoid(h)).astype(bf16)
    out_ref[...] = _dot(h, w2_ref[...]).astype(bf16)
    outT_ref[...] = _dot_nt(w2t_ref[...], h).astype(bf16)


def _compress(cmp_l, pe_r, w1_big, w2_big, w2t_big, layer, B):
    nc = cmp_l.shape[1] // B
    rw = CMP_STRIDE * 2 * D_KV
    wsel = lambda b: (layer, 0, 0)
    return pl.pallas_call(
        _compress_kernel,
        grid=(B,),
        in_specs=[pl.BlockSpec((CMP_STRIDE, nc, 2 * D_KV), lambda b: (0, b, 0)),
                  pl.BlockSpec((None, 8, rw), wsel),
                  pl.BlockSpec((None, rw, 2 * CH), wsel),
                  pl.BlockSpec((None, CH, 2 * D_KV), wsel),
                  pl.BlockSpec((None, 2 * D_KV, CH), wsel)],
        out_specs=[pl.BlockSpec((None, nc, 2 * D_KV), lambda b: (b, 0, 0)),
                   pl.BlockSpec((None, 2 * D_KV, nc), lambda b: (b, 0, 0))],
        out_shape=[jax.ShapeDtypeStruct((B, nc, 2 * D_KV), bf16),
                   jax.ShapeDtypeStruct((B, 2 * D_KV, nc), bf16)],
        compiler_params=pltpu.CompilerParams(dimension_semantics=("arbitrary",),
                                             vmem_limit_bytes=VMEM_LIMIT),
        name="compress",
    )(cmp_l, pe_r, w1_big, w2_big, w2t_big)


def _select_penalty(imp_t, t0):
    shape = (NSB, N_KV * TQ)
    jblk = lax.broadcasted_iota(jnp.int32, shape, 0)
    tq = lax.broadcasted_iota(jnp.int32, shape, 1) % TQ
    back = jnp.right_shift(t0 + tq, int(math.log2(SLC_BLOCK))) - jblk
    causal = back >= 0
    forced = jnp.logical_or(jblk == 0, jnp.logical_and(causal, back < N_LOCAL))
    score = jnp.where(forced, -jnp.inf, jnp.where(causal, imp_t, NEG))
    jf = jblk.astype(f32)
    pen = jnp.where(forced, 0.0, NEG)
    for _ in range(N_SELECT - N_LOCAL - 1):
        best = jnp.max(score, axis=0, keepdims=True)
        first = jnp.min(jnp.where(score == best, jf, float(NSB)), axis=0, keepdims=True)
        pick = jf == first
        pen = jnp.where(pick, 0.0, pen)
        score = jnp.where(pick, -jnp.inf, score)
    return pen


def _query_cols(q_ref):
    sub = lax.broadcasted_iota(jnp.int32, (LANES, TQ), 0)
    low = sub < HEAD_DIM
    pairs = [q_ref[:, LANES * p:LANES * (p + 1)].astype(f32).T for p in range(GROUP)]
    blocks = [(jnp.where(low, pairs[h % GROUP], 0.0) if h < GROUP else jnp.where(low, 0.0, pairs[h % GROUP])).astype(bf16)
              for h in range(N_HEADS)]
    return jnp.concatenate(blocks, axis=1)


def _merge_head_pairs(outs):
    sub = lax.broadcasted_iota(jnp.int32, (LANES, TQ), 0)
    return [jnp.where(sub < HEAD_DIM, outs[p], outs[p + GROUP]).T for p in range(GROUP)]


def _cmp_scores(q_cmp, cmp_ref, t0):
    nc = cmp_ref.shape[0]
    n_i = lax.broadcasted_iota(jnp.int32, (nc, LANES), 0)
    j_i = lax.broadcasted_iota(jnp.int32, (nc, LANES), 1)
    nstart = t0 // CMP_STRIDE - 8
    in_window = jnp.logical_and(j_i < 32, n_i == nstart + jnp.where(j_i < 16, j_i, j_i - 16))
    future = jnp.logical_and(j_i == 32, n_i >= nstart + 15)
    place_b = jnp.where(jnp.logical_or(in_window, future), 1.0, 0.0).astype(bf16)
    kc_ext = jnp.concatenate([cmp_ref[:, 0:D_KV], place_b], axis=1)
    return _dot(kc_ext, q_cmp)


def _cmp_finish(s1, g_ref, cmpT_ref, ovT_ref, pen_ref, y1_ref, t0):
    m1 = jnp.max(s1, axis=0, keepdims=True)
    p1 = jnp.exp2(s1 - m1)
    l1 = jnp.sum(p1, axis=0, keepdims=True)
    tcol = t0 + lax.broadcasted_iota(jnp.int32, (1, R), 1) % TQ
    p1 = p1 * jnp.where(tcol >= CMP_BLOCK - 1, 1.0 / l1, 0.0)
    o1 = _dot(cmpT_ref[D_KV:2 * D_KV, :], p1.astype(bf16))

    sig_t = _sigmoid(g_ref[...]).T
    outs = [sig_t[3 * h:3 * h + 1, :] * o1[:, h * TQ:(h + 1) * TQ] for h in range(N_HEADS)]
    for p, y in enumerate(_merge_head_pairs(outs)):
        y1_ref[:, p * LANES:(p + 1) * LANES] = y.astype(bf16)

    sums = []
    for g in range(N_KV):
        ps = p1[:, g * GROUP * TQ:g * GROUP * TQ + TQ]
        for r in range(1, GROUP):
            c0 = (g * GROUP + r) * TQ
            ps = ps + p1[:, c0:c0 + TQ]
        sums.append(ps)
    ps = jnp.concatenate(sums, axis=1)
    hi = ps.astype(bf16)
    lo = (ps - hi.astype(f32)).astype(bf16)
    imp_t = _dot(ovT_ref[...], hi) + _dot(ovT_ref[...], lo)
    pen_t = _select_penalty(imp_t, t0)
    pen_ref[...] = pen_t.astype(bf16)


def _nsa_kernel(q_ref, qn_ref, z_ref, g_ref, gn_ref, cmp_ref, cmpT_ref, ovT_ref, ks_ref, kw_ref, vsT_ref, vwT_ref,
                e_ref, tab_ref, acmp_ref, out_ref,
                pen_ref, y1_ref, m_ref, l_ref, acc_ref, sa_ref, sb_ref, mxa_ref, mxb_ref):
    qi = pl.program_id(1)
    nq = pl.num_programs(1)
    t0 = qi * TQ

    @pl.when(qi == 0)
    def _():
        q_cmp0 = jnp.concatenate([_query_cols(q_ref), acmp_ref[...]], axis=0)
        _cmp_finish(_cmp_scores(q_cmp0, cmp_ref, t0), g_ref, cmpT_ref, ovT_ref, pen_ref, y1_ref, t0)

    qcols = _query_cols(q_ref)
    q_cmp = jnp.concatenate([qcols, acmp_ref[...]], axis=0)
    pens = [pen_ref[:, g * TQ:(g + 1) * TQ] for g in range(N_KV) for _ in range(GROUP)]
    q_slc = jnp.concatenate([qcols, jnp.concatenate(pens, axis=1)], axis=0)
    y1 = y1_ref[...]

    m_ref[...] = jnp.full((1, R), NEG, f32)
    l_ref[...] = jnp.zeros((1, R), f32)
    acc_ref[...] = jnp.zeros((LANES, R), f32)
    n_far = (qi + KT // TQ) // (KT // TQ) - 1
    first = (qi + 1) * TQ - n_far * KT

    def tile_row(i):
        return pl.multiple_of(first + i * KT, LANES)

    def produce(i, s_ref, mx_ref):
        r0 = tile_row(i)
        k_ext = jnp.concatenate([ks_ref[pl.ds(r0, KT), :], e_ref[pl.ds(r0, KT), :]], axis=1)
        s = _dot(k_ext, q_slc)
        s_ref[...] = s
        mx_ref[...] = jnp.max(s, axis=0, keepdims=True)

    def consume(i, s_ref, mx_ref, last):
        m_prev = m_ref[...]
        if last:
            s = jnp.concatenate([s_ref[0:KT - 2 * TQ], s_ref[KT - 2 * TQ:KT] + tab_ref[TQ:3 * TQ]], axis=0)
            m_new = jnp.maximum(m_prev, jnp.max(s, axis=0, keepdims=True))
        else:
            s = s_ref[...]
            m_new = jnp.maximum(m_prev, mx_ref[...])
        alpha = jnp.exp2(m_prev - m_new)
        p = jnp.exp2(s - m_new)
        l_ref[...] = alpha * l_ref[...] + jnp.sum(p, axis=0, keepdims=True)
        acc_ref[...] = acc_ref[...] * alpha + _dot(vsT_ref[:, pl.ds(tile_row(i), KT)], p.astype(bf16))
        m_ref[...] = m_new

    tn = jnp.minimum(qi + 1, nq - 1) * TQ
    s1 = _cmp_scores(jnp.concatenate([_query_cols(qn_ref), acmp_ref[...]], axis=0), cmp_ref, tn)
    w0 = pl.multiple_of(t0, LANES)
    s3 = _dot(kw_ref[pl.ds(w0, N_WIN), :], q_cmp)

    _cmp_finish(s1, gn_ref, cmpT_ref, ovT_ref, pen_ref, y1_ref, tn)
    produce(0, sa_ref, mxa_ref)
    s3 = jnp.concatenate([s3[0:TQ] + tab_ref[0:TQ], s3[TQ:WINDOW - TQ], s3[WINDOW - TQ:] + tab_ref[TQ:3 * TQ]], axis=0)
    m3 = jnp.max(s3, axis=0, keepdims=True)
    p3 = jnp.exp2(s3 - m3)
    inv3 = 1.0 / jnp.sum(p3, axis=0, keepdims=True)
    o3 = _dot(vwT_ref[:, pl.ds(w0, N_WIN)], p3.astype(bf16))

    def pair(j, c):
        produce(2 * j + 1, sb_ref, mxb_ref)
        consume(2 * j, sa_ref, mxa_ref, False)
        produce(2 * j + 2, sa_ref, mxa_ref)
        consume(2 * j + 1, sb_ref, mxb_ref, False)
        return c

    lax.fori_loop(0, n_far // 2, pair, 0)

    @pl.when(n_far % 2 == 0)
    def _():
        consume(n_far, sa_ref, mxa_ref, True)

    @pl.when(n_far % 2 == 1)
    def _():
        produce(n_far, sb_ref, mxb_ref)
        consume(n_far - 1, sa_ref, mxa_ref, False)
        consume(n_far, sb_ref, mxb_ref, True)

    inv2 = 1.0 / l_ref[...]

    sig_t = _sigmoid(g_ref[...]).T
    outs = []
    for h in range(N_HEADS):
        c = slice(h * TQ, (h + 1) * TQ)
        g2 = sig_t[3 * h + 1:3 * h + 2, :] * inv2[:, c]
        g3 = sig_t[3 * h + 2:3 * h + 3, :] * inv3[:, c]
        outs.append(g2 * acc_ref[:, c] + g3 * o3[:, c])
    for p, y in enumerate(_merge_head_pairs(outs)):
        c = slice(p * LANES, (p + 1) * LANES)
        zb = z_ref[:, c].astype(f32)
        out_ref[:, c] = ((y + y1[:, c].astype(f32)) * zb * _sigmoid(zb)).astype(bf16)


def _nsa(q, z, g, cmp, cmpT, ks, kw, vsT, vwT, e_pad, ovT, tabs, acmp, B, T):
    nq = T // TQ
    nc = T // CMP_STRIDE
    tp = T + KPAD
    tile = lambda b, i: (b * nq + i, 0)
    nxt = lambda b, i: (b * nq + jnp.minimum(i + 1, nq - 1), 0)
    const2 = lambda b, i: (0, 0)
    batch3 = lambda b, i: (b, 0, 0)
    return pl.pallas_call(
        _nsa_kernel,
        grid=(B, nq),
        in_specs=[pl.BlockSpec((TQ, D_NSA), tile),
                  pl.BlockSpec((TQ, D_NSA), nxt),
                  pl.BlockSpec((TQ, D_NSA), tile),
                  pl.BlockSpec((TQ, LANES), tile),
                  pl.BlockSpec((TQ, LANES), nxt),
                  pl.BlockSpec((None, nc, 2 * D_KV), batch3),
                  pl.BlockSpec((None, 2 * D_KV, nc), batch3),
                  pl.BlockSpec((NSB, nc), const2),
                  pl.BlockSpec((None, tp, D_KV), batch3),
                  pl.BlockSpec((None, tp, 2 * D_KV), batch3),
                  pl.BlockSpec((None, D_KV, tp), batch3),
                  pl.BlockSpec((None, D_KV, tp), batch3),
                  pl.BlockSpec((tp, NSB), const2),
                  pl.BlockSpec((3 * TQ, R), const2),
                  pl.BlockSpec((LANES, R), const2)],
        out_specs=pl.BlockSpec((TQ, D_NSA), tile),
        out_shape=jax.ShapeDtypeStruct((B * T, D_NSA), bf16),
        scratch_shapes=[pltpu.VMEM((NSB, N_KV * TQ), bf16), pltpu.VMEM((TQ, D_NSA), bf16),
                        pltpu.VMEM((1, R), f32), pltpu.VMEM((1, R), f32), pltpu.VMEM((LANES, R), f32),
                        pltpu.VMEM((KT, R), f32), pltpu.VMEM((KT, R), f32),
                        pltpu.VMEM((1, R), f32), pltpu.VMEM((1, R), f32)],
        compiler_params=pltpu.CompilerParams(dimension_semantics=("arbitrary", "arbitrary"),
                                             vmem_limit_bytes=VMEM_LIMIT),
        name="nsa",
    )(q, q, z, g, g, cmp, cmpT, ovT, ks, kw, vsT, vwT, e_pad, tabs, acmp)


def _outproj_kernel(alpha, ymix_ref, ynsa_ref, x_ref, w_ref, g_ref, b_ref, out_ref):
    y = jnp.concatenate([ymix_ref[...], ynsa_ref[...]], axis=1)
    r = alpha * x_ref[...] + _dot(y, w_ref[...])
    mu = jnp.mean(r, axis=-1, keepdims=True)
    d = r - mu
    var = jnp.mean(d * d, axis=-1, keepdims=True)
    out_ref[...] = d * lax.rsqrt(var + LN_EPS) * g_ref[...] + b_ref[...]


def _outproj(ymix, ynsa, x2, w_out, ln_g, ln_b, alpha, layer):
    bt = x2.shape[0]
    row = lambda i: (i, 0)
    wsel = lambda i: (layer, 0, 0)
    return pl.pallas_call(
        functools.partial(_outproj_kernel, alpha),
        grid=(bt // TT,),
        in_specs=[pl.BlockSpec((TT, D_POOL + D_CONV), row),
                  pl.BlockSpec((TT, D_NSA), row),
                  pl.BlockSpec((TT, D_MODEL), row),
                  pl.BlockSpec((None, D_MODEL, D_MODEL), wsel),
                  pl.BlockSpec((None, 1, D_MODEL), wsel),
                  pl.BlockSpec((None, 1, D_MODEL), wsel)],
        out_specs=pl.BlockSpec((TT, D_MODEL), row),
        out_shape=jax.ShapeDtypeStruct((bt, D_MODEL), f32),
        compiler_params=pltpu.CompilerParams(dimension_semantics=("arbitrary",),
                                             vmem_limit_bytes=VMEM_LIMIT),
        name="outproj",
    )(ymix, ynsa, x2, w_out, ln_g, ln_b)


def _bucket_np(d):
    d = np.asarray(d)
    max_exact = N_BUCKETS // 2
    nf = np.maximum(d, 1).astype(np.float32)
    large = max_exact + (np.log(nf / np.float32(max_exact)) / np.float32(math.log(MAX_DISTANCE / max_exact))
                         * np.float32(N_BUCKETS - max_exact)).astype(np.int32)
    large = np.minimum(large, N_BUCKETS - 1)
    return np.where(d < max_exact, d, large)


_FAR_DIST = 113
assert _bucket_np(np.arange(_FAR_DIST, 4 * WINDOW)).min() == N_BUCKETS - 1

_PAIR_PERM = np.concatenate([np.concatenate([np.arange(HEAD_DIM) + HEAD_DIM * p,
                                             np.arange(HEAD_DIM) + HEAD_DIM * (p + GROUP)]) for p in range(GROUP)])


def _bias_tables(rel_bias):
    nd = 2 * TQ
    onehot = jnp.asarray(np.eye(N_BUCKETS, dtype=np.float32)[_bucket_np(np.arange(nd))])
    tabp = ((jnp.dot(onehot, rel_bias, precision=lax.Precision.HIGHEST)
             - rel_bias[N_BUCKETS - 1:N_BUCKETS, :]) * LOG2E).T
    sl = np.arange(TQ)[:, None]
    tl = np.arange(TQ)[None, :]
    neg = lambda n: jnp.full((N_HEADS, n), NEG, f32)

    def skew(u, rows):
        period = u.shape[1]
        return jnp.tile(u, (1, rows))[:, :rows * (period - 1)].reshape(N_HEADS, rows, period - 1)

    def tile_layout(t):
        return t.transpose(1, 0, 2).reshape(TQ, R)

    diag = tile_layout(skew(jnp.concatenate([tabp[:, 0:TQ], neg(TQ)], axis=1), TQ)[:, :, :TQ])
    prev = tile_layout(skew(jnp.concatenate([tabp[:, TQ:nd], tabp[:, 0:TQ]], axis=1), TQ)[:, :, :TQ])
    edge = jnp.asarray(np.tile(np.where(sl > tl, 0.0, NEG).astype(np.float32), (1, N_HEADS)))
    tabs = jnp.concatenate([edge, prev, diag], axis=0).astype(f32)

    j = np.arange(16)[None, :]
    off = 8 * CMP_STRIDE - (CMP_BLOCK - 1)
    dc = np.arange(TQ)[:, None] - CMP_STRIDE * j + off
    okc = jnp.asarray(dc >= 0)[None]
    period = 4 * TQ
    u = jnp.concatenate([tabp[:, off:nd], jnp.zeros((N_HEADS, off), f32), neg(period - nd - off), tabp[:, 0:off]], axis=1)
    vc = skew(u, nd)[:, 0:16 * CMP_STRIDE:CMP_STRIDE, 0:TQ].transpose(0, 2, 1)
    vc = jnp.where(okc, vc, NEG)
    hi = vc.astype(bf16)
    lo = jnp.where(okc, vc - hi.astype(f32), 0.0).astype(bf16)
    fut = jnp.full((N_HEADS, TQ, 1), NEG, bf16)
    pad = jnp.zeros((N_HEADS, TQ, LANES - _PAD_FLAG_COL - 1), bf16)
    acmp = jnp.concatenate([hi, lo, fut, pad], axis=-1).reshape(R, LANES).T
    return tabs, acmp


def _static_tables(T):
    nc = T // CMP_STRIDE
    ns = T // SLC_BLOCK
    cstart = np.arange(nc)[None, :] * CMP_STRIDE
    sstart = np.arange(NSB)[:, None] * SLC_BLOCK
    ov = np.clip(np.minimum(cstart + CMP_BLOCK, sstart + SLC_BLOCK) - np.maximum(cstart, sstart), 0, None) / CMP_STRIDE
    ov[ns:, :] = 0
    ov[:, nc - 1] = 0
    e_pad = np.concatenate([np.ones((KPAD, NSB), bool),
                            np.arange(T)[:, None] // SLC_BLOCK == np.arange(NSB)[None, :]], axis=0)
    return jnp.asarray(ov, bf16), jnp.asarray(e_pad, bf16)


def _pair_order(w, axis):
    shp = w.shape
    w = w.reshape(shp[:axis] + (N_KV, GROUP, HEAD_DIM) + shp[axis + 1:])
    return jnp.swapaxes(w, axis, axis + 1).reshape(shp)


def _prep_weights(w_in, w_out, pool_w, pe_k, w1_k, w2_k, pe_v, w1_v, w2_v):
    depth = w_in.shape[0]
    sizes = (D_POOL, D_POOL, D_CONV, D_CONV, D_CONV, D_CONV, D_NSA, D_KV, D_KV, D_KV, D_KV, D_KV, D_KV,
             3 * N_HEADS, D_NSA)
    offs = np.cumsum((0,) + sizes)
    w_in = w_in.astype(bf16)
    col = lambda i: w_in[:, :, offs[i]:offs[i + 1]]
    wg = jnp.pad(col(13), ((0, 0), (0, 0), (0, LANES - 3 * N_HEADS)))
    w_all = jnp.concatenate([w_in[:, :, 0:MIX_W], _pair_order(col(6), 2), _pair_order(col(14), 2),
                             col(7), col(8), col(9), col(11), wg], axis=2)
    wvt = jnp.swapaxes(jnp.concatenate([col(10), col(12)], axis=2), 1, 2)
    w_out = w_out.astype(bf16)
    nm = D_POOL + D_CONV
    w_out_p = jnp.concatenate([w_out[:, 0:nm], _pair_order(w_out[:, nm:], 1)], axis=1)

    eye_g = jnp.eye(len(POOL_WINDOWS), dtype=bf16)
    pw_bd = jnp.einsum('zgcd,gh->zgchd', pool_w.astype(bf16), eye_g).reshape(depth, D_POOL, D_POOL)

    half = CMP_BLOCK // 2
    eye2 = jnp.eye(2, dtype=bf16)
    w1s = jnp.stack([w1_k, w1_v], axis=1).astype(bf16).reshape(depth, 2, 2, half, HEAD_DIM, CMP_HIDDEN)
    w1_big = jnp.einsum('zkaldh,kK,gG->zlkgdaKGh', w1s, eye2, eye2).reshape(depth, half * 2 * D_KV, 2 * CH)
    pes = jnp.stack([pe_k, pe_v], axis=1).astype(bf16).reshape(depth, 2, 2, half, HEAD_DIM)
    pe_r = jnp.broadcast_to(pes.transpose(0, 2, 3, 1, 4)[:, :, :, :, None, :],
                            (depth, 2, half, 2, N_KV, HEAD_DIM)).reshape(depth, 2, half * 2 * D_KV)
    pe_r = jnp.pad(pe_r, ((0, 0), (0, 6), (0, 0)))
    w2s = jnp.stack([w2_k, w2_v], axis=1).astype(bf16)
    w2_big = jnp.einsum('zkhd,kK,gG->zkghKGd', w2s, eye2, eye2).reshape(depth, CH, 2 * D_KV)
    return w_all, wvt, w_out_p, pw_bd, w1_big, pe_r, w2_big, jnp.swapaxes(w2_big, 1, 2)


def _padded_kv_init(B, T):
    tp = T + KPAD
    flag = np.zeros((1, tp, 2 * D_KV), np.float32)
    flag[:, :KPAD, D_KV + _PAD_FLAG_COL] = 1.0
    return (jnp.zeros((B, tp, D_KV), bf16), jnp.broadcast_to(jnp.asarray(flag, bf16), (B, tp, 2 * D_KV)),
            jnp.zeros((B, D_KV, tp), bf16), jnp.zeros((B, D_KV, tp), bf16))


def kernel(x, w_in, w_out, pool_w, pool_scale, conv_w, cmp_pe_k, cmp_w1_k, cmp_w2_k, cmp_pe_v, cmp_w1_v, cmp_w2_v,
           rel_bias, ln_g, ln_b):
    B, T, D = x.shape
    depth = w_in.shape[0]
    assert D == D_MODEL and T % TT == 0 and T // SLC_BLOCK <= NSB and T // SLC_BLOCK >= N_SELECT
    assert KPAD % TT == 0 and N_SELECT > N_LOCAL
    alpha = (2 * depth) ** 0.25
    tabs, acmp = _bias_tables(rel_bias)
    ovT, e_pad = _static_tables(T)
    w_all, wvt, w_out_p, pw_bd, w1_big, pe_r, w2_big, w2t_big = _prep_weights(
        w_in, w_out, pool_w, cmp_pe_k, cmp_w1_k, cmp_w2_k, cmp_pe_v, cmp_w1_v, cmp_w2_v)
    pool_scale = pool_scale.reshape(depth, 1, D_POOL)
    ln_g = ln_g.reshape(depth, 1, D_MODEL)
    ln_b = ln_b.reshape(depth, 1, D_MODEL)
    h = x.reshape(B * T, D)
    for l in range(depth):
        ymix, q, z, cmp_l, g, ks, kw, vsT, vwT = _inproj(h, w_all, wvt, pw_bd, pool_scale, conv_w,
                                                         _padded_kv_init(B, T), l, B, T)
        cmp, cmpT = _compress(cmp_l, pe_r, w1_big, w2_big, w2t_big, l, B)
        ynsa = _nsa(q, z, g, cmp, cmpT, ks, kw, vsT, vwT, e_pad, ovT, tabs, acmp, B, T)
        h = _outproj(ymix, ynsa, h, w_out_p, ln_g, ln_b, alpha, l)
    return h.reshape(B, T, D)
```

```python
import functools
import math

import numpy as np
import jax
import jax.numpy as jnp
from jax import lax
from jax.experimental import pallas as pl
from jax.experimental.pallas import tpu as pltpu

f32 = jnp.float32
bf16 = jnp.bfloat16

D_MODEL = 1024
D_POOL = 256
D_CONV = 256
D_NSA = 512
HEAD_DIM = 64
N_HEADS = 8
N_KV = 2
GROUP = 4
D_KV = 128
POOL_GROUP = 64
POOL_WINDOWS = (2, 4, 8, 16)
CONV_WIDTH = 3
CMP_BLOCK = 32
CMP_STRIDE = 16
CMP_HIDDEN = 128
SLC_BLOCK = 64
N_SELECT = 16
N_LOCAL = 2
WINDOW = 512
N_BUCKETS = 32
MAX_DISTANCE = 128
LN_EPS = 1e-5
FORCED = 1e9
NEG = -1e30

LANES = 128
TQ = 128
R = N_HEADS * TQ
NSB = 128
TT = 512
HALO = 16
KT = 512
KPAD = KT
N_WIN = WINDOW + TQ
LOG2E = math.log2(math.e)
_PAD_FLAG_COL = 32
MIX_W = 2 * D_POOL + 4 * D_CONV
VMEM_LIMIT = 56 * 1024 * 1024

_NT = (((1,), (1,)), ((), ()))


def _dot(a, b):
    return jnp.dot(a, b, preferred_element_type=f32)


def _dot_nt(a, b):
    return lax.dot_general(a, b, _NT, preferred_element_type=f32)


def _sigmoid(x):
    return 1.0 / (1.0 + jnp.exp(-x))


_C_MIX = (0, MIX_W)
_C_Q = (_C_MIX[1], _C_MIX[1] + D_NSA)
_C_Z = (_C_Q[1], _C_Q[1] + D_NSA)
_C_CMP = (_C_Z[1], _C_Z[1] + 2 * D_KV)
_C_K = (_C_CMP[1], _C_CMP[1] + 2 * D_KV)
_C_G = (_C_K[1], _C_K[1] + LANES)
W_ALL = _C_G[1]
CMP_ROWS = TT // CMP_STRIDE


def _local_mixers(ext, i, pw_ref, ps_ref, cw_ref):
    e = ext[:, 0:D_POOL]
    s2 = e + pltpu.roll(e, 1, axis=0)
    s4 = s2 + pltpu.roll(s2, 2, axis=0)
    s8 = s4 + pltpu.roll(s4, 4, axis=0)
    s16 = s8 + pltpu.roll(s8, 8, axis=0)
    lane = lax.broadcasted_iota(jnp.int32, (TT, D_POOL), 1)
    grp = lane // POOL_GROUP
    wsum = jnp.where(grp == 0, s2[HALO:], jnp.where(grp == 1, s4[HALO:], jnp.where(grp == 2, s8[HALO:], s16[HALO:])))
    win = jnp.left_shift(2, grp)
    pos = i * TT + lax.broadcasted_iota(jnp.int32, (TT, D_POOL), 0)
    cnt = jnp.minimum(pos + 1, win).astype(f32)
    v = e[HALO:]
    pooled = wsum / cnt - v
    y_pool = _dot(pooled.astype(bf16), pw_ref[...]) * ps_ref[...]
    zp = ext[HALO:, D_POOL:2 * D_POOL]
    y_pool = y_pool * (zp * _sigmoid(zp))

    o = 2 * D_POOL
    cb = ext[HALO:, o:o + D_CONV]
    u = ext[:, o + D_CONV:o + 2 * D_CONV] * ext[:, o + 2 * D_CONV:o + 3 * D_CONV]
    zc = ext[HALO:, o + 3 * D_CONV:o + 4 * D_CONV]
    conv = cw_ref[CONV_WIDTH - 1:CONV_WIDTH, :] * u[HALO:]
    for k in range(CONV_WIDTH - 1):
        conv = conv + cw_ref[k:k + 1, :] * pltpu.roll(u, CONV_WIDTH - 1 - k, axis=0)[HALO:]
    y_conv = cb * conv * (zc * _sigmoid(zc))
    return y_pool, y_conv


def _inproj_kernel(x_ref, w_ref, wvt_ref, pw_ref, ps_ref, cw_ref, ks_in, kw_in, vs_in, vw_in,
                   ymix_ref, q_ref, z_ref, cmp_ref, g_ref, ks_ref, kw_ref, vsT_ref, vwT_ref,
                   halo_ref, cscr_ref):
    del ks_in, kw_in, vs_in, vw_in
    i = pl.program_id(1)
    x = x_ref[...].astype(bf16)

    def proj(c):
        return _dot(x, w_ref[:, c[0]:c[1]])

    mix = proj(_C_MIX)
    halo = jnp.where(i > 0, halo_ref[...], 0.0)
    y_pool, y_conv = _local_mixers(jnp.concatenate([halo, mix], axis=0), i, pw_ref, ps_ref, cw_ref)
    halo_ref[...] = mix[TT - HALO:]
    ymix_ref[:, 0:D_POOL] = y_pool.astype(bf16)
    ymix_ref[:, D_POOL:D_POOL + D_CONV] = y_conv.astype(bf16)

    q_ref[...] = (proj(_C_Q) * (HEAD_DIM ** -0.5 * LOG2E)).astype(bf16)
    z_ref[...] = proj(_C_Z).astype(bf16)
    g_ref[...] = proj(_C_G)
    kk = proj(_C_K).astype(bf16)
    ks_ref[...] = kk[:, 0:D_KV]
    kw_ref[:, 0:D_KV] = kk[:, D_KV:2 * D_KV]
    kw_ref[:, D_KV:2 * D_KV] = jnp.zeros((TT, D_KV), bf16)
    vt = _dot_nt(wvt_ref[...], x)
    vsT_ref[...] = vt[0:D_KV].astype(bf16)
    vwT_ref[...] = vt[D_KV:2 * D_KV].astype(bf16)

    kvc = proj(_C_CMP)
    for c in range(2):
        cscr_ref[c] = kvc[:, c * LANES:(c + 1) * LANES]
        for l in range(CMP_STRIDE):
            cmp_ref[l, :, c * LANES:(c + 1) * LANES] = (
                cscr_ref[c, pl.ds(l, CMP_ROWS, stride=CMP_STRIDE), :].astype(bf16))


def _inproj(x2, w_all, wvt, pw_bd, pool_scale, conv_w, kpads, layer, B, T):
    nt = T // TT
    bt = B * T
    row = lambda b, i: (b * nt + i, 0)
    wsel = lambda b, i: (layer, 0, 0)
    any_spec = pl.BlockSpec(memory_space=pl.ANY)
    tp = T + KPAD
    out_shape = [jax.ShapeDtypeStruct((bt, D_POOL + D_CONV), bf16),
                 jax.ShapeDtypeStruct((bt, D_NSA), bf16),
                 jax.ShapeDtypeStruct((bt, D_NSA), bf16),
                 jax.ShapeDtypeStruct((CMP_STRIDE, bt // CMP_STRIDE, 2 * D_KV), bf16),
                 jax.ShapeDtypeStruct((bt, LANES), f32),
                 jax.ShapeDtypeStruct((B, tp, D_KV), bf16),
                 jax.ShapeDtypeStruct((B, tp, 2 * D_KV), bf16),
                 jax.ShapeDtypeStruct((B, D_KV, tp), bf16),
                 jax.ShapeDtypeStruct((B, D_KV, tp), bf16)]
    out_specs = [pl.BlockSpec((TT, D_POOL + D_CONV), row),
                 pl.BlockSpec((TT, D_NSA), row),
                 pl.BlockSpec((TT, D_NSA), row),
                 pl.BlockSpec((CMP_STRIDE, CMP_ROWS, 2 * D_KV), lambda b, i: (0, b * nt + i, 0)),
                 pl.BlockSpec((TT, LANES), row),
                 pl.BlockSpec((None, TT, D_KV), lambda b, i: (b, i + KPAD // TT, 0)),
                 pl.BlockSpec((None, TT, 2 * D_KV), lambda b, i: (b, i + KPAD // TT, 0)),
                 pl.BlockSpec((None, D_KV, TT), lambda b, i: (b, 0, i + KPAD // TT)),
                 pl.BlockSpec((None, D_KV, TT), lambda b, i: (b, 0, i + KPAD // TT))]
    return pl.pallas_call(
        _inproj_kernel,
        grid=(B, nt),
        in_specs=[pl.BlockSpec((TT, D_MODEL), row),
                  pl.BlockSpec((None, D_MODEL, W_ALL), wsel),
                  pl.BlockSpec((None, 2 * D_KV, D_MODEL), wsel),
                  pl.BlockSpec((None, D_POOL, D_POOL), wsel),
                  pl.BlockSpec((None, 1, D_POOL), wsel),
                  pl.BlockSpec((None, CONV_WIDTH, D_CONV), wsel),
                  any_spec, any_spec, any_spec, any_spec],
        out_specs=out_specs,
        out_shape=out_shape,
        input_output_aliases={6: 5, 7: 6, 8: 7, 9: 8},
        scratch_shapes=[pltpu.VMEM((HALO, MIX_W), f32), pltpu.VMEM((2, TT, LANES), f32)],
        compiler_params=pltpu.CompilerParams(dimension_semantics=("arbitrary", "arbitrary"),
                                             vmem_limit_bytes=VMEM_LIMIT),
        name="inproj",
    )(x2, w_all, wvt, pw_bd, pool_scale, conv_w, *kpads)


CH = 4 * CMP_HIDDEN


def _compress_kernel(r_ref, pe_ref, w1_ref, w2_ref, w2t_ref, out_ref, outT_ref):
    nc = r_ref.shape[1]
    rows = jnp.concatenate([r_ref[l] for l in range(CMP_STRIDE)], axis=1)
    zz = _dot(rows, w1_ref[...])
    pb = _dot(pe_ref[...], w1_ref[...])
    z0 = zz[:, 0:CH] + pb[0:1, 0:CH]
    z1 = zz[:, CH:2 * CH] + pb[1:2, CH:2 * CH]
    h = z0 + pltpu.roll(z1, nc - 1, axis=0)
    h = (h * _sigmoid(h)).astype(bf16)
    out_ref[...] = _dot(h, w2_ref[...]).astype(bf16)
    outT_ref[...] = _dot_nt(w2t_ref[...], h).astype(bf16)


def _compress(cmp_l, pe_r, w1_big, w2_big, w2t_big, layer, B):
    nc = cmp_l.shape[1] // B
    rw = CMP_STRIDE * 2 * D_KV
    wsel = lambda b: (layer, 0, 0)
    return pl.pallas_call(
        _compress_kernel,
        grid=(B,),
        in_specs=[pl.BlockSpec((CMP_STRIDE, nc, 2 * D_KV), lambda b: (0, b, 0)),
                  pl.BlockSpec((None, 8, rw), wsel),
                  pl.BlockSpec((None, rw, 2 * CH), wsel),
                  pl.BlockSpec((None, CH, 2 * D_KV), wsel),
                  pl.BlockSpec((None, 2 * D_KV, CH), wsel)],
        out_specs=[pl.BlockSpec((None, nc, 2 * D_KV), lambda b: (b, 0, 0)),
                   pl.BlockSpec((None, 2 * D_KV, nc), lambda b: (b, 0, 0))],
        out_shape=[jax.ShapeDtypeStruct((B, nc, 2 * D_KV), bf16),
                   jax.ShapeDtypeStruct((B, 2 * D_KV, nc), bf16)],
        compiler_params=pltpu.CompilerParams(dimension_semantics=("arbitrary",),
                                             vmem_limit_bytes=VMEM_LIMIT),
        name="compress",
    )(cmp_l, pe_r, w1_big, w2_big, w2t_big)


def _select_penalty(imp_t, t0):
    shape = (NSB, N_KV * TQ)
    jblk = lax.broadcasted_iota(jnp.int32, shape, 0)
    tq = lax.broadcasted_iota(jnp.int32, shape, 1) % TQ
    back = jnp.right_shift(t0 + tq, int(math.log2(SLC_BLOCK))) - jblk
    causal = back >= 0
    forced = jnp.logical_or(jblk == 0, jnp.logical_and(causal, back < N_LOCAL))
    score = jnp.where(forced, -jnp.inf, jnp.where(causal, imp_t, NEG))
    jf = jblk.astype(f32)
    pen = jnp.where(forced, 0.0, NEG)
    for _ in range(N_SELECT - N_LOCAL - 1):
        best = jnp.max(score, axis=0, keepdims=True)
        first = jnp.min(jnp.where(score == best, jf, float(NSB)), axis=0, keepdims=True)
        pick = jf == first
        pen = jnp.where(pick, 0.0, pen)
        score = jnp.where(pick, -jnp.inf, score)
    return pen


def _query_cols(q_ref):
    sub = lax.broadcasted_iota(jnp.int32, (LANES, TQ), 0)
    low = sub < HEAD_DIM
    pairs = [q_ref[:, LANES * p:LANES * (p + 1)].astype(f32).T for p in range(GROUP)]
    blocks = [(jnp.where(low, pairs[h % GROUP], 0.0) if h < GROUP else jnp.where(low, 0.0, pairs[h % GROUP])).astype(bf16)
              for h in range(N_HEADS)]
    return jnp.concatenate(blocks, axis=1)


def _merge_head_pairs(outs):
    sub = lax.broadcasted_iota(jnp.int32, (LANES, TQ), 0)
    return [jnp.where(sub < HEAD_DIM, outs[p], outs[p + GROUP]).T for p in range(GROUP)]


def _cmp_scores(q_cmp, cmp_ref, t0):
    nc = cmp_ref.shape[0]
    n_i = lax.broadcasted_iota(jnp.int32, (nc, LANES), 0)
    j_i = lax.broadcasted_iota(jnp.int32, (nc, LANES), 1)
    nstart = t0 // CMP_STRIDE - 8
    in_window = jnp.logical_and(j_i < 32, n_i == nstart + jnp.where(j_i < 16, j_i, j_i - 16))
    future = jnp.logical_and(j_i == 32, n_i >= nstart + 15)
    place_b = jnp.where(jnp.logical_or(in_window, future), 1.0, 0.0).astype(bf16)
    kc_ext = jnp.concatenate([cmp_ref[:, 0:D_KV], place_b], axis=1)
    return _dot(kc_ext, q_cmp)


def _cmp_finish(s1, g_ref, cmpT_ref, ovT_ref, pen_ref, y1_ref, t0):
    m1 = jnp.max(s1, axis=0, keepdims=True)
    p1 = jnp.exp2(s1 - m1)
    l1 = jnp.sum(p1, axis=0, keepdims=True)
    tcol = t0 + lax.broadcasted_iota(jnp.int32, (1, R), 1) % TQ
    inv1 = jnp.where(tcol >= CMP_BLOCK - 1, 1.0 / l1, 0.0)
    both = _dot(jnp.concatenate([cmpT_ref[D_KV:2 * D_KV, :], ovT_ref[...]], axis=0), p1.astype(bf16)) * inv1
    o1 = both[0:D_KV]

    sig_t = _sigmoid(g_ref[...]).T
    outs = [sig_t[3 * h:3 * h + 1, :] * o1[:, h * TQ:(h + 1) * TQ] for h in range(N_HEADS)]
    for p, y in enumerate(_merge_head_pairs(outs)):
        y1_ref[:, p * LANES:(p + 1) * LANES] = y.astype(bf16)

    imp = both[D_KV:D_KV + NSB]
    sums = []
    for g in range(N_KV):
        acc = imp[:, g * GROUP * TQ:g * GROUP * TQ + TQ]
        for r in range(1, GROUP):
            c0 = (g * GROUP + r) * TQ
            acc = acc + imp[:, c0:c0 + TQ]
        sums.append(acc)
    imp_t = jnp.concatenate(sums, axis=1)
    pen_t = _select_penalty(imp_t, t0)
    pen_ref[...] = pen_t.astype(bf16)


def _nsa_kernel(q_ref, qn_ref, z_ref, g_ref, gn_ref, cmp_ref, cmpT_ref, ovT_ref, ks_ref, kw_ref, vsT_ref, vwT_ref,
                e_ref, tab_ref, acmp_ref, out_ref,
                pen_ref, y1_ref, m_ref, l_ref, acc_ref, sa_ref, sb_ref, mxa_ref, mxb_ref):
    qi = pl.program_id(1)
    nq = pl.num_programs(1)
    t0 = qi * TQ

    @pl.when(qi == 0)
    def _():
        q_cmp0 = jnp.concatenate([_query_cols(q_ref), acmp_ref[...]], axis=0)
        _cmp_finish(_cmp_scores(q_cmp0, cmp_ref, t0), g_ref, cmpT_ref, ovT_ref, pen_ref, y1_ref, t0)

    qcols = _query_cols(q_ref)
    q_cmp = jnp.concatenate([qcols, acmp_ref[...]], axis=0)
    pens = [pen_ref[:, g * TQ:(g + 1) * TQ] for g in range(N_KV) for _ in range(GROUP)]
    q_slc = jnp.concatenate([qcols, jnp.concatenate(pens, axis=1)], axis=0)
    y1 = y1_ref[...]

    m_ref[...] = jnp.full((1, R), NEG, f32)
    l_ref[...] = jnp.zeros((1, R), f32)
    acc_ref[...] = jnp.zeros((LANES, R), f32)
    n_far = (qi + KT // TQ) // (KT // TQ) - 1
    first = (qi + 1) * TQ - n_far * KT

    def tile_row(i):
        return pl.multiple_of(first + i * KT, LANES)

    def produce(i, s_ref, mx_ref):
        r0 = tile_row(i)
        k_ext = jnp.concatenate([ks_ref[pl.ds(r0, KT), :], e_ref[pl.ds(r0, KT), :]], axis=1)
        s = _dot(k_ext, q_slc)
        s_ref[...] = s
        mx_ref[...] = jnp.max(s, axis=0, keepdims=True)

    def consume(i, s_ref, mx_ref, last):
        m_prev = m_ref[...]
        if last:
            s = jnp.concatenate([s_ref[0:KT - 2 * TQ], s_ref[KT - 2 * TQ:KT] + tab_ref[TQ:3 * TQ]], axis=0)
            m_new = jnp.maximum(m_prev, jnp.max(s, axis=0, keepdims=True))
        else:
            s = s_ref[...]
            m_new = jnp.maximum(m_prev, mx_ref[...])
        alpha = jnp.exp2(m_prev - m_new)
        p = jnp.exp2(s - m_new)
        l_ref[...] = alpha * l_ref[...] + jnp.sum(p, axis=0, keepdims=True)
        acc_ref[...] = acc_ref[...] * alpha + _dot(vsT_ref[:, pl.ds(tile_row(i), KT)], p.astype(bf16))
        m_ref[...] = m_new

    tn = jnp.minimum(qi + 1, nq - 1) * TQ
    s1 = _cmp_scores(jnp.concatenate([_query_cols(qn_ref), acmp_ref[...]], axis=0), cmp_ref, tn)
    w0 = pl.multiple_of(t0, LANES)
    s3 = _dot(kw_ref[pl.ds(w0, N_WIN), :], q_cmp)

    _cmp_finish(s1, gn_ref, cmpT_ref, ovT_ref, pen_ref, y1_ref, tn)
    produce(0, sa_ref, mxa_ref)
    s3 = jnp.concatenate([s3[0:TQ] + tab_ref[0:TQ], s3[TQ:WINDOW - TQ], s3[WINDOW - TQ:] + tab_ref[TQ:3 * TQ]], axis=0)
    m3 = jnp.max(s3, axis=0, keepdims=True)
    p3 = jnp.exp2(s3 - m3)
    inv3 = 1.0 / jnp.sum(p3, axis=0, keepdims=True)
    o3 = _dot(vwT_ref[:, pl.ds(w0, N_WIN)], p3.astype(bf16))

    def pair(j, c):
        produce(2 * j + 1, sb_ref, mxb_ref)
        consume(2 * j, sa_ref, mxa_ref, False)
        produce(2 * j + 2, sa_ref, mxa_ref)
        consume(2 * j + 1, sb_ref, mxb_ref, False)
        return c

    lax.fori_loop(0, n_far // 2, pair, 0)

    @pl.when(n_far % 2 == 0)
    def _():
        consume(n_far, sa_ref, mxa_ref, True)

    @pl.when(n_far % 2 == 1)
    def _():
        produce(n_far, sb_ref, mxb_ref)
        consume(n_far - 1, sa_ref, mxa_ref, False)
        consume(n_far, sb_ref, mxb_ref, True)

    inv2 = 1.0 / l_ref[...]

    sig_t = _sigmoid(g_ref[...]).T
    outs = []
    for h in range(N_HEADS):
        c = slice(h * TQ, (h + 1) * TQ)
        g2 = sig_t[3 * h + 1:3 * h + 2, :] * inv2[:, c]
        g3 = sig_t[3 * h + 2:3 * h + 3, :] * inv3[:, c]
        outs.append(g2 * acc_ref[:, c] + g3 * o3[:, c])
    for p, y in enumerate(_merge_head_pairs(outs)):
        c = slice(p * LANES, (p + 1) * LANES)
        zb = z_ref[:, c].astype(f32)
        out_ref[:, c] = ((y + y1[:, c].astype(f32)) * zb * _sigmoid(zb)).astype(bf16)


def _nsa(q, z, g, cmp, cmpT, ks, kw, vsT, vwT, e_pad, ovT, tabs, acmp, B, T):
    nq = T // TQ
    nc = T // CMP_STRIDE
    tp = T + KPAD
    tile = lambda b, i: (b * nq + i, 0)
    nxt = lambda b, i: (b * nq + jnp.minimum(i + 1, nq - 1), 0)
    const2 = lambda b, i: (0, 0)
    batch3 = lambda b, i: (b, 0, 0)
    return pl.pallas_call(
        _nsa_kernel,
        grid=(B, nq),
        in_specs=[pl.BlockSpec((TQ, D_NSA), tile),
                  pl.BlockSpec((TQ, D_NSA), nxt),
                  pl.BlockSpec((TQ, D_NSA), tile),
                  pl.BlockSpec((TQ, LANES), tile),
                  pl.BlockSpec((TQ, LANES), nxt),
                  pl.BlockSpec((None, nc, 2 * D_KV), batch3),
                  pl.BlockSpec((None, 2 * D_KV, nc), batch3),
                  pl.BlockSpec((NSB, nc), const2),
                  pl.BlockSpec((None, tp, D_KV), batch3),
                  pl.BlockSpec((None, tp, 2 * D_KV), batch3),
                  pl.BlockSpec((None, D_KV, tp), batch3),
                  pl.BlockSpec((None, D_KV, tp), batch3),
                  pl.BlockSpec((tp, NSB), const2),
                  pl.BlockSpec((3 * TQ, R), const2),
                  pl.BlockSpec((LANES, R), const2)],
        out_specs=pl.BlockSpec((TQ, D_NSA), tile),
        out_shape=jax.ShapeDtypeStruct((B * T, D_NSA), bf16),
        scratch_shapes=[pltpu.VMEM((NSB, N_KV * TQ), bf16), pltpu.VMEM((TQ, D_NSA), bf16),
                        pltpu.VMEM((1, R), f32), pltpu.VMEM((1, R), f32), pltpu.VMEM((LANES, R), f32),
                        pltpu.VMEM((KT, R), f32), pltpu.VMEM((KT, R), f32),
                        pltpu.VMEM((1, R), f32), pltpu.VMEM((1, R), f32)],
        compiler_params=pltpu.CompilerParams(dimension_semantics=("arbitrary", "arbitrary"),
                                             vmem_limit_bytes=VMEM_LIMIT),
        name="nsa",
    )(q, q, z, g, g, cmp, cmpT, ovT, ks, kw, vsT, vwT, e_pad, tabs, acmp)


def _outproj_kernel(alpha, ymix_ref, ynsa_ref, x_ref, w_ref, g_ref, b_ref, out_ref):
    y = jnp.concatenate([ymix_ref[...], ynsa_ref[...]], axis=1)
    r = alpha * x_ref[...] + _dot(y, w_ref[...])
    mu = jnp.mean(r, axis=-1, keepdims=True)
    d = r - mu
    var = jnp.mean(d * d, axis=-1, keepdims=True)
    out_ref[...] = d * lax.rsqrt(var + LN_EPS) * g_ref[...] + b_ref[...]


def _outproj(ymix, ynsa, x2, w_out, ln_g, ln_b, alpha, layer):
    bt = x2.shape[0]
    row = lambda i: (i, 0)
    wsel = lambda i: (layer, 0, 0)
    return pl.pallas_call(
        functools.partial(_outproj_kernel, alpha),
        grid=(bt // TT,),
        in_specs=[pl.BlockSpec((TT, D_POOL + D_CONV), row),
                  pl.BlockSpec((TT, D_NSA), row),
                  pl.BlockSpec((TT, D_MODEL), row),
                  pl.BlockSpec((None, D_MODEL, D_MODEL), wsel),
                  pl.BlockSpec((None, 1, D_MODEL), wsel),
                  pl.BlockSpec((None, 1, D_MODEL), wsel)],
        out_specs=pl.BlockSpec((TT, D_MODEL), row),
        out_shape=jax.ShapeDtypeStruct((bt, D_MODEL), f32),
        compiler_params=pltpu.CompilerParams(dimension_semantics=("arbitrary",),
                                             vmem_limit_bytes=VMEM_LIMIT),
        name="outproj",
    )(ymix, ynsa, x2, w_out, ln_g, ln_b)


def _bucket_np(d):
    d = np.asarray(d)
    max_exact = N_BUCKETS // 2
    nf = np.maximum(d, 1).astype(np.float32)
    large = max_exact + (np.log(nf / np.float32(max_exact)) / np.float32(math.log(MAX_DISTANCE / max_exact))
                         * np.float32(N_BUCKETS - max_exact)).astype(np.int32)
    large = np.minimum(large, N_BUCKETS - 1)
    return np.where(d < max_exact, d, large)


_FAR_DIST = 113
assert _bucket_np(np.arange(_FAR_DIST, 4 * WINDOW)).min() == N_BUCKETS - 1

_PAIR_PERM = np.concatenate([np.concatenate([np.arange(HEAD_DIM) + HEAD_DIM * p,
                                             np.arange(HEAD_DIM) + HEAD_DIM * (p + GROUP)]) for p in range(GROUP)])


def _bias_tables(rel_bias):
    nd = 2 * TQ
    onehot = jnp.asarray(np.eye(N_BUCKETS, dtype=np.float32)[_bucket_np(np.arange(nd))])
    tabp = ((jnp.dot(onehot, rel_bias, precision=lax.Precision.HIGHEST)
             - rel_bias[N_BUCKETS - 1:N_BUCKETS, :]) * LOG2E).T
    sl = np.arange(TQ)[:, None]
    tl = np.arange(TQ)[None, :]
    neg = lambda n: jnp.full((N_HEADS, n), NEG, f32)

    def skew(u, rows):
        period = u.shape[1]
        return jnp.tile(u, (1, rows))[:, :rows * (period - 1)].reshape(N_HEADS, rows, period - 1)

    def tile_layout(t):
        return t.transpose(1, 0, 2).reshape(TQ, R)

    diag = tile_layout(skew(jnp.concatenate([tabp[:, 0:TQ], neg(TQ)], axis=1), TQ)[:, :, :TQ])
    prev = tile_layout(skew(jnp.concatenate([tabp[:, TQ:nd], tabp[:, 0:TQ]], axis=1), TQ)[:, :, :TQ])
    edge = jnp.asarray(np.tile(np.where(sl > tl, 0.0, NEG).astype(np.float32), (1, N_HEADS)))
    tabs = jnp.concatenate([edge, prev, diag], axis=0).astype(f32)

    j = np.arange(16)[None, :]
    off = 8 * CMP_STRIDE - (CMP_BLOCK - 1)
    dc = np.arange(TQ)[:, None] - CMP_STRIDE * j + off
    okc = jnp.asarray(dc >= 0)[None]
    period = 4 * TQ
    u = jnp.concatenate([tabp[:, off:nd], jnp.zeros((N_HEADS, off), f32), neg(period - nd - off), tabp[:, 0:off]], axis=1)
    vc = skew(u, nd)[:, 0:16 * CMP_STRIDE:CMP_STRIDE, 0:TQ].transpose(0, 2, 1)
    vc = jnp.where(okc, vc, NEG)
    hi = vc.astype(bf16)
    lo = jnp.where(okc, vc - hi.astype(f32), 0.0).astype(bf16)
    fut = jnp.full((N_HEADS, TQ, 1), NEG, bf16)
    pad = jnp.zeros((N_HEADS, TQ, LANES - _PAD_FLAG_COL - 1), bf16)
    acmp = jnp.concatenate([hi, lo, fut, pad], axis=-1).reshape(R, LANES).T
    return tabs, acmp


def _static_tables(T):
    nc = T // CMP_STRIDE
    ns = T // SLC_BLOCK
    cstart = np.arange(nc)[None, :] * CMP_STRIDE
    sstart = np.arange(NSB)[:, None] * SLC_BLOCK
    ov = np.clip(np.minimum(cstart + CMP_BLOCK, sstart + SLC_BLOCK) - np.maximum(cstart, sstart), 0, None) / CMP_STRIDE
    ov[ns:, :] = 0
    ov[:, nc - 1] = 0
    e_pad = np.concatenate([np.ones((KPAD, NSB), bool),
                            np.arange(T)[:, None] // SLC_BLOCK == np.arange(NSB)[None, :]], axis=0)
    return jnp.asarray(ov, bf16), jnp.asarray(e_pad, bf16)


def _pair_order(w, axis):
    shp = w.shape
    w = w.reshape(shp[:axis] + (N_KV, GROUP, HEAD_DIM) + shp[axis + 1:])
    return jnp.swapaxes(w, axis, axis + 1).reshape(shp)


def _prep_weights(w_in, w_out, pool_w, pe_k, w1_k, w2_k, pe_v, w1_v, w2_v):
    depth = w_in.shape[0]
    sizes = (D_POOL, D_POOL, D_CONV, D_CONV, D_CONV, D_CONV, D_NSA, D_KV, D_KV, D_KV, D_KV, D_KV, D_KV,
             3 * N_HEADS, D_NSA)
    offs = np.cumsum((0,) + sizes)
    w_in = w_in.astype(bf16)
    col = lambda i: w_in[:, :, offs[i]:offs[i + 1]]
    wg = jnp.pad(col(13), ((0, 0), (0, 0), (0, LANES - 3 * N_HEADS)))
    w_all = jnp.concatenate([w_in[:, :, 0:MIX_W], _pair_order(col(6), 2), _pair_order(col(14), 2),
                             col(7), col(8), col(9), col(11), wg], axis=2)
    wvt = jnp.swapaxes(jnp.concatenate([col(10), col(12)], axis=2), 1, 2)
    w_out = w_out.astype(bf16)
    nm = D_POOL + D_CONV
    w_out_p = jnp.concatenate([w_out[:, 0:nm], _pair_order(w_out[:, nm:], 1)], axis=1)

    eye_g = jnp.eye(len(POOL_WINDOWS), dtype=bf16)
    pw_bd = jnp.einsum('zgcd,gh->zgchd', pool_w.astype(bf16), eye_g).reshape(depth, D_POOL, D_POOL)

    half = CMP_BLOCK // 2
    eye2 = jnp.eye(2, dtype=bf16)
    rw = half * 2 * D_KV
    cols = []
    for a in range(2):
        for kv, w1 in enumerate((w1_k, w1_v)):
            wsel = w1.astype(bf16).reshape(depth, 2, half, 1, 1, HEAD_DIM, CMP_HIDDEN)[:, a]
            for g in range(N_KV):
                blk = jnp.pad(wsel, ((0, 0), (0, 0), (kv, 1 - kv), (g, N_KV - 1 - g), (0, 0), (0, 0)))
                cols.append(blk.reshape(depth, rw, CMP_HIDDEN))
    w1_big = jnp.concatenate(cols, axis=2)
    pes = jnp.stack([pe_k, pe_v], axis=1).astype(bf16).reshape(depth, 2, 2, half, HEAD_DIM)
    pe_r = jnp.broadcast_to(pes.transpose(0, 2, 3, 1, 4)[:, :, :, :, None, :],
                            (depth, 2, half, 2, N_KV, HEAD_DIM)).reshape(depth, 2, half * 2 * D_KV)
    pe_r = jnp.pad(pe_r, ((0, 0), (0, 6), (0, 0)))
    w2s = jnp.stack([w2_k, w2_v], axis=1).astype(bf16)
    w2_big = jnp.einsum('zkhd,kK,gG->zkghKGd', w2s, eye2, eye2).reshape(depth, CH, 2 * D_KV)
    return w_all, wvt, w_out_p, pw_bd, w1_big, pe_r, w2_big, jnp.swapaxes(w2_big, 1, 2)


def _padded_kv_init(B, T):
    tp = T + KPAD
    flag = np.zeros((1, tp, 2 * D_KV), np.float32)
    flag[:, :KPAD, D_KV + _PAD_FLAG_COL] = 1.0
    return (jnp.zeros((B, tp, D_KV), bf16), jnp.broadcast_to(jnp.asarray(flag, bf16), (B, tp, 2 * D_KV)),
            jnp.zeros((B, D_KV, tp), bf16), jnp.zeros((B, D_KV, tp), bf16))


def kernel(x, w_in, w_out, pool_w, pool_scale, conv_w, cmp_pe_k, cmp_w1_k, cmp_w2_k, cmp_pe_v, cmp_w1_v, cmp_w2_v,
           rel_bias, ln_g, ln_b):
    B, T, D = x.shape
    depth = w_in.shape[0]
    assert D == D_MODEL and T % TT == 0 and T // SLC_BLOCK <= NSB and T // SLC_BLOCK >= N_SELECT
    assert KPAD % TT == 0 and N_SELECT > N_LOCAL
    alpha = (2 * depth) ** 0.25
    tabs, acmp = _bias_tables(rel_bias)
    ovT, e_pad = _static_tables(T)
    w_all, wvt, w_out_p, pw_bd, w1_big, pe_r, w2_big, w2t_big = _prep_weights(
        w_in, w_out, pool_w, cmp_pe_k, cmp_w1_k, cmp_w2_k, cmp_pe_v, cmp_w1_v, cmp_w2_v)
    pool_scale = pool_scale.reshape(depth, 1, D_POOL)
    ln_g = ln_g.reshape(depth, 1, D_MODEL)
    ln_b = ln_b.reshape(depth, 1, D_MODEL)
    h = x.reshape(B * T, D)
    for l in range(depth):
        ymix, q, z, cmp_l, g, ks, kw, vsT, vwT = _inproj(h, w_all, wvt, pw_bd, pool_scale, conv_w,
                                                         _padded_kv_init(B, T), l, B, T)
        cmp, cmpT = _compress(cmp_l, pe_r, w1_big, w2_big, w2t_big, l, B)
        ynsa = _nsa(q, z, g, cmp, cmpT, ks, kw, vsT, vwT, e_pad, ovT, tabs, acmp, B, T)
        h = _outproj(ymix, ynsa, h, w_out_p, ln_g, ln_b, alpha, l)
    return h.reshape(B, T, D)
```

```python
import functools
import math

import numpy as np
import jax
import jax.numpy as jnp
from jax import lax
from jax.experimental import pallas as pl
from jax.experimental.pallas import tpu as pltpu

f32 = jnp.float32
bf16 = jnp.bfloat16

D_MODEL = 1024
D_POOL = 256
D_CONV = 256
D_NSA = 512
HEAD_DIM = 64
N_HEADS = 8
N_KV = 2
GROUP = 4
D_KV = 128
POOL_GROUP = 64
POOL_WINDOWS = (2, 4, 8, 16)
CONV_WIDTH = 3
CMP_BLOCK = 32
CMP_STRIDE = 16
CMP_HIDDEN = 128
SLC_BLOCK = 64
N_SELECT = 16
N_LOCAL = 2
WINDOW = 512
N_BUCKETS = 32
MAX_DISTANCE = 128
LN_EPS = 1e-5
FORCED = 1e9
NEG = -1e30

LANES = 128
TQ = 128
R = N_HEADS * TQ
NSB = 128
TT = 512
HALO = 16
KT = 512
KPAD = KT
N_WIN = WINDOW + TQ
SUM_ROWS = 16
VROWS = D_KV + SUM_ROWS
LOG2E = math.log2(math.e)
_PAD_FLAG_COL = 32
MIX_W = 2 * D_POOL + 4 * D_CONV
VMEM_LIMIT = 56 * 1024 * 1024

_NT = (((1,), (1,)), ((), ()))


def _dot(a, b):
    return jnp.dot(a, b, preferred_element_type=f32)


def _dot_nt(a, b):
    return lax.dot_general(a, b, _NT, preferred_element_type=f32)


def _sigmoid(x):
    return 1.0 / (1.0 + jnp.exp(-x))


_C_MIX = (0, MIX_W)
_C_Q = (_C_MIX[1], _C_MIX[1] + D_NSA)
_C_Z = (_C_Q[1], _C_Q[1] + D_NSA)
_C_CMP = (_C_Z[1], _C_Z[1] + 2 * D_KV)
_C_K = (_C_CMP[1], _C_CMP[1] + 2 * D_KV)
_C_G = (_C_K[1], _C_K[1] + LANES)
W_ALL = _C_G[1]
CMP_ROWS = TT // CMP_STRIDE


def _local_mixers(ext, i, pw_ref, ps_ref, cw_ref):
    e = ext[:, 0:D_POOL]
    s2 = e + pltpu.roll(e, 1, axis=0)
    s4 = s2 + pltpu.roll(s2, 2, axis=0)
    s8 = s4 + pltpu.roll(s4, 4, axis=0)
    s16 = s8 + pltpu.roll(s8, 8, axis=0)
    lane = lax.broadcasted_iota(jnp.int32, (TT, D_POOL), 1)
    grp = lane // POOL_GROUP
    wsum = jnp.where(grp == 0, s2[HALO:], jnp.where(grp == 1, s4[HALO:], jnp.where(grp == 2, s8[HALO:], s16[HALO:])))
    win = jnp.left_shift(2, grp)
    pos = i * TT + lax.broadcasted_iota(jnp.int32, (TT, D_POOL), 0)
    cnt = jnp.minimum(pos + 1, win).astype(f32)
    v = e[HALO:]
    pooled = wsum / cnt - v
    y_pool = _dot(pooled.astype(bf16), pw_ref[...]) * ps_ref[...]
    zp = ext[HALO:, D_POOL:2 * D_POOL]
    y_pool = y_pool * (zp * _sigmoid(zp))

    o = 2 * D_POOL
    cb = ext[HALO:, o:o + D_CONV]
    u = ext[:, o + D_CONV:o + 2 * D_CONV] * ext[:, o + 2 * D_CONV:o + 3 * D_CONV]
    zc = ext[HALO:, o + 3 * D_CONV:o + 4 * D_CONV]
    conv = cw_ref[CONV_WIDTH - 1:CONV_WIDTH, :] * u[HALO:]
    for k in range(CONV_WIDTH - 1):
        conv = conv + cw_ref[k:k + 1, :] * pltpu.roll(u, CONV_WIDTH - 1 - k, axis=0)[HALO:]
    y_conv = cb * conv * (zc * _sigmoid(zc))
    return y_pool, y_conv


def _inproj_kernel(x_ref, w_ref, wvt_ref, pw_ref, ps_ref, cw_ref, ks_in, kw_in, vs_in, vw_in,
                   ymix_ref, q_ref, z_ref, cmp_ref, g_ref, ks_ref, kw_ref, vsT_ref, vwT_ref,
                   halo_ref, cscr_ref):
    del ks_in, kw_in, vs_in, vw_in
    i = pl.program_id(1)
    x = x_ref[...].astype(bf16)

    def proj(c):
        return _dot(x, w_ref[:, c[0]:c[1]])

    mix = proj(_C_MIX)
    halo = jnp.where(i > 0, halo_ref[...], 0.0)
    y_pool, y_conv = _local_mixers(jnp.concatenate([halo, mix], axis=0), i, pw_ref, ps_ref, cw_ref)
    halo_ref[...] = mix[TT - HALO:]
    ymix_ref[:, 0:D_POOL] = y_pool.astype(bf16)
    ymix_ref[:, D_POOL:D_POOL + D_CONV] = y_conv.astype(bf16)

    q_ref[...] = (proj(_C_Q) * (HEAD_DIM ** -0.5 * LOG2E)).astype(bf16)
    z_ref[...] = proj(_C_Z).astype(bf16)
    g_ref[...] = proj(_C_G)
    kk = proj(_C_K).astype(bf16)
    ks_ref[...] = kk[:, 0:D_KV]
    kw_ref[:, 0:D_KV] = kk[:, D_KV:2 * D_KV]
    kw_ref[:, D_KV:2 * D_KV] = jnp.zeros((TT, D_KV), bf16)
    vt = _dot_nt(wvt_ref[...], x)
    vsT_ref[...] = vt[0:D_KV].astype(bf16)
    vwT_ref[...] = vt[D_KV:2 * D_KV].astype(bf16)

    kvc = proj(_C_CMP)
    for c in range(2):
        cscr_ref[c] = kvc[:, c * LANES:(c + 1) * LANES]
        for l in range(CMP_STRIDE):
            cmp_ref[l, :, c * LANES:(c + 1) * LANES] = (
                cscr_ref[c, pl.ds(l, CMP_ROWS, stride=CMP_STRIDE), :].astype(bf16))


def _inproj(x2, w_all, wvt, pw_bd, pool_scale, conv_w, kpads, layer, B, T):
    nt = T // TT
    bt = B * T
    row = lambda b, i: (b * nt + i, 0)
    wsel = lambda b, i: (layer, 0, 0)
    any_spec = pl.BlockSpec(memory_space=pl.ANY)
    tp = T + KPAD
    out_shape = [jax.ShapeDtypeStruct((bt, D_POOL + D_CONV), bf16),
                 jax.ShapeDtypeStruct((bt, D_NSA), bf16),
                 jax.ShapeDtypeStruct((bt, D_NSA), bf16),
                 jax.ShapeDtypeStruct((CMP_STRIDE, bt // CMP_STRIDE, 2 * D_KV), bf16),
                 jax.ShapeDtypeStruct((bt, LANES), f32),
                 jax.ShapeDtypeStruct((B, tp, D_KV), bf16),
                 jax.ShapeDtypeStruct((B, tp, 2 * D_KV), bf16),
                 jax.ShapeDtypeStruct((B, VROWS, tp), bf16),
                 jax.ShapeDtypeStruct((B, VROWS, tp), bf16)]
    out_specs = [pl.BlockSpec((TT, D_POOL + D_CONV), row),
                 pl.BlockSpec((TT, D_NSA), row),
                 pl.BlockSpec((TT, D_NSA), row),
                 pl.BlockSpec((CMP_STRIDE, CMP_ROWS, 2 * D_KV), lambda b, i: (0, b * nt + i, 0)),
                 pl.BlockSpec((TT, LANES), row),
                 pl.BlockSpec((None, TT, D_KV), lambda b, i: (b, i + KPAD // TT, 0)),
                 pl.BlockSpec((None, TT, 2 * D_KV), lambda b, i: (b, i + KPAD // TT, 0)),
                 pl.BlockSpec((None, D_KV, TT), lambda b, i: (b, 0, i + KPAD // TT)),
                 pl.BlockSpec((None, D_KV, TT), lambda b, i: (b, 0, i + KPAD // TT))]
    return pl.pallas_call(
        _inproj_kernel,
        grid=(B, nt),
        in_specs=[pl.BlockSpec((TT, D_MODEL), row),
                  pl.BlockSpec((None, D_MODEL, W_ALL), wsel),
                  pl.BlockSpec((None, 2 * D_KV, D_MODEL), wsel),
                  pl.BlockSpec((None, D_POOL, D_POOL), wsel),
                  pl.BlockSpec((None, 1, D_POOL), wsel),
                  pl.BlockSpec((None, CONV_WIDTH, D_CONV), wsel),
                  any_spec, any_spec, any_spec, any_spec],
        out_specs=out_specs,
        out_shape=out_shape,
        input_output_aliases={6: 5, 7: 6, 8: 7, 9: 8},
        scratch_shapes=[pltpu.VMEM((HALO, MIX_W), f32), pltpu.VMEM((2, TT, LANES), f32)],
        compiler_params=pltpu.CompilerParams(dimension_semantics=("arbitrary", "arbitrary"),
                                             vmem_limit_bytes=VMEM_LIMIT),
        name="inproj",
    )(x2, w_all, wvt, pw_bd, pool_scale, conv_w, *kpads)


CH = 4 * CMP_HIDDEN


def _compress_kernel(r_ref, pe_ref, w1_ref, w2_ref, w2t_ref, out_ref, outT_ref):
    nc = r_ref.shape[1]
    rows = jnp.concatenate([r_ref[l] for l in range(CMP_STRIDE)], axis=1)
    zz = _dot(rows, w1_ref[...])
    pb = _dot(pe_ref[...], w1_ref[...])
    z0 = zz[:, 0:CH] + pb[0:1, 0:CH]
    z1 = zz[:, CH:2 * CH] + pb[1:2, CH:2 * CH]
    h = z0 + pltpu.roll(z1, nc - 1, axis=0)
    h = (h * _sigmoid(h)).astype(bf16)
    out_ref[...] = _dot(h, w2_ref[...]).astype(bf16)
    outT_ref[...] = _dot_nt(w2t_ref[...], h).astype(bf16)


def _compress(cmp_l, pe_r, w1_big, w2_big, w2t_big, layer, B):
    nc = cmp_l.shape[1] // B
    rw = CMP_STRIDE * 2 * D_KV
    wsel = lambda b: (layer, 0, 0)
    return pl.pallas_call(
        _compress_kernel,
        grid=(B,),
        in_specs=[pl.BlockSpec((CMP_STRIDE, nc, 2 * D_KV), lambda b: (0, b, 0)),
                  pl.BlockSpec((None, 8, rw), wsel),
                  pl.BlockSpec((None, rw, 2 * CH), wsel),
                  pl.BlockSpec((None, CH, 2 * D_KV), wsel),
                  pl.BlockSpec((None, 2 * D_KV, CH), wsel)],
        out_specs=[pl.BlockSpec((None, nc, 2 * D_KV), lambda b: (b, 0, 0)),
                   pl.BlockSpec((None, 2 * D_KV, nc), lambda b: (b, 0, 0))],
        out_shape=[jax.ShapeDtypeStruct((B, nc, 2 * D_KV), bf16),
                   jax.ShapeDtypeStruct((B, 2 * D_KV, nc), bf16)],
        compiler_params=pltpu.CompilerParams(dimension_semantics=("arbitrary",),
                                             vmem_limit_bytes=VMEM_LIMIT),
        name="compress",
    )(cmp_l, pe_r, w1_big, w2_big, w2t_big)


def _select_penalty(imp_t, t0):
    shape = (NSB, N_KV * TQ)
    jblk = lax.broadcasted_iota(jnp.int32, shape, 0)
    tq = lax.broadcasted_iota(jnp.int32, shape, 1) % TQ
    back = jnp.right_shift(t0 + tq, int(math.log2(SLC_BLOCK))) - jblk
    causal = back >= 0
    forced = jnp.logical_or(jblk == 0, jnp.logical_and(causal, back < N_LOCAL))
    score = jnp.where(forced, -jnp.inf, jnp.where(causal, imp_t, NEG))
    jf = jblk.astype(f32)
    for _ in range(N_SELECT - N_LOCAL - 1):
        best = jnp.max(score, axis=0, keepdims=True)
        first = jnp.min(jnp.where(score == best, jf, float(NSB)), axis=0, keepdims=True)
        score = jnp.where(jf == first, -jnp.inf, score)
    return jnp.where(score == -jnp.inf, 0.0, NEG)


def _query_cols(q_ref):
    sub = lax.broadcasted_iota(jnp.int32, (LANES, TQ), 0)
    low = sub < HEAD_DIM
    pairs = [q_ref[:, LANES * p:LANES * (p + 1)].astype(f32).T for p in range(GROUP)]
    blocks = [(jnp.where(low, pairs[h % GROUP], 0.0) if h < GROUP else jnp.where(low, 0.0, pairs[h % GROUP])).astype(bf16)
              for h in range(N_HEADS)]
    return jnp.concatenate(blocks, axis=1)


def _merge_head_pairs(outs):
    sub = lax.broadcasted_iota(jnp.int32, (LANES, TQ), 0)
    return [jnp.where(sub < HEAD_DIM, outs[p], outs[p + GROUP]).T for p in range(GROUP)]


def _cmp_scores(q_cmp, cmp_ref, t0):
    nc = cmp_ref.shape[0]
    n_i = lax.broadcasted_iota(jnp.int32, (nc, LANES), 0)
    j_i = lax.broadcasted_iota(jnp.int32, (nc, LANES), 1)
    nstart = t0 // CMP_STRIDE - 8
    in_window = jnp.logical_and(j_i < 32, n_i == nstart + jnp.where(j_i < 16, j_i, j_i - 16))
    future = jnp.logical_and(j_i == 32, n_i >= nstart + 15)
    place_b = jnp.where(jnp.logical_or(in_window, future), 1.0, 0.0).astype(bf16)
    kc_ext = jnp.concatenate([cmp_ref[:, 0:D_KV], place_b], axis=1)
    return _dot(kc_ext, q_cmp)


def _cmp_finish(s1, g_ref, cmpT_ref, ovT_ref, pen_ref, y1_ref, t0):
    m1 = jnp.max(s1, axis=0, keepdims=True)
    p1 = jnp.exp2(s1 - m1)
    both = _dot(jnp.concatenate([cmpT_ref[D_KV:2 * D_KV, :], ovT_ref[...]], axis=0), p1.astype(bf16))
    l1 = both[D_KV + NSB:D_KV + NSB + 1]
    tcol = t0 + lax.broadcasted_iota(jnp.int32, (1, R), 1) % TQ
    inv1 = jnp.where(tcol >= CMP_BLOCK - 1, 1.0 / l1, 0.0)
    o1 = both[0:D_KV] * inv1

    sig_t = _sigmoid(g_ref[...]).T
    outs = [sig_t[3 * h:3 * h + 1, :] * o1[:, h * TQ:(h + 1) * TQ] for h in range(N_HEADS)]
    for p, y in enumerate(_merge_head_pairs(outs)):
        y1_ref[:, p * LANES:(p + 1) * LANES] = y.astype(bf16)

    imp = both[D_KV:D_KV + NSB] * inv1
    sums = []
    for g in range(N_KV):
        acc = imp[:, g * GROUP * TQ:g * GROUP * TQ + TQ]
        for r in range(1, GROUP):
            c0 = (g * GROUP + r) * TQ
            acc = acc + imp[:, c0:c0 + TQ]
        sums.append(acc)
    imp_t = jnp.concatenate(sums, axis=1)
    pen_t = _select_penalty(imp_t, t0)
    pen_ref[...] = pen_t.astype(bf16)


def _nsa_kernel(q_ref, qn_ref, z_ref, g_ref, gn_ref, cmp_ref, cmpT_ref, ovT_ref, ks_ref, kw_ref, vsT_ref, vwT_ref,
                e_ref, tab_ref, acmp_ref, out_ref,
                pen_ref, y1_ref, m_ref, l_ref, acc_ref, sa_ref, sb_ref, mxa_ref, mxb_ref):
    qi = pl.program_id(1)
    nq = pl.num_programs(1)
    t0 = qi * TQ

    @pl.when(qi == 0)
    def _():
        q_cmp0 = jnp.concatenate([_query_cols(q_ref), acmp_ref[...]], axis=0)
        _cmp_finish(_cmp_scores(q_cmp0, cmp_ref, t0), g_ref, cmpT_ref, ovT_ref, pen_ref, y1_ref, t0)

    qcols = _query_cols(q_ref)
    q_cmp = jnp.concatenate([qcols, acmp_ref[...]], axis=0)
    pens = [pen_ref[:, g * TQ:(g + 1) * TQ] for g in range(N_KV) for _ in range(GROUP)]
    q_slc = jnp.concatenate([qcols, jnp.concatenate(pens, axis=1)], axis=0)
    y1 = y1_ref[...]

    m_ref[...] = jnp.full((1, R), NEG, f32)
    l_ref[...] = jnp.zeros((1, R), f32)
    acc_ref[...] = jnp.zeros((LANES, R), f32)
    n_far = (qi + KT // TQ) // (KT // TQ) - 1
    first = (qi + 1) * TQ - n_far * KT

    def tile_row(i):
        return pl.multiple_of(first + i * KT, LANES)

    def produce(i, s_ref, mx_ref):
        r0 = tile_row(i)
        k_ext = jnp.concatenate([ks_ref[pl.ds(r0, KT), :], e_ref[pl.ds(r0, KT), :]], axis=1)
        s = _dot(k_ext, q_slc)
        s_ref[...] = s
        mx_ref[...] = jnp.max(s, axis=0, keepdims=True)

    def consume(i, s_ref, mx_ref, last):
        m_prev = m_ref[...]
        if last:
            s = jnp.concatenate([s_ref[0:KT - 2 * TQ], s_ref[KT - 2 * TQ:KT] + tab_ref[TQ:3 * TQ]], axis=0)
            m_new = jnp.maximum(m_prev, jnp.max(s, axis=0, keepdims=True))
        else:
            s = s_ref[...]
            m_new = jnp.maximum(m_prev, mx_ref[...])
        alpha = jnp.exp2(m_prev - m_new)
        p = jnp.exp2(s - m_new)
        pv = _dot(vsT_ref[:, pl.ds(tile_row(i), KT)], p.astype(bf16))
        l_ref[...] = alpha * l_ref[...] + pv[D_KV:D_KV + 1]
        acc_ref[...] = acc_ref[...] * alpha + pv[0:D_KV]
        m_ref[...] = m_new

    tn = jnp.minimum(qi + 1, nq - 1) * TQ
    s1 = _cmp_scores(jnp.concatenate([_query_cols(qn_ref), acmp_ref[...]], axis=0), cmp_ref, tn)
    w0 = pl.multiple_of(t0, LANES)
    s3 = _dot(kw_ref[pl.ds(w0, N_WIN), :], q_cmp)

    _cmp_finish(s1, gn_ref, cmpT_ref, ovT_ref, pen_ref, y1_ref, tn)
    produce(0, sa_ref, mxa_ref)
    s3 = jnp.concatenate([s3[0:TQ] + tab_ref[0:TQ], s3[TQ:WINDOW - TQ], s3[WINDOW - TQ:] + tab_ref[TQ:3 * TQ]], axis=0)
    m3 = jnp.max(s3, axis=0, keepdims=True)
    p3 = jnp.exp2(s3 - m3)
    o3 = _dot(vwT_ref[:, pl.ds(w0, N_WIN)], p3.astype(bf16))
    inv3 = 1.0 / o3[D_KV:D_KV + 1]

    def pair(j, c):
        produce(2 * j + 1, sb_ref, mxb_ref)
        consume(2 * j, sa_ref, mxa_ref, False)
        produce(2 * j + 2, sa_ref, mxa_ref)
        consume(2 * j + 1, sb_ref, mxb_ref, False)
        return c

    lax.fori_loop(0, n_far // 2, pair, 0)

    @pl.when(n_far % 2 == 0)
    def _():
        consume(n_far, sa_ref, mxa_ref, True)

    @pl.when(n_far % 2 == 1)
    def _():
        produce(n_far, sb_ref, mxb_ref)
        consume(n_far - 1, sa_ref, mxa_ref, False)
        consume(n_far, sb_ref, mxb_ref, True)

    inv2 = 1.0 / l_ref[...]

    sig_t = _sigmoid(g_ref[...]).T
    outs = []
    for h in range(N_HEADS):
        c = slice(h * TQ, (h + 1) * TQ)
        g2 = sig_t[3 * h + 1:3 * h + 2, :] * inv2[:, c]
        g3 = sig_t[3 * h + 2:3 * h + 3, :] * inv3[:, c]
        outs.append(g2 * acc_ref[:, c] + g3 * o3[0:D_KV, c])
    for p, y in enumerate(_merge_head_pairs(outs)):
        c = slice(p * LANES, (p + 1) * LANES)
        zb = z_ref[:, c].astype(f32)
        out_ref[:, c] = ((y + y1[:, c].astype(f32)) * zb * _sigmoid(zb)).astype(bf16)


def _nsa(q, z, g, cmp, cmpT, ks, kw, vsT, vwT, e_pad, ovT, tabs, acmp, B, T):
    nq = T // TQ
    nc = T // CMP_STRIDE
    tp = T + KPAD
    tile = lambda b, i: (b * nq + i, 0)
    nxt = lambda b, i: (b * nq + jnp.minimum(i + 1, nq - 1), 0)
    const2 = lambda b, i: (0, 0)
    batch3 = lambda b, i: (b, 0, 0)
    return pl.pallas_call(
        _nsa_kernel,
        grid=(B, nq),
        in_specs=[pl.BlockSpec((TQ, D_NSA), tile),
                  pl.BlockSpec((TQ, D_NSA), nxt),
                  pl.BlockSpec((TQ, D_NSA), tile),
                  pl.BlockSpec((TQ, LANES), tile),
                  pl.BlockSpec((TQ, LANES), nxt),
                  pl.BlockSpec((None, nc, 2 * D_KV), batch3),
                  pl.BlockSpec((None, 2 * D_KV, nc), batch3),
                  pl.BlockSpec((NSB + SUM_ROWS, nc), const2),
                  pl.BlockSpec((None, tp, D_KV), batch3),
                  pl.BlockSpec((None, tp, 2 * D_KV), batch3),
                  pl.BlockSpec((None, VROWS, tp), batch3),
                  pl.BlockSpec((None, VROWS, tp), batch3),
                  pl.BlockSpec((tp, NSB), const2),
                  pl.BlockSpec((3 * TQ, R), const2),
                  pl.BlockSpec((LANES, R), const2)],
        out_specs=pl.BlockSpec((TQ, D_NSA), tile),
        out_shape=jax.ShapeDtypeStruct((B * T, D_NSA), bf16),
        scratch_shapes=[pltpu.VMEM((NSB, N_KV * TQ), bf16), pltpu.VMEM((TQ, D_NSA), bf16),
                        pltpu.VMEM((1, R), f32), pltpu.VMEM((1, R), f32), pltpu.VMEM((LANES, R), f32),
                        pltpu.VMEM((KT, R), f32), pltpu.VMEM((KT, R), f32),
                        pltpu.VMEM((1, R), f32), pltpu.VMEM((1, R), f32)],
        compiler_params=pltpu.CompilerParams(dimension_semantics=("arbitrary", "arbitrary"),
                                             vmem_limit_bytes=VMEM_LIMIT),
        name="nsa",
    )(q, q, z, g, g, cmp, cmpT, ovT, ks, kw, vsT, vwT, e_pad, tabs, acmp)


def _outproj_kernel(alpha, ymix_ref, ynsa_ref, x_ref, w_ref, g_ref, b_ref, out_ref):
    y = jnp.concatenate([ymix_ref[...], ynsa_ref[...]], axis=1)
    r = alpha * x_ref[...] + _dot(y, w_ref[...])
    mu = jnp.mean(r, axis=-1, keepdims=True)
    d = r - mu
    var = jnp.mean(d * d, axis=-1, keepdims=True)
    out_ref[...] = d * lax.rsqrt(var + LN_EPS) * g_ref[...] + b_ref[...]


def _outproj(ymix, ynsa, x2, w_out, ln_g, ln_b, alpha, layer):
    bt = x2.shape[0]
    row = lambda i: (i, 0)
    wsel = lambda i: (layer, 0, 0)
    return pl.pallas_call(
        functools.partial(_outproj_kernel, alpha),
        grid=(bt // TT,),
        in_specs=[pl.BlockSpec((TT, D_POOL + D_CONV), row),
                  pl.BlockSpec((TT, D_NSA), row),
                  pl.BlockSpec((TT, D_MODEL), row),
                  pl.BlockSpec((None, D_MODEL, D_MODEL), wsel),
                  pl.BlockSpec((None, 1, D_MODEL), wsel),
                  pl.BlockSpec((None, 1, D_MODEL), wsel)],
        out_specs=pl.BlockSpec((TT, D_MODEL), row),
        out_shape=jax.ShapeDtypeStruct((bt, D_MODEL), f32),
        compiler_params=pltpu.CompilerParams(dimension_semantics=("arbitrary",),
                                             vmem_limit_bytes=VMEM_LIMIT),
        name="outproj",
    )(ymix, ynsa, x2, w_out, ln_g, ln_b)


def _bucket_np(d):
    d = np.asarray(d)
    max_exact = N_BUCKETS // 2
    nf = np.maximum(d, 1).astype(np.float32)
    large = max_exact + (np.log(nf / np.float32(max_exact)) / np.float32(math.log(MAX_DISTANCE / max_exact))
                         * np.float32(N_BUCKETS - max_exact)).astype(np.int32)
    large = np.minimum(large, N_BUCKETS - 1)
    return np.where(d < max_exact, d, large)


_FAR_DIST = 113
assert _bucket_np(np.arange(_FAR_DIST, 4 * WINDOW)).min() == N_BUCKETS - 1

_PAIR_PERM = np.concatenate([np.concatenate([np.arange(HEAD_DIM) + HEAD_DIM * p,
                                             np.arange(HEAD_DIM) + HEAD_DIM * (p + GROUP)]) for p in range(GROUP)])


def _bias_tables(rel_bias):
    nd = 2 * TQ
    onehot = jnp.asarray(np.eye(N_BUCKETS, dtype=np.float32)[_bucket_np(np.arange(nd))])
    tabp = ((jnp.dot(onehot, rel_bias, precision=lax.Precision.HIGHEST)
             - rel_bias[N_BUCKETS - 1:N_BUCKETS, :]) * LOG2E).T
    sl = np.arange(TQ)[:, None]
    tl = np.arange(TQ)[None, :]
    neg = lambda n: jnp.full((N_HEADS, n), NEG, f32)

    def skew(u, rows):
        period = u.shape[1]
        return jnp.tile(u, (1, rows))[:, :rows * (period - 1)].reshape(N_HEADS, rows, period - 1)

    def tile_layout(t):
        return t.transpose(1, 0, 2).reshape(TQ, R)

    diag = tile_layout(skew(jnp.concatenate([tabp[:, 0:TQ], neg(TQ)], axis=1), TQ)[:, :, :TQ])
    prev = tile_layout(skew(jnp.concatenate([tabp[:, TQ:nd], tabp[:, 0:TQ]], axis=1), TQ)[:, :, :TQ])
    edge = jnp.asarray(np.tile(np.where(sl > tl, 0.0, NEG).astype(np.float32), (1, N_HEADS)))
    tabs = jnp.concatenate([edge, prev, diag], axis=0).astype(f32)

    j = np.arange(16)[None, :]
    off = 8 * CMP_STRIDE - (CMP_BLOCK - 1)
    dc = np.arange(TQ)[:, None] - CMP_STRIDE * j + off
    okc = jnp.asarray(dc >= 0)[None]
    period = 4 * TQ
    u = jnp.concatenate([tabp[:, off:nd], jnp.zeros((N_HEADS, off), f32), neg(period - nd - off), tabp[:, 0:off]], axis=1)
    vc = skew(u, nd)[:, 0:16 * CMP_STRIDE:CMP_STRIDE, 0:TQ].transpose(0, 2, 1)
    vc = jnp.where(okc, vc, NEG)
    hi = vc.astype(bf16)
    lo = jnp.where(okc, vc - hi.astype(f32), 0.0).astype(bf16)
    fut = jnp.full((N_HEADS, TQ, 1), NEG, bf16)
    pad = jnp.zeros((N_HEADS, TQ, LANES - _PAD_FLAG_COL - 1), bf16)
    acmp = jnp.concatenate([hi, lo, fut, pad], axis=-1).reshape(R, LANES).T
    return tabs, acmp


def _static_tables(T):
    nc = T // CMP_STRIDE
    ns = T // SLC_BLOCK
    cstart = np.arange(nc)[None, :] * CMP_STRIDE
    sstart = np.arange(NSB)[:, None] * SLC_BLOCK
    ov = np.clip(np.minimum(cstart + CMP_BLOCK, sstart + SLC_BLOCK) - np.maximum(cstart, sstart), 0, None) / CMP_STRIDE
    ov[ns:, :] = 0
    ov[:, nc - 1] = 0
    ov = np.concatenate([ov, np.ones((SUM_ROWS, nc))], axis=0)
    e_pad = np.concatenate([np.ones((KPAD, NSB), bool),
                            np.arange(T)[:, None] // SLC_BLOCK == np.arange(NSB)[None, :]], axis=0)
    return jnp.asarray(ov, bf16), jnp.asarray(e_pad, bf16)


def _pair_order(w, axis):
    shp = w.shape
    w = w.reshape(shp[:axis] + (N_KV, GROUP, HEAD_DIM) + shp[axis + 1:])
    return jnp.swapaxes(w, axis, axis + 1).reshape(shp)


def _prep_weights(w_in, w_out, pool_w, pe_k, w1_k, w2_k, pe_v, w1_v, w2_v):
    depth = w_in.shape[0]
    sizes = (D_POOL, D_POOL, D_CONV, D_CONV, D_CONV, D_CONV, D_NSA, D_KV, D_KV, D_KV, D_KV, D_KV, D_KV,
             3 * N_HEADS, D_NSA)
    offs = np.cumsum((0,) + sizes)
    w_in = w_in.astype(bf16)
    col = lambda i: w_in[:, :, offs[i]:offs[i + 1]]
    wg = jnp.pad(col(13), ((0, 0), (0, 0), (0, LANES - 3 * N_HEADS)))
    w_all = jnp.concatenate([w_in[:, :, 0:MIX_W], _pair_order(col(6), 2), _pair_order(col(14), 2),
                             col(7), col(8), col(9), col(11), wg], axis=2)
    wvt = jnp.swapaxes(jnp.concatenate([col(10), col(12)], axis=2), 1, 2)
    w_out = w_out.astype(bf16)
    nm = D_POOL + D_CONV
    w_out_p = jnp.concatenate([w_out[:, 0:nm], _pair_order(w_out[:, nm:], 1)], axis=1)

    eye_g = jnp.eye(len(POOL_WINDOWS), dtype=bf16)
    pw_bd = jnp.einsum('zgcd,gh->zgchd', pool_w.astype(bf16), eye_g).reshape(depth, D_POOL, D_POOL)

    half = CMP_BLOCK // 2
    eye2 = jnp.eye(2, dtype=bf16)
    rw = half * 2 * D_KV
    cols = []
    for a in range(2):
        for kv, w1 in enumerate((w1_k, w1_v)):
            wsel = w1.astype(bf16).reshape(depth, 2, half, 1, 1, HEAD_DIM, CMP_HIDDEN)[:, a]
            for g in range(N_KV):
                blk = jnp.pad(wsel, ((0, 0), (0, 0), (kv, 1 - kv), (g, N_KV - 1 - g), (0, 0), (0, 0)))
                cols.append(blk.reshape(depth, rw, CMP_HIDDEN))
    w1_big = jnp.concatenate(cols, axis=2)
    pes = jnp.stack([pe_k, pe_v], axis=1).astype(bf16).reshape(depth, 2, 2, half, HEAD_DIM)
    pe_r = jnp.broadcast_to(pes.transpose(0, 2, 3, 1, 4)[:, :, :, :, None, :],
                            (depth, 2, half, 2, N_KV, HEAD_DIM)).reshape(depth, 2, half * 2 * D_KV)
    pe_r = jnp.pad(pe_r, ((0, 0), (0, 6), (0, 0)))
    w2s = jnp.stack([w2_k, w2_v], axis=1).astype(bf16)
    w2_big = jnp.einsum('zkhd,kK,gG->zkghKGd', w2s, eye2, eye2).reshape(depth, CH, 2 * D_KV)
    return w_all, wvt, w_out_p, pw_bd, w1_big, pe_r, w2_big, jnp.swapaxes(w2_big, 1, 2)


def _padded_kv_init(B, T):
    tp = T + KPAD
    flag = np.zeros((1, tp, 2 * D_KV), np.float32)
    flag[:, :KPAD, D_KV + _PAD_FLAG_COL] = 1.0
    ones_rows = np.zeros((1, VROWS, 1), np.float32)
    ones_rows[:, D_KV:] = 1.0
    vinit = jnp.broadcast_to(jnp.asarray(ones_rows, bf16), (B, VROWS, tp))
    return (jnp.zeros((B, tp, D_KV), bf16), jnp.broadcast_to(jnp.asarray(flag, bf16), (B, tp, 2 * D_KV)),
            vinit, vinit)


def kernel(x, w_in, w_out, pool_w, pool_scale, conv_w, cmp_pe_k, cmp_w1_k, cmp_w2_k, cmp_pe_v, cmp_w1_v, cmp_w2_v,
           rel_bias, ln_g, ln_b):
    B, T, D = x.shape
    depth = w_in.shape[0]
    assert D == D_MODEL and T % TT == 0 and T // SLC_BLOCK <= NSB and T // SLC_BLOCK >= N_SELECT
    assert KPAD % TT == 0 and N_SELECT > N_LOCAL
    alpha = (2 * depth) ** 0.25
    tabs, acmp = _bias_tables(rel_bias)
    ovT, e_pad = _static_tables(T)
    w_all, wvt, w_out_p, pw_bd, w1_big, pe_r, w2_big, w2t_big = _prep_weights(
        w_in, w_out, pool_w, cmp_pe_k, cmp_w1_k, cmp_w2_k, cmp_pe_v, cmp_w1_v, cmp_w2_v)
    pool_scale = pool_scale.reshape(depth, 1, D_POOL)
    ln_g = ln_g.reshape(depth, 1, D_MODEL)
    ln_b = ln_b.reshape(depth, 1, D_MODEL)
    h = x.reshape(B * T, D)
    for l in range(depth):
        ymix, q, z, cmp_l, g, ks, kw, vsT, vwT = _inproj(h, w_all, wvt, pw_bd, pool_scale, conv_w,
                                                         _padded_kv_init(B, T), l, B, T)
        cmp, cmpT = _compress(cmp_l, pe_r, w1_big, w2_big, w2t_big, l, B)
        ynsa = _nsa(q, z, g, cmp, cmpT, ks, kw, vsT, vwT, e_pad, ovT, tabs, acmp, B, T)
        h = _outproj(ymix, ynsa, h, w_out_p, ln_g, ln_b, alpha, l)
    return h.reshape(B, T, D)
```

```python
import functools
import math

import numpy as np
import jax
import jax.numpy as jnp
from jax import lax
from jax.experimental import pallas as pl
from jax.experimental.pallas import tpu as pltpu

f32 = jnp.float32
bf16 = jnp.bfloat16

D_MODEL = 1024
D_POOL = 256
D_CONV = 256
D_NSA = 512
HEAD_DIM = 64
N_HEADS = 8
N_KV = 2
GROUP = 4
D_KV = 128
POOL_GROUP = 64
POOL_WINDOWS = (2, 4, 8, 16)
CONV_WIDTH = 3
CMP_BLOCK = 32
CMP_STRIDE = 16
CMP_HIDDEN = 128
SLC_BLOCK = 64
N_SELECT = 16
N_LOCAL = 2
WINDOW = 512
N_BUCKETS = 32
MAX_DISTANCE = 128
LN_EPS = 1e-5
FORCED = 1e9
NEG = -1e30

LANES = 128
TQ = 128
R = N_HEADS * TQ
NSB = 128
TT = 512
HALO = 16
KT = 512
KPAD = KT
N_WIN = WINDOW + TQ
SUM_ROWS = 16
VROWS = HEAD_DIM + SUM_ROWS
GC = R // N_KV
LOG2E = math.log2(math.e)
_PAD_FLAG_COL = 32
MIX_W = 2 * D_POOL + 4 * D_CONV
VMEM_LIMIT = 56 * 1024 * 1024

_NT = (((1,), (1,)), ((), ()))


def _dot(a, b):
    return jnp.dot(a, b, preferred_element_type=f32)


def _dot_nt(a, b):
    return lax.dot_general(a, b, _NT, preferred_element_type=f32)


def _sigmoid(x):
    return 1.0 / (1.0 + jnp.exp(-x))


_C_MIX = (0, MIX_W)
_C_Q = (_C_MIX[1], _C_MIX[1] + D_NSA)
_C_Z = (_C_Q[1], _C_Q[1] + D_NSA)
_C_CMP = (_C_Z[1], _C_Z[1] + 2 * D_KV)
_C_K = (_C_CMP[1], _C_CMP[1] + 2 * D_KV)
_C_G = (_C_K[1], _C_K[1] + LANES)
W_ALL = _C_G[1]
CMP_ROWS = TT // CMP_STRIDE


def _local_mixers(ext, i, pw_ref, ps_ref, cw_ref):
    e = ext[:, 0:D_POOL]
    s2 = e + pltpu.roll(e, 1, axis=0)
    s4 = s2 + pltpu.roll(s2, 2, axis=0)
    s8 = s4 + pltpu.roll(s4, 4, axis=0)
    s16 = s8 + pltpu.roll(s8, 8, axis=0)
    lane = lax.broadcasted_iota(jnp.int32, (TT, D_POOL), 1)
    grp = lane // POOL_GROUP
    wsum = jnp.where(grp == 0, s2[HALO:], jnp.where(grp == 1, s4[HALO:], jnp.where(grp == 2, s8[HALO:], s16[HALO:])))
    win = jnp.left_shift(2, grp)
    pos = i * TT + lax.broadcasted_iota(jnp.int32, (TT, D_POOL), 0)
    cnt = jnp.minimum(pos + 1, win).astype(f32)
    v = e[HALO:]
    pooled = wsum / cnt - v
    y_pool = _dot(pooled.astype(bf16), pw_ref[...]) * ps_ref[...]
    zp = ext[HALO:, D_POOL:2 * D_POOL]
    y_pool = y_pool * (zp * _sigmoid(zp))

    o = 2 * D_POOL
    cb = ext[HALO:, o:o + D_CONV]
    u = ext[:, o + D_CONV:o + 2 * D_CONV] * ext[:, o + 2 * D_CONV:o + 3 * D_CONV]
    zc = ext[HALO:, o + 3 * D_CONV:o + 4 * D_CONV]
    conv = cw_ref[CONV_WIDTH - 1:CONV_WIDTH, :] * u[HALO:]
    for k in range(CONV_WIDTH - 1):
        conv = conv + cw_ref[k:k + 1, :] * pltpu.roll(u, CONV_WIDTH - 1 - k, axis=0)[HALO:]
    y_conv = cb * conv * (zc * _sigmoid(zc))
    return y_pool, y_conv


def _inproj_kernel(x_ref, w_ref, wvt_ref, pw_ref, ps_ref, cw_ref, ks_in, kw_in, vs_in, vw_in,
                   ymix_ref, q_ref, z_ref, cmp_ref, g_ref, ks_ref, kw_ref, vsT_ref, vwT_ref,
                   halo_ref, cscr_ref):
    del ks_in, kw_in, vs_in, vw_in
    i = pl.program_id(1)
    x = x_ref[...].astype(bf16)

    def proj(c):
        return _dot(x, w_ref[:, c[0]:c[1]])

    mix = proj(_C_MIX)
    halo = jnp.where(i > 0, halo_ref[...], 0.0)
    y_pool, y_conv = _local_mixers(jnp.concatenate([halo, mix], axis=0), i, pw_ref, ps_ref, cw_ref)
    halo_ref[...] = mix[TT - HALO:]
    ymix_ref[:, 0:D_POOL] = y_pool.astype(bf16)
    ymix_ref[:, D_POOL:D_POOL + D_CONV] = y_conv.astype(bf16)

    q_ref[...] = (proj(_C_Q) * (HEAD_DIM ** -0.5 * LOG2E)).astype(bf16)
    z_ref[...] = proj(_C_Z).astype(bf16)
    g_ref[...] = proj(_C_G)
    kk = proj(_C_K).astype(bf16)
    ks_ref[...] = kk[:, 0:D_KV]
    kw_ref[:, 0:D_KV] = kk[:, D_KV:2 * D_KV]
    kw_ref[:, D_KV:2 * D_KV] = jnp.zeros((TT, D_KV), bf16)
    vt = _dot_nt(wvt_ref[...], x)
    for g in range(N_KV):
        vsT_ref[g] = vt[g * HEAD_DIM:(g + 1) * HEAD_DIM].astype(bf16)
        vwT_ref[g] = vt[D_KV + g * HEAD_DIM:D_KV + (g + 1) * HEAD_DIM].astype(bf16)

    kvc = proj(_C_CMP)
    for c in range(2):
        cscr_ref[c] = kvc[:, c * LANES:(c + 1) * LANES]
        for l in range(CMP_STRIDE):
            cmp_ref[l, :, c * LANES:(c + 1) * LANES] = (
                cscr_ref[c, pl.ds(l, CMP_ROWS, stride=CMP_STRIDE), :].astype(bf16))


def _inproj(x2, w_all, wvt, pw_bd, pool_scale, conv_w, kpads, layer, B, T):
    nt = T // TT
    bt = B * T
    row = lambda b, i: (b * nt + i, 0)
    wsel = lambda b, i: (layer, 0, 0)
    any_spec = pl.BlockSpec(memory_space=pl.ANY)
    tp = T + KPAD
    out_shape = [jax.ShapeDtypeStruct((bt, D_POOL + D_CONV), bf16),
                 jax.ShapeDtypeStruct((bt, D_NSA), bf16),
                 jax.ShapeDtypeStruct((bt, D_NSA), bf16),
                 jax.ShapeDtypeStruct((CMP_STRIDE, bt // CMP_STRIDE, 2 * D_KV), bf16),
                 jax.ShapeDtypeStruct((bt, LANES), f32),
                 jax.ShapeDtypeStruct((B, tp, D_KV), bf16),
                 jax.ShapeDtypeStruct((B, tp, 2 * D_KV), bf16),
                 jax.ShapeDtypeStruct((B, N_KV, VROWS, tp), bf16),
                 jax.ShapeDtypeStruct((B, N_KV, VROWS, tp), bf16)]
    out_specs = [pl.BlockSpec((TT, D_POOL + D_CONV), row),
                 pl.BlockSpec((TT, D_NSA), row),
                 pl.BlockSpec((TT, D_NSA), row),
                 pl.BlockSpec((CMP_STRIDE, CMP_ROWS, 2 * D_KV), lambda b, i: (0, b * nt + i, 0)),
                 pl.BlockSpec((TT, LANES), row),
                 pl.BlockSpec((None, TT, D_KV), lambda b, i: (b, i + KPAD // TT, 0)),
                 pl.BlockSpec((None, TT, 2 * D_KV), lambda b, i: (b, i + KPAD // TT, 0)),
                 pl.BlockSpec((None, N_KV, HEAD_DIM, TT), lambda b, i: (b, 0, 0, i + KPAD // TT)),
                 pl.BlockSpec((None, N_KV, HEAD_DIM, TT), lambda b, i: (b, 0, 0, i + KPAD // TT))]
    return pl.pallas_call(
        _inproj_kernel,
        grid=(B, nt),
        in_specs=[pl.BlockSpec((TT, D_MODEL), row),
                  pl.BlockSpec((None, D_MODEL, W_ALL), wsel),
                  pl.BlockSpec((None, 2 * D_KV, D_MODEL), wsel),
                  pl.BlockSpec((None, D_POOL, D_POOL), wsel),
                  pl.BlockSpec((None, 1, D_POOL), wsel),
                  pl.BlockSpec((None, CONV_WIDTH, D_CONV), wsel),
                  any_spec, any_spec, any_spec, any_spec],
        out_specs=out_specs,
        out_shape=out_shape,
        input_output_aliases={6: 5, 7: 6, 8: 7, 9: 8},
        scratch_shapes=[pltpu.VMEM((HALO, MIX_W), f32), pltpu.VMEM((2, TT, LANES), f32)],
        compiler_params=pltpu.CompilerParams(dimension_semantics=("arbitrary", "arbitrary"),
                                             vmem_limit_bytes=VMEM_LIMIT),
        name="inproj",
    )(x2, w_all, wvt, pw_bd, pool_scale, conv_w, *kpads)


CH = 4 * CMP_HIDDEN


def _compress_kernel(r_ref, pe_ref, w1_ref, w2_ref, w2t_ref, out_ref, outT_ref):
    nc = r_ref.shape[1]
    rows = jnp.concatenate([r_ref[l] for l in range(CMP_STRIDE)], axis=1)
    zz = _dot(rows, w1_ref[...])
    pb = _dot(pe_ref[...], w1_ref[...])
    z0 = zz[:, 0:CH] + pb[0:1, 0:CH]
    z1 = zz[:, CH:2 * CH] + pb[1:2, CH:2 * CH]
    h = z0 + pltpu.roll(z1, nc - 1, axis=0)
    h = (h * _sigmoid(h)).astype(bf16)
    out_ref[...] = _dot(h, w2_ref[...]).astype(bf16)
    outT_ref[...] = _dot_nt(w2t_ref[...], h).astype(bf16)


def _compress(cmp_l, pe_r, w1_big, w2_big, w2t_big, layer, B):
    nc = cmp_l.shape[1] // B
    rw = CMP_STRIDE * 2 * D_KV
    wsel = lambda b: (layer, 0, 0)
    return pl.pallas_call(
        _compress_kernel,
        grid=(B,),
        in_specs=[pl.BlockSpec((CMP_STRIDE, nc, 2 * D_KV), lambda b: (0, b, 0)),
                  pl.BlockSpec((None, 8, rw), wsel),
                  pl.BlockSpec((None, rw, 2 * CH), wsel),
                  pl.BlockSpec((None, CH, 2 * D_KV), wsel),
                  pl.BlockSpec((None, 2 * D_KV, CH), wsel)],
        out_specs=[pl.BlockSpec((None, nc, 2 * D_KV), lambda b: (b, 0, 0)),
                   pl.BlockSpec((None, 2 * D_KV, nc), lambda b: (b, 0, 0))],
        out_shape=[jax.ShapeDtypeStruct((B, nc, 2 * D_KV), bf16),
                   jax.ShapeDtypeStruct((B, 2 * D_KV, nc), bf16)],
        compiler_params=pltpu.CompilerParams(dimension_semantics=("arbitrary",),
                                             vmem_limit_bytes=VMEM_LIMIT),
        name="compress",
    )(cmp_l, pe_r, w1_big, w2_big, w2t_big)


def _select_penalty(imp_t, t0):
    shape = (NSB, N_KV * TQ)
    jblk = lax.broadcasted_iota(jnp.int32, shape, 0)
    tq = lax.broadcasted_iota(jnp.int32, shape, 1) % TQ
    back = jnp.right_shift(t0 + tq, int(math.log2(SLC_BLOCK))) - jblk
    causal = back >= 0
    forced = jnp.logical_or(jblk == 0, jnp.logical_and(causal, back < N_LOCAL))
    score = jnp.where(forced, -jnp.inf, jnp.where(causal, imp_t, NEG))
    jf = jblk.astype(f32)
    for _ in range(N_SELECT - N_LOCAL - 1):
        best = jnp.max(score, axis=0, keepdims=True)
        first = jnp.min(jnp.where(score == best, jf, float(NSB)), axis=0, keepdims=True)
        score = jnp.where(jf == first, -jnp.inf, score)
    return jnp.where(score == -jnp.inf, 0.0, NEG)


def _query_cols(q_ref):
    sub = lax.broadcasted_iota(jnp.int32, (LANES, TQ), 0)
    low = sub < HEAD_DIM
    pairs = [q_ref[:, LANES * p:LANES * (p + 1)].astype(f32).T for p in range(GROUP)]
    blocks = [(jnp.where(low, pairs[h % GROUP], 0.0) if h < GROUP else jnp.where(low, 0.0, pairs[h % GROUP])).astype(bf16)
              for h in range(N_HEADS)]
    return jnp.concatenate(blocks, axis=1)


def _merge_head_pairs(outs):
    return [jnp.concatenate([outs[p], outs[p + GROUP]], axis=0).T for p in range(GROUP)]


def _cmp_scores(q_cmp, cmp_ref, t0):
    nc = cmp_ref.shape[0]
    n_i = lax.broadcasted_iota(jnp.int32, (nc, LANES), 0)
    j_i = lax.broadcasted_iota(jnp.int32, (nc, LANES), 1)
    nstart = t0 // CMP_STRIDE - 8
    in_window = jnp.logical_and(j_i < 32, n_i == nstart + jnp.where(j_i < 16, j_i, j_i - 16))
    future = jnp.logical_and(j_i == 32, n_i >= nstart + 15)
    place_b = jnp.where(jnp.logical_or(in_window, future), 1.0, 0.0).astype(bf16)
    kc_ext = jnp.concatenate([cmp_ref[:, 0:D_KV], place_b], axis=1)
    return _dot(kc_ext, q_cmp)


def _cmp_finish(s1, g_ref, cmpT_ref, ovT_ref, pen_ref, y1_ref, t0):
    m1 = jnp.max(s1, axis=0, keepdims=True)
    p1 = jnp.exp2(s1 - m1)
    p1 = p1.astype(bf16)
    tcol = t0 + lax.broadcasted_iota(jnp.int32, (1, GC), 1) % TQ
    sig_t = _sigmoid(g_ref[...]).T
    outs, sums = [], []
    for g in range(N_KV):
        lhs = jnp.concatenate([cmpT_ref[D_KV + g * HEAD_DIM:D_KV + (g + 1) * HEAD_DIM, :], ovT_ref[...]], axis=0)
        both = _dot(lhs, p1[:, g * GC:(g + 1) * GC])
        l1 = both[HEAD_DIM + NSB:HEAD_DIM + NSB + 1]
        both = both * jnp.where(tcol >= CMP_BLOCK - 1, 1.0 / l1, 0.0)
        for r in range(GROUP):
            h = g * GROUP + r
            outs.append(sig_t[3 * h:3 * h + 1, :] * both[0:HEAD_DIM, r * TQ:(r + 1) * TQ])
        imp = both[HEAD_DIM:HEAD_DIM + NSB]
        acc = imp[:, 0:TQ]
        for r in range(1, GROUP):
            acc = acc + imp[:, r * TQ:(r + 1) * TQ]
        sums.append(acc)
    for p, y in enumerate(_merge_head_pairs(outs)):
        y1_ref[:, p * LANES:(p + 1) * LANES] = y.astype(bf16)
    imp_t = jnp.concatenate(sums, axis=1)
    pen_t = _select_penalty(imp_t, t0)
    pen_ref[...] = pen_t.astype(bf16)


def _nsa_kernel(q_ref, qn_ref, z_ref, g_ref, gn_ref, cmp_ref, cmpT_ref, ovT_ref, ks_ref, kw_ref, vsT_ref, vwT_ref,
                e_ref, tab_ref, acmp_ref, out_ref,
                pen_ref, y1_ref, m_ref, l_ref, acc_ref, sa_ref, sb_ref, mxa_ref, mxb_ref):
    qi = pl.program_id(1)
    nq = pl.num_programs(1)
    t0 = qi * TQ

    @pl.when(qi == 0)
    def _():
        q_cmp0 = jnp.concatenate([_query_cols(q_ref), acmp_ref[...]], axis=0)
        _cmp_finish(_cmp_scores(q_cmp0, cmp_ref, t0), g_ref, cmpT_ref, ovT_ref, pen_ref, y1_ref, t0)

    qcols = _query_cols(q_ref)
    q_cmp = jnp.concatenate([qcols, acmp_ref[...]], axis=0)
    pens = [pen_ref[:, g * TQ:(g + 1) * TQ] for g in range(N_KV) for _ in range(GROUP)]
    q_slc = jnp.concatenate([qcols, jnp.concatenate(pens, axis=1)], axis=0)
    y1 = y1_ref[...]

    m_ref[...] = jnp.full((1, R), NEG, f32)
    l_ref[...] = jnp.zeros((1, R), f32)
    acc_ref[...] = jnp.zeros((HEAD_DIM, R), f32)
    n_far = (qi + KT // TQ) // (KT // TQ) - 1
    first = (qi + 1) * TQ - n_far * KT

    def tile_row(i):
        return pl.multiple_of(first + i * KT, LANES)

    def produce(i, s_ref, mx_ref):
        r0 = tile_row(i)
        k_ext = jnp.concatenate([ks_ref[pl.ds(r0, KT), :], e_ref[pl.ds(r0, KT), :]], axis=1)
        s = _dot(k_ext, q_slc)
        s_ref[...] = s
        mx_ref[...] = jnp.max(s, axis=0, keepdims=True)

    def consume(i, s_ref, mx_ref, last):
        m_prev = m_ref[...]
        if last:
            s = jnp.concatenate([s_ref[0:KT - 2 * TQ], s_ref[KT - 2 * TQ:KT] + tab_ref[TQ:3 * TQ]], axis=0)
            m_new = jnp.maximum(m_prev, jnp.max(s, axis=0, keepdims=True))
        else:
            s = s_ref[...]
            m_new = jnp.maximum(m_prev, mx_ref[...])
        alpha = jnp.exp2(m_prev - m_new)
        p = jnp.exp2(s - m_new)
        p = p.astype(bf16)
        for g in range(N_KV):
            c = slice(g * GC, (g + 1) * GC)
            pv = _dot(vsT_ref[g, :, pl.ds(tile_row(i), KT)], p[:, c])
            l_ref[:, c] = alpha[:, c] * l_ref[:, c] + pv[HEAD_DIM:HEAD_DIM + 1]
            acc_ref[:, c] = acc_ref[:, c] * alpha[:, c] + pv[0:HEAD_DIM]
        m_ref[...] = m_new

    tn = jnp.minimum(qi + 1, nq - 1) * TQ
    s1 = _cmp_scores(jnp.concatenate([_query_cols(qn_ref), acmp_ref[...]], axis=0), cmp_ref, tn)
    w0 = pl.multiple_of(t0, LANES)
    s3 = _dot(kw_ref[pl.ds(w0, N_WIN), :], q_cmp)

    _cmp_finish(s1, gn_ref, cmpT_ref, ovT_ref, pen_ref, y1_ref, tn)
    produce(0, sa_ref, mxa_ref)
    s3 = jnp.concatenate([s3[0:TQ] + tab_ref[0:TQ], s3[TQ:WINDOW - TQ], s3[WINDOW - TQ:] + tab_ref[TQ:3 * TQ]], axis=0)
    m3 = jnp.max(s3, axis=0, keepdims=True)
    p3 = jnp.exp2(s3 - m3)
    p3 = p3.astype(bf16)
    o3 = jnp.concatenate([_dot(vwT_ref[g, :, pl.ds(w0, N_WIN)], p3[:, g * GC:(g + 1) * GC]) for g in range(N_KV)],
                         axis=1)
    inv3 = 1.0 / o3[HEAD_DIM:HEAD_DIM + 1]

    def pair(j, c):
        produce(2 * j + 1, sb_ref, mxb_ref)
        consume(2 * j, sa_ref, mxa_ref, False)
        produce(2 * j + 2, sa_ref, mxa_ref)
        consume(2 * j + 1, sb_ref, mxb_ref, False)
        return c

    lax.fori_loop(0, n_far // 2, pair, 0)

    def finish():
        inv2 = 1.0 / l_ref[...]
        sig_t = _sigmoid(g_ref[...]).T
        outs = []
        for h in range(N_HEADS):
            c = slice(h * TQ, (h + 1) * TQ)
            g2 = sig_t[3 * h + 1:3 * h + 2, :] * inv2[:, c]
            g3 = sig_t[3 * h + 2:3 * h + 3, :] * inv3[:, c]
            outs.append(g2 * acc_ref[:, c] + g3 * o3[0:HEAD_DIM, c])
        for p, y in enumerate(_merge_head_pairs(outs)):
            c = slice(p * LANES, (p + 1) * LANES)
            zb = z_ref[:, c].astype(f32)
            out_ref[:, c] = ((y + y1[:, c].astype(f32)) * zb * _sigmoid(zb)).astype(bf16)

    @pl.when(n_far % 2 == 0)
    def _():
        consume(n_far, sa_ref, mxa_ref, True)
        finish()

    @pl.when(n_far % 2 == 1)
    def _():
        produce(n_far, sb_ref, mxb_ref)
        consume(n_far - 1, sa_ref, mxa_ref, False)
        consume(n_far, sb_ref, mxb_ref, True)
        finish()


def _nsa(q, z, g, cmp, cmpT, ks, kw, vsT, vwT, e_pad, ovT, tabs, acmp, B, T):
    nq = T // TQ
    nc = T // CMP_STRIDE
    tp = T + KPAD
    tile = lambda b, i: (b * nq + i, 0)
    nxt = lambda b, i: (b * nq + jnp.minimum(i + 1, nq - 1), 0)
    const2 = lambda b, i: (0, 0)
    batch3 = lambda b, i: (b, 0, 0)
    return pl.pallas_call(
        _nsa_kernel,
        grid=(B, nq),
        in_specs=[pl.BlockSpec((TQ, D_NSA), tile),
                  pl.BlockSpec((TQ, D_NSA), nxt),
                  pl.BlockSpec((TQ, D_NSA), tile),
                  pl.BlockSpec((TQ, LANES), tile),
                  pl.BlockSpec((TQ, LANES), nxt),
                  pl.BlockSpec((None, nc, 2 * D_KV), batch3),
                  pl.BlockSpec((None, 2 * D_KV, nc), batch3),
                  pl.BlockSpec((NSB + SUM_ROWS, nc), const2),
                  pl.BlockSpec((None, tp, D_KV), batch3),
                  pl.BlockSpec((None, tp, 2 * D_KV), batch3),
                  pl.BlockSpec((None, N_KV, VROWS, tp), lambda b, i: (b, 0, 0, 0)),
                  pl.BlockSpec((None, N_KV, VROWS, tp), lambda b, i: (b, 0, 0, 0)),
                  pl.BlockSpec((tp, NSB), const2),
                  pl.BlockSpec((3 * TQ, R), const2),
                  pl.BlockSpec((LANES, R), const2)],
        out_specs=pl.BlockSpec((TQ, D_NSA), tile),
        out_shape=jax.ShapeDtypeStruct((B * T, D_NSA), bf16),
        scratch_shapes=[pltpu.VMEM((NSB, N_KV * TQ), bf16), pltpu.VMEM((TQ, D_NSA), bf16),
                        pltpu.VMEM((1, R), f32), pltpu.VMEM((1, R), f32), pltpu.VMEM((HEAD_DIM, R), f32),
                        pltpu.VMEM((KT, R), f32), pltpu.VMEM((KT, R), f32),
                        pltpu.VMEM((1, R), f32), pltpu.VMEM((1, R), f32)],
        compiler_params=pltpu.CompilerParams(dimension_semantics=("arbitrary", "arbitrary"),
                                             vmem_limit_bytes=VMEM_LIMIT),
        name="nsa",
    )(q, q, z, g, g, cmp, cmpT, ovT, ks, kw, vsT, vwT, e_pad, tabs, acmp)


def _outproj_kernel(alpha, ymix_ref, ynsa_ref, x_ref, w_ref, g_ref, b_ref, out_ref):
    y = jnp.concatenate([ymix_ref[...], ynsa_ref[...]], axis=1)
    r = alpha * x_ref[...] + _dot(y, w_ref[...])
    mu = jnp.mean(r, axis=-1, keepdims=True)
    d = r - mu
    var = jnp.mean(d * d, axis=-1, keepdims=True)
    out_ref[...] = d * lax.rsqrt(var + LN_EPS) * g_ref[...] + b_ref[...]


def _outproj(ymix, ynsa, x2, w_out, ln_g, ln_b, alpha, layer):
    bt = x2.shape[0]
    row = lambda i: (i, 0)
    wsel = lambda i: (layer, 0, 0)
    return pl.pallas_call(
        functools.partial(_outproj_kernel, alpha),
        grid=(bt // TT,),
        in_specs=[pl.BlockSpec((TT, D_POOL + D_CONV), row),
                  pl.BlockSpec((TT, D_NSA), row),
                  pl.BlockSpec((TT, D_MODEL), row),
                  pl.BlockSpec((None, D_MODEL, D_MODEL), wsel),
                  pl.BlockSpec((None, 1, D_MODEL), wsel),
                  pl.BlockSpec((None, 1, D_MODEL), wsel)],
        out_specs=pl.BlockSpec((TT, D_MODEL), row),
        out_shape=jax.ShapeDtypeStruct((bt, D_MODEL), f32),
        compiler_params=pltpu.CompilerParams(dimension_semantics=("arbitrary",),
                                             vmem_limit_bytes=VMEM_LIMIT),
        name="outproj",
    )(ymix, ynsa, x2, w_out, ln_g, ln_b)


def _bucket_np(d):
    d = np.asarray(d)
    max_exact = N_BUCKETS // 2
    nf = np.maximum(d, 1).astype(np.float32)
    large = max_exact + (np.log(nf / np.float32(max_exact)) / np.float32(math.log(MAX_DISTANCE / max_exact))
                         * np.float32(N_BUCKETS - max_exact)).astype(np.int32)
    large = np.minimum(large, N_BUCKETS - 1)
    return np.where(d < max_exact, d, large)


_FAR_DIST = 113
assert _bucket_np(np.arange(_FAR_DIST, 4 * WINDOW)).min() == N_BUCKETS - 1

_PAIR_PERM = np.concatenate([np.concatenate([np.arange(HEAD_DIM) + HEAD_DIM * p,
                                             np.arange(HEAD_DIM) + HEAD_DIM * (p + GROUP)]) for p in range(GROUP)])


def _bias_tables(rel_bias):
    nd = 2 * TQ
    onehot = jnp.asarray(np.eye(N_BUCKETS, dtype=np.float32)[_bucket_np(np.arange(nd))])
    tabp = ((jnp.dot(onehot, rel_bias, precision=lax.Precision.HIGHEST)
             - rel_bias[N_BUCKETS - 1:N_BUCKETS, :]) * LOG2E).T
    sl = np.arange(TQ)[:, None]
    tl = np.arange(TQ)[None, :]
    neg = lambda n: jnp.full((N_HEADS, n), NEG, f32)

    def skew(u, rows):
        period = u.shape[1]
        return jnp.tile(u, (1, rows))[:, :rows * (period - 1)].reshape(N_HEADS, rows, period - 1)

    def tile_layout(t):
        return t.transpose(1, 0, 2).reshape(TQ, R)

    diag = tile_layout(skew(jnp.concatenate([tabp[:, 0:TQ], neg(TQ)], axis=1), TQ)[:, :, :TQ])
    prev = tile_layout(skew(jnp.concatenate([tabp[:, TQ:nd], tabp[:, 0:TQ]], axis=1), TQ)[:, :, :TQ])
    edge = jnp.asarray(np.tile(np.where(sl > tl, 0.0, NEG).astype(np.float32), (1, N_HEADS)))
    tabs = jnp.concatenate([edge, prev, diag], axis=0).astype(f32)

    j = np.arange(16)[None, :]
    off = 8 * CMP_STRIDE - (CMP_BLOCK - 1)
    dc = np.arange(TQ)[:, None] - CMP_STRIDE * j + off
    okc = jnp.asarray(dc >= 0)[None]
    period = 4 * TQ
    u = jnp.concatenate([tabp[:, off:nd], jnp.zeros((N_HEADS, off), f32), neg(period - nd - off), tabp[:, 0:off]], axis=1)
    vc = skew(u, nd)[:, 0:16 * CMP_STRIDE:CMP_STRIDE, 0:TQ].transpose(0, 2, 1)
    vc = jnp.where(okc, vc, NEG)
    hi = vc.astype(bf16)
    lo = jnp.where(okc, vc - hi.astype(f32), 0.0).astype(bf16)
    fut = jnp.full((N_HEADS, TQ, 1), NEG, bf16)
    pad = jnp.zeros((N_HEADS, TQ, LANES - _PAD_FLAG_COL - 1), bf16)
    acmp = jnp.concatenate([hi, lo, fut, pad], axis=-1).reshape(R, LANES).T
    return tabs, acmp


def _static_tables(T):
    nc = T // CMP_STRIDE
    ns = T // SLC_BLOCK
    cstart = np.arange(nc)[None, :] * CMP_STRIDE
    sstart = np.arange(NSB)[:, None] * SLC_BLOCK
    ov = np.clip(np.minimum(cstart + CMP_BLOCK, sstart + SLC_BLOCK) - np.maximum(cstart, sstart), 0, None) / CMP_STRIDE
    ov[ns:, :] = 0
    ov[:, nc - 1] = 0
    ov = np.concatenate([ov, np.ones((SUM_ROWS, nc))], axis=0)
    e_pad = np.concatenate([np.ones((KPAD, NSB), bool),
                            np.arange(T)[:, None] // SLC_BLOCK == np.arange(NSB)[None, :]], axis=0)
    return jnp.asarray(ov, bf16), jnp.asarray(e_pad, bf16)


def _pair_order(w, axis):
    shp = w.shape
    w = w.reshape(shp[:axis] + (N_KV, GROUP, HEAD_DIM) + shp[axis + 1:])
    return jnp.swapaxes(w, axis, axis + 1).reshape(shp)


def _prep_weights(w_in, w_out, pool_w, pe_k, w1_k, w2_k, pe_v, w1_v, w2_v):
    depth = w_in.shape[0]
    sizes = (D_POOL, D_POOL, D_CONV, D_CONV, D_CONV, D_CONV, D_NSA, D_KV, D_KV, D_KV, D_KV, D_KV, D_KV,
             3 * N_HEADS, D_NSA)
    offs = np.cumsum((0,) + sizes)
    w_in = w_in.astype(bf16)
    col = lambda i: w_in[:, :, offs[i]:offs[i + 1]]
    wg = jnp.pad(col(13), ((0, 0), (0, 0), (0, LANES - 3 * N_HEADS)))
    w_all = jnp.concatenate([w_in[:, :, 0:MIX_W], _pair_order(col(6), 2), _pair_order(col(14), 2),
                             col(7), col(8), col(9), col(11), wg], axis=2)
    wvt = jnp.swapaxes(jnp.concatenate([col(10), col(12)], axis=2), 1, 2)
    w_out = w_out.astype(bf16)
    nm = D_POOL + D_CONV
    w_out_p = jnp.concatenate([w_out[:, 0:nm], _pair_order(w_out[:, nm:], 1)], axis=1)

    eye_g = jnp.eye(len(POOL_WINDOWS), dtype=bf16)
    pw_bd = jnp.einsum('zgcd,gh->zgchd', pool_w.astype(bf16), eye_g).reshape(depth, D_POOL, D_POOL)

    half = CMP_BLOCK // 2
    eye2 = jnp.eye(2, dtype=bf16)
    rw = half * 2 * D_KV
    cols = []
    for a in range(2):
        for kv, w1 in enumerate((w1_k, w1_v)):
            wsel = w1.astype(bf16).reshape(depth, 2, half, 1, 1, HEAD_DIM, CMP_HIDDEN)[:, a]
            for g in range(N_KV):
                blk = jnp.pad(wsel, ((0, 0), (0, 0), (kv, 1 - kv), (g, N_KV - 1 - g), (0, 0), (0, 0)))
                cols.append(blk.reshape(depth, rw, CMP_HIDDEN))
    w1_big = jnp.concatenate(cols, axis=2)
    pes = jnp.stack([pe_k, pe_v], axis=1).astype(bf16).reshape(depth, 2, 2, half, HEAD_DIM)
    pe_r = jnp.broadcast_to(pes.transpose(0, 2, 3, 1, 4)[:, :, :, :, None, :],
                            (depth, 2, half, 2, N_KV, HEAD_DIM)).reshape(depth, 2, half * 2 * D_KV)
    pe_r = jnp.pad(pe_r, ((0, 0), (0, 6), (0, 0)))
    w2s = jnp.stack([w2_k, w2_v], axis=1).astype(bf16)
    w2_big = jnp.einsum('zkhd,kK,gG->zkghKGd', w2s, eye2, eye2).reshape(depth, CH, 2 * D_KV)
    return w_all, wvt, w_out_p, pw_bd, w1_big, pe_r, w2_big, jnp.swapaxes(w2_big, 1, 2)


def _padded_kv_init(B, T):
    tp = T + KPAD
    flag = np.zeros((1, tp, 2 * D_KV), np.float32)
    flag[:, :KPAD, D_KV + _PAD_FLAG_COL] = 1.0
    ones_rows = np.zeros((1, 1, VROWS, 1), np.float32)
    ones_rows[:, :, HEAD_DIM:] = 1.0
    vinit = jnp.broadcast_to(jnp.asarray(ones_rows, bf16), (B, N_KV, VROWS, tp))
    return (jnp.zeros((B, tp, D_KV), bf16), jnp.broadcast_to(jnp.asarray(flag, bf16), (B, tp, 2 * D_KV)),
            vinit, vinit)


def kernel(x, w_in, w_out, pool_w, pool_scale, conv_w, cmp_pe_k, cmp_w1_k, cmp_w2_k, cmp_pe_v, cmp_w1_v, cmp_w2_v,
           rel_bias, ln_g, ln_b):
    B, T, D = x.shape
    depth = w_in.shape[0]
    assert D == D_MODEL and T % TT == 0 and T // SLC_BLOCK <= NSB and T // SLC_BLOCK >= N_SELECT
    assert KPAD % TT == 0 and N_SELECT > N_LOCAL
    alpha = (2 * depth) ** 0.25
    tabs, acmp = _bias_tables(rel_bias)
    ovT, e_pad = _static_tables(T)
    w_all, wvt, w_out_p, pw_bd, w1_big, pe_r, w2_big, w2t_big = _prep_weights(
        w_in, w_out, pool_w, cmp_pe_k, cmp_w1_k, cmp_w2_k, cmp_pe_v, cmp_w1_v, cmp_w2_v)
    pool_scale = pool_scale.reshape(depth, 1, D_POOL)
    ln_g = ln_g.reshape(depth, 1, D_MODEL)
    ln_b = ln_b.reshape(depth, 1, D_MODEL)
    h = x.reshape(B * T, D)
    for l in range(depth):
        ymix, q, z, cmp_l, g, ks, kw, vsT, vwT = _inproj(h, w_all, wvt, pw_bd, pool_scale, conv_w,
                                                         _padded_kv_init(B, T), l, B, T)
        cmp, cmpT = _compress(cmp_l, pe_r, w1_big, w2_big, w2t_big, l, B)
        ynsa = _nsa(q, z, g, cmp, cmpT, ks, kw, vsT, vwT, e_pad, ovT, tabs, acmp, B, T)
        h = _outproj(ymix, ynsa, h, w_out_p, ln_g, ln_b, alpha, l)
    return h.reshape(B, T, D)
```

```python
import functools
import math

import numpy as np
import jax
import jax.numpy as jnp
from jax import lax
from jax.experimental import pallas as pl
from jax.experimental.pallas import tpu as pltpu

f32 = jnp.float32
bf16 = jnp.bfloat16

D_MODEL = 1024
D_POOL = 256
D_CONV = 256
D_NSA = 512
HEAD_DIM = 64
N_HEADS = 8
N_KV = 2
GROUP = 4
D_KV = 128
POOL_GROUP = 64
POOL_WINDOWS = (2, 4, 8, 16)
CONV_WIDTH = 3
CMP_BLOCK = 32
CMP_STRIDE = 16
CMP_HIDDEN = 128
SLC_BLOCK = 64
N_SELECT = 16
N_LOCAL = 2
WINDOW = 512
N_BUCKETS = 32
MAX_DISTANCE = 128
LN_EPS = 1e-5
FORCED = 1e9
NEG = -1e30

LANES = 128
TQ = 128
R = N_HEADS * TQ
NSB = 128
TT = 512
TT_OUT = 1024
HALO = 16
KT = 512
KPAD = KT
N_WIN = WINDOW + TQ
SUM_ROWS = 16
VROWS = HEAD_DIM + SUM_ROWS
GC = R // N_KV
LOG2E = math.log2(math.e)
_PAD_FLAG_COL = 32
MIX_W = 2 * D_POOL + 4 * D_CONV
VMEM_LIMIT = 56 * 1024 * 1024

_NT = (((1,), (1,)), ((), ()))


def _dot(a, b):
    return jnp.dot(a, b, preferred_element_type=f32)


def _dot_nt(a, b):
    return lax.dot_general(a, b, _NT, preferred_element_type=f32)


def _sigmoid(x):
    return 1.0 / (1.0 + jnp.exp(-x))


_C_MIX = (0, MIX_W)
_C_Q = (_C_MIX[1], _C_MIX[1] + D_NSA)
_C_Z = (_C_Q[1], _C_Q[1] + D_NSA)
_C_CMP = (_C_Z[1], _C_Z[1] + 2 * D_KV)
_C_K = (_C_CMP[1], _C_CMP[1] + 2 * D_KV)
_C_G = (_C_K[1], _C_K[1] + LANES)
W_ALL = _C_G[1]
CMP_ROWS = TT // CMP_STRIDE


def _local_mixers(ext, i, pw_ref, ps_ref, cw_ref):
    e = ext[:, 0:D_POOL]
    s2 = e + pltpu.roll(e, 1, axis=0)
    s4 = s2 + pltpu.roll(s2, 2, axis=0)
    s8 = s4 + pltpu.roll(s4, 4, axis=0)
    s16 = s8 + pltpu.roll(s8, 8, axis=0)
    lane = lax.broadcasted_iota(jnp.int32, (TT, D_POOL), 1)
    grp = lane // POOL_GROUP
    wsum = jnp.where(grp == 0, s2[HALO:], jnp.where(grp == 1, s4[HALO:], jnp.where(grp == 2, s8[HALO:], s16[HALO:])))
    win = jnp.left_shift(2, grp)
    pos = i * TT + lax.broadcasted_iota(jnp.int32, (TT, D_POOL), 0)
    cnt = jnp.minimum(pos + 1, win).astype(f32)
    v = e[HALO:]
    pooled = wsum / cnt - v
    y_pool = _dot(pooled.astype(bf16), pw_ref[...]) * ps_ref[...]
    zp = ext[HALO:, D_POOL:2 * D_POOL]
    y_pool = y_pool * (zp * _sigmoid(zp))

    o = 2 * D_POOL
    cb = ext[HALO:, o:o + D_CONV]
    u = ext[:, o + D_CONV:o + 2 * D_CONV] * ext[:, o + 2 * D_CONV:o + 3 * D_CONV]
    zc = ext[HALO:, o + 3 * D_CONV:o + 4 * D_CONV]
    conv = cw_ref[CONV_WIDTH - 1:CONV_WIDTH, :] * u[HALO:]
    for k in range(CONV_WIDTH - 1):
        conv = conv + cw_ref[k:k + 1, :] * pltpu.roll(u, CONV_WIDTH - 1 - k, axis=0)[HALO:]
    y_conv = cb * conv * (zc * _sigmoid(zc))
    return y_pool, y_conv


def _inproj_kernel(x_ref, w_ref, wvt_ref, pw_ref, ps_ref, cw_ref, ks_in, kw_in, vs_in, vw_in,
                   ymix_ref, q_ref, z_ref, cmp_ref, g_ref, ks_ref, kw_ref, vsT_ref, vwT_ref,
                   halo_ref, cscr_ref):
    del ks_in, kw_in, vs_in, vw_in
    i = pl.program_id(1)
    x = x_ref[...].astype(bf16)

    def proj(c):
        return _dot(x, w_ref[:, c[0]:c[1]])

    mix = proj(_C_MIX)
    halo = jnp.where(i > 0, halo_ref[...], 0.0)
    y_pool, y_conv = _local_mixers(jnp.concatenate([halo, mix], axis=0), i, pw_ref, ps_ref, cw_ref)
    halo_ref[...] = mix[TT - HALO:]
    ymix_ref[:, 0:D_POOL] = y_pool.astype(bf16)
    ymix_ref[:, D_POOL:D_POOL + D_CONV] = y_conv.astype(bf16)

    q_ref[...] = (proj(_C_Q) * (HEAD_DIM ** -0.5 * LOG2E)).astype(bf16)
    z_ref[...] = proj(_C_Z).astype(bf16)
    g_ref[...] = proj(_C_G)
    kk = proj(_C_K).astype(bf16)
    ks_ref[...] = kk[:, 0:D_KV]
    kw_ref[:, 0:D_KV] = kk[:, D_KV:2 * D_KV]
    kw_ref[:, D_KV:2 * D_KV] = jnp.zeros((TT, D_KV), bf16)
    vt = _dot_nt(wvt_ref[...], x)
    for g in range(N_KV):
        vsT_ref[g] = vt[g * HEAD_DIM:(g + 1) * HEAD_DIM].astype(bf16)
        vwT_ref[g] = vt[D_KV + g * HEAD_DIM:D_KV + (g + 1) * HEAD_DIM].astype(bf16)

    kvc = proj(_C_CMP)
    for c in range(2):
        cscr_ref[c] = kvc[:, c * LANES:(c + 1) * LANES]
        for l in range(CMP_STRIDE):
            cmp_ref[l, :, c * LANES:(c + 1) * LANES] = (
                cscr_ref[c, pl.ds(l, CMP_ROWS, stride=CMP_STRIDE), :].astype(bf16))


def _inproj(x2, w_all, wvt, pw_bd, pool_scale, conv_w, kpads, layer, B, T):
    nt = T // TT
    bt = B * T
    row = lambda b, i: (b * nt + i, 0)
    wsel = lambda b, i: (layer, 0, 0)
    any_spec = pl.BlockSpec(memory_space=pl.ANY)
    tp = T + KPAD
    out_shape = [jax.ShapeDtypeStruct((bt, D_POOL + D_CONV), bf16),
                 jax.ShapeDtypeStruct((bt, D_NSA), bf16),
                 jax.ShapeDtypeStruct((bt, D_NSA), bf16),
                 jax.ShapeDtypeStruct((CMP_STRIDE, bt // CMP_STRIDE, 2 * D_KV), bf16),
                 jax.ShapeDtypeStruct((bt, LANES), f32),
                 jax.ShapeDtypeStruct((B, tp, D_KV), bf16),
                 jax.ShapeDtypeStruct((B, tp, 2 * D_KV), bf16),
                 jax.ShapeDtypeStruct((B, N_KV, VROWS, tp), bf16),
                 jax.ShapeDtypeStruct((B, N_KV, VROWS, tp), bf16)]
    out_specs = [pl.BlockSpec((TT, D_POOL + D_CONV), row),
                 pl.BlockSpec((TT, D_NSA), row),
                 pl.BlockSpec((TT, D_NSA), row),
                 pl.BlockSpec((CMP_STRIDE, CMP_ROWS, 2 * D_KV), lambda b, i: (0, b * nt + i, 0)),
                 pl.BlockSpec((TT, LANES), row),
                 pl.BlockSpec((None, TT, D_KV), lambda b, i: (b, i + KPAD // TT, 0)),
                 pl.BlockSpec((None, TT, 2 * D_KV), lambda b, i: (b, i + KPAD // TT, 0)),
                 pl.BlockSpec((None, N_KV, HEAD_DIM, TT), lambda b, i: (b, 0, 0, i + KPAD // TT)),
                 pl.BlockSpec((None, N_KV, HEAD_DIM, TT), lambda b, i: (b, 0, 0, i + KPAD // TT))]
    return pl.pallas_call(
        _inproj_kernel,
        grid=(B, nt),
        in_specs=[pl.BlockSpec((TT, D_MODEL), row),
                  pl.BlockSpec((None, D_MODEL, W_ALL), wsel),
                  pl.BlockSpec((None, 2 * D_KV, D_MODEL), wsel),
                  pl.BlockSpec((None, D_POOL, D_POOL), wsel),
                  pl.BlockSpec((None, 1, D_POOL), wsel),
                  pl.BlockSpec((None, CONV_WIDTH, D_CONV), wsel),
                  any_spec, any_spec, any_spec, any_spec],
        out_specs=out_specs,
        out_shape=out_shape,
        input_output_aliases={6: 5, 7: 6, 8: 7, 9: 8},
        scratch_shapes=[pltpu.VMEM((HALO, MIX_W), f32), pltpu.VMEM((2, TT, LANES), f32)],
        compiler_params=pltpu.CompilerParams(dimension_semantics=("arbitrary", "arbitrary"),
                                             vmem_limit_bytes=VMEM_LIMIT),
        name="inproj",
    )(x2, w_all, wvt, pw_bd, pool_scale, conv_w, *kpads)


CH = 4 * CMP_HIDDEN


def _compress_kernel(r_ref, pe_ref, w1_ref, w2_ref, w2t_ref, out_ref, outT_ref):
    nc = r_ref.shape[1]
    rows = jnp.concatenate([r_ref[l] for l in range(CMP_STRIDE)], axis=1)
    zz = _dot(rows, w1_ref[...])
    pb = _dot(pe_ref[...], w1_ref[...])
    z0 = zz[:, 0:CH] + pb[0:1, 0:CH]
    z1 = zz[:, CH:2 * CH] + pb[1:2, CH:2 * CH]
    h = z0 + pltpu.roll(z1, nc - 1, axis=0)
    h = (h * _sigmoid(h)).astype(bf16)
    out_ref[...] = _dot(h, w2_ref[...]).astype(bf16)
    outT_ref[...] = _dot_nt(w2t_ref[...], h).astype(bf16)


def _compress(cmp_l, pe_r, w1_big, w2_big, w2t_big, layer, B):
    nc = cmp_l.shape[1] // B
    rw = CMP_STRIDE * 2 * D_KV
    wsel = lambda b: (layer, 0, 0)
    return pl.pallas_call(
        _compress_kernel,
        grid=(B,),
        in_specs=[pl.BlockSpec((CMP_STRIDE, nc, 2 * D_KV), lambda b: (0, b, 0)),
                  pl.BlockSpec((None, 8, rw), wsel),
                  pl.BlockSpec((None, rw, 2 * CH), wsel),
                  pl.BlockSpec((None, CH, 2 * D_KV), wsel),
                  pl.BlockSpec((None, 2 * D_KV, CH), wsel)],
        out_specs=[pl.BlockSpec((None, nc, 2 * D_KV), lambda b: (b, 0, 0)),
                   pl.BlockSpec((None, 2 * D_KV, nc), lambda b: (b, 0, 0))],
        out_shape=[jax.ShapeDtypeStruct((B, nc, 2 * D_KV), bf16),
                   jax.ShapeDtypeStruct((B, 2 * D_KV, nc), bf16)],
        compiler_params=pltpu.CompilerParams(dimension_semantics=("arbitrary",),
                                             vmem_limit_bytes=VMEM_LIMIT),
        name="compress",
    )(cmp_l, pe_r, w1_big, w2_big, w2t_big)


def _select_penalty(imp_t, t0):
    shape = imp_t.shape
    jblk = lax.broadcasted_iota(jnp.int32, shape, 0)
    tq = lax.broadcasted_iota(jnp.int32, shape, 1) % TQ
    back = jnp.right_shift(t0 + tq, int(math.log2(SLC_BLOCK))) - jblk
    causal = back >= 0
    forced = jnp.logical_or(jblk == 0, jnp.logical_and(causal, back < N_LOCAL))
    score = jnp.where(forced, -jnp.inf, jnp.where(causal, imp_t, NEG))
    jf = jblk.astype(f32)
    for _ in range(N_SELECT - N_LOCAL - 1):
        best = jnp.max(score, axis=0, keepdims=True)
        first = jnp.min(jnp.where(score == best, jf, float(shape[0])), axis=0, keepdims=True)
        score = jnp.where(jf == first, -jnp.inf, score)
    return jnp.where(score == -jnp.inf, 0.0, NEG)


def _query_cols(q_ref):
    sub = lax.broadcasted_iota(jnp.int32, (LANES, TQ), 0)
    low = sub < HEAD_DIM
    pairs = [q_ref[:, LANES * p:LANES * (p + 1)].astype(f32).T for p in range(GROUP)]
    blocks = [(jnp.where(low, pairs[h % GROUP], 0.0) if h < GROUP else jnp.where(low, 0.0, pairs[h % GROUP])).astype(bf16)
              for h in range(N_HEADS)]
    return jnp.concatenate(blocks, axis=1)


def _merge_head_pairs(outs):
    return [jnp.concatenate([outs[p], outs[p + GROUP]], axis=0).T for p in range(GROUP)]


def _cmp_scores(q_cmp, cmp_ref, t0, nc):
    n_i = lax.broadcasted_iota(jnp.int32, (nc, LANES), 0)
    j_i = lax.broadcasted_iota(jnp.int32, (nc, LANES), 1)
    nstart = t0 // CMP_STRIDE - 8
    in_window = jnp.logical_and(j_i < 32, n_i == nstart + jnp.where(j_i < 16, j_i, j_i - 16))
    future = jnp.logical_and(j_i == 32, n_i >= nstart + 15)
    place_b = jnp.where(jnp.logical_or(in_window, future), 1.0, 0.0).astype(bf16)
    kc_ext = jnp.concatenate([cmp_ref[0:nc, 0:D_KV], place_b], axis=1)
    return _dot(kc_ext, q_cmp)


def _cmp_finish(s1, g_ref, cmpT_ref, ovT_ref, pen_ref, y1_ref, t0, nsel):
    nc = s1.shape[0]
    m1 = jnp.max(s1, axis=0, keepdims=True)
    p1 = jnp.exp2(s1 - m1)
    p1 = p1.astype(bf16)
    tcol = t0 + lax.broadcasted_iota(jnp.int32, (1, GC), 1) % TQ
    sig_t = _sigmoid(g_ref[...]).T
    outs, sums = [], []
    for g in range(N_KV):
        lhs = jnp.concatenate([cmpT_ref[D_KV + g * HEAD_DIM:D_KV + (g + 1) * HEAD_DIM, 0:nc], ovT_ref[0:nsel, 0:nc],
                               ovT_ref[NSB:NSB + SUM_ROWS, 0:nc]], axis=0)
        both = _dot(lhs, p1[:, g * GC:(g + 1) * GC])
        l1 = both[HEAD_DIM + nsel:HEAD_DIM + nsel + 1]
        both = both * jnp.where(tcol >= CMP_BLOCK - 1, 1.0 / l1, 0.0)
        for r in range(GROUP):
            h = g * GROUP + r
            outs.append(sig_t[3 * h:3 * h + 1, :] * both[0:HEAD_DIM, r * TQ:(r + 1) * TQ])
        imp = both[HEAD_DIM:HEAD_DIM + nsel]
        acc = imp[:, 0:TQ]
        for r in range(1, GROUP):
            acc = acc + imp[:, r * TQ:(r + 1) * TQ]
        sums.append(acc)
    for p, y in enumerate(_merge_head_pairs(outs)):
        y1_ref[:, p * LANES:(p + 1) * LANES] = y.astype(bf16)
    imp_t = jnp.concatenate(sums, axis=1)
    pen_ref[0:nsel] = _select_penalty(imp_t, t0).astype(bf16)
    if nsel < NSB:
        pen_ref[nsel:NSB] = jnp.full((NSB - nsel, N_KV * TQ), NEG, bf16)


def _nsa_kernel(q_ref, qn_ref, z_ref, g_ref, gn_ref, cmp_ref, cmpT_ref, ovT_ref, ks_ref, kw_ref, vsT_ref, vwT_ref,
                e_ref, tab_ref, acmp_ref, out_ref,
                pen_ref, y1_ref, o3_ref, m_ref, l_ref, acc_ref, sa_ref, sb_ref, mxa_ref, mxb_ref):
    qi = pl.program_id(1)
    nq = pl.num_programs(1)
    t0 = qi * TQ
    nc = cmp_ref.shape[0]
    nc_e, nsel_e = nc // 2, NSB // 2
    early_q = min(nc_e // (TQ // CMP_STRIDE), nsel_e // (TQ // SLC_BLOCK)) - 1

    @pl.when(qi == 0)
    def _():
        q_cmp0 = jnp.concatenate([_query_cols(q_ref), acmp_ref[...]], axis=0)
        _cmp_finish(_cmp_scores(q_cmp0, cmp_ref, t0, nc_e), g_ref, cmpT_ref, ovT_ref, pen_ref, y1_ref, t0, nsel_e)

    qcols = _query_cols(q_ref)
    q_cmp = jnp.concatenate([qcols, acmp_ref[...]], axis=0)
    pens = [pen_ref[:, g * TQ:(g + 1) * TQ] for g in range(N_KV) for _ in range(GROUP)]
    q_slc = jnp.concatenate([qcols, jnp.concatenate(pens, axis=1)], axis=0)
    y1 = y1_ref[...]

    m_ref[...] = jnp.full((1, R), NEG, f32)
    l_ref[...] = jnp.zeros((1, R), f32)
    acc_ref[...] = jnp.zeros((HEAD_DIM, R), f32)
    n_far = (qi + KT // TQ) // (KT // TQ) - 1
    first = (qi + 1) * TQ - n_far * KT

    def tile_row(i):
        return pl.multiple_of(first + i * KT, LANES)

    def produce(i, s_ref, mx_ref):
        r0 = tile_row(i)
        k_ext = jnp.concatenate([ks_ref[pl.ds(r0, KT), :], e_ref[pl.ds(r0, KT), :]], axis=1)
        s = _dot(k_ext, q_slc)
        s_ref[...] = s
        mx_ref[...] = jnp.max(s, axis=0, keepdims=True)

    def consume(i, s_ref, mx_ref, last):
        m_prev = m_ref[...]
        if last:
            s = jnp.concatenate([s_ref[0:KT - 2 * TQ], s_ref[KT - 2 * TQ:KT] + tab_ref[TQ:3 * TQ]], axis=0)
            m_new = jnp.maximum(m_prev, jnp.max(s, axis=0, keepdims=True))
        else:
            s = s_ref[...]
            m_new = jnp.maximum(m_prev, mx_ref[...])
        alpha = jnp.exp2(m_prev - m_new)
        p = jnp.exp2(s - m_new)
        p = p.astype(bf16)
        for g in range(N_KV):
            c = slice(g * GC, (g + 1) * GC)
            pv = _dot(vsT_ref[g, :, pl.ds(tile_row(i), KT)], p[:, c])
            l_ref[:, c] = alpha[:, c] * l_ref[:, c] + pv[HEAD_DIM:HEAD_DIM + 1]
            acc_ref[:, c] = acc_ref[:, c] * alpha[:, c] + pv[0:HEAD_DIM]
        m_ref[...] = m_new

    qn = jnp.minimum(qi + 1, nq - 1)

    def front(nc_use, nsel_use):
        tn = qn * TQ
        s1 = _cmp_scores(jnp.concatenate([_query_cols(qn_ref), acmp_ref[...]], axis=0), cmp_ref, tn, nc_use)
        w0 = pl.multiple_of(t0, LANES)
        s3 = _dot(kw_ref[pl.ds(w0, N_WIN), :], q_cmp)
        _cmp_finish(s1, gn_ref, cmpT_ref, ovT_ref, pen_ref, y1_ref, tn, nsel_use)
        produce(0, sa_ref, mxa_ref)
        s3 = jnp.concatenate([s3[0:TQ] + tab_ref[0:TQ], s3[TQ:WINDOW - TQ], s3[WINDOW - TQ:] + tab_ref[TQ:3 * TQ]],
                             axis=0)
        p3 = jnp.exp2(s3 - jnp.max(s3, axis=0, keepdims=True)).astype(bf16)
        for g in range(N_KV):
            c = slice(g * GC, (g + 1) * GC)
            o3_ref[:, c] = _dot(vwT_ref[g, :, pl.ds(w0, N_WIN)], p3[:, c])

    @pl.when(qn <= early_q)
    def _():
        front(nc_e, nsel_e)

    @pl.when(qn > early_q)
    def _():
        front(nc, NSB)

    def pair(j, c):
        produce(2 * j + 1, sb_ref, mxb_ref)
        consume(2 * j, sa_ref, mxa_ref, False)
        produce(2 * j + 2, sa_ref, mxa_ref)
        consume(2 * j + 1, sb_ref, mxb_ref, False)
        return c

    lax.fori_loop(0, n_far // 2, pair, 0)

    def finish():
        inv2 = 1.0 / l_ref[...]
        inv3 = 1.0 / o3_ref[HEAD_DIM:HEAD_DIM + 1, :]
        sig_t = _sigmoid(g_ref[...]).T
        outs = []
        for h in range(N_HEADS):
            c = slice(h * TQ, (h + 1) * TQ)
            g2 = sig_t[3 * h + 1:3 * h + 2, :] * inv2[:, c]
            g3 = sig_t[3 * h + 2:3 * h + 3, :] * inv3[:, c]
            outs.append(g2 * acc_ref[:, c] + g3 * o3_ref[0:HEAD_DIM, c])
        for p, y in enumerate(_merge_head_pairs(outs)):
            c = slice(p * LANES, (p + 1) * LANES)
            zb = z_ref[:, c].astype(f32)
            out_ref[:, c] = ((y + y1[:, c].astype(f32)) * zb * _sigmoid(zb)).astype(bf16)

    @pl.when(n_far % 2 == 0)
    def _():
        consume(n_far, sa_ref, mxa_ref, True)
        finish()

    @pl.when(n_far % 2 == 1)
    def _():
        produce(n_far, sb_ref, mxb_ref)
        consume(n_far - 1, sa_ref, mxa_ref, False)
        consume(n_far, sb_ref, mxb_ref, True)
        finish()


def _nsa(q, z, g, cmp, cmpT, ks, kw, vsT, vwT, e_pad, ovT, tabs, acmp, B, T):
    nq = T // TQ
    nc = T // CMP_STRIDE
    tp = T + KPAD
    tile = lambda b, i: (b * nq + i, 0)
    nxt = lambda b, i: (b * nq + jnp.minimum(i + 1, nq - 1), 0)
    const2 = lambda b, i: (0, 0)
    batch3 = lambda b, i: (b, 0, 0)
    return pl.pallas_call(
        _nsa_kernel,
        grid=(B, nq),
        in_specs=[pl.BlockSpec((TQ, D_NSA), tile),
                  pl.BlockSpec((TQ, D_NSA), nxt),
                  pl.BlockSpec((TQ, D_NSA), tile),
                  pl.BlockSpec((TQ, LANES), tile),
                  pl.BlockSpec((TQ, LANES), nxt),
                  pl.BlockSpec((None, nc, 2 * D_KV), batch3),
                  pl.BlockSpec((None, 2 * D_KV, nc), batch3),
                  pl.BlockSpec((NSB + SUM_ROWS, nc), const2),
                  pl.BlockSpec((None, tp, D_KV), batch3),
                  pl.BlockSpec((None, tp, 2 * D_KV), batch3),
                  pl.BlockSpec((None, N_KV, VROWS, tp), lambda b, i: (b, 0, 0, 0)),
                  pl.BlockSpec((None, N_KV, VROWS, tp), lambda b, i: (b, 0, 0, 0)),
                  pl.BlockSpec((tp, NSB), const2),
                  pl.BlockSpec((3 * TQ, R), const2),
                  pl.BlockSpec((LANES, R), const2)],
        out_specs=pl.BlockSpec((TQ, D_NSA), tile),
        out_shape=jax.ShapeDtypeStruct((B * T, D_NSA), bf16),
        scratch_shapes=[pltpu.VMEM((NSB, N_KV * TQ), bf16), pltpu.VMEM((TQ, D_NSA), bf16), pltpu.VMEM((VROWS, R), f32),
                        pltpu.VMEM((1, R), f32), pltpu.VMEM((1, R), f32), pltpu.VMEM((HEAD_DIM, R), f32),
                        pltpu.VMEM((KT, R), f32), pltpu.VMEM((KT, R), f32),
                        pltpu.VMEM((1, R), f32), pltpu.VMEM((1, R), f32)],
        compiler_params=pltpu.CompilerParams(dimension_semantics=("arbitrary", "arbitrary"),
                                             vmem_limit_bytes=VMEM_LIMIT),
        name="nsa",
    )(q, q, z, g, g, cmp, cmpT, ovT, ks, kw, vsT, vwT, e_pad, tabs, acmp)


def _outproj_kernel(alpha, ymix_ref, ynsa_ref, x_ref, w_ref, g_ref, b_ref, out_ref):
    y = jnp.concatenate([ymix_ref[...], ynsa_ref[...]], axis=1)
    r = alpha * x_ref[...] + _dot(y, w_ref[...])
    mu = jnp.mean(r, axis=-1, keepdims=True)
    d = r - mu
    var = jnp.mean(d * d, axis=-1, keepdims=True)
    out_ref[...] = d * lax.rsqrt(var + LN_EPS) * g_ref[...] + b_ref[...]


def _outproj(ymix, ynsa, x2, w_out, ln_g, ln_b, alpha, layer):
    bt = x2.shape[0]
    row = lambda i: (i, 0)
    wsel = lambda i: (layer, 0, 0)
    return pl.pallas_call(
        functools.partial(_outproj_kernel, alpha),
        grid=(bt // TT_OUT,),
        in_specs=[pl.BlockSpec((TT_OUT, D_POOL + D_CONV), row),
                  pl.BlockSpec((TT_OUT, D_NSA), row),
                  pl.BlockSpec((TT_OUT, D_MODEL), row),
                  pl.BlockSpec((None, D_MODEL, D_MODEL), wsel),
                  pl.BlockSpec((None, 1, D_MODEL), wsel),
                  pl.BlockSpec((None, 1, D_MODEL), wsel)],
        out_specs=pl.BlockSpec((TT_OUT, D_MODEL), row),
        out_shape=jax.ShapeDtypeStruct((bt, D_MODEL), f32),
        compiler_params=pltpu.CompilerParams(dimension_semantics=("arbitrary",),
                                             vmem_limit_bytes=VMEM_LIMIT),
        name="outproj",
    )(ymix, ynsa, x2, w_out, ln_g, ln_b)


def _bucket_np(d):
    d = np.asarray(d)
    max_exact = N_BUCKETS // 2
    nf = np.maximum(d, 1).astype(np.float32)
    large = max_exact + (np.log(nf / np.float32(max_exact)) / np.float32(math.log(MAX_DISTANCE / max_exact))
                         * np.float32(N_BUCKETS - max_exact)).astype(np.int32)
    large = np.minimum(large, N_BUCKETS - 1)
    return np.where(d < max_exact, d, large)


_FAR_DIST = 113
assert _bucket_np(np.arange(_FAR_DIST, 4 * WINDOW)).min() == N_BUCKETS - 1

_PAIR_PERM = np.concatenate([np.concatenate([np.arange(HEAD_DIM) + HEAD_DIM * p,
                                             np.arange(HEAD_DIM) + HEAD_DIM * (p + GROUP)]) for p in range(GROUP)])


def _bias_tables(rel_bias):
    nd = 2 * TQ
    onehot = jnp.asarray(np.eye(N_BUCKETS, dtype=np.float32)[_bucket_np(np.arange(nd))])
    tabp = ((jnp.dot(onehot, rel_bias, precision=lax.Precision.HIGHEST)
             - rel_bias[N_BUCKETS - 1:N_BUCKETS, :]) * LOG2E).T
    sl = np.arange(TQ)[:, None]
    tl = np.arange(TQ)[None, :]
    neg = lambda n: jnp.full((N_HEADS, n), NEG, f32)

    def skew(u, rows):
        period = u.shape[1]
        return jnp.tile(u, (1, rows))[:, :rows * (period - 1)].reshape(N_HEADS, rows, period - 1)

    def tile_layout(t):
        return t.transpose(1, 0, 2).reshape(TQ, R)

    diag = tile_layout(skew(jnp.concatenate([tabp[:, 0:TQ], neg(TQ)], axis=1), TQ)[:, :, :TQ])
    prev = tile_layout(skew(jnp.concatenate([tabp[:, TQ:nd], tabp[:, 0:TQ]], axis=1), TQ)[:, :, :TQ])
    edge = jnp.asarray(np.tile(np.where(sl > tl, 0.0, NEG).astype(np.float32), (1, N_HEADS)))
    tabs = jnp.concatenate([edge, prev, diag], axis=0).astype(f32)

    j = np.arange(16)[None, :]
    off = 8 * CMP_STRIDE - (CMP_BLOCK - 1)
    dc = np.arange(TQ)[:, None] - CMP_STRIDE * j + off
    okc = jnp.asarray(dc >= 0)[None]
    period = 4 * TQ
    u = jnp.concatenate([tabp[:, off:nd], jnp.zeros((N_HEADS, off), f32), neg(period - nd - off), tabp[:, 0:off]], axis=1)
    vc = skew(u, nd)[:, 0:16 * CMP_STRIDE:CMP_STRIDE, 0:TQ].transpose(0, 2, 1)
    vc = jnp.where(okc, vc, NEG)
    hi = vc.astype(bf16)
    lo = jnp.where(okc, vc - hi.astype(f32), 0.0).astype(bf16)
    fut = jnp.full((N_HEADS, TQ, 1), NEG, bf16)
    pad = jnp.zeros((N_HEADS, TQ, LANES - _PAD_FLAG_COL - 1), bf16)
    acmp = jnp.concatenate([hi, lo, fut, pad], axis=-1).reshape(R, LANES).T
    return tabs, acmp


def _static_tables(T):
    nc = T // CMP_STRIDE
    ns = T // SLC_BLOCK
    cstart = np.arange(nc)[None, :] * CMP_STRIDE
    sstart = np.arange(NSB)[:, None] * SLC_BLOCK
    ov = np.clip(np.minimum(cstart + CMP_BLOCK, sstart + SLC_BLOCK) - np.maximum(cstart, sstart), 0, None) / CMP_STRIDE
    ov[ns:, :] = 0
    ov[:, nc - 1] = 0
    ov = np.concatenate([ov, np.ones((SUM_ROWS, nc))], axis=0)
    e_pad = np.concatenate([np.ones((KPAD, NSB), bool),
                            np.arange(T)[:, None] // SLC_BLOCK == np.arange(NSB)[None, :]], axis=0)
    return jnp.asarray(ov, bf16), jnp.asarray(e_pad, bf16)


def _pair_order(w, axis):
    shp = w.shape
    w = w.reshape(shp[:axis] + (N_KV, GROUP, HEAD_DIM) + shp[axis + 1:])
    return jnp.swapaxes(w, axis, axis + 1).reshape(shp)


def _prep_weights(w_in, w_out, pool_w, pe_k, w1_k, w2_k, pe_v, w1_v, w2_v):
    depth = w_in.shape[0]
    sizes = (D_POOL, D_POOL, D_CONV, D_CONV, D_CONV, D_CONV, D_NSA, D_KV, D_KV, D_KV, D_KV, D_KV, D_KV,
             3 * N_HEADS, D_NSA)
    offs = np.cumsum((0,) + sizes)
    w_in = w_in.astype(bf16)
    col = lambda i: w_in[:, :, offs[i]:offs[i + 1]]
    wg = jnp.pad(col(13), ((0, 0), (0, 0), (0, LANES - 3 * N_HEADS)))
    w_all = jnp.concatenate([w_in[:, :, 0:MIX_W], _pair_order(col(6), 2), _pair_order(col(14), 2),
                             col(7), col(8), col(9), col(11), wg], axis=2)
    wvt = jnp.swapaxes(jnp.concatenate([col(10), col(12)], axis=2), 1, 2)
    w_out = w_out.astype(bf16)
    nm = D_POOL + D_CONV
    w_out_p = jnp.concatenate([w_out[:, 0:nm], _pair_order(w_out[:, nm:], 1)], axis=1)

    eye_g = jnp.eye(len(POOL_WINDOWS), dtype=bf16)
    pw_bd = jnp.einsum('zgcd,gh->zgchd', pool_w.astype(bf16), eye_g).reshape(depth, D_POOL, D_POOL)

    half = CMP_BLOCK // 2
    eye2 = jnp.eye(2, dtype=bf16)
    rw = half * 2 * D_KV
    cols = []
    for a in range(2):
        for kv, w1 in enumerate((w1_k, w1_v)):
            wsel = w1.astype(bf16).reshape(depth, 2, half, 1, 1, HEAD_DIM, CMP_HIDDEN)[:, a]
            for g in range(N_KV):
                blk = jnp.pad(wsel, ((0, 0), (0, 0), (kv, 1 - kv), (g, N_KV - 1 - g), (0, 0), (0, 0)))
                cols.append(blk.reshape(depth, rw, CMP_HIDDEN))
    w1_big = jnp.concatenate(cols, axis=2)
    pes = jnp.stack([pe_k, pe_v], axis=1).astype(bf16).reshape(depth, 2, 2, half, HEAD_DIM)
    pe_r = jnp.broadcast_to(pes.transpose(0, 2, 3, 1, 4)[:, :, :, :, None, :],
                            (depth, 2, half, 2, N_KV, HEAD_DIM)).reshape(depth, 2, half * 2 * D_KV)
    pe_r = jnp.pad(pe_r, ((0, 0), (0, 6), (0, 0)))
    w2s = jnp.stack([w2_k, w2_v], axis=1).astype(bf16)
    w2_big = jnp.einsum('zkhd,kK,gG->zkghKGd', w2s, eye2, eye2).reshape(depth, CH, 2 * D_KV)
    return w_all, wvt, w_out_p, pw_bd, w1_big, pe_r, w2_big, jnp.swapaxes(w2_big, 1, 2)


def _padded_kv_init(B, T):
    tp = T + KPAD
    flag = np.zeros((1, tp, 2 * D_KV), np.float32)
    flag[:, :KPAD, D_KV + _PAD_FLAG_COL] = 1.0
    ones_rows = np.zeros((1, 1, VROWS, 1), np.float32)
    ones_rows[:, :, HEAD_DIM:] = 1.0
    vinit = jnp.broadcast_to(jnp.asarray(ones_rows, bf16), (B, N_KV, VROWS, tp))
    return (jnp.zeros((B, tp, D_KV), bf16), jnp.broadcast_to(jnp.asarray(flag, bf16), (B, tp, 2 * D_KV)),
            vinit, vinit)


def kernel(x, w_in, w_out, pool_w, pool_scale, conv_w, cmp_pe_k, cmp_w1_k, cmp_w2_k, cmp_pe_v, cmp_w1_v, cmp_w2_v,
           rel_bias, ln_g, ln_b):
    B, T, D = x.shape
    depth = w_in.shape[0]
    assert D == D_MODEL and T % TT == 0 and T // SLC_BLOCK <= NSB and T // SLC_BLOCK >= N_SELECT
    assert KPAD % TT == 0 and N_SELECT > N_LOCAL
    alpha = (2 * depth) ** 0.25
    tabs, acmp = _bias_tables(rel_bias)
    ovT, e_pad = _static_tables(T)
    w_all, wvt, w_out_p, pw_bd, w1_big, pe_r, w2_big, w2t_big = _prep_weights(
        w_in, w_out, pool_w, cmp_pe_k, cmp_w1_k, cmp_w2_k, cmp_pe_v, cmp_w1_v, cmp_w2_v)
    pool_scale = pool_scale.reshape(depth, 1, D_POOL)
    ln_g = ln_g.reshape(depth, 1, D_MODEL)
    ln_b = ln_b.reshape(depth, 1, D_MODEL)
    h = x.reshape(B * T, D)
    for l in range(depth):
        ymix, q, z, cmp_l, g, ks, kw, vsT, vwT = _inproj(h, w_all, wvt, pw_bd, pool_scale, conv_w,
                                                         _padded_kv_init(B, T), l, B, T)
        cmp, cmpT = _compress(cmp_l, pe_r, w1_big, w2_big, w2t_big, l, B)
        ynsa = _nsa(q, z, g, cmp, cmpT, ks, kw, vsT, vwT, e_pad, ovT, tabs, acmp, B, T)
        h = _outproj(ymix, ynsa, h, w_out_p, ln_g, ln_b, alpha, l)
    return h.reshape(B, T, D)
```

```python
import functools
import math

import numpy as np
import jax
import jax.numpy as jnp
from jax import lax
from jax.experimental import pallas as pl
from jax.experimental.pallas import tpu as pltpu

f32 = jnp.float32
bf16 = jnp.bfloat16

D_MODEL = 1024
D_POOL = 256
D_CONV = 256
D_NSA = 512
HEAD_DIM = 64
N_HEADS = 8
N_KV = 2
GROUP = 4
D_KV = 128
POOL_GROUP = 64
POOL_WINDOWS = (2, 4, 8, 16)
CONV_WIDTH = 3
CMP_BLOCK = 32
CMP_STRIDE = 16
CMP_HIDDEN = 128
SLC_BLOCK = 64
N_SELECT = 16
N_LOCAL = 2
WINDOW = 512
N_BUCKETS = 32
MAX_DISTANCE = 128
LN_EPS = 1e-5
FORCED = 1e9
NEG = -1e30

LANES = 128
TQ = 128
R = N_HEADS * TQ
NSB = 128
TT = 512
TT_OUT = 1024
N_CLASSES = 4
HALO = 16
KT = 512
KPAD = KT
N_WIN = WINDOW + TQ
SUM_ROWS = 16
VROWS = HEAD_DIM + SUM_ROWS
GC = R // N_KV
LOG2E = math.log2(math.e)
_PAD_FLAG_COL = 32
MIX_W = 2 * D_POOL + 4 * D_CONV
VMEM_LIMIT = 56 * 1024 * 1024

_NT = (((1,), (1,)), ((), ()))


def _dot(a, b):
    return jnp.dot(a, b, preferred_element_type=f32)


def _dot_nt(a, b):
    return lax.dot_general(a, b, _NT, preferred_element_type=f32)


def _sigmoid(x):
    return 1.0 / (1.0 + jnp.exp(-x))


_C_MIX = (0, MIX_W)
_C_Q = (_C_MIX[1], _C_MIX[1] + D_NSA)
_C_Z = (_C_Q[1], _C_Q[1] + D_NSA)
_C_CMP = (_C_Z[1], _C_Z[1] + 2 * D_KV)
_C_K = (_C_CMP[1], _C_CMP[1] + 2 * D_KV)
_C_G = (_C_K[1], _C_K[1] + LANES)
W_ALL = _C_G[1]
CMP_ROWS = TT // CMP_STRIDE


def _local_mixers(ext, i, pw_ref, ps_ref, cw_ref):
    e = ext[:, 0:D_POOL]
    s2 = e + pltpu.roll(e, 1, axis=0)
    s4 = s2 + pltpu.roll(s2, 2, axis=0)
    s8 = s4 + pltpu.roll(s4, 4, axis=0)
    s16 = s8 + pltpu.roll(s8, 8, axis=0)
    lane = lax.broadcasted_iota(jnp.int32, (TT, D_POOL), 1)
    grp = lane // POOL_GROUP
    wsum = jnp.where(grp == 0, s2[HALO:], jnp.where(grp == 1, s4[HALO:], jnp.where(grp == 2, s8[HALO:], s16[HALO:])))
    win = jnp.left_shift(2, grp)
    pos = i * TT + lax.broadcasted_iota(jnp.int32, (TT, D_POOL), 0)
    cnt = jnp.minimum(pos + 1, win).astype(f32)
    v = e[HALO:]
    pooled = wsum / cnt - v
    y_pool = _dot(pooled.astype(bf16), pw_ref[...]) * ps_ref[...]
    zp = ext[HALO:, D_POOL:2 * D_POOL]
    y_pool = y_pool * (zp * _sigmoid(zp))

    o = 2 * D_POOL
    cb = ext[HALO:, o:o + D_CONV]
    u = ext[:, o + D_CONV:o + 2 * D_CONV] * ext[:, o + 2 * D_CONV:o + 3 * D_CONV]
    zc = ext[HALO:, o + 3 * D_CONV:o + 4 * D_CONV]
    conv = cw_ref[CONV_WIDTH - 1:CONV_WIDTH, :] * u[HALO:]
    for k in range(CONV_WIDTH - 1):
        conv = conv + cw_ref[k:k + 1, :] * pltpu.roll(u, CONV_WIDTH - 1 - k, axis=0)[HALO:]
    y_conv = cb * conv * (zc * _sigmoid(zc))
    return y_pool, y_conv


def _inproj_kernel(x_ref, w_ref, wvt_ref, pw_ref, ps_ref, cw_ref, ks_in, kw_in, vs_in, vw_in,
                   ymix_ref, q_ref, z_ref, cmp_ref, g_ref, ks_ref, kw_ref, vsT_ref, vwT_ref,
                   halo_ref, cscr_ref):
    del ks_in, kw_in, vs_in, vw_in
    i = pl.program_id(1)
    x = x_ref[...].astype(bf16)

    def proj(c):
        return _dot(x, w_ref[:, c[0]:c[1]])

    mix = proj(_C_MIX)
    halo = jnp.where(i > 0, halo_ref[...], 0.0)
    y_pool, y_conv = _local_mixers(jnp.concatenate([halo, mix], axis=0), i, pw_ref, ps_ref, cw_ref)
    halo_ref[...] = mix[TT - HALO:]
    ymix_ref[:, 0:D_POOL] = y_pool.astype(bf16)
    ymix_ref[:, D_POOL:D_POOL + D_CONV] = y_conv.astype(bf16)

    q_ref[...] = (proj(_C_Q) * (HEAD_DIM ** -0.5 * LOG2E)).astype(bf16)
    z_ref[...] = proj(_C_Z).astype(bf16)
    g_ref[...] = proj(_C_G)
    kk = proj(_C_K).astype(bf16)
    ks_ref[...] = kk[:, 0:D_KV]
    kw_ref[:, 0:D_KV] = kk[:, D_KV:2 * D_KV]
    kw_ref[:, D_KV:2 * D_KV] = jnp.zeros((TT, D_KV), bf16)
    vt = _dot_nt(wvt_ref[...], x)
    for g in range(N_KV):
        vsT_ref[g] = vt[g * HEAD_DIM:(g + 1) * HEAD_DIM].astype(bf16)
        vwT_ref[g] = vt[D_KV + g * HEAD_DIM:D_KV + (g + 1) * HEAD_DIM].astype(bf16)

    kvc = proj(_C_CMP)
    for c in range(2):
        cscr_ref[c] = kvc[:, c * LANES:(c + 1) * LANES]
        for l in range(CMP_STRIDE):
            cmp_ref[l, :, c * LANES:(c + 1) * LANES] = (
                cscr_ref[c, pl.ds(l, CMP_ROWS, stride=CMP_STRIDE), :].astype(bf16))


def _inproj(x2, w_all, wvt, pw_bd, pool_scale, conv_w, kpads, layer, B, T):
    nt = T // TT
    bt = B * T
    row = lambda b, i: (b * nt + i, 0)
    wsel = lambda b, i: (layer, 0, 0)
    any_spec = pl.BlockSpec(memory_space=pl.ANY)
    tp = T + KPAD
    out_shape = [jax.ShapeDtypeStruct((bt, D_POOL + D_CONV), bf16),
                 jax.ShapeDtypeStruct((bt, D_NSA), bf16),
                 jax.ShapeDtypeStruct((bt, D_NSA), bf16),
                 jax.ShapeDtypeStruct((CMP_STRIDE, bt // CMP_STRIDE, 2 * D_KV), bf16),
                 jax.ShapeDtypeStruct((bt, LANES), f32),
                 jax.ShapeDtypeStruct((B, tp, D_KV), bf16),
                 jax.ShapeDtypeStruct((B, tp, 2 * D_KV), bf16),
                 jax.ShapeDtypeStruct((B, N_KV, VROWS, tp), bf16),
                 jax.ShapeDtypeStruct((B, N_KV, VROWS, tp), bf16)]
    out_specs = [pl.BlockSpec((TT, D_POOL + D_CONV), row),
                 pl.BlockSpec((TT, D_NSA), row),
                 pl.BlockSpec((TT, D_NSA), row),
                 pl.BlockSpec((CMP_STRIDE, CMP_ROWS, 2 * D_KV), lambda b, i: (0, b * nt + i, 0)),
                 pl.BlockSpec((TT, LANES), row),
                 pl.BlockSpec((None, TT, D_KV), lambda b, i: (b, i + KPAD // TT, 0)),
                 pl.BlockSpec((None, TT, 2 * D_KV), lambda b, i: (b, i + KPAD // TT, 0)),
                 pl.BlockSpec((None, N_KV, HEAD_DIM, TT), lambda b, i: (b, 0, 0, i + KPAD // TT)),
                 pl.BlockSpec((None, N_KV, HEAD_DIM, TT), lambda b, i: (b, 0, 0, i + KPAD // TT))]
    return pl.pallas_call(
        _inproj_kernel,
        grid=(B, nt),
        in_specs=[pl.BlockSpec((TT, D_MODEL), row),
                  pl.BlockSpec((None, D_MODEL, W_ALL), wsel),
                  pl.BlockSpec((None, 2 * D_KV, D_MODEL), wsel),
                  pl.BlockSpec((None, D_POOL, D_POOL), wsel),
                  pl.BlockSpec((None, 1, D_POOL), wsel),
                  pl.BlockSpec((None, CONV_WIDTH, D_CONV), wsel),
                  any_spec, any_spec, any_spec, any_spec],
        out_specs=out_specs,
        out_shape=out_shape,
        input_output_aliases={6: 5, 7: 6, 8: 7, 9: 8},
        scratch_shapes=[pltpu.VMEM((HALO, MIX_W), f32), pltpu.VMEM((2, TT, LANES), f32)],
        compiler_params=pltpu.CompilerParams(dimension_semantics=("arbitrary", "arbitrary"),
                                             vmem_limit_bytes=VMEM_LIMIT),
        name="inproj",
    )(x2, w_all, wvt, pw_bd, pool_scale, conv_w, *kpads)


CH = 4 * CMP_HIDDEN


def _compress_kernel(r_ref, pe_ref, w1_ref, w2_ref, w2t_ref, out_ref, outT_ref):
    nc = r_ref.shape[1]
    rows = jnp.concatenate([r_ref[l] for l in range(CMP_STRIDE)], axis=1)
    zz = _dot(rows, w1_ref[...])
    pb = _dot(pe_ref[...], w1_ref[...])
    z0 = zz[:, 0:CH] + pb[0:1, 0:CH]
    z1 = zz[:, CH:2 * CH] + pb[1:2, CH:2 * CH]
    h = z0 + pltpu.roll(z1, nc - 1, axis=0)
    h = (h * _sigmoid(h)).astype(bf16)
    out_ref[...] = _dot(h, w2_ref[...]).astype(bf16)
    outT_ref[...] = _dot_nt(w2t_ref[...], h).astype(bf16)


def _compress(cmp_l, pe_r, w1_big, w2_big, w2t_big, layer, B):
    nc = cmp_l.shape[1] // B
    rw = CMP_STRIDE * 2 * D_KV
    wsel = lambda b: (layer, 0, 0)
    return pl.pallas_call(
        _compress_kernel,
        grid=(B,),
        in_specs=[pl.BlockSpec((CMP_STRIDE, nc, 2 * D_KV), lambda b: (0, b, 0)),
                  pl.BlockSpec((None, 8, rw), wsel),
                  pl.BlockSpec((None, rw, 2 * CH), wsel),
                  pl.BlockSpec((None, CH, 2 * D_KV), wsel),
                  pl.BlockSpec((None, 2 * D_KV, CH), wsel)],
        out_specs=[pl.BlockSpec((None, nc, 2 * D_KV), lambda b: (b, 0, 0)),
                   pl.BlockSpec((None, 2 * D_KV, nc), lambda b: (b, 0, 0))],
        out_shape=[jax.ShapeDtypeStruct((B, nc, 2 * D_KV), bf16),
                   jax.ShapeDtypeStruct((B, 2 * D_KV, nc), bf16)],
        compiler_params=pltpu.CompilerParams(dimension_semantics=("arbitrary",),
                                             vmem_limit_bytes=VMEM_LIMIT),
        name="compress",
    )(cmp_l, pe_r, w1_big, w2_big, w2t_big)


def _select_penalty(imp_t, t0):
    shape = imp_t.shape
    jblk = lax.broadcasted_iota(jnp.int32, shape, 0)
    tq = lax.broadcasted_iota(jnp.int32, shape, 1) % TQ
    back = jnp.right_shift(t0 + tq, int(math.log2(SLC_BLOCK))) - jblk
    causal = back >= 0
    forced = jnp.logical_or(jblk == 0, jnp.logical_and(causal, back < N_LOCAL))
    score = jnp.where(forced, -jnp.inf, jnp.where(causal, imp_t, NEG))
    jf = jblk.astype(f32)
    for _ in range(N_SELECT - N_LOCAL - 1):
        best = jnp.max(score, axis=0, keepdims=True)
        first = jnp.min(jnp.where(score == best, jf, float(shape[0])), axis=0, keepdims=True)
        score = jnp.where(jf == first, -jnp.inf, score)
    return jnp.where(score == -jnp.inf, 0.0, NEG)


def _query_cols(q_ref):
    sub = lax.broadcasted_iota(jnp.int32, (LANES, TQ), 0)
    low = sub < HEAD_DIM
    pairs = [q_ref[:, LANES * p:LANES * (p + 1)].astype(f32).T for p in range(GROUP)]
    blocks = [(jnp.where(low, pairs[h % GROUP], 0.0) if h < GROUP else jnp.where(low, 0.0, pairs[h % GROUP])).astype(bf16)
              for h in range(N_HEADS)]
    return jnp.concatenate(blocks, axis=1)


def _merge_head_pairs(outs):
    return [jnp.concatenate([outs[p], outs[p + GROUP]], axis=0).T for p in range(GROUP)]


def _cmp_scores(q_cmp, cmp_ref, t0, nc):
    n_i = lax.broadcasted_iota(jnp.int32, (nc, LANES), 0)
    j_i = lax.broadcasted_iota(jnp.int32, (nc, LANES), 1)
    nstart = t0 // CMP_STRIDE - 8
    in_window = jnp.logical_and(j_i < 32, n_i == nstart + jnp.where(j_i < 16, j_i, j_i - 16))
    future = jnp.logical_and(j_i == 32, n_i >= nstart + 15)
    place_b = jnp.where(jnp.logical_or(in_window, future), 1.0, 0.0).astype(bf16)
    kc_ext = jnp.concatenate([cmp_ref[0:nc, 0:D_KV], place_b], axis=1)
    return _dot(kc_ext, q_cmp)


def _cmp_finish(s1, g_ref, cmpT_ref, ovT_ref, pen_ref, y1_ref, t0, nsel):
    nc = s1.shape[0]
    m1 = jnp.max(s1, axis=0, keepdims=True)
    p1 = jnp.exp2(s1 - m1)
    p1 = p1.astype(bf16)
    tcol = t0 + lax.broadcasted_iota(jnp.int32, (1, GC), 1) % TQ
    sig_t = _sigmoid(g_ref[...]).T
    outs, sums = [], []
    for g in range(N_KV):
        lhs = jnp.concatenate([cmpT_ref[D_KV + g * HEAD_DIM:D_KV + (g + 1) * HEAD_DIM, 0:nc], ovT_ref[0:nsel, 0:nc],
                               ovT_ref[NSB:NSB + SUM_ROWS, 0:nc]], axis=0)
        both = _dot(lhs, p1[:, g * GC:(g + 1) * GC])
        l1 = both[HEAD_DIM + nsel:HEAD_DIM + nsel + 1]
        both = both * jnp.where(tcol >= CMP_BLOCK - 1, 1.0 / l1, 0.0)
        for r in range(GROUP):
            h = g * GROUP + r
            outs.append(sig_t[3 * h:3 * h + 1, :] * both[0:HEAD_DIM, r * TQ:(r + 1) * TQ])
        imp = both[HEAD_DIM:HEAD_DIM + nsel]
        acc = imp[:, 0:TQ]
        for r in range(1, GROUP):
            acc = acc + imp[:, r * TQ:(r + 1) * TQ]
        sums.append(acc)
    for p, y in enumerate(_merge_head_pairs(outs)):
        y1_ref[:, p * LANES:(p + 1) * LANES] = y.astype(bf16)
    imp_t = jnp.concatenate(sums, axis=1)
    pen_ref[0:nsel] = _select_penalty(imp_t, t0).astype(bf16)
    if nsel < NSB:
        pen_ref[nsel:NSB] = jnp.full((NSB - nsel, N_KV * TQ), NEG, bf16)


def _nsa_kernel(q_ref, qn_ref, z_ref, g_ref, gn_ref, cmp_ref, cmpT_ref, ovT_ref, ks_ref, kw_ref, vsT_ref, vwT_ref,
                e_ref, tab_ref, acmp_ref, out_ref,
                pen_ref, y1_ref, o3_ref, m_ref, l_ref, acc_ref, sa_ref, sb_ref, mxa_ref, mxb_ref):
    qi = pl.program_id(1)
    nq = pl.num_programs(1)
    t0 = qi * TQ
    nc = cmp_ref.shape[0]
    classes = [(nc * k // N_CLASSES, NSB * k // N_CLASSES) for k in range(1, N_CLASSES + 1)]
    last_q = [min(c // (TQ // CMP_STRIDE), s // (TQ // SLC_BLOCK)) - 1 for c, s in classes]

    @pl.when(qi == 0)
    def _():
        q_cmp0 = jnp.concatenate([_query_cols(q_ref), acmp_ref[...]], axis=0)
        _cmp_finish(_cmp_scores(q_cmp0, cmp_ref, t0, classes[0][0]), g_ref, cmpT_ref, ovT_ref, pen_ref, y1_ref, t0,
                    classes[0][1])

    qcols = _query_cols(q_ref)
    q_cmp = jnp.concatenate([qcols, acmp_ref[...]], axis=0)
    pens = [pen_ref[:, g * TQ:(g + 1) * TQ] for g in range(N_KV) for _ in range(GROUP)]
    q_slc = jnp.concatenate([qcols, jnp.concatenate(pens, axis=1)], axis=0)
    y1 = y1_ref[...]

    m_ref[...] = jnp.full((1, R), NEG, f32)
    l_ref[...] = jnp.zeros((1, R), f32)
    acc_ref[...] = jnp.zeros((HEAD_DIM, R), f32)
    n_far = (qi + KT // TQ) // (KT // TQ) - 1
    first = (qi + 1) * TQ - n_far * KT

    def tile_row(i):
        return pl.multiple_of(first + i * KT, LANES)

    def produce(i, s_ref, mx_ref):
        r0 = tile_row(i)
        k_ext = jnp.concatenate([ks_ref[pl.ds(r0, KT), :], e_ref[pl.ds(r0, KT), :]], axis=1)
        s = _dot(k_ext, q_slc)
        s_ref[...] = s
        mx_ref[...] = jnp.max(s, axis=0, keepdims=True)

    def consume(i, s_ref, mx_ref, last):
        m_prev = m_ref[...]
        if last:
            s = jnp.concatenate([s_ref[0:KT - 2 * TQ], s_ref[KT - 2 * TQ:KT] + tab_ref[TQ:3 * TQ]], axis=0)
            m_new = jnp.maximum(m_prev, jnp.max(s, axis=0, keepdims=True))
        else:
            s = s_ref[...]
            m_new = jnp.maximum(m_prev, mx_ref[...])
        alpha = jnp.exp2(m_prev - m_new)
        p = jnp.exp2(s - m_new)
        p = p.astype(bf16)
        for g in range(N_KV):
            c = slice(g * GC, (g + 1) * GC)
            pv = _dot(vsT_ref[g, :, pl.ds(tile_row(i), KT)], p[:, c])
            l_ref[:, c] = alpha[:, c] * l_ref[:, c] + pv[HEAD_DIM:HEAD_DIM + 1]
            acc_ref[:, c] = acc_ref[:, c] * alpha[:, c] + pv[0:HEAD_DIM]
        m_ref[...] = m_new

    qn = jnp.minimum(qi + 1, nq - 1)

    def front(nc_use, nsel_use):
        tn = qn * TQ
        s1 = _cmp_scores(jnp.concatenate([_query_cols(qn_ref), acmp_ref[...]], axis=0), cmp_ref, tn, nc_use)
        w0 = pl.multiple_of(t0, LANES)
        s3 = _dot(kw_ref[pl.ds(w0, N_WIN), :], q_cmp)
        _cmp_finish(s1, gn_ref, cmpT_ref, ovT_ref, pen_ref, y1_ref, tn, nsel_use)
        produce(0, sa_ref, mxa_ref)
        s3 = jnp.concatenate([s3[0:TQ] + tab_ref[0:TQ], s3[TQ:WINDOW - TQ], s3[WINDOW - TQ:] + tab_ref[TQ:3 * TQ]],
                             axis=0)
        p3 = jnp.exp2(s3 - jnp.max(s3, axis=0, keepdims=True)).astype(bf16)
        for g in range(N_KV):
            c = slice(g * GC, (g + 1) * GC)
            o3_ref[:, c] = _dot(vwT_ref[g, :, pl.ds(w0, N_WIN)], p3[:, c])

    for k, (nc_k, nsel_k) in enumerate(classes):
        lo = last_q[k - 1] if k else -1
        if k < N_CLASSES - 1:
            pl.when(jnp.logical_and(qn > lo, qn <= last_q[k]))(functools.partial(front, nc_k, nsel_k))
        else:
            pl.when(qn > lo)(functools.partial(front, nc_k, nsel_k))

    def pair(j, c):
        produce(2 * j + 1, sb_ref, mxb_ref)
        consume(2 * j, sa_ref, mxa_ref, False)
        produce(2 * j + 2, sa_ref, mxa_ref)
        consume(2 * j + 1, sb_ref, mxb_ref, False)
        return c

    lax.fori_loop(0, n_far // 2, pair, 0)

    def finish():
        inv2 = 1.0 / l_ref[...]
        inv3 = 1.0 / o3_ref[HEAD_DIM:HEAD_DIM + 1, :]
        sig_t = _sigmoid(g_ref[...]).T
        outs = []
        for h in range(N_HEADS):
            c = slice(h * TQ, (h + 1) * TQ)
            g2 = sig_t[3 * h + 1:3 * h + 2, :] * inv2[:, c]
            g3 = sig_t[3 * h + 2:3 * h + 3, :] * inv3[:, c]
            outs.append(g2 * acc_ref[:, c] + g3 * o3_ref[0:HEAD_DIM, c])
        for p, y in enumerate(_merge_head_pairs(outs)):
            c = slice(p * LANES, (p + 1) * LANES)
            zb = z_ref[:, c].astype(f32)
            out_ref[:, c] = ((y + y1[:, c].astype(f32)) * zb * _sigmoid(zb)).astype(bf16)

    @pl.when(n_far % 2 == 0)
    def _():
        consume(n_far, sa_ref, mxa_ref, True)
        finish()

    @pl.when(n_far % 2 == 1)
    def _():
        produce(n_far, sb_ref, mxb_ref)
        consume(n_far - 1, sa_ref, mxa_ref, False)
        consume(n_far, sb_ref, mxb_ref, True)
        finish()


def _nsa(q, z, g, cmp, cmpT, ks, kw, vsT, vwT, e_pad, ovT, tabs, acmp, B, T):
    nq = T // TQ
    nc = T // CMP_STRIDE
    tp = T + KPAD
    tile = lambda b, i: (b * nq + i, 0)
    nxt = lambda b, i: (b * nq + jnp.minimum(i + 1, nq - 1), 0)
    const2 = lambda b, i: (0, 0)
    batch3 = lambda b, i: (b, 0, 0)
    return pl.pallas_call(
        _nsa_kernel,
        grid=(B, nq),
        in_specs=[pl.BlockSpec((TQ, D_NSA), tile),
                  pl.BlockSpec((TQ, D_NSA), nxt),
                  pl.BlockSpec((TQ, D_NSA), tile),
                  pl.BlockSpec((TQ, LANES), tile),
                  pl.BlockSpec((TQ, LANES), nxt),
                  pl.BlockSpec((None, nc, 2 * D_KV), batch3),
                  pl.BlockSpec((None, 2 * D_KV, nc), batch3),
                  pl.BlockSpec((NSB + SUM_ROWS, nc), const2),
                  pl.BlockSpec((None, tp, D_KV), batch3),
                  pl.BlockSpec((None, tp, 2 * D_KV), batch3),
                  pl.BlockSpec((None, N_KV, VROWS, tp), lambda b, i: (b, 0, 0, 0)),
                  pl.BlockSpec((None, N_KV, VROWS, tp), lambda b, i: (b, 0, 0, 0)),
                  pl.BlockSpec((tp, NSB), const2),
                  pl.BlockSpec((3 * TQ, R), const2),
                  pl.BlockSpec((LANES, R), const2)],
        out_specs=pl.BlockSpec((TQ, D_NSA), tile),
        out_shape=jax.ShapeDtypeStruct((B * T, D_NSA), bf16),
        scratch_shapes=[pltpu.VMEM((NSB, N_KV * TQ), bf16), pltpu.VMEM((TQ, D_NSA), bf16), pltpu.VMEM((VROWS, R), f32),
                        pltpu.VMEM((1, R), f32), pltpu.VMEM((1, R), f32), pltpu.VMEM((HEAD_DIM, R), f32),
                        pltpu.VMEM((KT, R), f32), pltpu.VMEM((KT, R), f32),
                        pltpu.VMEM((1, R), f32), pltpu.VMEM((1, R), f32)],
        compiler_params=pltpu.CompilerParams(dimension_semantics=("arbitrary", "arbitrary"),
                                             vmem_limit_bytes=VMEM_LIMIT),
        name="nsa",
    )(q, q, z, g, g, cmp, cmpT, ovT, ks, kw, vsT, vwT, e_pad, tabs, acmp)


def _outproj_kernel(alpha, ymix_ref, ynsa_ref, x_ref, w_ref, g_ref, b_ref, out_ref):
    y = jnp.concatenate([ymix_ref[...], ynsa_ref[...]], axis=1)
    r = alpha * x_ref[...] + _dot(y, w_ref[...])
    mu = jnp.mean(r, axis=-1, keepdims=True)
    d = r - mu
    var = jnp.mean(d * d, axis=-1, keepdims=True)
    out_ref[...] = d * lax.rsqrt(var + LN_EPS) * g_ref[...] + b_ref[...]


def _outproj(ymix, ynsa, x2, w_out, ln_g, ln_b, alpha, layer):
    bt = x2.shape[0]
    row = lambda i: (i, 0)
    wsel = lambda i: (layer, 0, 0)
    return pl.pallas_call(
        functools.partial(_outproj_kernel, alpha),
        grid=(bt // TT_OUT,),
        in_specs=[pl.BlockSpec((TT_OUT, D_POOL + D_CONV), row),
                  pl.BlockSpec((TT_OUT, D_NSA), row),
                  pl.BlockSpec((TT_OUT, D_MODEL), row),
                  pl.BlockSpec((None, D_MODEL, D_MODEL), wsel),
                  pl.BlockSpec((None, 1, D_MODEL), wsel),
                  pl.BlockSpec((None, 1, D_MODEL), wsel)],
        out_specs=pl.BlockSpec((TT_OUT, D_MODEL), row),
        out_shape=jax.ShapeDtypeStruct((bt, D_MODEL), f32),
        compiler_params=pltpu.CompilerParams(dimension_semantics=("arbitrary",),
                                             vmem_limit_bytes=VMEM_LIMIT),
        name="outproj",
    )(ymix, ynsa, x2, w_out, ln_g, ln_b)


def _bucket_np(d):
    d = np.asarray(d)
    max_exact = N_BUCKETS // 2
    nf = np.maximum(d, 1).astype(np.float32)
    large = max_exact + (np.log(nf / np.float32(max_exact)) / np.float32(math.log(MAX_DISTANCE / max_exact))
                         * np.float32(N_BUCKETS - max_exact)).astype(np.int32)
    large = np.minimum(large, N_BUCKETS - 1)
    return np.where(d < max_exact, d, large)


_FAR_DIST = 113
assert _bucket_np(np.arange(_FAR_DIST, 4 * WINDOW)).min() == N_BUCKETS - 1

_PAIR_PERM = np.concatenate([np.concatenate([np.arange(HEAD_DIM) + HEAD_DIM * p,
                                             np.arange(HEAD_DIM) + HEAD_DIM * (p + GROUP)]) for p in range(GROUP)])


def _bias_tables(rel_bias):
    nd = 2 * TQ
    onehot = jnp.asarray(np.eye(N_BUCKETS, dtype=np.float32)[_bucket_np(np.arange(nd))])
    tabp = ((jnp.dot(onehot, rel_bias, precision=lax.Precision.HIGHEST)
             - rel_bias[N_BUCKETS - 1:N_BUCKETS, :]) * LOG2E).T
    sl = np.arange(TQ)[:, None]
    tl = np.arange(TQ)[None, :]
    neg = lambda n: jnp.full((N_HEADS, n), NEG, f32)

    def skew(u, rows):
        period = u.shape[1]
        return jnp.tile(u, (1, rows))[:, :rows * (period - 1)].reshape(N_HEADS, rows, period - 1)

    def tile_layout(t):
        return t.transpose(1, 0, 2).reshape(TQ, R)

    diag = tile_layout(skew(jnp.concatenate([tabp[:, 0:TQ], neg(TQ)], axis=1), TQ)[:, :, :TQ])
    prev = tile_layout(skew(jnp.concatenate([tabp[:, TQ:nd], tabp[:, 0:TQ]], axis=1), TQ)[:, :, :TQ])
    edge = jnp.asarray(np.tile(np.where(sl > tl, 0.0, NEG).astype(np.float32), (1, N_HEADS)))
    tabs = jnp.concatenate([edge, prev, diag], axis=0).astype(f32)

    j = np.arange(16)[None, :]
    off = 8 * CMP_STRIDE - (CMP_BLOCK - 1)
    dc = np.arange(TQ)[:, None] - CMP_STRIDE * j + off
    okc = jnp.asarray(dc >= 0)[None]
    period = 4 * TQ
    u = jnp.concatenate([tabp[:, off:nd], jnp.zeros((N_HEADS, off), f32), neg(period - nd - off), tabp[:, 0:off]], axis=1)
    vc = skew(u, nd)[:, 0:16 * CMP_STRIDE:CMP_STRIDE, 0:TQ].transpose(0, 2, 1)
    vc = jnp.where(okc, vc, NEG)
    hi = vc.astype(bf16)
    lo = jnp.where(okc, vc - hi.astype(f32), 0.0).astype(bf16)
    fut = jnp.full((N_HEADS, TQ, 1), NEG, bf16)
    pad = jnp.zeros((N_HEADS, TQ, LANES - _PAD_FLAG_COL - 1), bf16)
    acmp = jnp.concatenate([hi, lo, fut, pad], axis=-1).reshape(R, LANES).T
    return tabs, acmp


def _static_tables(T):
    nc = T // CMP_STRIDE
    ns = T // SLC_BLOCK
    cstart = np.arange(nc)[None, :] * CMP_STRIDE
    sstart = np.arange(NSB)[:, None] * SLC_BLOCK
    ov = np.clip(np.minimum(cstart + CMP_BLOCK, sstart + SLC_BLOCK) - np.maximum(cstart, sstart), 0, None) / CMP_STRIDE
    ov[ns:, :] = 0
    ov[:, nc - 1] = 0
    ov = np.concatenate([ov, np.ones((SUM_ROWS, nc))], axis=0)
    e_pad = np.concatenate([np.ones((KPAD, NSB), bool),
                            np.arange(T)[:, None] // SLC_BLOCK == np.arange(NSB)[None, :]], axis=0)
    return jnp.asarray(ov, bf16), jnp.asarray(e_pad, bf16)


def _pair_order(w, axis):
    shp = w.shape
    w = w.reshape(shp[:axis] + (N_KV, GROUP, HEAD_DIM) + shp[axis + 1:])
    return jnp.swapaxes(w, axis, axis + 1).reshape(shp)


def _prep_weights(w_in, w_out, pool_w, pe_k, w1_k, w2_k, pe_v, w1_v, w2_v):
    depth = w_in.shape[0]
    sizes = (D_POOL, D_POOL, D_CONV, D_CONV, D_CONV, D_CONV, D_NSA, D_KV, D_KV, D_KV, D_KV, D_KV, D_KV,
             3 * N_HEADS, D_NSA)
    offs = np.cumsum((0,) + sizes)
    w_in = w_in.astype(bf16)
    col = lambda i: w_in[:, :, offs[i]:offs[i + 1]]
    wg = jnp.pad(col(13), ((0, 0), (0, 0), (0, LANES - 3 * N_HEADS)))
    w_all = jnp.concatenate([w_in[:, :, 0:MIX_W], _pair_order(col(6), 2), _pair_order(col(14), 2),
                             col(7), col(8), col(9), col(11), wg], axis=2)
    wvt = jnp.swapaxes(jnp.concatenate([col(10), col(12)], axis=2), 1, 2)
    w_out = w_out.astype(bf16)
    nm = D_POOL + D_CONV
    w_out_p = jnp.concatenate([w_out[:, 0:nm], _pair_order(w_out[:, nm:], 1)], axis=1)

    eye_g = jnp.eye(len(POOL_WINDOWS), dtype=bf16)
    pw_bd = jnp.einsum('zgcd,gh->zgchd', pool_w.astype(bf16), eye_g).reshape(depth, D_POOL, D_POOL)

    half = CMP_BLOCK // 2
    eye2 = jnp.eye(2, dtype=bf16)
    rw = half * 2 * D_KV
    cols = []
    for a in range(2):
        for kv, w1 in enumerate((w1_k, w1_v)):
            wsel = w1.astype(bf16).reshape(depth, 2, half, 1, 1, HEAD_DIM, CMP_HIDDEN)[:, a]
            for g in range(N_KV):
                blk = jnp.pad(wsel, ((0, 0), (0, 0), (kv, 1 - kv), (g, N_KV - 1 - g), (0, 0), (0, 0)))
                cols.append(blk.reshape(depth, rw, CMP_HIDDEN))
    w1_big = jnp.concatenate(cols, axis=2)
    pes = jnp.stack([pe_k, pe_v], axis=1).astype(bf16).reshape(depth, 2, 2, half, HEAD_DIM)
    pe_r = jnp.broadcast_to(pes.transpose(0, 2, 3, 1, 4)[:, :, :, :, None, :],
                            (depth, 2, half, 2, N_KV, HEAD_DIM)).reshape(depth, 2, half * 2 * D_KV)
    pe_r = jnp.pad(pe_r, ((0, 0), (0, 6), (0, 0)))
    w2s = jnp.stack([w2_k, w2_v], axis=1).astype(bf16)
    w2_big = jnp.einsum('zkhd,kK,gG->zkghKGd', w2s, eye2, eye2).reshape(depth, CH, 2 * D_KV)
    return w_all, wvt, w_out_p, pw_bd, w1_big, pe_r, w2_big, jnp.swapaxes(w2_big, 1, 2)


def _padded_kv_init(B, T):
    tp = T + KPAD
    flag = np.zeros((1, tp, 2 * D_KV), np.float32)
    flag[:, :KPAD, D_KV + _PAD_FLAG_COL] = 1.0
    ones_rows = np.zeros((1, 1, VROWS, 1), np.float32)
    ones_rows[:, :, HEAD_DIM:] = 1.0
    vinit = jnp.broadcast_to(jnp.asarray(ones_rows, bf16), (B, N_KV, VROWS, tp))
    return (jnp.zeros((B, tp, D_KV), bf16), jnp.broadcast_to(jnp.asarray(flag, bf16), (B, tp, 2 * D_KV)),
            vinit, vinit)


def kernel(x, w_in, w_out, pool_w, pool_scale, conv_w, cmp_pe_k, cmp_w1_k, cmp_w2_k, cmp_pe_v, cmp_w1_v, cmp_w2_v,
           rel_bias, ln_g, ln_b):
    B, T, D = x.shape
    depth = w_in.shape[0]
    assert D == D_MODEL and T % TT == 0 and T // SLC_BLOCK <= NSB and T // SLC_BLOCK >= N_SELECT
    assert KPAD % TT == 0 and N_SELECT > N_LOCAL
    alpha = (2 * depth) ** 0.25
    tabs, acmp = _bias_tables(rel_bias)
    ovT, e_pad = _static_tables(T)
    w_all, wvt, w_out_p, pw_bd, w1_big, pe_r, w2_big, w2t_big = _prep_weights(
        w_in, w_out, pool_w, cmp_pe_k, cmp_w1_k, cmp_w2_k, cmp_pe_v, cmp_w1_v, cmp_w2_v)
    pool_scale = pool_scale.reshape(depth, 1, D_POOL)
    ln_g = ln_g.reshape(depth, 1, D_MODEL)
    ln_b = ln_b.reshape(depth, 1, D_MODEL)
    h = x.reshape(B * T, D)
    for l in range(depth):
        ymix, q, z, cmp_l, g, ks, kw, vsT, vwT = _inproj(h, w_all, wvt, pw_bd, pool_scale, conv_w,
                                                         _padded_kv_init(B, T), l, B, T)
        cmp, cmpT = _compress(cmp_l, pe_r, w1_big, w2_big, w2t_big, l, B)
        ynsa = _nsa(q, z, g, cmp, cmpT, ks, kw, vsT, vwT, e_pad, ovT, tabs, acmp, B, T)
        h = _outproj(ymix, ynsa, h, w_out_p, ln_g, ln_b, alpha, l)
    return h.reshape(B, T, D)
```

```python
import functools
import math

import numpy as np
import jax
import jax.numpy as jnp
from jax import lax
from jax.experimental import pallas as pl
from jax.experimental.pallas import tpu as pltpu

f32 = jnp.float32
bf16 = jnp.bfloat16

D_MODEL = 1024
D_POOL = 256
D_CONV = 256
D_NSA = 512
HEAD_DIM = 64
N_HEADS = 8
N_KV = 2
GROUP = 4
D_KV = 128
POOL_GROUP = 64
POOL_WINDOWS = (2, 4, 8, 16)
CONV_WIDTH = 3
CMP_BLOCK = 32
CMP_STRIDE = 16
CMP_HIDDEN = 128
SLC_BLOCK = 64
N_SELECT = 16
N_LOCAL = 2
WINDOW = 512
N_BUCKETS = 32
MAX_DISTANCE = 128
LN_EPS = 1e-5
FORCED = 1e9
NEG = -1e30

LANES = 128
TQ = 128
R = N_HEADS * TQ
NSB = 128
TT = 512
TT_OUT = 1024
UNROLL = 4
N_CLASSES = 4
HALO = 16
KT = 512
KPAD = KT
N_WIN = WINDOW + TQ
SUM_ROWS = 16
VROWS = HEAD_DIM + SUM_ROWS
GC = R // N_KV
LOG2E = math.log2(math.e)
_PAD_FLAG_COL = 32
MIX_W = 2 * D_POOL + 4 * D_CONV
VMEM_LIMIT = 56 * 1024 * 1024

_NT = (((1,), (1,)), ((), ()))


def _dot(a, b):
    return jnp.dot(a, b, preferred_element_type=f32)


def _dot_nt(a, b):
    return lax.dot_general(a, b, _NT, preferred_element_type=f32)


def _sigmoid(x):
    return 1.0 / (1.0 + jnp.exp(-x))


_C_MIX = (0, MIX_W)
_C_Q = (_C_MIX[1], _C_MIX[1] + D_NSA)
_C_Z = (_C_Q[1], _C_Q[1] + D_NSA)
_C_CMP = (_C_Z[1], _C_Z[1] + 2 * D_KV)
_C_K = (_C_CMP[1], _C_CMP[1] + 2 * D_KV)
_C_G = (_C_K[1], _C_K[1] + LANES)
W_ALL = _C_G[1]
CMP_ROWS = TT // CMP_STRIDE


def _local_mixers(ext, i, pw_ref, ps_ref, cw_ref):
    e = ext[:, 0:D_POOL]
    s2 = e + pltpu.roll(e, 1, axis=0)
    s4 = s2 + pltpu.roll(s2, 2, axis=0)
    s8 = s4 + pltpu.roll(s4, 4, axis=0)
    s16 = s8 + pltpu.roll(s8, 8, axis=0)
    lane = lax.broadcasted_iota(jnp.int32, (TT, D_POOL), 1)
    grp = lane // POOL_GROUP
    wsum = jnp.where(grp == 0, s2[HALO:], jnp.where(grp == 1, s4[HALO:], jnp.where(grp == 2, s8[HALO:], s16[HALO:])))
    win = jnp.left_shift(2, grp)
    pos = i * TT + lax.broadcasted_iota(jnp.int32, (TT, D_POOL), 0)
    cnt = jnp.minimum(pos + 1, win).astype(f32)
    v = e[HALO:]
    pooled = wsum / cnt - v
    y_pool = _dot(pooled.astype(bf16), pw_ref[...]) * ps_ref[...]
    zp = ext[HALO:, D_POOL:2 * D_POOL]
    y_pool = y_pool * (zp * _sigmoid(zp))

    o = 2 * D_POOL
    cb = ext[HALO:, o:o + D_CONV]
    u = ext[:, o + D_CONV:o + 2 * D_CONV] * ext[:, o + 2 * D_CONV:o + 3 * D_CONV]
    zc = ext[HALO:, o + 3 * D_CONV:o + 4 * D_CONV]
    conv = cw_ref[CONV_WIDTH - 1:CONV_WIDTH, :] * u[HALO:]
    for k in range(CONV_WIDTH - 1):
        conv = conv + cw_ref[k:k + 1, :] * pltpu.roll(u, CONV_WIDTH - 1 - k, axis=0)[HALO:]
    y_conv = cb * conv * (zc * _sigmoid(zc))
    return y_pool, y_conv


def _inproj_kernel(x_ref, w_ref, wvt_ref, pw_ref, ps_ref, cw_ref, ks_in, kw_in, vs_in, vw_in,
                   ymix_ref, q_ref, z_ref, cmp_ref, g_ref, ks_ref, kw_ref, vsT_ref, vwT_ref,
                   halo_ref, cscr_ref):
    del ks_in, kw_in, vs_in, vw_in
    i = pl.program_id(1)
    x = x_ref[...].astype(bf16)

    def proj(c):
        return _dot(x, w_ref[:, c[0]:c[1]])

    mix = proj(_C_MIX)
    halo = jnp.where(i > 0, halo_ref[...], 0.0)
    y_pool, y_conv = _local_mixers(jnp.concatenate([halo, mix], axis=0), i, pw_ref, ps_ref, cw_ref)
    halo_ref[...] = mix[TT - HALO:]
    ymix_ref[:, 0:D_POOL] = y_pool.astype(bf16)
    ymix_ref[:, D_POOL:D_POOL + D_CONV] = y_conv.astype(bf16)

    q_ref[...] = (proj(_C_Q) * (HEAD_DIM ** -0.5 * LOG2E)).astype(bf16)
    z_ref[...] = proj(_C_Z).astype(bf16)
    g_ref[...] = proj(_C_G)
    kk = proj(_C_K).astype(bf16)
    ks_ref[...] = kk[:, 0:D_KV]
    kw_ref[:, 0:D_KV] = kk[:, D_KV:2 * D_KV]
    kw_ref[:, D_KV:2 * D_KV] = jnp.zeros((TT, D_KV), bf16)
    vt = _dot_nt(wvt_ref[...], x)
    for g in range(N_KV):
        vsT_ref[g] = vt[g * HEAD_DIM:(g + 1) * HEAD_DIM].astype(bf16)
        vwT_ref[g] = vt[D_KV + g * HEAD_DIM:D_KV + (g + 1) * HEAD_DIM].astype(bf16)

    kvc = proj(_C_CMP)
    for c in range(2):
        cscr_ref[c] = kvc[:, c * LANES:(c + 1) * LANES]
        for l in range(CMP_STRIDE):
            cmp_ref[l, :, c * LANES:(c + 1) * LANES] = (
                cscr_ref[c, pl.ds(l, CMP_ROWS, stride=CMP_STRIDE), :].astype(bf16))


def _inproj(x2, w_all, wvt, pw_bd, pool_scale, conv_w, kpads, layer, B, T):
    nt = T // TT
    bt = B * T
    row = lambda b, i: (b * nt + i, 0)
    wsel = lambda b, i: (layer, 0, 0)
    any_spec = pl.BlockSpec(memory_space=pl.ANY)
    tp = T + KPAD
    out_shape = [jax.ShapeDtypeStruct((bt, D_POOL + D_CONV), bf16),
                 jax.ShapeDtypeStruct((bt, D_NSA), bf16),
                 jax.ShapeDtypeStruct((bt, D_NSA), bf16),
                 jax.ShapeDtypeStruct((CMP_STRIDE, bt // CMP_STRIDE, 2 * D_KV), bf16),
                 jax.ShapeDtypeStruct((bt, LANES), f32),
                 jax.ShapeDtypeStruct((B, tp, D_KV), bf16),
                 jax.ShapeDtypeStruct((B, tp, 2 * D_KV), bf16),
                 jax.ShapeDtypeStruct((B, N_KV, VROWS, tp), bf16),
                 jax.ShapeDtypeStruct((B, N_KV, VROWS, tp), bf16)]
    out_specs = [pl.BlockSpec((TT, D_POOL + D_CONV), row),
                 pl.BlockSpec((TT, D_NSA), row),
                 pl.BlockSpec((TT, D_NSA), row),
                 pl.BlockSpec((CMP_STRIDE, CMP_ROWS, 2 * D_KV), lambda b, i: (0, b * nt + i, 0)),
                 pl.BlockSpec((TT, LANES), row),
                 pl.BlockSpec((None, TT, D_KV), lambda b, i: (b, i + KPAD // TT, 0)),
                 pl.BlockSpec((None, TT, 2 * D_KV), lambda b, i: (b, i + KPAD // TT, 0)),
                 pl.BlockSpec((None, N_KV, HEAD_DIM, TT), lambda b, i: (b, 0, 0, i + KPAD // TT)),
                 pl.BlockSpec((None, N_KV, HEAD_DIM, TT), lambda b, i: (b, 0, 0, i + KPAD // TT))]
    return pl.pallas_call(
        _inproj_kernel,
        grid=(B, nt),
        in_specs=[pl.BlockSpec((TT, D_MODEL), row),
                  pl.BlockSpec((None, D_MODEL, W_ALL), wsel),
                  pl.BlockSpec((None, 2 * D_KV, D_MODEL), wsel),
                  pl.BlockSpec((None, D_POOL, D_POOL), wsel),
                  pl.BlockSpec((None, 1, D_POOL), wsel),
                  pl.BlockSpec((None, CONV_WIDTH, D_CONV), wsel),
                  any_spec, any_spec, any_spec, any_spec],
        out_specs=out_specs,
        out_shape=out_shape,
        input_output_aliases={6: 5, 7: 6, 8: 7, 9: 8},
        scratch_shapes=[pltpu.VMEM((HALO, MIX_W), f32), pltpu.VMEM((2, TT, LANES), f32)],
        compiler_params=pltpu.CompilerParams(dimension_semantics=("arbitrary", "arbitrary"),
                                             vmem_limit_bytes=VMEM_LIMIT),
        name="inproj",
    )(x2, w_all, wvt, pw_bd, pool_scale, conv_w, *kpads)


CH = 4 * CMP_HIDDEN


def _compress_kernel(r_ref, pe_ref, w1_ref, w2_ref, w2t_ref, out_ref, outT_ref):
    nc = r_ref.shape[1]
    rows = jnp.concatenate([r_ref[l] for l in range(CMP_STRIDE)], axis=1)
    zz = _dot(rows, w1_ref[...])
    pb = _dot(pe_ref[...], w1_ref[...])
    z0 = zz[:, 0:CH] + pb[0:1, 0:CH]
    z1 = zz[:, CH:2 * CH] + pb[1:2, CH:2 * CH]
    h = z0 + pltpu.roll(z1, nc - 1, axis=0)
    h = (h * _sigmoid(h)).astype(bf16)
    out_ref[...] = _dot(h, w2_ref[...]).astype(bf16)
    outT_ref[...] = _dot_nt(w2t_ref[...], h).astype(bf16)


def _compress(cmp_l, pe_r, w1_big, w2_big, w2t_big, layer, B):
    nc = cmp_l.shape[1] // B
    rw = CMP_STRIDE * 2 * D_KV
    wsel = lambda b: (layer, 0, 0)
    return pl.pallas_call(
        _compress_kernel,
        grid=(B,),
        in_specs=[pl.BlockSpec((CMP_STRIDE, nc, 2 * D_KV), lambda b: (0, b, 0)),
                  pl.BlockSpec((None, 8, rw), wsel),
                  pl.BlockSpec((None, rw, 2 * CH), wsel),
                  pl.BlockSpec((None, CH, 2 * D_KV), wsel),
                  pl.BlockSpec((None, 2 * D_KV, CH), wsel)],
        out_specs=[pl.BlockSpec((None, nc, 2 * D_KV), lambda b: (b, 0, 0)),
                   pl.BlockSpec((None, 2 * D_KV, nc), lambda b: (b, 0, 0))],
        out_shape=[jax.ShapeDtypeStruct((B, nc, 2 * D_KV), bf16),
                   jax.ShapeDtypeStruct((B, 2 * D_KV, nc), bf16)],
        compiler_params=pltpu.CompilerParams(dimension_semantics=("arbitrary",),
                                             vmem_limit_bytes=VMEM_LIMIT),
        name="compress",
    )(cmp_l, pe_r, w1_big, w2_big, w2t_big)


def _select_penalty(imp_t, t0):
    shape = imp_t.shape
    jblk = lax.broadcasted_iota(jnp.int32, shape, 0)
    tq = lax.broadcasted_iota(jnp.int32, shape, 1) % TQ
    back = jnp.right_shift(t0 + tq, int(math.log2(SLC_BLOCK))) - jblk
    causal = back >= 0
    forced = jnp.logical_or(jblk == 0, jnp.logical_and(causal, back < N_LOCAL))
    score = jnp.where(forced, -jnp.inf, jnp.where(causal, imp_t, NEG))
    jf = jblk.astype(f32)
    for _ in range(N_SELECT - N_LOCAL - 1):
        best = jnp.max(score, axis=0, keepdims=True)
        first = jnp.min(jnp.where(score == best, jf, float(shape[0])), axis=0, keepdims=True)
        score = jnp.where(jf == first, -jnp.inf, score)
    return jnp.where(score == -jnp.inf, 0.0, NEG)


def _query_cols(q_ref):
    sub = lax.broadcasted_iota(jnp.int32, (LANES, TQ), 0)
    low = sub < HEAD_DIM
    pairs = [q_ref[:, LANES * p:LANES * (p + 1)].astype(f32).T for p in range(GROUP)]
    blocks = [(jnp.where(low, pairs[h % GROUP], 0.0) if h < GROUP else jnp.where(low, 0.0, pairs[h % GROUP])).astype(bf16)
              for h in range(N_HEADS)]
    return jnp.concatenate(blocks, axis=1)


def _merge_head_pairs(outs):
    return [jnp.concatenate([outs[p], outs[p + GROUP]], axis=0).T for p in range(GROUP)]


def _cmp_scores(q_cmp, cmp_ref, t0, nc):
    n_i = lax.broadcasted_iota(jnp.int32, (nc, LANES), 0)
    j_i = lax.broadcasted_iota(jnp.int32, (nc, LANES), 1)
    nstart = t0 // CMP_STRIDE - 8
    in_window = jnp.logical_and(j_i < 32, n_i == nstart + jnp.where(j_i < 16, j_i, j_i - 16))
    future = jnp.logical_and(j_i == 32, n_i >= nstart + 15)
    place_b = jnp.where(jnp.logical_or(in_window, future), 1.0, 0.0).astype(bf16)
    kc_ext = jnp.concatenate([cmp_ref[0:nc, 0:D_KV], place_b], axis=1)
    return _dot(kc_ext, q_cmp)


def _cmp_finish(s1, g_ref, cmpT_ref, ovT_ref, pen_ref, y1_ref, t0, nsel):
    nc = s1.shape[0]
    m1 = jnp.max(s1, axis=0, keepdims=True)
    p1 = jnp.exp2(s1 - m1)
    p1 = p1.astype(bf16)
    tcol = t0 + lax.broadcasted_iota(jnp.int32, (1, GC), 1) % TQ
    sig_t = _sigmoid(g_ref[...]).T
    outs, sums = [], []
    for g in range(N_KV):
        lhs = jnp.concatenate([cmpT_ref[D_KV + g * HEAD_DIM:D_KV + (g + 1) * HEAD_DIM, 0:nc], ovT_ref[0:nsel, 0:nc],
                               ovT_ref[NSB:NSB + SUM_ROWS, 0:nc]], axis=0)
        both = _dot(lhs, p1[:, g * GC:(g + 1) * GC])
        l1 = both[HEAD_DIM + nsel:HEAD_DIM + nsel + 1]
        both = both * jnp.where(tcol >= CMP_BLOCK - 1, 1.0 / l1, 0.0)
        for r in range(GROUP):
            h = g * GROUP + r
            outs.append(sig_t[3 * h:3 * h + 1, :] * both[0:HEAD_DIM, r * TQ:(r + 1) * TQ])
        imp = both[HEAD_DIM:HEAD_DIM + nsel]
        acc = imp[:, 0:TQ]
        for r in range(1, GROUP):
            acc = acc + imp[:, r * TQ:(r + 1) * TQ]
        sums.append(acc)
    for p, y in enumerate(_merge_head_pairs(outs)):
        y1_ref[:, p * LANES:(p + 1) * LANES] = y.astype(bf16)
    imp_t = jnp.concatenate(sums, axis=1)
    pen_ref[0:nsel] = _select_penalty(imp_t, t0).astype(bf16)
    if nsel < NSB:
        pen_ref[nsel:NSB] = jnp.full((NSB - nsel, N_KV * TQ), NEG, bf16)


def _nsa_kernel(q_ref, qn_ref, z_ref, g_ref, gn_ref, cmp_ref, cmpT_ref, ovT_ref, ks_ref, kw_ref, vsT_ref, vwT_ref,
                e_ref, tab_ref, acmp_ref, out_ref,
                pen_ref, y1_ref, o3_ref, m_ref, l_ref, acc_ref, sa_ref, sb_ref, mxa_ref, mxb_ref):
    qi = pl.program_id(1)
    nq = pl.num_programs(1)
    t0 = qi * TQ
    nc = cmp_ref.shape[0]
    classes = [(nc * k // N_CLASSES, NSB * k // N_CLASSES) for k in range(1, N_CLASSES + 1)]
    last_q = [min(c // (TQ // CMP_STRIDE), s // (TQ // SLC_BLOCK)) - 1 for c, s in classes]

    @pl.when(qi == 0)
    def _():
        q_cmp0 = jnp.concatenate([_query_cols(q_ref), acmp_ref[...]], axis=0)
        _cmp_finish(_cmp_scores(q_cmp0, cmp_ref, t0, classes[0][0]), g_ref, cmpT_ref, ovT_ref, pen_ref, y1_ref, t0,
                    classes[0][1])

    qcols = _query_cols(q_ref)
    q_cmp = jnp.concatenate([qcols, acmp_ref[...]], axis=0)
    pens = [pen_ref[:, g * TQ:(g + 1) * TQ] for g in range(N_KV) for _ in range(GROUP)]
    q_slc = jnp.concatenate([qcols, jnp.concatenate(pens, axis=1)], axis=0)
    y1 = y1_ref[...]

    m_ref[...] = jnp.full((1, R), NEG, f32)
    l_ref[...] = jnp.zeros((1, R), f32)
    acc_ref[...] = jnp.zeros((HEAD_DIM, R), f32)
    n_far = (qi + KT // TQ) // (KT // TQ) - 1
    first = (qi + 1) * TQ - n_far * KT

    def tile_row(i):
        return pl.multiple_of(first + i * KT, LANES)

    def produce(i, s_ref, mx_ref):
        r0 = tile_row(i)
        k_ext = jnp.concatenate([ks_ref[pl.ds(r0, KT), :], e_ref[pl.ds(r0, KT), :]], axis=1)
        s = _dot(k_ext, q_slc)
        s_ref[...] = s
        mx_ref[...] = jnp.max(s, axis=0, keepdims=True)

    def consume(i, s_ref, mx_ref, last):
        m_prev = m_ref[...]
        if last:
            s = jnp.concatenate([s_ref[0:KT - 2 * TQ], s_ref[KT - 2 * TQ:KT] + tab_ref[TQ:3 * TQ]], axis=0)
            m_new = jnp.maximum(m_prev, jnp.max(s, axis=0, keepdims=True))
        else:
            s = s_ref[...]
            m_new = jnp.maximum(m_prev, mx_ref[...])
        alpha = jnp.exp2(m_prev - m_new)
        p = jnp.exp2(s - m_new)
        p = p.astype(bf16)
        for g in range(N_KV):
            c = slice(g * GC, (g + 1) * GC)
            pv = _dot(vsT_ref[g, :, pl.ds(tile_row(i), KT)], p[:, c])
            l_ref[:, c] = alpha[:, c] * l_ref[:, c] + pv[HEAD_DIM:HEAD_DIM + 1]
            acc_ref[:, c] = acc_ref[:, c] * alpha[:, c] + pv[0:HEAD_DIM]
        m_ref[...] = m_new

    qn = jnp.minimum(qi + 1, nq - 1)

    def front(nc_use, nsel_use):
        tn = qn * TQ
        s1 = _cmp_scores(jnp.concatenate([_query_cols(qn_ref), acmp_ref[...]], axis=0), cmp_ref, tn, nc_use)
        w0 = pl.multiple_of(t0, LANES)
        s3 = _dot(kw_ref[pl.ds(w0, N_WIN), :], q_cmp)
        _cmp_finish(s1, gn_ref, cmpT_ref, ovT_ref, pen_ref, y1_ref, tn, nsel_use)
        produce(0, sa_ref, mxa_ref)
        s3 = jnp.concatenate([s3[0:TQ] + tab_ref[0:TQ], s3[TQ:WINDOW - TQ], s3[WINDOW - TQ:] + tab_ref[TQ:3 * TQ]],
                             axis=0)
        p3 = jnp.exp2(s3 - jnp.max(s3, axis=0, keepdims=True)).astype(bf16)
        for g in range(N_KV):
            c = slice(g * GC, (g + 1) * GC)
            o3_ref[:, c] = _dot(vwT_ref[g, :, pl.ds(w0, N_WIN)], p3[:, c])

    for k, (nc_k, nsel_k) in enumerate(classes):
        lo = last_q[k - 1] if k else -1
        if k < N_CLASSES - 1:
            pl.when(jnp.logical_and(qn > lo, qn <= last_q[k]))(functools.partial(front, nc_k, nsel_k))
        else:
            pl.when(qn > lo)(functools.partial(front, nc_k, nsel_k))

    bufs = ((sa_ref, mxa_ref), (sb_ref, mxb_ref))

    def pipeline(base, count, last_at_end):
        for k in range(count):
            if k + 1 < count or not last_at_end:
                produce(base + k + 1, *bufs[(k + 1) % 2])
            consume(base + k, *bufs[k % 2], last_at_end and k == count - 1)

    def unrolled(j, c):
        pipeline(UNROLL * j, UNROLL, False)
        return c

    lax.fori_loop(0, n_far // UNROLL, unrolled, 0)

    def finish():
        inv2 = 1.0 / l_ref[...]
        inv3 = 1.0 / o3_ref[HEAD_DIM:HEAD_DIM + 1, :]
        sig_t = _sigmoid(g_ref[...]).T
        outs = []
        for h in range(N_HEADS):
            c = slice(h * TQ, (h + 1) * TQ)
            g2 = sig_t[3 * h + 1:3 * h + 2, :] * inv2[:, c]
            g3 = sig_t[3 * h + 2:3 * h + 3, :] * inv3[:, c]
            outs.append(g2 * acc_ref[:, c] + g3 * o3_ref[0:HEAD_DIM, c])
        for p, y in enumerate(_merge_head_pairs(outs)):
            c = slice(p * LANES, (p + 1) * LANES)
            zb = z_ref[:, c].astype(f32)
            out_ref[:, c] = ((y + y1[:, c].astype(f32)) * zb * _sigmoid(zb)).astype(bf16)

    for r in range(UNROLL):
        def drain(r=r):
            pipeline(n_far - r, r + 1, True)
            finish()
        pl.when(n_far % UNROLL == r)(drain)


def _nsa(q, z, g, cmp, cmpT, ks, kw, vsT, vwT, e_pad, ovT, tabs, acmp, B, T):
    nq = T // TQ
    nc = T // CMP_STRIDE
    tp = T + KPAD
    tile = lambda b, i: (b * nq + i, 0)
    nxt = lambda b, i: (b * nq + jnp.minimum(i + 1, nq - 1), 0)
    const2 = lambda b, i: (0, 0)
    batch3 = lambda b, i: (b, 0, 0)
    return pl.pallas_call(
        _nsa_kernel,
        grid=(B, nq),
        in_specs=[pl.BlockSpec((TQ, D_NSA), tile),
                  pl.BlockSpec((TQ, D_NSA), nxt),
                  pl.BlockSpec((TQ, D_NSA), tile),
                  pl.BlockSpec((TQ, LANES), tile),
                  pl.BlockSpec((TQ, LANES), nxt),
                  pl.BlockSpec((None, nc, 2 * D_KV), batch3),
                  pl.BlockSpec((None, 2 * D_KV, nc), batch3),
                  pl.BlockSpec((NSB + SUM_ROWS, nc), const2),
                  pl.BlockSpec((None, tp, D_KV), batch3),
                  pl.BlockSpec((None, tp, 2 * D_KV), batch3),
                  pl.BlockSpec((None, N_KV, VROWS, tp), lambda b, i: (b, 0, 0, 0)),
                  pl.BlockSpec((None, N_KV, VROWS, tp), lambda b, i: (b, 0, 0, 0)),
                  pl.BlockSpec((tp, NSB), const2),
                  pl.BlockSpec((3 * TQ, R), const2),
                  pl.BlockSpec((LANES, R), const2)],
        out_specs=pl.BlockSpec((TQ, D_NSA), tile),
        out_shape=jax.ShapeDtypeStruct((B * T, D_NSA), bf16),
        scratch_shapes=[pltpu.VMEM((NSB, N_KV * TQ), bf16), pltpu.VMEM((TQ, D_NSA), bf16), pltpu.VMEM((VROWS, R), f32),
                        pltpu.VMEM((1, R), f32), pltpu.VMEM((1, R), f32), pltpu.VMEM((HEAD_DIM, R), f32),
                        pltpu.VMEM((KT, R), f32), pltpu.VMEM((KT, R), f32),
                        pltpu.VMEM((1, R), f32), pltpu.VMEM((1, R), f32)],
        compiler_params=pltpu.CompilerParams(dimension_semantics=("arbitrary", "arbitrary"),
                                             vmem_limit_bytes=VMEM_LIMIT),
        name="nsa",
    )(q, q, z, g, g, cmp, cmpT, ovT, ks, kw, vsT, vwT, e_pad, tabs, acmp)


def _outproj_kernel(alpha, ymix_ref, ynsa_ref, x_ref, w_ref, g_ref, b_ref, out_ref):
    y = jnp.concatenate([ymix_ref[...], ynsa_ref[...]], axis=1)
    r = alpha * x_ref[...] + _dot(y, w_ref[...])
    mu = jnp.mean(r, axis=-1, keepdims=True)
    d = r - mu
    var = jnp.mean(d * d, axis=-1, keepdims=True)
    out_ref[...] = d * lax.rsqrt(var + LN_EPS) * g_ref[...] + b_ref[...]


def _outproj(ymix, ynsa, x2, w_out, ln_g, ln_b, alpha, layer):
    bt = x2.shape[0]
    row = lambda i: (i, 0)
    wsel = lambda i: (layer, 0, 0)
    return pl.pallas_call(
        functools.partial(_outproj_kernel, alpha),
        grid=(bt // TT_OUT,),
        in_specs=[pl.BlockSpec((TT_OUT, D_POOL + D_CONV), row),
                  pl.BlockSpec((TT_OUT, D_NSA), row),
                  pl.BlockSpec((TT_OUT, D_MODEL), row),
                  pl.BlockSpec((None, D_MODEL, D_MODEL), wsel),
                  pl.BlockSpec((None, 1, D_MODEL), wsel),
                  pl.BlockSpec((None, 1, D_MODEL), wsel)],
        out_specs=pl.BlockSpec((TT_OUT, D_MODEL), row),
        out_shape=jax.ShapeDtypeStruct((bt, D_MODEL), f32),
        compiler_params=pltpu.CompilerParams(dimension_semantics=("arbitrary",),
                                             vmem_limit_bytes=VMEM_LIMIT),
        name="outproj",
    )(ymix, ynsa, x2, w_out, ln_g, ln_b)


def _bucket_np(d):
    d = np.asarray(d)
    max_exact = N_BUCKETS // 2
    nf = np.maximum(d, 1).astype(np.float32)
    large = max_exact + (np.log(nf / np.float32(max_exact)) / np.float32(math.log(MAX_DISTANCE / max_exact))
                         * np.float32(N_BUCKETS - max_exact)).astype(np.int32)
    large = np.minimum(large, N_BUCKETS - 1)
    return np.where(d < max_exact, d, large)


_FAR_DIST = 113
assert _bucket_np(np.arange(_FAR_DIST, 4 * WINDOW)).min() == N_BUCKETS - 1

_PAIR_PERM = np.concatenate([np.concatenate([np.arange(HEAD_DIM) + HEAD_DIM * p,
                                             np.arange(HEAD_DIM) + HEAD_DIM * (p + GROUP)]) for p in range(GROUP)])


def _bias_tables(rel_bias):
    nd = 2 * TQ
    onehot = jnp.asarray(np.eye(N_BUCKETS, dtype=np.float32)[_bucket_np(np.arange(nd))])
    tabp = ((jnp.dot(onehot, rel_bias, precision=lax.Precision.HIGHEST)
             - rel_bias[N_BUCKETS - 1:N_BUCKETS, :]) * LOG2E).T
    sl = np.arange(TQ)[:, None]
    tl = np.arange(TQ)[None, :]
    neg = lambda n: jnp.full((N_HEADS, n), NEG, f32)

    def skew(u, rows):
        period = u.shape[1]
        return jnp.tile(u, (1, rows))[:, :rows * (period - 1)].reshape(N_HEADS, rows, period - 1)

    def tile_layout(t):
        return t.transpose(1, 0, 2).reshape(TQ, R)

    diag = tile_layout(skew(jnp.concatenate([tabp[:, 0:TQ], neg(TQ)], axis=1), TQ)[:, :, :TQ])
    prev = tile_layout(skew(jnp.concatenate([tabp[:, TQ:nd], tabp[:, 0:TQ]], axis=1), TQ)[:, :, :TQ])
    edge = jnp.asarray(np.tile(np.where(sl > tl, 0.0, NEG).astype(np.float32), (1, N_HEADS)))
    tabs = jnp.concatenate([edge, prev, diag], axis=0).astype(f32)

    j = np.arange(16)[None, :]
    off = 8 * CMP_STRIDE - (CMP_BLOCK - 1)
    dc = np.arange(TQ)[:, None] - CMP_STRIDE * j + off
    okc = jnp.asarray(dc >= 0)[None]
    period = 4 * TQ
    u = jnp.concatenate([tabp[:, off:nd], jnp.zeros((N_HEADS, off), f32), neg(period - nd - off), tabp[:, 0:off]], axis=1)
    vc = skew(u, nd)[:, 0:16 * CMP_STRIDE:CMP_STRIDE, 0:TQ].transpose(0, 2, 1)
    vc = jnp.where(okc, vc, NEG)
    hi = vc.astype(bf16)
    lo = jnp.where(okc, vc - hi.astype(f32), 0.0).astype(bf16)
    fut = jnp.full((N_HEADS, TQ, 1), NEG, bf16)
    pad = jnp.zeros((N_HEADS, TQ, LANES - _PAD_FLAG_COL - 1), bf16)
    acmp = jnp.concatenate([hi, lo, fut, pad], axis=-1).reshape(R, LANES).T
    return tabs, acmp


def _static_tables(T):
    nc = T // CMP_STRIDE
    ns = T // SLC_BLOCK
    cstart = np.arange(nc)[None, :] * CMP_STRIDE
    sstart = np.arange(NSB)[:, None] * SLC_BLOCK
    ov = np.clip(np.minimum(cstart + CMP_BLOCK, sstart + SLC_BLOCK) - np.maximum(cstart, sstart), 0, None) / CMP_STRIDE
    ov[ns:, :] = 0
    ov[:, nc - 1] = 0
    ov = np.concatenate([ov, np.ones((SUM_ROWS, nc))], axis=0)
    e_pad = np.concatenate([np.ones((KPAD, NSB), bool),
                            np.arange(T)[:, None] // SLC_BLOCK == np.arange(NSB)[None, :]], axis=0)
    return jnp.asarray(ov, bf16), jnp.asarray(e_pad, bf16)


def _pair_order(w, axis):
    shp = w.shape
    w = w.reshape(shp[:axis] + (N_KV, GROUP, HEAD_DIM) + shp[axis + 1:])
    return jnp.swapaxes(w, axis, axis + 1).reshape(shp)


def _prep_weights(w_in, w_out, pool_w, pe_k, w1_k, w2_k, pe_v, w1_v, w2_v):
    depth = w_in.shape[0]
    sizes = (D_POOL, D_POOL, D_CONV, D_CONV, D_CONV, D_CONV, D_NSA, D_KV, D_KV, D_KV, D_KV, D_KV, D_KV,
             3 * N_HEADS, D_NSA)
    offs = np.cumsum((0,) + sizes)
    w_in = w_in.astype(bf16)
    col = lambda i: w_in[:, :, offs[i]:offs[i + 1]]
    wg = jnp.pad(col(13), ((0, 0), (0, 0), (0, LANES - 3 * N_HEADS)))
    w_all = jnp.concatenate([w_in[:, :, 0:MIX_W], _pair_order(col(6), 2), _pair_order(col(14), 2),
                             col(7), col(8), col(9), col(11), wg], axis=2)
    wvt = jnp.swapaxes(jnp.concatenate([col(10), col(12)], axis=2), 1, 2)
    w_out = w_out.astype(bf16)
    nm = D_POOL + D_CONV
    w_out_p = jnp.concatenate([w_out[:, 0:nm], _pair_order(w_out[:, nm:], 1)], axis=1)

    eye_g = jnp.eye(len(POOL_WINDOWS), dtype=bf16)
    pw_bd = jnp.einsum('zgcd,gh->zgchd', pool_w.astype(bf16), eye_g).reshape(depth, D_POOL, D_POOL)

    half = CMP_BLOCK // 2
    eye2 = jnp.eye(2, dtype=bf16)
    rw = half * 2 * D_KV
    cols = []
    for a in range(2):
        for kv, w1 in enumerate((w1_k, w1_v)):
            wsel = w1.astype(bf16).reshape(depth, 2, half, 1, 1, HEAD_DIM, CMP_HIDDEN)[:, a]
            for g in range(N_KV):
                blk = jnp.pad(wsel, ((0, 0), (0, 0), (kv, 1 - kv), (g, N_KV - 1 - g), (0, 0), (0, 0)))
                cols.append(blk.reshape(depth, rw, CMP_HIDDEN))
    w1_big = jnp.concatenate(cols, axis=2)
    pes = jnp.stack([pe_k, pe_v], axis=1).astype(bf16).reshape(depth, 2, 2, half, HEAD_DIM)
    pe_r = jnp.broadcast_to(pes.transpose(0, 2, 3, 1, 4)[:, :, :, :, None, :],
                            (depth, 2, half, 2, N_KV, HEAD_DIM)).reshape(depth, 2, half * 2 * D_KV)
    pe_r = jnp.pad(pe_r, ((0, 0), (0, 6), (0, 0)))
    w2s = jnp.stack([w2_k, w2_v], axis=1).astype(bf16)
    w2_big = jnp.einsum('zkhd,kK,gG->zkghKGd', w2s, eye2, eye2).reshape(depth, CH, 2 * D_KV)
    return w_all, wvt, w_out_p, pw_bd, w1_big, pe_r, w2_big, jnp.swapaxes(w2_big, 1, 2)


def _padded_kv_init(B, T):
    tp = T + KPAD
    flag = np.zeros((1, tp, 2 * D_KV), np.float32)
    flag[:, :KPAD, D_KV + _PAD_FLAG_COL] = 1.0
    ones_rows = np.zeros((1, 1, VROWS, 1), np.float32)
    ones_rows[:, :, HEAD_DIM:] = 1.0
    vinit = jnp.broadcast_to(jnp.asarray(ones_rows, bf16), (B, N_KV, VROWS, tp))
    return (jnp.zeros((B, tp, D_KV), bf16), jnp.broadcast_to(jnp.asarray(flag, bf16), (B, tp, 2 * D_KV)),
            vinit, vinit)


def kernel(x, w_in, w_out, pool_w, pool_scale, conv_w, cmp_pe_k, cmp_w1_k, cmp_w2_k, cmp_pe_v, cmp_w1_v, cmp_w2_v,
           rel_bias, ln_g, ln_b):
    B, T, D = x.shape
    depth = w_in.shape[0]
    assert D == D_MODEL and T % TT == 0 and T // SLC_BLOCK <= NSB and T // SLC_BLOCK >= N_SELECT
    assert KPAD % TT == 0 and N_SELECT > N_LOCAL
    alpha = (2 * depth) ** 0.25
    tabs, acmp = _bias_tables(rel_bias)
    ovT, e_pad = _static_tables(T)
    w_all, wvt, w_out_p, pw_bd, w1_big, pe_r, w2_big, w2t_big = _prep_weights(
        w_in, w_out, pool_w, cmp_pe_k, cmp_w1_k, cmp_w2_k, cmp_pe_v, cmp_w1_v, cmp_w2_v)
    pool_scale = pool_scale.reshape(depth, 1, D_POOL)
    ln_g = ln_g.reshape(depth, 1, D_MODEL)
    ln_b = ln_b.reshape(depth, 1, D_MODEL)
    h = x.reshape(B * T, D)
    for l in range(depth):
        ymix, q, z, cmp_l, g, ks, kw, vsT, vwT = _inproj(h, w_all, wvt, pw_bd, pool_scale, conv_w,
                                                         _padded_kv_init(B, T), l, B, T)
        cmp, cmpT = _compress(cmp_l, pe_r, w1_big, w2_big, w2t_big, l, B)
        ynsa = _nsa(q, z, g, cmp, cmpT, ks, kw, vsT, vwT, e_pad, ovT, tabs, acmp, B, T)
        h = _outproj(ymix, ynsa, h, w_out_p, ln_g, ln_b, alpha, l)
    return h.reshape(B, T, D)
```

```python
import functools
import math

import numpy as np
import jax
import jax.numpy as jnp
from jax import lax
from jax.experimental import pallas as pl
from jax.experimental.pallas import tpu as pltpu

f32 = jnp.float32
bf16 = jnp.bfloat16

D_MODEL = 1024
D_POOL = 256
D_CONV = 256
D_NSA = 512
HEAD_DIM = 64
N_HEADS = 8
N_KV = 2
GROUP = 4
D_KV = 128
POOL_GROUP = 64
POOL_WINDOWS = (2, 4, 8, 16)
CONV_WIDTH = 3
CMP_BLOCK = 32
CMP_STRIDE = 16
CMP_HIDDEN = 128
SLC_BLOCK = 64
N_SELECT = 16
N_LOCAL = 2
WINDOW = 512
N_BUCKETS = 32
MAX_DISTANCE = 128
LN_EPS = 1e-5
FORCED = 1e9
NEG = -1e30

LANES = 128
TQ = 128
R = N_HEADS * TQ
NSB = 128
TT = 512
TT_OUT = 1024
UNROLL = 8
N_CLASSES = 4
HALO = 16
KT = 512
KPAD = KT
N_WIN = WINDOW + TQ
SUM_ROWS = 16
VROWS = HEAD_DIM + SUM_ROWS
GC = R // N_KV
LOG2E = math.log2(math.e)
_PAD_FLAG_COL = 32
MIX_W = 2 * D_POOL + 4 * D_CONV
VMEM_LIMIT = 56 * 1024 * 1024

_NT = (((1,), (1,)), ((), ()))


def _dot(a, b):
    return jnp.dot(a, b, preferred_element_type=f32)


def _dot_nt(a, b):
    return lax.dot_general(a, b, _NT, preferred_element_type=f32)


def _sigmoid(x):
    return 1.0 / (1.0 + jnp.exp(-x))


_C_MIX = (0, MIX_W)
_C_Q = (_C_MIX[1], _C_MIX[1] + D_NSA)
_C_Z = (_C_Q[1], _C_Q[1] + D_NSA)
_C_CMP = (_C_Z[1], _C_Z[1] + 2 * D_KV)
_C_K = (_C_CMP[1], _C_CMP[1] + 2 * D_KV)
_C_G = (_C_K[1], _C_K[1] + LANES)
W_ALL = _C_G[1]
CMP_ROWS = TT // CMP_STRIDE


def _local_mixers(ext, i, pw_ref, ps_ref, cw_ref):
    e = ext[:, 0:D_POOL]
    s2 = e + pltpu.roll(e, 1, axis=0)
    s4 = s2 + pltpu.roll(s2, 2, axis=0)
    s8 = s4 + pltpu.roll(s4, 4, axis=0)
    s16 = s8 + pltpu.roll(s8, 8, axis=0)
    lane = lax.broadcasted_iota(jnp.int32, (TT, D_POOL), 1)
    grp = lane // POOL_GROUP
    wsum = jnp.where(grp == 0, s2[HALO:], jnp.where(grp == 1, s4[HALO:], jnp.where(grp == 2, s8[HALO:], s16[HALO:])))
    win = jnp.left_shift(2, grp)
    pos = i * TT + lax.broadcasted_iota(jnp.int32, (TT, D_POOL), 0)
    cnt = jnp.minimum(pos + 1, win).astype(f32)
    v = e[HALO:]
    pooled = wsum / cnt - v
    y_pool = _dot(pooled.astype(bf16), pw_ref[...]) * ps_ref[...]
    zp = ext[HALO:, D_POOL:2 * D_POOL]
    y_pool = y_pool * (zp * _sigmoid(zp))

    o = 2 * D_POOL
    cb = ext[HALO:, o:o + D_CONV]
    u = ext[:, o + D_CONV:o + 2 * D_CONV] * ext[:, o + 2 * D_CONV:o + 3 * D_CONV]
    zc = ext[HALO:, o + 3 * D_CONV:o + 4 * D_CONV]
    conv = cw_ref[CONV_WIDTH - 1:CONV_WIDTH, :] * u[HALO:]
    for k in range(CONV_WIDTH - 1):
        conv = conv + cw_ref[k:k + 1, :] * pltpu.roll(u, CONV_WIDTH - 1 - k, axis=0)[HALO:]
    y_conv = cb * conv * (zc * _sigmoid(zc))
    return y_pool, y_conv


def _inproj_kernel(x_ref, w_ref, wvt_ref, pw_ref, ps_ref, cw_ref, ks_in, kw_in, vs_in, vw_in,
                   ymix_ref, q_ref, z_ref, cmp_ref, g_ref, ks_ref, kw_ref, vsT_ref, vwT_ref,
                   halo_ref, cscr_ref):
    del ks_in, kw_in, vs_in, vw_in
    i = pl.program_id(1)
    x = x_ref[...].astype(bf16)

    def proj(c):
        return _dot(x, w_ref[:, c[0]:c[1]])

    mix = proj(_C_MIX)
    halo = jnp.where(i > 0, halo_ref[...], 0.0)
    y_pool, y_conv = _local_mixers(jnp.concatenate([halo, mix], axis=0), i, pw_ref, ps_ref, cw_ref)
    halo_ref[...] = mix[TT - HALO:]
    ymix_ref[:, 0:D_POOL] = y_pool.astype(bf16)
    ymix_ref[:, D_POOL:D_POOL + D_CONV] = y_conv.astype(bf16)

    q_ref[...] = (proj(_C_Q) * (HEAD_DIM ** -0.5 * LOG2E)).astype(bf16)
    z_ref[...] = proj(_C_Z).astype(bf16)
    g_ref[...] = proj(_C_G)
    kk = proj(_C_K).astype(bf16)
    ks_ref[...] = kk[:, 0:D_KV]
    kw_ref[:, 0:D_KV] = kk[:, D_KV:2 * D_KV]
    kw_ref[:, D_KV:2 * D_KV] = jnp.zeros((TT, D_KV), bf16)
    vt = _dot_nt(wvt_ref[...], x)
    for g in range(N_KV):
        vsT_ref[g] = vt[g * HEAD_DIM:(g + 1) * HEAD_DIM].astype(bf16)
        vwT_ref[g] = vt[D_KV + g * HEAD_DIM:D_KV + (g + 1) * HEAD_DIM].astype(bf16)

    kvc = proj(_C_CMP)
    for c in range(2):
        cscr_ref[c] = kvc[:, c * LANES:(c + 1) * LANES]
        for l in range(CMP_STRIDE):
            cmp_ref[l, :, c * LANES:(c + 1) * LANES] = (
                cscr_ref[c, pl.ds(l, CMP_ROWS, stride=CMP_STRIDE), :].astype(bf16))


def _inproj(x2, w_all, wvt, pw_bd, pool_scale, conv_w, kpads, layer, B, T):
    nt = T // TT
    bt = B * T
    row = lambda b, i: (b * nt + i, 0)
    wsel = lambda b, i: (layer, 0, 0)
    any_spec = pl.BlockSpec(memory_space=pl.ANY)
    tp = T + KPAD
    out_shape = [jax.ShapeDtypeStruct((bt, D_POOL + D_CONV), bf16),
                 jax.ShapeDtypeStruct((bt, D_NSA), bf16),
                 jax.ShapeDtypeStruct((bt, D_NSA), bf16),
                 jax.ShapeDtypeStruct((CMP_STRIDE, bt // CMP_STRIDE, 2 * D_KV), bf16),
                 jax.ShapeDtypeStruct((bt, LANES), f32),
                 jax.ShapeDtypeStruct((B, tp, D_KV), bf16),
                 jax.ShapeDtypeStruct((B, tp, 2 * D_KV), bf16),
                 jax.ShapeDtypeStruct((B, N_KV, VROWS, tp), bf16),
                 jax.ShapeDtypeStruct((B, N_KV, VROWS, tp), bf16)]
    out_specs = [pl.BlockSpec((TT, D_POOL + D_CONV), row),
                 pl.BlockSpec((TT, D_NSA), row),
                 pl.BlockSpec((TT, D_NSA), row),
                 pl.BlockSpec((CMP_STRIDE, CMP_ROWS, 2 * D_KV), lambda b, i: (0, b * nt + i, 0)),
                 pl.BlockSpec((TT, LANES), row),
                 pl.BlockSpec((None, TT, D_KV), lambda b, i: (b, i + KPAD // TT, 0)),
                 pl.BlockSpec((None, TT, 2 * D_KV), lambda b, i: (b, i + KPAD // TT, 0)),
                 pl.BlockSpec((None, N_KV, HEAD_DIM, TT), lambda b, i: (b, 0, 0, i + KPAD // TT)),
                 pl.BlockSpec((None, N_KV, HEAD_DIM, TT), lambda b, i: (b, 0, 0, i + KPAD // TT))]
    return pl.pallas_call(
        _inproj_kernel,
        grid=(B, nt),
        in_specs=[pl.BlockSpec((TT, D_MODEL), row),
                  pl.BlockSpec((None, D_MODEL, W_ALL), wsel),
                  pl.BlockSpec((None, 2 * D_KV, D_MODEL), wsel),
                  pl.BlockSpec((None, D_POOL, D_POOL), wsel),
                  pl.BlockSpec((None, 1, D_POOL), wsel),
                  pl.BlockSpec((None, CONV_WIDTH, D_CONV), wsel),
                  any_spec, any_spec, any_spec, any_spec],
        out_specs=out_specs,
        out_shape=out_shape,
        input_output_aliases={6: 5, 7: 6, 8: 7, 9: 8},
        scratch_shapes=[pltpu.VMEM((HALO, MIX_W), f32), pltpu.VMEM((2, TT, LANES), f32)],
        compiler_params=pltpu.CompilerParams(dimension_semantics=("arbitrary", "arbitrary"),
                                             vmem_limit_bytes=VMEM_LIMIT),
        name="inproj",
    )(x2, w_all, wvt, pw_bd, pool_scale, conv_w, *kpads)


CH = 4 * CMP_HIDDEN


def _compress_kernel(r_ref, pe_ref, w1_ref, w2_ref, w2t_ref, out_ref, outT_ref):
    nc = r_ref.shape[1]
    rows = jnp.concatenate([r_ref[l] for l in range(CMP_STRIDE)], axis=1)
    zz = _dot(rows, w1_ref[...])
    pb = _dot(pe_ref[...], w1_ref[...])
    z0 = zz[:, 0:CH] + pb[0:1, 0:CH]
    z1 = zz[:, CH:2 * CH] + pb[1:2, CH:2 * CH]
    h = z0 + pltpu.roll(z1, nc - 1, axis=0)
    h = (h * _sigmoid(h)).astype(bf16)
    out_ref[...] = _dot(h, w2_ref[...]).astype(bf16)
    outT_ref[...] = _dot_nt(w2t_ref[...], h).astype(bf16)


def _compress(cmp_l, pe_r, w1_big, w2_big, w2t_big, layer, B):
    nc = cmp_l.shape[1] // B
    rw = CMP_STRIDE * 2 * D_KV
    wsel = lambda b: (layer, 0, 0)
    return pl.pallas_call(
        _compress_kernel,
        grid=(B,),
        in_specs=[pl.BlockSpec((CMP_STRIDE, nc, 2 * D_KV), lambda b: (0, b, 0)),
                  pl.BlockSpec((None, 8, rw), wsel),
                  pl.BlockSpec((None, rw, 2 * CH), wsel),
                  pl.BlockSpec((None, CH, 2 * D_KV), wsel),
                  pl.BlockSpec((None, 2 * D_KV, CH), wsel)],
        out_specs=[pl.BlockSpec((None, nc, 2 * D_KV), lambda b: (b, 0, 0)),
                   pl.BlockSpec((None, 2 * D_KV, nc), lambda b: (b, 0, 0))],
        out_shape=[jax.ShapeDtypeStruct((B, nc, 2 * D_KV), bf16),
                   jax.ShapeDtypeStruct((B, 2 * D_KV, nc), bf16)],
        compiler_params=pltpu.CompilerParams(dimension_semantics=("arbitrary",),
                                             vmem_limit_bytes=VMEM_LIMIT),
        name="compress",
    )(cmp_l, pe_r, w1_big, w2_big, w2t_big)


def _select_penalty(imp_t, t0):
    shape = imp_t.shape
    jblk = lax.broadcasted_iota(jnp.int32, shape, 0)
    tq = lax.broadcasted_iota(jnp.int32, shape, 1) % TQ
    back = jnp.right_shift(t0 + tq, int(math.log2(SLC_BLOCK))) - jblk
    causal = back >= 0
    forced = jnp.logical_or(jblk == 0, jnp.logical_and(causal, back < N_LOCAL))
    score = jnp.where(forced, -jnp.inf, jnp.where(causal, imp_t, NEG))
    jf = jblk.astype(f32)
    for _ in range(N_SELECT - N_LOCAL - 1):
        best = jnp.max(score, axis=0, keepdims=True)
        first = jnp.min(jnp.where(score == best, jf, float(shape[0])), axis=0, keepdims=True)
        score = jnp.where(jf == first, -jnp.inf, score)
    return jnp.where(score == -jnp.inf, 0.0, NEG)


def _query_cols(q_ref):
    sub = lax.broadcasted_iota(jnp.int32, (LANES, TQ), 0)
    low = sub < HEAD_DIM
    pairs = [q_ref[:, LANES * p:LANES * (p + 1)].astype(f32).T for p in range(GROUP)]
    blocks = [(jnp.where(low, pairs[h % GROUP], 0.0) if h < GROUP else jnp.where(low, 0.0, pairs[h % GROUP])).astype(bf16)
              for h in range(N_HEADS)]
    return jnp.concatenate(blocks, axis=1)


def _merge_head_pairs(outs):
    return [jnp.concatenate([outs[p], outs[p + GROUP]], axis=0).T for p in range(GROUP)]


def _cmp_scores(q_cmp, cmp_ref, t0, nc):
    n_i = lax.broadcasted_iota(jnp.int32, (nc, LANES), 0)
    j_i = lax.broadcasted_iota(jnp.int32, (nc, LANES), 1)
    nstart = t0 // CMP_STRIDE - 8
    in_window = jnp.logical_and(j_i < 32, n_i == nstart + jnp.where(j_i < 16, j_i, j_i - 16))
    future = jnp.logical_and(j_i == 32, n_i >= nstart + 15)
    place_b = jnp.where(jnp.logical_or(in_window, future), 1.0, 0.0).astype(bf16)
    kc_ext = jnp.concatenate([cmp_ref[0:nc, 0:D_KV], place_b], axis=1)
    return _dot(kc_ext, q_cmp)


def _cmp_finish(s1, g_ref, cmpT_ref, ovT_ref, pen_ref, y1_ref, t0, nsel):
    nc = s1.shape[0]
    m1 = jnp.max(s1, axis=0, keepdims=True)
    p1 = jnp.exp2(s1 - m1)
    p1 = p1.astype(bf16)
    tcol = t0 + lax.broadcasted_iota(jnp.int32, (1, GC), 1) % TQ
    sig_t = _sigmoid(g_ref[...]).T
    outs, sums = [], []
    for g in range(N_KV):
        lhs = jnp.concatenate([cmpT_ref[D_KV + g * HEAD_DIM:D_KV + (g + 1) * HEAD_DIM, 0:nc], ovT_ref[0:nsel, 0:nc],
                               ovT_ref[NSB:NSB + SUM_ROWS, 0:nc]], axis=0)
        both = _dot(lhs, p1[:, g * GC:(g + 1) * GC])
        l1 = both[HEAD_DIM + nsel:HEAD_DIM + nsel + 1]
        both = both * jnp.where(tcol >= CMP_BLOCK - 1, 1.0 / l1, 0.0)
        for r in range(GROUP):
            h = g * GROUP + r
            outs.append(sig_t[3 * h:3 * h + 1, :] * both[0:HEAD_DIM, r * TQ:(r + 1) * TQ])
        imp = both[HEAD_DIM:HEAD_DIM + nsel]
        acc = imp[:, 0:TQ]
        for r in range(1, GROUP):
            acc = acc + imp[:, r * TQ:(r + 1) * TQ]
        sums.append(acc)
    for p, y in enumerate(_merge_head_pairs(outs)):
        y1_ref[:, p * LANES:(p + 1) * LANES] = y.astype(bf16)
    imp_t = jnp.concatenate(sums, axis=1)
    pen_ref[0:nsel] = _select_penalty(imp_t, t0).astype(bf16)
    if nsel < NSB:
        pen_ref[nsel:NSB] = jnp.full((NSB - nsel, N_KV * TQ), NEG, bf16)


def _nsa_kernel(q_ref, qn_ref, z_ref, g_ref, gn_ref, cmp_ref, cmpT_ref, ovT_ref, ks_ref, kw_ref, vsT_ref, vwT_ref,
                e_ref, tab_ref, acmp_ref, out_ref,
                pen_ref, y1_ref, o3_ref, m_ref, l_ref, acc_ref, sa_ref, sb_ref, mxa_ref, mxb_ref):
    qi = pl.program_id(1)
    nq = pl.num_programs(1)
    t0 = qi * TQ
    nc = cmp_ref.shape[0]
    classes = [(nc * k // N_CLASSES, NSB * k // N_CLASSES) for k in range(1, N_CLASSES + 1)]
    last_q = [min(c // (TQ // CMP_STRIDE), s // (TQ // SLC_BLOCK)) - 1 for c, s in classes]

    @pl.when(qi == 0)
    def _():
        q_cmp0 = jnp.concatenate([_query_cols(q_ref), acmp_ref[...]], axis=0)
        _cmp_finish(_cmp_scores(q_cmp0, cmp_ref, t0, classes[0][0]), g_ref, cmpT_ref, ovT_ref, pen_ref, y1_ref, t0,
                    classes[0][1])

    qcols = _query_cols(q_ref)
    q_cmp = jnp.concatenate([qcols, acmp_ref[...]], axis=0)
    pens = [pen_ref[:, g * TQ:(g + 1) * TQ] for g in range(N_KV) for _ in range(GROUP)]
    q_slc = jnp.concatenate([qcols, jnp.concatenate(pens, axis=1)], axis=0)
    y1 = y1_ref[...]

    m_ref[...] = jnp.full((1, R), NEG, f32)
    l_ref[...] = jnp.zeros((1, R), f32)
    acc_ref[...] = jnp.zeros((HEAD_DIM, R), f32)
    n_far = (qi + KT // TQ) // (KT // TQ) - 1
    first = (qi + 1) * TQ - n_far * KT

    def tile_row(i):
        return pl.multiple_of(first + i * KT, LANES)

    def produce(i, s_ref, mx_ref):
        r0 = tile_row(i)
        k_ext = jnp.concatenate([ks_ref[pl.ds(r0, KT), :], e_ref[pl.ds(r0, KT), :]], axis=1)
        s = _dot(k_ext, q_slc)
        s_ref[...] = s
        mx_ref[...] = jnp.max(s, axis=0, keepdims=True)

    def consume(i, s_ref, mx_ref, last):
        m_prev = m_ref[...]
        if last:
            s = jnp.concatenate([s_ref[0:KT - 2 * TQ], s_ref[KT - 2 * TQ:KT] + tab_ref[TQ:3 * TQ]], axis=0)
            m_new = jnp.maximum(m_prev, jnp.max(s, axis=0, keepdims=True))
        else:
            s = s_ref[...]
            m_new = jnp.maximum(m_prev, mx_ref[...])
        alpha = jnp.exp2(m_prev - m_new)
        p = jnp.exp2(s - m_new)
        p = p.astype(bf16)
        for g in range(N_KV):
            c = slice(g * GC, (g + 1) * GC)
            pv = _dot(vsT_ref[g, :, pl.ds(tile_row(i), KT)], p[:, c])
            l_ref[:, c] = alpha[:, c] * l_ref[:, c] + pv[HEAD_DIM:HEAD_DIM + 1]
            acc_ref[:, c] = acc_ref[:, c] * alpha[:, c] + pv[0:HEAD_DIM]
        m_ref[...] = m_new

    qn = jnp.minimum(qi + 1, nq - 1)

    def front(nc_use, nsel_use):
        tn = qn * TQ
        s1 = _cmp_scores(jnp.concatenate([_query_cols(qn_ref), acmp_ref[...]], axis=0), cmp_ref, tn, nc_use)
        w0 = pl.multiple_of(t0, LANES)
        s3 = _dot(kw_ref[pl.ds(w0, N_WIN), :], q_cmp)
        _cmp_finish(s1, gn_ref, cmpT_ref, ovT_ref, pen_ref, y1_ref, tn, nsel_use)
        produce(0, sa_ref, mxa_ref)
        s3 = jnp.concatenate([s3[0:TQ] + tab_ref[0:TQ], s3[TQ:WINDOW - TQ], s3[WINDOW - TQ:] + tab_ref[TQ:3 * TQ]],
                             axis=0)
        p3 = jnp.exp2(s3 - jnp.max(s3, axis=0, keepdims=True)).astype(bf16)
        for g in range(N_KV):
            c = slice(g * GC, (g + 1) * GC)
            o3_ref[:, c] = _dot(vwT_ref[g, :, pl.ds(w0, N_WIN)], p3[:, c])

    for k, (nc_k, nsel_k) in enumerate(classes):
        lo = last_q[k - 1] if k else -1
        if k < N_CLASSES - 1:
            pl.when(jnp.logical_and(qn > lo, qn <= last_q[k]))(functools.partial(front, nc_k, nsel_k))
        else:
            pl.when(qn > lo)(functools.partial(front, nc_k, nsel_k))

    bufs = ((sa_ref, mxa_ref), (sb_ref, mxb_ref))

    def pipeline(base, count, last_at_end):
        for k in range(count):
            if k + 1 < count or not last_at_end:
                produce(base + k + 1, *bufs[(k + 1) % 2])
            consume(base + k, *bufs[k % 2], last_at_end and k == count - 1)

    def unrolled(j, c):
        pipeline(UNROLL * j, UNROLL, False)
        return c

    lax.fori_loop(0, n_far // UNROLL, unrolled, 0)

    def finish():
        inv2 = 1.0 / l_ref[...]
        inv3 = 1.0 / o3_ref[HEAD_DIM:HEAD_DIM + 1, :]
        sig_t = _sigmoid(g_ref[...]).T
        outs = []
        for h in range(N_HEADS):
            c = slice(h * TQ, (h + 1) * TQ)
            g2 = sig_t[3 * h + 1:3 * h + 2, :] * inv2[:, c]
            g3 = sig_t[3 * h + 2:3 * h + 3, :] * inv3[:, c]
            outs.append(g2 * acc_ref[:, c] + g3 * o3_ref[0:HEAD_DIM, c])
        for p, y in enumerate(_merge_head_pairs(outs)):
            c = slice(p * LANES, (p + 1) * LANES)
            zb = z_ref[:, c].astype(f32)
            out_ref[:, c] = ((y + y1[:, c].astype(f32)) * zb * _sigmoid(zb)).astype(bf16)

    for r in range(UNROLL):
        def drain(r=r):
            pipeline(n_far - r, r + 1, True)
            finish()
        pl.when(n_far % UNROLL == r)(drain)


def _nsa(q, z, g, cmp, cmpT, ks, kw, vsT, vwT, e_pad, ovT, tabs, acmp, B, T):
    nq = T // TQ
    nc = T // CMP_STRIDE
    tp = T + KPAD
    tile = lambda b, i: (b * nq + i, 0)
    nxt = lambda b, i: (b * nq + jnp.minimum(i + 1, nq - 1), 0)
    const2 = lambda b, i: (0, 0)
    batch3 = lambda b, i: (b, 0, 0)
    return pl.pallas_call(
        _nsa_kernel,
        grid=(B, nq),
        in_specs=[pl.BlockSpec((TQ, D_NSA), tile),
                  pl.BlockSpec((TQ, D_NSA), nxt),
                  pl.BlockSpec((TQ, D_NSA), tile),
                  pl.BlockSpec((TQ, LANES), tile),
                  pl.BlockSpec((TQ, LANES), nxt),
                  pl.BlockSpec((None, nc, 2 * D_KV), batch3),
                  pl.BlockSpec((None, 2 * D_KV, nc), batch3),
                  pl.BlockSpec((NSB + SUM_ROWS, nc), const2),
                  pl.BlockSpec((None, tp, D_KV), batch3),
                  pl.BlockSpec((None, tp, 2 * D_KV), batch3),
                  pl.BlockSpec((None, N_KV, VROWS, tp), lambda b, i: (b, 0, 0, 0)),
                  pl.BlockSpec((None, N_KV, VROWS, tp), lambda b, i: (b, 0, 0, 0)),
                  pl.BlockSpec((tp, NSB), const2),
                  pl.BlockSpec((3 * TQ, R), const2),
                  pl.BlockSpec((LANES, R), const2)],
        out_specs=pl.BlockSpec((TQ, D_NSA), tile),
        out_shape=jax.ShapeDtypeStruct((B * T, D_NSA), bf16),
        scratch_shapes=[pltpu.VMEM((NSB, N_KV * TQ), bf16), pltpu.VMEM((TQ, D_NSA), bf16), pltpu.VMEM((VROWS, R), f32),
                        pltpu.VMEM((1, R), f32), pltpu.VMEM((1, R), f32), pltpu.VMEM((HEAD_DIM, R), f32),
                        pltpu.VMEM((KT, R), f32), pltpu.VMEM((KT, R), f32),
                        pltpu.VMEM((1, R), f32), pltpu.VMEM((1, R), f32)],
        compiler_params=pltpu.CompilerParams(dimension_semantics=("arbitrary", "arbitrary"),
                                             vmem_limit_bytes=VMEM_LIMIT),
        name="nsa",
    )(q, q, z, g, g, cmp, cmpT, ovT, ks, kw, vsT, vwT, e_pad, tabs, acmp)


def _outproj_kernel(alpha, ymix_ref, ynsa_ref, x_ref, w_ref, g_ref, b_ref, out_ref):
    y = jnp.concatenate([ymix_ref[...], ynsa_ref[...]], axis=1)
    r = alpha * x_ref[...] + _dot(y, w_ref[...])
    mu = jnp.mean(r, axis=-1, keepdims=True)
    d = r - mu
    var = jnp.mean(d * d, axis=-1, keepdims=True)
    out_ref[...] = d * lax.rsqrt(var + LN_EPS) * g_ref[...] + b_ref[...]


def _outproj(ymix, ynsa, x2, w_out, ln_g, ln_b, alpha, layer):
    bt = x2.shape[0]
    row = lambda i: (i, 0)
    wsel = lambda i: (layer, 0, 0)
    return pl.pallas_call(
        functools.partial(_outproj_kernel, alpha),
        grid=(bt // TT_OUT,),
        in_specs=[pl.BlockSpec((TT_OUT, D_POOL + D_CONV), row),
                  pl.BlockSpec((TT_OUT, D_NSA), row),
                  pl.BlockSpec((TT_OUT, D_MODEL), row),
                  pl.BlockSpec((None, D_MODEL, D_MODEL), wsel),
                  pl.BlockSpec((None, 1, D_MODEL), wsel),
                  pl.BlockSpec((None, 1, D_MODEL), wsel)],
        out_specs=pl.BlockSpec((TT_OUT, D_MODEL), row),
        out_shape=jax.ShapeDtypeStruct((bt, D_MODEL), f32),
        compiler_params=pltpu.CompilerParams(dimension_semantics=("arbitrary",),
                                             vmem_limit_bytes=VMEM_LIMIT),
        name="outproj",
    )(ymix, ynsa, x2, w_out, ln_g, ln_b)


def _bucket_np(d):
    d = np.asarray(d)
    max_exact = N_BUCKETS // 2
    nf = np.maximum(d, 1).astype(np.float32)
    large = max_exact + (np.log(nf / np.float32(max_exact)) / np.float32(math.log(MAX_DISTANCE / max_exact))
                         * np.float32(N_BUCKETS - max_exact)).astype(np.int32)
    large = np.minimum(large, N_BUCKETS - 1)
    return np.where(d < max_exact, d, large)


_FAR_DIST = 113
assert _bucket_np(np.arange(_FAR_DIST, 4 * WINDOW)).min() == N_BUCKETS - 1

_PAIR_PERM = np.concatenate([np.concatenate([np.arange(HEAD_DIM) + HEAD_DIM * p,
                                             np.arange(HEAD_DIM) + HEAD_DIM * (p + GROUP)]) for p in range(GROUP)])


def _bias_tables(rel_bias):
    nd = 2 * TQ
    onehot = jnp.asarray(np.eye(N_BUCKETS, dtype=np.float32)[_bucket_np(np.arange(nd))])
    tabp = ((jnp.dot(onehot, rel_bias, precision=lax.Precision.HIGHEST)
             - rel_bias[N_BUCKETS - 1:N_BUCKETS, :]) * LOG2E).T
    sl = np.arange(TQ)[:, None]
    tl = np.arange(TQ)[None, :]
    neg = lambda n: jnp.full((N_HEADS, n), NEG, f32)

    def skew(u, rows):
        period = u.shape[1]
        return jnp.tile(u, (1, rows))[:, :rows * (period - 1)].reshape(N_HEADS, rows, period - 1)

    def tile_layout(t):
        return t.transpose(1, 0, 2).reshape(TQ, R)

    diag = tile_layout(skew(jnp.concatenate([tabp[:, 0:TQ], neg(TQ)], axis=1), TQ)[:, :, :TQ])
    prev = tile_layout(skew(jnp.concatenate([tabp[:, TQ:nd], tabp[:, 0:TQ]], axis=1), TQ)[:, :, :TQ])
    edge = jnp.asarray(np.tile(np.where(sl > tl, 0.0, NEG).astype(np.float32), (1, N_HEADS)))
    tabs = jnp.concatenate([edge, prev, diag], axis=0).astype(f32)

    j = np.arange(16)[None, :]
    off = 8 * CMP_STRIDE - (CMP_BLOCK - 1)
    dc = np.arange(TQ)[:, None] - CMP_STRIDE * j + off
    okc = jnp.asarray(dc >= 0)[None]
    period = 4 * TQ
    u = jnp.concatenate([tabp[:, off:nd], jnp.zeros((N_HEADS, off), f32), neg(period - nd - off), tabp[:, 0:off]], axis=1)
    vc = skew(u, nd)[:, 0:16 * CMP_STRIDE:CMP_STRIDE, 0:TQ].transpose(0, 2, 1)
    vc = jnp.where(okc, vc, NEG)
    hi = vc.astype(bf16)
    lo = jnp.where(okc, vc - hi.astype(f32), 0.0).astype(bf16)
    fut = jnp.full((N_HEADS, TQ, 1), NEG, bf16)
    pad = jnp.zeros((N_HEADS, TQ, LANES - _PAD_FLAG_COL - 1), bf16)
    acmp = jnp.concatenate([hi, lo, fut, pad], axis=-1).reshape(R, LANES).T
    return tabs, acmp


def _static_tables(T):
    nc = T // CMP_STRIDE
    ns = T // SLC_BLOCK
    cstart = np.arange(nc)[None, :] * CMP_STRIDE
    sstart = np.arange(NSB)[:, None] * SLC_BLOCK
    ov = np.clip(np.minimum(cstart + CMP_BLOCK, sstart + SLC_BLOCK) - np.maximum(cstart, sstart), 0, None) / CMP_STRIDE
    ov[ns:, :] = 0
    ov[:, nc - 1] = 0
    ov = np.concatenate([ov, np.ones((SUM_ROWS, nc))], axis=0)
    e_pad = np.concatenate([np.ones((KPAD, NSB), bool),
                            np.arange(T)[:, None] // SLC_BLOCK == np.arange(NSB)[None, :]], axis=0)
    return jnp.asarray(ov, bf16), jnp.asarray(e_pad, bf16)


def _pair_order(w, axis):
    shp = w.shape
    w = w.reshape(shp[:axis] + (N_KV, GROUP, HEAD_DIM) + shp[axis + 1:])
    return jnp.swapaxes(w, axis, axis + 1).reshape(shp)


def _prep_weights(w_in, w_out, pool_w, pe_k, w1_k, w2_k, pe_v, w1_v, w2_v):
    depth = w_in.shape[0]
    sizes = (D_POOL, D_POOL, D_CONV, D_CONV, D_CONV, D_CONV, D_NSA, D_KV, D_KV, D_KV, D_KV, D_KV, D_KV,
             3 * N_HEADS, D_NSA)
    offs = np.cumsum((0,) + sizes)
    w_in = w_in.astype(bf16)
    col = lambda i: w_in[:, :, offs[i]:offs[i + 1]]
    wg = jnp.pad(col(13), ((0, 0), (0, 0), (0, LANES - 3 * N_HEADS)))
    w_all = jnp.concatenate([w_in[:, :, 0:MIX_W], _pair_order(col(6), 2), _pair_order(col(14), 2),
                             col(7), col(8), col(9), col(11), wg], axis=2)
    wvt = jnp.swapaxes(jnp.concatenate([col(10), col(12)], axis=2), 1, 2)
    w_out = w_out.astype(bf16)
    nm = D_POOL + D_CONV
    w_out_p = jnp.concatenate([w_out[:, 0:nm], _pair_order(w_out[:, nm:], 1)], axis=1)

    eye_g = jnp.eye(len(POOL_WINDOWS), dtype=bf16)
    pw_bd = jnp.einsum('zgcd,gh->zgchd', pool_w.astype(bf16), eye_g).reshape(depth, D_POOL, D_POOL)

    half = CMP_BLOCK // 2
    eye2 = jnp.eye(2, dtype=bf16)
    rw = half * 2 * D_KV
    cols = []
    for a in range(2):
        for kv, w1 in enumerate((w1_k, w1_v)):
            wsel = w1.astype(bf16).reshape(depth, 2, half, 1, 1, HEAD_DIM, CMP_HIDDEN)[:, a]
            for g in range(N_KV):
                blk = jnp.pad(wsel, ((0, 0), (0, 0), (kv, 1 - kv), (g, N_KV - 1 - g), (0, 0), (0, 0)))
                cols.append(blk.reshape(depth, rw, CMP_HIDDEN))
    w1_big = jnp.concatenate(cols, axis=2)
    pes = jnp.stack([pe_k, pe_v], axis=1).astype(bf16).reshape(depth, 2, 2, half, HEAD_DIM)
    pe_r = jnp.broadcast_to(pes.transpose(0, 2, 3, 1, 4)[:, :, :, :, None, :],
                            (depth, 2, half, 2, N_KV, HEAD_DIM)).reshape(depth, 2, half * 2 * D_KV)
    pe_r = jnp.pad(pe_r, ((0, 0), (0, 6), (0, 0)))
    w2s = jnp.stack([w2_k, w2_v], axis=1).astype(bf16)
    w2_big = jnp.einsum('zkhd,kK,gG->zkghKGd', w2s, eye2, eye2).reshape(depth, CH, 2 * D_KV)
    return w_all, wvt, w_out_p, pw_bd, w1_big, pe_r, w2_big, jnp.swapaxes(w2_big, 1, 2)


def _padded_kv_init(B, T):
    tp = T + KPAD
    flag = np.zeros((1, tp, 2 * D_KV), np.float32)
    flag[:, :KPAD, D_KV + _PAD_FLAG_COL] = 1.0
    ones_rows = np.zeros((1, 1, VROWS, 1), np.float32)
    ones_rows[:, :, HEAD_DIM:] = 1.0
    vinit = jnp.broadcast_to(jnp.asarray(ones_rows, bf16), (B, N_KV, VROWS, tp))
    return (jnp.zeros((B, tp, D_KV), bf16), jnp.broadcast_to(jnp.asarray(flag, bf16), (B, tp, 2 * D_KV)),
            vinit, vinit)


def kernel(x, w_in, w_out, pool_w, pool_scale, conv_w, cmp_pe_k, cmp_w1_k, cmp_w2_k, cmp_pe_v, cmp_w1_v, cmp_w2_v,
           rel_bias, ln_g, ln_b):
    B, T, D = x.shape
    depth = w_in.shape[0]
    assert D == D_MODEL and T % TT == 0 and T // SLC_BLOCK <= NSB and T // SLC_BLOCK >= N_SELECT
    assert KPAD % TT == 0 and N_SELECT > N_LOCAL
    alpha = (2 * depth) ** 0.25
    tabs, acmp = _bias_tables(rel_bias)
    ovT, e_pad = _static_tables(T)
    w_all, wvt, w_out_p, pw_bd, w1_big, pe_r, w2_big, w2t_big = _prep_weights(
        w_in, w_out, pool_w, cmp_pe_k, cmp_w1_k, cmp_w2_k, cmp_pe_v, cmp_w1_v, cmp_w2_v)
    pool_scale = pool_scale.reshape(depth, 1, D_POOL)
    ln_g = ln_g.reshape(depth, 1, D_MODEL)
    ln_b = ln_b.reshape(depth, 1, D_MODEL)
    h = x.reshape(B * T, D)
    for l in range(depth):
        ymix, q, z, cmp_l, g, ks, kw, vsT, vwT = _inproj(h, w_all, wvt, pw_bd, pool_scale, conv_w,
                                                         _padded_kv_init(B, T), l, B, T)
        cmp, cmpT = _compress(cmp_l, pe_r, w1_big, w2_big, w2t_big, l, B)
        ynsa = _nsa(q, z, g, cmp, cmpT, ks, kw, vsT, vwT, e_pad, ovT, tabs, acmp, B, T)
        h = _outproj(ymix, ynsa, h, w_out_p, ln_g, ln_b, alpha, l)
    return h.reshape(B, T, D)
```

```python
import functools
import math

import numpy as np
import jax
import jax.numpy as jnp
from jax import lax
from jax.experimental import pallas as pl
from jax.experimental.pallas import tpu as pltpu

f32 = jnp.float32
bf16 = jnp.bfloat16

D_MODEL = 1024
D_POOL = 256
D_CONV = 256
D_NSA = 512
HEAD_DIM = 64
N_HEADS = 8
N_KV = 2
GROUP = 4
D_KV = 128
POOL_GROUP = 64
POOL_WINDOWS = (2, 4, 8, 16)
CONV_WIDTH = 3
CMP_BLOCK = 32
CMP_STRIDE = 16
CMP_HIDDEN = 128
SLC_BLOCK = 64
N_SELECT = 16
N_LOCAL = 2
WINDOW = 512
N_BUCKETS = 32
MAX_DISTANCE = 128
LN_EPS = 1e-5
FORCED = 1e9
NEG = -1e30

LANES = 128
TQ = 128
R = N_HEADS * TQ
NSB = 128
TT = 512
TT_OUT = 1024
UNROLL = 8
N_CLASSES = 4
HALO = 16
KT = 512
KPAD = KT
N_WIN = WINDOW + TQ
SUM_ROWS = 16
VROWS = HEAD_DIM + SUM_ROWS
GC = R // N_KV
LOG2E = math.log2(math.e)
_PAD_FLAG_COL = 32
MIX_W = 2 * D_POOL + 4 * D_CONV
VMEM_LIMIT = 56 * 1024 * 1024

_NT = (((1,), (1,)), ((), ()))


def _dot(a, b):
    return jnp.dot(a, b, preferred_element_type=f32)


def _dot_nt(a, b):
    return lax.dot_general(a, b, _NT, preferred_element_type=f32)


def _sigmoid(x):
    return 1.0 / (1.0 + jnp.exp(-x))


_C_MIX = (0, MIX_W)
_C_Q = (_C_MIX[1], _C_MIX[1] + D_NSA)
_C_Z = (_C_Q[1], _C_Q[1] + D_NSA)
_C_CMP = (_C_Z[1], _C_Z[1] + 2 * D_KV)
_C_K = (_C_CMP[1], _C_CMP[1] + 2 * D_KV)
_C_G = (_C_K[1], _C_K[1] + LANES)
W_ALL = _C_G[1]
CMP_ROWS = TT // CMP_STRIDE


def _local_mixers(ext, i, pw_ref, ps_ref, cw_ref):
    e = ext[:, 0:D_POOL]
    s2 = e + pltpu.roll(e, 1, axis=0)
    s4 = s2 + pltpu.roll(s2, 2, axis=0)
    s8 = s4 + pltpu.roll(s4, 4, axis=0)
    s16 = s8 + pltpu.roll(s8, 8, axis=0)
    lane = lax.broadcasted_iota(jnp.int32, (TT, D_POOL), 1)
    grp = lane // POOL_GROUP
    wsum = jnp.where(grp == 0, s2[HALO:], jnp.where(grp == 1, s4[HALO:], jnp.where(grp == 2, s8[HALO:], s16[HALO:])))
    win = jnp.left_shift(2, grp)
    pos = i * TT + lax.broadcasted_iota(jnp.int32, (TT, D_POOL), 0)
    cnt = jnp.minimum(pos + 1, win).astype(f32)
    v = e[HALO:]
    pooled = wsum / cnt - v
    y_pool = _dot(pooled.astype(bf16), pw_ref[...]) * ps_ref[...]
    zp = ext[HALO:, D_POOL:2 * D_POOL]
    y_pool = y_pool * (zp * _sigmoid(zp))

    o = 2 * D_POOL
    cb = ext[HALO:, o:o + D_CONV]
    u = ext[:, o + D_CONV:o + 2 * D_CONV] * ext[:, o + 2 * D_CONV:o + 3 * D_CONV]
    zc = ext[HALO:, o + 3 * D_CONV:o + 4 * D_CONV]
    conv = cw_ref[CONV_WIDTH - 1:CONV_WIDTH, :] * u[HALO:]
    for k in range(CONV_WIDTH - 1):
        conv = conv + cw_ref[k:k + 1, :] * pltpu.roll(u, CONV_WIDTH - 1 - k, axis=0)[HALO:]
    y_conv = cb * conv * (zc * _sigmoid(zc))
    return y_pool, y_conv


def _inproj_kernel(x_ref, w_ref, wvt_ref, pw_ref, ps_ref, cw_ref, ks_in, kw_in, vs_in, vw_in,
                   ymix_ref, q_ref, z_ref, cmp_ref, g_ref, ks_ref, kw_ref, vsT_ref, vwT_ref,
                   halo_ref, cscr_ref):
    del ks_in, kw_in, vs_in, vw_in
    i = pl.program_id(1)
    x = x_ref[...].astype(bf16)

    def proj(c):
        return _dot(x, w_ref[:, c[0]:c[1]])

    mix = proj(_C_MIX)
    halo = jnp.where(i > 0, halo_ref[...], 0.0)
    y_pool, y_conv = _local_mixers(jnp.concatenate([halo, mix], axis=0), i, pw_ref, ps_ref, cw_ref)
    halo_ref[...] = mix[TT - HALO:]
    ymix_ref[:, 0:D_POOL] = y_pool.astype(bf16)
    ymix_ref[:, D_POOL:D_POOL + D_CONV] = y_conv.astype(bf16)

    q_ref[...] = (proj(_C_Q) * (HEAD_DIM ** -0.5 * LOG2E)).astype(bf16)
    z_ref[...] = proj(_C_Z).astype(bf16)
    g_ref[...] = proj(_C_G)
    kk = proj(_C_K).astype(bf16)
    ks_ref[...] = kk[:, 0:D_KV]
    kw_ref[:, 0:D_KV] = kk[:, D_KV:2 * D_KV]
    kw_ref[:, D_KV:2 * D_KV] = jnp.zeros((TT, D_KV), bf16)
    vt = _dot_nt(wvt_ref[...], x)
    for g in range(N_KV):
        vsT_ref[g] = vt[g * HEAD_DIM:(g + 1) * HEAD_DIM].astype(bf16)
        vwT_ref[g] = vt[D_KV + g * HEAD_DIM:D_KV + (g + 1) * HEAD_DIM].astype(bf16)

    kvc = proj(_C_CMP)
    for c in range(2):
        cscr_ref[c] = kvc[:, c * LANES:(c + 1) * LANES]
        for l in range(CMP_STRIDE):
            cmp_ref[l, :, c * LANES:(c + 1) * LANES] = (
                cscr_ref[c, pl.ds(l, CMP_ROWS, stride=CMP_STRIDE), :].astype(bf16))


def _inproj(x2, w_all, wvt, pw_bd, pool_scale, conv_w, kpads, layer, B, T):
    nt = T // TT
    bt = B * T
    row = lambda b, i: (b * nt + i, 0)
    wsel = lambda b, i: (layer, 0, 0)
    any_spec = pl.BlockSpec(memory_space=pl.ANY)
    tp = T + KPAD
    out_shape = [jax.ShapeDtypeStruct((bt, D_POOL + D_CONV), bf16),
                 jax.ShapeDtypeStruct((bt, D_NSA), bf16),
                 jax.ShapeDtypeStruct((bt, D_NSA), bf16),
                 jax.ShapeDtypeStruct((CMP_STRIDE, bt // CMP_STRIDE, 2 * D_KV), bf16),
                 jax.ShapeDtypeStruct((bt, LANES), f32),
                 jax.ShapeDtypeStruct((B, tp, D_KV), bf16),
                 jax.ShapeDtypeStruct((B, tp, 2 * D_KV), bf16),
                 jax.ShapeDtypeStruct((B, N_KV, VROWS, tp), bf16),
                 jax.ShapeDtypeStruct((B, N_KV, VROWS, tp), bf16)]
    out_specs = [pl.BlockSpec((TT, D_POOL + D_CONV), row),
                 pl.BlockSpec((TT, D_NSA), row),
                 pl.BlockSpec((TT, D_NSA), row),
                 pl.BlockSpec((CMP_STRIDE, CMP_ROWS, 2 * D_KV), lambda b, i: (0, b * nt + i, 0)),
                 pl.BlockSpec((TT, LANES), row),
                 pl.BlockSpec((None, TT, D_KV), lambda b, i: (b, i + KPAD // TT, 0)),
                 pl.BlockSpec((None, TT, 2 * D_KV), lambda b, i: (b, i + KPAD // TT, 0)),
                 pl.BlockSpec((None, N_KV, HEAD_DIM, TT), lambda b, i: (b, 0, 0, i + KPAD // TT)),
                 pl.BlockSpec((None, N_KV, HEAD_DIM, TT), lambda b, i: (b, 0, 0, i + KPAD // TT))]
    return pl.pallas_call(
        _inproj_kernel,
        grid=(B, nt),
        in_specs=[pl.BlockSpec((TT, D_MODEL), row),
                  pl.BlockSpec((None, D_MODEL, W_ALL), wsel),
                  pl.BlockSpec((None, 2 * D_KV, D_MODEL), wsel),
                  pl.BlockSpec((None, D_POOL, D_POOL), wsel),
                  pl.BlockSpec((None, 1, D_POOL), wsel),
                  pl.BlockSpec((None, CONV_WIDTH, D_CONV), wsel),
                  any_spec, any_spec, any_spec, any_spec],
        out_specs=out_specs,
        out_shape=out_shape,
        input_output_aliases={6: 5, 7: 6, 8: 7, 9: 8},
        scratch_shapes=[pltpu.VMEM((HALO, MIX_W), f32), pltpu.VMEM((2, TT, LANES), f32)],
        compiler_params=pltpu.CompilerParams(dimension_semantics=("arbitrary", "arbitrary"),
                                             vmem_limit_bytes=VMEM_LIMIT),
        name="inproj",
    )(x2, w_all, wvt, pw_bd, pool_scale, conv_w, *kpads)


CH = 4 * CMP_HIDDEN


def _compress_kernel(r_ref, pe_ref, w1_ref, w2_ref, w2t_ref, out_ref, outT_ref):
    nc = r_ref.shape[1]
    rows = jnp.concatenate([r_ref[l] for l in range(CMP_STRIDE)], axis=1)
    zz = _dot(rows, w1_ref[...])
    pb = _dot(pe_ref[...], w1_ref[...])
    z0 = zz[:, 0:CH] + pb[0:1, 0:CH]
    z1 = zz[:, CH:2 * CH] + pb[1:2, CH:2 * CH]
    h = z0 + pltpu.roll(z1, nc - 1, axis=0)
    h = (h * _sigmoid(h)).astype(bf16)
    out_ref[...] = _dot(h, w2_ref[...]).astype(bf16)
    outT_ref[...] = _dot_nt(w2t_ref[...], h).astype(bf16)


def _compress(cmp_l, pe_r, w1_big, w2_big, w2t_big, layer, B):
    nc = cmp_l.shape[1] // B
    rw = CMP_STRIDE * 2 * D_KV
    wsel = lambda b: (layer, 0, 0)
    return pl.pallas_call(
        _compress_kernel,
        grid=(B,),
        in_specs=[pl.BlockSpec((CMP_STRIDE, nc, 2 * D_KV), lambda b: (0, b, 0)),
                  pl.BlockSpec((None, 8, rw), wsel),
                  pl.BlockSpec((None, rw, 2 * CH), wsel),
                  pl.BlockSpec((None, CH, 2 * D_KV), wsel),
                  pl.BlockSpec((None, 2 * D_KV, CH), wsel)],
        out_specs=[pl.BlockSpec((None, nc, 2 * D_KV), lambda b: (b, 0, 0)),
                   pl.BlockSpec((None, 2 * D_KV, nc), lambda b: (b, 0, 0))],
        out_shape=[jax.ShapeDtypeStruct((B, nc, 2 * D_KV), bf16),
                   jax.ShapeDtypeStruct((B, 2 * D_KV, nc), bf16)],
        compiler_params=pltpu.CompilerParams(dimension_semantics=("arbitrary",),
                                             vmem_limit_bytes=VMEM_LIMIT),
        name="compress",
    )(cmp_l, pe_r, w1_big, w2_big, w2t_big)


def _select_penalty(imp_t, t0):
    shape = imp_t.shape
    jblk = lax.broadcasted_iota(jnp.int32, shape, 0)
    tq = lax.broadcasted_iota(jnp.int32, shape, 1) % TQ
    back = jnp.right_shift(t0 + tq, int(math.log2(SLC_BLOCK))) - jblk
    causal = back >= 0
    forced = jnp.logical_or(jblk == 0, jnp.logical_and(causal, back < N_LOCAL))
    score = jnp.where(forced, -jnp.inf, jnp.where(causal, imp_t, NEG))
    jf = jblk.astype(f32)
    for _ in range(N_SELECT - N_LOCAL - 1):
        best = jnp.max(score, axis=0, keepdims=True)
        first = jnp.min(jnp.where(score == best, jf, float(shape[0])), axis=0, keepdims=True)
        score = jnp.where(jf == first, -jnp.inf, score)
    return jnp.where(score == -jnp.inf, 0.0, NEG)


def _query_cols(q_ref):
    sub = lax.broadcasted_iota(jnp.int32, (LANES, TQ), 0)
    low = sub < HEAD_DIM
    pairs = [q_ref[:, LANES * p:LANES * (p + 1)].astype(f32).T for p in range(GROUP)]
    blocks = [(jnp.where(low, pairs[h % GROUP], 0.0) if h < GROUP else jnp.where(low, 0.0, pairs[h % GROUP])).astype(bf16)
              for h in range(N_HEADS)]
    return jnp.concatenate(blocks, axis=1)


def _merge_head_pairs(outs):
    return [jnp.concatenate([outs[p], outs[p + GROUP]], axis=0).T for p in range(GROUP)]


def _cmp_scores(q_cmp, cmp_ref, t0, nc):
    n_i = lax.broadcasted_iota(jnp.int32, (nc, LANES), 0)
    j_i = lax.broadcasted_iota(jnp.int32, (nc, LANES), 1)
    nstart = t0 // CMP_STRIDE - 8
    in_window = jnp.logical_and(j_i < 32, n_i == nstart + jnp.where(j_i < 16, j_i, j_i - 16))
    future = jnp.logical_and(j_i == 32, n_i >= nstart + 15)
    place_b = jnp.where(jnp.logical_or(in_window, future), 1.0, 0.0).astype(bf16)
    kc_ext = jnp.concatenate([cmp_ref[0:nc, 0:D_KV], place_b], axis=1)
    return _dot(kc_ext, q_cmp)


def _cmp_finish(s1, sig_t, cmpT_ref, ovT_ref, pen_ref, y1_ref, t0, nsel):
    nc = s1.shape[0]
    m1 = jnp.max(s1, axis=0, keepdims=True)
    p1 = jnp.exp2(s1 - m1)
    p1 = p1.astype(bf16)
    tcol = t0 + lax.broadcasted_iota(jnp.int32, (1, GC), 1) % TQ
    outs, sums = [], []
    for g in range(N_KV):
        lhs = jnp.concatenate([cmpT_ref[D_KV + g * HEAD_DIM:D_KV + (g + 1) * HEAD_DIM, 0:nc], ovT_ref[0:nsel, 0:nc],
                               ovT_ref[NSB:NSB + SUM_ROWS, 0:nc]], axis=0)
        both = _dot(lhs, p1[:, g * GC:(g + 1) * GC])
        l1 = both[HEAD_DIM + nsel:HEAD_DIM + nsel + 1]
        both = both * jnp.where(tcol >= CMP_BLOCK - 1, 1.0 / l1, 0.0)
        for r in range(GROUP):
            h = g * GROUP + r
            outs.append(sig_t[3 * h:3 * h + 1, :] * both[0:HEAD_DIM, r * TQ:(r + 1) * TQ])
        imp = both[HEAD_DIM:HEAD_DIM + nsel]
        acc = imp[:, 0:TQ]
        for r in range(1, GROUP):
            acc = acc + imp[:, r * TQ:(r + 1) * TQ]
        sums.append(acc)
    for p, y in enumerate(_merge_head_pairs(outs)):
        y1_ref[:, p * LANES:(p + 1) * LANES] = y.astype(bf16)
    imp_t = jnp.concatenate(sums, axis=1)
    pen_ref[0:nsel] = _select_penalty(imp_t, t0).astype(bf16)
    if nsel < NSB:
        pen_ref[nsel:NSB] = jnp.full((NSB - nsel, N_KV * TQ), NEG, bf16)


def _nsa_kernel(q0_ref, qn_ref, z_ref, g0_ref, gn_ref, cmp_ref, cmpT_ref, ovT_ref, ks_ref, kw_ref, vsT_ref, vwT_ref,
                e_ref, tab_ref, acmp_ref, out_ref,
                qc_ref, sg_ref, pen_ref, y1_ref, o3_ref, m_ref, l_ref, acc_ref, sa_ref, sb_ref, mxa_ref, mxb_ref):
    qi = pl.program_id(1)
    nq = pl.num_programs(1)
    t0 = qi * TQ
    nc = cmp_ref.shape[0]
    classes = [(nc * k // N_CLASSES, NSB * k // N_CLASSES) for k in range(1, N_CLASSES + 1)]
    last_q = [min(c // (TQ // CMP_STRIDE), s // (TQ // SLC_BLOCK)) - 1 for c, s in classes]

    @pl.when(qi == 0)
    def _():
        qc_ref[...] = _query_cols(q0_ref)
        sg_ref[...] = _sigmoid(g0_ref[...]).T
        q_cmp0 = jnp.concatenate([qc_ref[...], acmp_ref[...]], axis=0)
        _cmp_finish(_cmp_scores(q_cmp0, cmp_ref, t0, classes[0][0]), sg_ref[...], cmpT_ref, ovT_ref, pen_ref, y1_ref, t0,
                    classes[0][1])

    qcols = qc_ref[...]
    sig_cur = sg_ref[...]
    q_cmp = jnp.concatenate([qcols, acmp_ref[...]], axis=0)
    pens = [pen_ref[:, g * TQ:(g + 1) * TQ] for g in range(N_KV) for _ in range(GROUP)]
    q_slc = jnp.concatenate([qcols, jnp.concatenate(pens, axis=1)], axis=0)
    y1 = y1_ref[...]

    m_ref[...] = jnp.full((1, R), NEG, f32)
    l_ref[...] = jnp.zeros((1, R), f32)
    acc_ref[...] = jnp.zeros((HEAD_DIM, R), f32)
    n_far = (qi + KT // TQ) // (KT // TQ) - 1
    first = (qi + 1) * TQ - n_far * KT

    def tile_row(i):
        return pl.multiple_of(first + i * KT, LANES)

    def produce(i, s_ref, mx_ref):
        r0 = tile_row(i)
        k_ext = jnp.concatenate([ks_ref[pl.ds(r0, KT), :], e_ref[pl.ds(r0, KT), :]], axis=1)
        s = _dot(k_ext, q_slc)
        s_ref[...] = s
        mx_ref[...] = jnp.max(s, axis=0, keepdims=True)

    def consume(i, s_ref, mx_ref, last):
        m_prev = m_ref[...]
        if last:
            s = jnp.concatenate([s_ref[0:KT - 2 * TQ], s_ref[KT - 2 * TQ:KT] + tab_ref[TQ:3 * TQ]], axis=0)
            m_new = jnp.maximum(m_prev, jnp.max(s, axis=0, keepdims=True))
        else:
            s = s_ref[...]
            m_new = jnp.maximum(m_prev, mx_ref[...])
        alpha = jnp.exp2(m_prev - m_new)
        p = jnp.exp2(s - m_new)
        p = p.astype(bf16)
        for g in range(N_KV):
            c = slice(g * GC, (g + 1) * GC)
            pv = _dot(vsT_ref[g, :, pl.ds(tile_row(i), KT)], p[:, c])
            l_ref[:, c] = alpha[:, c] * l_ref[:, c] + pv[HEAD_DIM:HEAD_DIM + 1]
            acc_ref[:, c] = acc_ref[:, c] * alpha[:, c] + pv[0:HEAD_DIM]
        m_ref[...] = m_new

    qn = jnp.minimum(qi + 1, nq - 1)

    def front(nc_use, nsel_use):
        tn = qn * TQ
        qc_next = _query_cols(qn_ref)
        sig_next = _sigmoid(gn_ref[...]).T
        qc_ref[...] = qc_next
        sg_ref[...] = sig_next
        s1 = _cmp_scores(jnp.concatenate([qc_next, acmp_ref[...]], axis=0), cmp_ref, tn, nc_use)
        w0 = pl.multiple_of(t0, LANES)
        s3 = _dot(kw_ref[pl.ds(w0, N_WIN), :], q_cmp)
        _cmp_finish(s1, sig_next, cmpT_ref, ovT_ref, pen_ref, y1_ref, tn, nsel_use)
        produce(0, sa_ref, mxa_ref)
        s3 = jnp.concatenate([s3[0:TQ] + tab_ref[0:TQ], s3[TQ:WINDOW - TQ], s3[WINDOW - TQ:] + tab_ref[TQ:3 * TQ]],
                             axis=0)
        p3 = jnp.exp2(s3 - jnp.max(s3, axis=0, keepdims=True)).astype(bf16)
        for g in range(N_KV):
            c = slice(g * GC, (g + 1) * GC)
            o3_ref[:, c] = _dot(vwT_ref[g, :, pl.ds(w0, N_WIN)], p3[:, c])

    for k, (nc_k, nsel_k) in enumerate(classes):
        lo = last_q[k - 1] if k else -1
        if k < N_CLASSES - 1:
            pl.when(jnp.logical_and(qn > lo, qn <= last_q[k]))(functools.partial(front, nc_k, nsel_k))
        else:
            pl.when(qn > lo)(functools.partial(front, nc_k, nsel_k))

    bufs = ((sa_ref, mxa_ref), (sb_ref, mxb_ref))

    def pipeline(base, count, last_at_end):
        for k in range(count):
            if k + 1 < count or not last_at_end:
                produce(base + k + 1, *bufs[(k + 1) % 2])
            consume(base + k, *bufs[k % 2], last_at_end and k == count - 1)

    def unrolled(j, c):
        pipeline(UNROLL * j, UNROLL, False)
        return c

    lax.fori_loop(0, n_far // UNROLL, unrolled, 0)

    def finish():
        inv2 = 1.0 / l_ref[...]
        inv3 = 1.0 / o3_ref[HEAD_DIM:HEAD_DIM + 1, :]
        sig_t = sig_cur
        outs = []
        for h in range(N_HEADS):
            c = slice(h * TQ, (h + 1) * TQ)
            g2 = sig_t[3 * h + 1:3 * h + 2, :] * inv2[:, c]
            g3 = sig_t[3 * h + 2:3 * h + 3, :] * inv3[:, c]
            outs.append(g2 * acc_ref[:, c] + g3 * o3_ref[0:HEAD_DIM, c])
        for p, y in enumerate(_merge_head_pairs(outs)):
            c = slice(p * LANES, (p + 1) * LANES)
            zb = z_ref[:, c].astype(f32)
            out_ref[:, c] = ((y + y1[:, c].astype(f32)) * zb * _sigmoid(zb)).astype(bf16)

    for r in range(UNROLL):
        def drain(r=r):
            pipeline(n_far - r, r + 1, True)
            finish()
        pl.when(n_far % UNROLL == r)(drain)


def _nsa(q, z, g, cmp, cmpT, ks, kw, vsT, vwT, e_pad, ovT, tabs, acmp, B, T):
    nq = T // TQ
    nc = T // CMP_STRIDE
    tp = T + KPAD
    tile = lambda b, i: (b * nq + i, 0)
    nxt = lambda b, i: (b * nq + jnp.minimum(i + 1, nq - 1), 0)
    head = lambda b, i: (b * nq, 0)
    const2 = lambda b, i: (0, 0)
    batch3 = lambda b, i: (b, 0, 0)
    return pl.pallas_call(
        _nsa_kernel,
        grid=(B, nq),
        in_specs=[pl.BlockSpec((TQ, D_NSA), head),
                  pl.BlockSpec((TQ, D_NSA), nxt),
                  pl.BlockSpec((TQ, D_NSA), tile),
                  pl.BlockSpec((TQ, LANES), head),
                  pl.BlockSpec((TQ, LANES), nxt),
                  pl.BlockSpec((None, nc, 2 * D_KV), batch3),
                  pl.BlockSpec((None, 2 * D_KV, nc), batch3),
                  pl.BlockSpec((NSB + SUM_ROWS, nc), const2),
                  pl.BlockSpec((None, tp, D_KV), batch3),
                  pl.BlockSpec((None, tp, 2 * D_KV), batch3),
                  pl.BlockSpec((None, N_KV, VROWS, tp), lambda b, i: (b, 0, 0, 0)),
                  pl.BlockSpec((None, N_KV, VROWS, tp), lambda b, i: (b, 0, 0, 0)),
                  pl.BlockSpec((tp, NSB), const2),
                  pl.BlockSpec((3 * TQ, R), const2),
                  pl.BlockSpec((LANES, R), const2)],
        out_specs=pl.BlockSpec((TQ, D_NSA), tile),
        out_shape=jax.ShapeDtypeStruct((B * T, D_NSA), bf16),
        scratch_shapes=[pltpu.VMEM((LANES, R), bf16), pltpu.VMEM((LANES, TQ), f32),
                        pltpu.VMEM((NSB, N_KV * TQ), bf16), pltpu.VMEM((TQ, D_NSA), bf16), pltpu.VMEM((VROWS, R), f32),
                        pltpu.VMEM((1, R), f32), pltpu.VMEM((1, R), f32), pltpu.VMEM((HEAD_DIM, R), f32),
                        pltpu.VMEM((KT, R), f32), pltpu.VMEM((KT, R), f32),
                        pltpu.VMEM((1, R), f32), pltpu.VMEM((1, R), f32)],
        compiler_params=pltpu.CompilerParams(dimension_semantics=("arbitrary", "arbitrary"),
                                             vmem_limit_bytes=VMEM_LIMIT),
        name="nsa",
    )(q, q, z, g, g, cmp, cmpT, ovT, ks, kw, vsT, vwT, e_pad, tabs, acmp)


def _outproj_kernel(alpha, ymix_ref, ynsa_ref, x_ref, w_ref, g_ref, b_ref, out_ref):
    y = jnp.concatenate([ymix_ref[...], ynsa_ref[...]], axis=1)
    r = alpha * x_ref[...] + _dot(y, w_ref[...])
    mu = jnp.mean(r, axis=-1, keepdims=True)
    d = r - mu
    var = jnp.mean(d * d, axis=-1, keepdims=True)
    out_ref[...] = d * lax.rsqrt(var + LN_EPS) * g_ref[...] + b_ref[...]


def _outproj(ymix, ynsa, x2, w_out, ln_g, ln_b, alpha, layer):
    bt = x2.shape[0]
    row = lambda i: (i, 0)
    wsel = lambda i: (layer, 0, 0)
    return pl.pallas_call(
        functools.partial(_outproj_kernel, alpha),
        grid=(bt // TT_OUT,),
        in_specs=[pl.BlockSpec((TT_OUT, D_POOL + D_CONV), row),
                  pl.BlockSpec((TT_OUT, D_NSA), row),
                  pl.BlockSpec((TT_OUT, D_MODEL), row),
                  pl.BlockSpec((None, D_MODEL, D_MODEL), wsel),
                  pl.BlockSpec((None, 1, D_MODEL), wsel),
                  pl.BlockSpec((None, 1, D_MODEL), wsel)],
        out_specs=pl.BlockSpec((TT_OUT, D_MODEL), row),
        out_shape=jax.ShapeDtypeStruct((bt, D_MODEL), f32),
        compiler_params=pltpu.CompilerParams(dimension_semantics=("arbitrary",),
                                             vmem_limit_bytes=VMEM_LIMIT),
        name="outproj",
    )(ymix, ynsa, x2, w_out, ln_g, ln_b)


def _bucket_np(d):
    d = np.asarray(d)
    max_exact = N_BUCKETS // 2
    nf = np.maximum(d, 1).astype(np.float32)
    large = max_exact + (np.log(nf / np.float32(max_exact)) / np.float32(math.log(MAX_DISTANCE / max_exact))
                         * np.float32(N_BUCKETS - max_exact)).astype(np.int32)
    large = np.minimum(large, N_BUCKETS - 1)
    return np.where(d < max_exact, d, large)


_FAR_DIST = 113
assert _bucket_np(np.arange(_FAR_DIST, 4 * WINDOW)).min() == N_BUCKETS - 1

_PAIR_PERM = np.concatenate([np.concatenate([np.arange(HEAD_DIM) + HEAD_DIM * p,
                                             np.arange(HEAD_DIM) + HEAD_DIM * (p + GROUP)]) for p in range(GROUP)])


def _bias_tables(rel_bias):
    nd = 2 * TQ
    onehot = jnp.asarray(np.eye(N_BUCKETS, dtype=np.float32)[_bucket_np(np.arange(nd))])
    tabp = ((jnp.dot(onehot, rel_bias, precision=lax.Precision.HIGHEST)
             - rel_bias[N_BUCKETS - 1:N_BUCKETS, :]) * LOG2E).T
    sl = np.arange(TQ)[:, None]
    tl = np.arange(TQ)[None, :]
    neg = lambda n: jnp.full((N_HEADS, n), NEG, f32)

    def skew(u, rows):
        period = u.shape[1]
        return jnp.tile(u, (1, rows))[:, :rows * (period - 1)].reshape(N_HEADS, rows, period - 1)

    def tile_layout(t):
        return t.transpose(1, 0, 2).reshape(TQ, R)

    diag = tile_layout(skew(jnp.concatenate([tabp[:, 0:TQ], neg(TQ)], axis=1), TQ)[:, :, :TQ])
    prev = tile_layout(skew(jnp.concatenate([tabp[:, TQ:nd], tabp[:, 0:TQ]], axis=1), TQ)[:, :, :TQ])
    edge = jnp.asarray(np.tile(np.where(sl > tl, 0.0, NEG).astype(np.float32), (1, N_HEADS)))
    tabs = jnp.concatenate([edge, prev, diag], axis=0).astype(f32)

    j = np.arange(16)[None, :]
    off = 8 * CMP_STRIDE - (CMP_BLOCK - 1)
    dc = np.arange(TQ)[:, None] - CMP_STRIDE * j + off
    okc = jnp.asarray(dc >= 0)[None]
    period = 4 * TQ
    u = jnp.concatenate([tabp[:, off:nd], jnp.zeros((N_HEADS, off), f32), neg(period - nd - off), tabp[:, 0:off]], axis=1)
    vc = skew(u, nd)[:, 0:16 * CMP_STRIDE:CMP_STRIDE, 0:TQ].transpose(0, 2, 1)
    vc = jnp.where(okc, vc, NEG)
    hi = vc.astype(bf16)
    lo = jnp.where(okc, vc - hi.astype(f32), 0.0).astype(bf16)
    fut = jnp.full((N_HEADS, TQ, 1), NEG, bf16)
    pad = jnp.zeros((N_HEADS, TQ, LANES - _PAD_FLAG_COL - 1), bf16)
    acmp = jnp.concatenate([hi, lo, fut, pad], axis=-1).reshape(R, LANES).T
    return tabs, acmp


def _static_tables(T):
    nc = T // CMP_STRIDE
    ns = T // SLC_BLOCK
    cstart = np.arange(nc)[None, :] * CMP_STRIDE
    sstart = np.arange(NSB)[:, None] * SLC_BLOCK
    ov = np.clip(np.minimum(cstart + CMP_BLOCK, sstart + SLC_BLOCK) - np.maximum(cstart, sstart), 0, None) / CMP_STRIDE
    ov[ns:, :] = 0
    ov[:, nc - 1] = 0
    ov = np.concatenate([ov, np.ones((SUM_ROWS, nc))], axis=0)
    e_pad = np.concatenate([np.ones((KPAD, NSB), bool),
                            np.arange(T)[:, None] // SLC_BLOCK == np.arange(NSB)[None, :]], axis=0)
    return jnp.asarray(ov, bf16), jnp.asarray(e_pad, bf16)


def _pair_order(w, axis):
    shp = w.shape
    w = w.reshape(shp[:axis] + (N_KV, GROUP, HEAD_DIM) + shp[axis + 1:])
    return jnp.swapaxes(w, axis, axis + 1).reshape(shp)


def _prep_weights(w_in, w_out, pool_w, pe_k, w1_k, w2_k, pe_v, w1_v, w2_v):
    depth = w_in.shape[0]
    sizes = (D_POOL, D_POOL, D_CONV, D_CONV, D_CONV, D_CONV, D_NSA, D_KV, D_KV, D_KV, D_KV, D_KV, D_KV,
             3 * N_HEADS, D_NSA)
    offs = np.cumsum((0,) + sizes)
    w_in = w_in.astype(bf16)
    col = lambda i: w_in[:, :, offs[i]:offs[i + 1]]
    wg = jnp.pad(col(13), ((0, 0), (0, 0), (0, LANES - 3 * N_HEADS)))
    w_all = jnp.concatenate([w_in[:, :, 0:MIX_W], _pair_order(col(6), 2), _pair_order(col(14), 2),
                             col(7), col(8), col(9), col(11), wg], axis=2)
    wvt = jnp.swapaxes(jnp.concatenate([col(10), col(12)], axis=2), 1, 2)
    w_out = w_out.astype(bf16)
    nm = D_POOL + D_CONV
    w_out_p = jnp.concatenate([w_out[:, 0:nm], _pair_order(w_out[:, nm:], 1)], axis=1)

    eye_g = jnp.eye(len(POOL_WINDOWS), dtype=bf16)
    pw_bd = jnp.einsum('zgcd,gh->zgchd', pool_w.astype(bf16), eye_g).reshape(depth, D_POOL, D_POOL)

    half = CMP_BLOCK // 2
    eye2 = jnp.eye(2, dtype=bf16)
    rw = half * 2 * D_KV
    cols = []
    for a in range(2):
        for kv, w1 in enumerate((w1_k, w1_v)):
            wsel = w1.astype(bf16).reshape(depth, 2, half, 1, 1, HEAD_DIM, CMP_HIDDEN)[:, a]
            for g in range(N_KV):
                blk = jnp.pad(wsel, ((0, 0), (0, 0), (kv, 1 - kv), (g, N_KV - 1 - g), (0, 0), (0, 0)))
                cols.append(blk.reshape(depth, rw, CMP_HIDDEN))
    w1_big = jnp.concatenate(cols, axis=2)
    pes = jnp.stack([pe_k, pe_v], axis=1).astype(bf16).reshape(depth, 2, 2, half, HEAD_DIM)
    pe_r = jnp.broadcast_to(pes.transpose(0, 2, 3, 1, 4)[:, :, :, :, None, :],
                            (depth, 2, half, 2, N_KV, HEAD_DIM)).reshape(depth, 2, half * 2 * D_KV)
    pe_r = jnp.pad(pe_r, ((0, 0), (0, 6), (0, 0)))
    w2s = jnp.stack([w2_k, w2_v], axis=1).astype(bf16)
    w2_big = jnp.einsum('zkhd,kK,gG->zkghKGd', w2s, eye2, eye2).reshape(depth, CH, 2 * D_KV)
    return w_all, wvt, w_out_p, pw_bd, w1_big, pe_r, w2_big, jnp.swapaxes(w2_big, 1, 2)


def _padded_kv_init(B, T):
    tp = T + KPAD
    flag = np.zeros((1, tp, 2 * D_KV), np.float32)
    flag[:, :KPAD, D_KV + _PAD_FLAG_COL] = 1.0
    ones_rows = np.zeros((1, 1, VROWS, 1), np.float32)
    ones_rows[:, :, HEAD_DIM:] = 1.0
    vinit = jnp.broadcast_to(jnp.asarray(ones_rows, bf16), (B, N_KV, VROWS, tp))
    return (jnp.zeros((B, tp, D_KV), bf16), jnp.broadcast_to(jnp.asarray(flag, bf16), (B, tp, 2 * D_KV)),
            vinit, vinit)


def kernel(x, w_in, w_out, pool_w, pool_scale, conv_w, cmp_pe_k, cmp_w1_k, cmp_w2_k, cmp_pe_v, cmp_w1_v, cmp_w2_v,
           rel_bias, ln_g, ln_b):
    B, T, D = x.shape
    depth = w_in.shape[0]
    assert D == D_MODEL and T % TT == 0 and T // SLC_BLOCK <= NSB and T // SLC_BLOCK >= N_SELECT
    assert KPAD % TT == 0 and N_SELECT > N_LOCAL
    alpha = (2 * depth) ** 0.25
    tabs, acmp = _bias_tables(rel_bias)
    ovT, e_pad = _static_tables(T)
    w_all, wvt, w_out_p, pw_bd, w1_big, pe_r, w2_big, w2t_big = _prep_weights(
        w_in, w_out, pool_w, cmp_pe_k, cmp_w1_k, cmp_w2_k, cmp_pe_v, cmp_w1_v, cmp_w2_v)
    pool_scale = pool_scale.reshape(depth, 1, D_POOL)
    ln_g = ln_g.reshape(depth, 1, D_MODEL)
    ln_b = ln_b.reshape(depth, 1, D_MODEL)
    h = x.reshape(B * T, D)
    for l in range(depth):
        ymix, q, z, cmp_l, g, ks, kw, vsT, vwT = _inproj(h, w_all, wvt, pw_bd, pool_scale, conv_w,
                                                         _padded_kv_init(B, T), l, B, T)
        cmp, cmpT = _compress(cmp_l, pe_r, w1_big, w2_big, w2t_big, l, B)
        ynsa = _nsa(q, z, g, cmp, cmpT, ks, kw, vsT, vwT, e_pad, ovT, tabs, acmp, B, T)
        h = _outproj(ymix, ynsa, h, w_out_p, ln_g, ln_b, alpha, l)
    return h.reshape(B, T, D)
```

```python
import functools
import math

import numpy as np
import jax
import jax.numpy as jnp
from jax import lax
from jax.experimental import pallas as pl
from jax.experimental.pallas import tpu as pltpu

f32 = jnp.float32
bf16 = jnp.bfloat16

D_MODEL = 1024
D_POOL = 256
D_CONV = 256
D_NSA = 512
HEAD_DIM = 64
N_HEADS = 8
N_KV = 2
GROUP = 4
D_KV = 128
POOL_GROUP = 64
POOL_WINDOWS = (2, 4, 8, 16)
CONV_WIDTH = 3
CMP_BLOCK = 32
CMP_STRIDE = 16
CMP_HIDDEN = 128
SLC_BLOCK = 64
N_SELECT = 16
N_LOCAL = 2
WINDOW = 512
N_BUCKETS = 32
MAX_DISTANCE = 128
LN_EPS = 1e-5
FORCED = 1e9
NEG = -1e30

LANES = 128
TQ = 128
R = N_HEADS * TQ
NSB = 128
TT = 512
TT_OUT = 1024
UNROLL = 8
N_CLASSES = 4
HALO = 16
KT = 512
KPAD = KT
N_WIN = WINDOW + TQ
SUM_ROWS = 16
VROWS = HEAD_DIM + SUM_ROWS
GC = R // N_KV
LOG2E = math.log2(math.e)
_PAD_FLAG_COL = 32
MIX_W = 2 * D_POOL + 4 * D_CONV
VMEM_LIMIT = 56 * 1024 * 1024

_NT = (((1,), (1,)), ((), ()))


def _dot(a, b):
    return jnp.dot(a, b, preferred_element_type=f32)


def _dot_nt(a, b):
    return lax.dot_general(a, b, _NT, preferred_element_type=f32)


def _sigmoid(x):
    return 1.0 / (1.0 + jnp.exp(-x))


_C_MIX = (0, MIX_W)
_C_Q = (_C_MIX[1], _C_MIX[1] + D_NSA)
_C_Z = (_C_Q[1], _C_Q[1] + D_NSA)
_C_CMP = (_C_Z[1], _C_Z[1] + 2 * D_KV)
_C_K = (_C_CMP[1], _C_CMP[1] + 2 * D_KV)
_C_G = (_C_K[1], _C_K[1] + LANES)
W_ALL = _C_G[1]
CMP_ROWS = TT // CMP_STRIDE


def _local_mixers(ext, i, pw_ref, ps_ref, cw_ref):
    e = ext[:, 0:D_POOL]
    s2 = e + pltpu.roll(e, 1, axis=0)
    s4 = s2 + pltpu.roll(s2, 2, axis=0)
    s8 = s4 + pltpu.roll(s4, 4, axis=0)
    s16 = s8 + pltpu.roll(s8, 8, axis=0)
    lane = lax.broadcasted_iota(jnp.int32, (TT, D_POOL), 1)
    grp = lane // POOL_GROUP
    wsum = jnp.where(grp == 0, s2[HALO:], jnp.where(grp == 1, s4[HALO:], jnp.where(grp == 2, s8[HALO:], s16[HALO:])))
    win = jnp.left_shift(2, grp)
    pos = i * TT + lax.broadcasted_iota(jnp.int32, (TT, D_POOL), 0)
    cnt = jnp.minimum(pos + 1, win).astype(f32)
    v = e[HALO:]
    pooled = wsum / cnt - v
    y_pool = _dot(pooled.astype(bf16), pw_ref[...]) * ps_ref[...]
    zp = ext[HALO:, D_POOL:2 * D_POOL]
    y_pool = y_pool * (zp * _sigmoid(zp))

    o = 2 * D_POOL
    cb = ext[HALO:, o:o + D_CONV]
    u = ext[:, o + D_CONV:o + 2 * D_CONV] * ext[:, o + 2 * D_CONV:o + 3 * D_CONV]
    zc = ext[HALO:, o + 3 * D_CONV:o + 4 * D_CONV]
    conv = cw_ref[CONV_WIDTH - 1:CONV_WIDTH, :] * u[HALO:]
    for k in range(CONV_WIDTH - 1):
        conv = conv + cw_ref[k:k + 1, :] * pltpu.roll(u, CONV_WIDTH - 1 - k, axis=0)[HALO:]
    y_conv = cb * conv * (zc * _sigmoid(zc))
    return y_pool, y_conv


def _inproj_kernel(x_ref, w_ref, wvt_ref, pw_ref, ps_ref, cw_ref,
                   ymix_ref, q_ref, z_ref, cmp_ref, g_ref, ks_ref, kw_ref, vsT_ref, vwT_ref,
                   halo_ref, cscr_ref):
    step = pl.program_id(1)
    ones_rows = jnp.ones((SUM_ROWS, TT), bf16)

    @pl.when(step == 0)
    def _():
        ks_ref[...] = jnp.zeros((TT, D_KV), bf16)
        kw_ref[:, 0:D_KV] = jnp.zeros((TT, D_KV), bf16)
        lane = lax.broadcasted_iota(jnp.int32, (TT, D_KV), 1)
        kw_ref[:, D_KV:2 * D_KV] = jnp.where(lane == _PAD_FLAG_COL, 1.0, 0.0).astype(bf16)
        for g in range(N_KV):
            for ref in (vsT_ref, vwT_ref):
                ref[g, 0:HEAD_DIM] = jnp.zeros((HEAD_DIM, TT), bf16)
                ref[g, HEAD_DIM:VROWS] = ones_rows

    @pl.when(step > 0)
    def _():
        i = step - 1
        x = x_ref[...].astype(bf16)

        def proj(c):
            return _dot(x, w_ref[:, c[0]:c[1]])

        mix = proj(_C_MIX)
        halo = jnp.where(i > 0, halo_ref[...], 0.0)
        y_pool, y_conv = _local_mixers(jnp.concatenate([halo, mix], axis=0), i, pw_ref, ps_ref, cw_ref)
        halo_ref[...] = mix[TT - HALO:]
        ymix_ref[:, 0:D_POOL] = y_pool.astype(bf16)
        ymix_ref[:, D_POOL:D_POOL + D_CONV] = y_conv.astype(bf16)

        q_ref[...] = (proj(_C_Q) * (HEAD_DIM ** -0.5 * LOG2E)).astype(bf16)
        z_ref[...] = proj(_C_Z).astype(bf16)
        g_ref[...] = proj(_C_G)
        kk = proj(_C_K).astype(bf16)
        ks_ref[...] = kk[:, 0:D_KV]
        kw_ref[:, 0:D_KV] = kk[:, D_KV:2 * D_KV]
        kw_ref[:, D_KV:2 * D_KV] = jnp.zeros((TT, D_KV), bf16)
        vt = _dot_nt(wvt_ref[...], x)
        for g in range(N_KV):
            vsT_ref[g, 0:HEAD_DIM] = vt[g * HEAD_DIM:(g + 1) * HEAD_DIM].astype(bf16)
            vwT_ref[g, 0:HEAD_DIM] = vt[D_KV + g * HEAD_DIM:D_KV + (g + 1) * HEAD_DIM].astype(bf16)
            vsT_ref[g, HEAD_DIM:VROWS] = ones_rows
            vwT_ref[g, HEAD_DIM:VROWS] = ones_rows

        kvc = proj(_C_CMP)
        for c in range(2):
            cscr_ref[c] = kvc[:, c * LANES:(c + 1) * LANES]
            for l in range(CMP_STRIDE):
                cmp_ref[l, :, c * LANES:(c + 1) * LANES] = (
                    cscr_ref[c, pl.ds(l, CMP_ROWS, stride=CMP_STRIDE), :].astype(bf16))


def _inproj(x2, w_all, wvt, pw_bd, pool_scale, conv_w, layer, B, T):
    nt = T // TT
    bt = B * T
    row = lambda b, s: (b * nt + jnp.maximum(s - 1, 0), 0)
    wsel = lambda b, s: (layer, 0, 0)
    tp = T + KPAD
    out_shape = [jax.ShapeDtypeStruct((bt, D_POOL + D_CONV), bf16),
                 jax.ShapeDtypeStruct((bt, D_NSA), bf16),
                 jax.ShapeDtypeStruct((bt, D_NSA), bf16),
                 jax.ShapeDtypeStruct((CMP_STRIDE, bt // CMP_STRIDE, 2 * D_KV), bf16),
                 jax.ShapeDtypeStruct((bt, LANES), f32),
                 jax.ShapeDtypeStruct((B, tp, D_KV), bf16),
                 jax.ShapeDtypeStruct((B, tp, 2 * D_KV), bf16),
                 jax.ShapeDtypeStruct((B, N_KV, VROWS, tp), bf16),
                 jax.ShapeDtypeStruct((B, N_KV, VROWS, tp), bf16)]
    out_specs = [pl.BlockSpec((TT, D_POOL + D_CONV), row),
                 pl.BlockSpec((TT, D_NSA), row),
                 pl.BlockSpec((TT, D_NSA), row),
                 pl.BlockSpec((CMP_STRIDE, CMP_ROWS, 2 * D_KV), lambda b, s: (0, b * nt + jnp.maximum(s - 1, 0), 0)),
                 pl.BlockSpec((TT, LANES), row),
                 pl.BlockSpec((None, TT, D_KV), lambda b, s: (b, s, 0)),
                 pl.BlockSpec((None, TT, 2 * D_KV), lambda b, s: (b, s, 0)),
                 pl.BlockSpec((None, N_KV, VROWS, TT), lambda b, s: (b, 0, 0, s)),
                 pl.BlockSpec((None, N_KV, VROWS, TT), lambda b, s: (b, 0, 0, s))]
    return pl.pallas_call(
        _inproj_kernel,
        grid=(B, nt + 1),
        in_specs=[pl.BlockSpec((TT, D_MODEL), row),
                  pl.BlockSpec((None, D_MODEL, W_ALL), wsel),
                  pl.BlockSpec((None, 2 * D_KV, D_MODEL), wsel),
                  pl.BlockSpec((None, D_POOL, D_POOL), wsel),
                  pl.BlockSpec((None, 1, D_POOL), wsel),
                  pl.BlockSpec((None, CONV_WIDTH, D_CONV), wsel)],
        out_specs=out_specs,
        out_shape=out_shape,
        scratch_shapes=[pltpu.VMEM((HALO, MIX_W), f32), pltpu.VMEM((2, TT, LANES), f32)],
        compiler_params=pltpu.CompilerParams(dimension_semantics=("arbitrary", "arbitrary"),
                                             vmem_limit_bytes=VMEM_LIMIT),
        name="inproj",
    )(x2, w_all, wvt, pw_bd, pool_scale, conv_w)


CH = 4 * CMP_HIDDEN


def _compress_kernel(r_ref, pe_ref, w1_ref, w2_ref, w2t_ref, out_ref, outT_ref):
    nc = r_ref.shape[1]
    rows = jnp.concatenate([r_ref[l] for l in range(CMP_STRIDE)], axis=1)
    zz = _dot(rows, w1_ref[...])
    pb = _dot(pe_ref[...], w1_ref[...])
    z0 = zz[:, 0:CH] + pb[0:1, 0:CH]
    z1 = zz[:, CH:2 * CH] + pb[1:2, CH:2 * CH]
    h = z0 + pltpu.roll(z1, nc - 1, axis=0)
    h = (h * _sigmoid(h)).astype(bf16)
    out_ref[...] = _dot(h, w2_ref[...]).astype(bf16)
    outT_ref[...] = _dot_nt(w2t_ref[...], h).astype(bf16)


def _compress(cmp_l, pe_r, w1_big, w2_big, w2t_big, layer, B):
    nc = cmp_l.shape[1] // B
    rw = CMP_STRIDE * 2 * D_KV
    wsel = lambda b: (layer, 0, 0)
    return pl.pallas_call(
        _compress_kernel,
        grid=(B,),
        in_specs=[pl.BlockSpec((CMP_STRIDE, nc, 2 * D_KV), lambda b: (0, b, 0)),
                  pl.BlockSpec((None, 8, rw), wsel),
                  pl.BlockSpec((None, rw, 2 * CH), wsel),
                  pl.BlockSpec((None, CH, 2 * D_KV), wsel),
                  pl.BlockSpec((None, 2 * D_KV, CH), wsel)],
        out_specs=[pl.BlockSpec((None, nc, 2 * D_KV), lambda b: (b, 0, 0)),
                   pl.BlockSpec((None, 2 * D_KV, nc), lambda b: (b, 0, 0))],
        out_shape=[jax.ShapeDtypeStruct((B, nc, 2 * D_KV), bf16),
                   jax.ShapeDtypeStruct((B, 2 * D_KV, nc), bf16)],
        compiler_params=pltpu.CompilerParams(dimension_semantics=("arbitrary",),
                                             vmem_limit_bytes=VMEM_LIMIT),
        name="compress",
    )(cmp_l, pe_r, w1_big, w2_big, w2t_big)


def _select_penalty(imp_t, t0):
    shape = imp_t.shape
    jblk = lax.broadcasted_iota(jnp.int32, shape, 0)
    tq = lax.broadcasted_iota(jnp.int32, shape, 1) % TQ
    back = jnp.right_shift(t0 + tq, int(math.log2(SLC_BLOCK))) - jblk
    causal = back >= 0
    forced = jnp.logical_or(jblk == 0, jnp.logical_and(causal, back < N_LOCAL))
    score = jnp.where(forced, -jnp.inf, jnp.where(causal, imp_t, NEG))
    jf = jblk.astype(f32)
    for _ in range(N_SELECT - N_LOCAL - 1):
        best = jnp.max(score, axis=0, keepdims=True)
        first = jnp.min(jnp.where(score == best, jf, float(shape[0])), axis=0, keepdims=True)
        score = jnp.where(jf == first, -jnp.inf, score)
    return jnp.where(score == -jnp.inf, 0.0, NEG)


def _query_cols(q_ref):
    sub = lax.broadcasted_iota(jnp.int32, (LANES, TQ), 0)
    low = sub < HEAD_DIM
    pairs = [q_ref[:, LANES * p:LANES * (p + 1)].astype(f32).T for p in range(GROUP)]
    blocks = [(jnp.where(low, pairs[h % GROUP], 0.0) if h < GROUP else jnp.where(low, 0.0, pairs[h % GROUP])).astype(bf16)
              for h in range(N_HEADS)]
    return jnp.concatenate(blocks, axis=1)


def _merge_head_pairs(outs):
    return [jnp.concatenate([outs[p], outs[p + GROUP]], axis=0).T for p in range(GROUP)]


def _cmp_scores(q_cmp, cmp_ref, t0, nc):
    n_i = lax.broadcasted_iota(jnp.int32, (nc, LANES), 0)
    j_i = lax.broadcasted_iota(jnp.int32, (nc, LANES), 1)
    nstart = t0 // CMP_STRIDE - 8
    in_window = jnp.logical_and(j_i < 32, n_i == nstart + jnp.where(j_i < 16, j_i, j_i - 16))
    future = jnp.logical_and(j_i == 32, n_i >= nstart + 15)
    place_b = jnp.where(jnp.logical_or(in_window, future), 1.0, 0.0).astype(bf16)
    kc_ext = jnp.concatenate([cmp_ref[0:nc, 0:D_KV], place_b], axis=1)
    return _dot(kc_ext, q_cmp)


def _cmp_finish(s1, sig_t, cmpT_ref, ovT_ref, pen_ref, y1_ref, t0, nsel):
    nc = s1.shape[0]
    m1 = jnp.max(s1, axis=0, keepdims=True)
    p1 = jnp.exp2(s1 - m1)
    p1 = p1.astype(bf16)
    tcol = t0 + lax.broadcasted_iota(jnp.int32, (1, GC), 1) % TQ
    outs, sums = [], []
    for g in range(N_KV):
        lhs = jnp.concatenate([cmpT_ref[D_KV + g * HEAD_DIM:D_KV + (g + 1) * HEAD_DIM, 0:nc], ovT_ref[0:nsel, 0:nc],
                               ovT_ref[NSB:NSB + SUM_ROWS, 0:nc]], axis=0)
        both = _dot(lhs, p1[:, g * GC:(g + 1) * GC])
        l1 = both[HEAD_DIM + nsel:HEAD_DIM + nsel + 1]
        both = both * jnp.where(tcol >= CMP_BLOCK - 1, 1.0 / l1, 0.0)
        for r in range(GROUP):
            h = g * GROUP + r
            outs.append(sig_t[3 * h:3 * h + 1, :] * both[0:HEAD_DIM, r * TQ:(r + 1) * TQ])
        imp = both[HEAD_DIM:HEAD_DIM + nsel]
        acc = imp[:, 0:TQ]
        for r in range(1, GROUP):
            acc = acc + imp[:, r * TQ:(r + 1) * TQ]
        sums.append(acc)
    for p, y in enumerate(_merge_head_pairs(outs)):
        y1_ref[:, p * LANES:(p + 1) * LANES] = y.astype(bf16)
    imp_t = jnp.concatenate(sums, axis=1)
    pen_ref[0:nsel] = _select_penalty(imp_t, t0).astype(bf16)
    if nsel < NSB:
        pen_ref[nsel:NSB] = jnp.full((NSB - nsel, N_KV * TQ), NEG, bf16)


def _nsa_kernel(q0_ref, qn_ref, z_ref, g0_ref, gn_ref, cmp_ref, cmpT_ref, ovT_ref, ks_ref, kw_ref, vsT_ref, vwT_ref,
                e_ref, tab_ref, acmp_ref, out_ref,
                qc_ref, sg_ref, pen_ref, y1_ref, o3_ref, m_ref, l_ref, acc_ref, sa_ref, sb_ref, mxa_ref, mxb_ref):
    qi = pl.program_id(1)
    nq = pl.num_programs(1)
    t0 = qi * TQ
    nc = cmp_ref.shape[0]
    classes = [(nc * k // N_CLASSES, NSB * k // N_CLASSES) for k in range(1, N_CLASSES + 1)]
    last_q = [min(c // (TQ // CMP_STRIDE), s // (TQ // SLC_BLOCK)) - 1 for c, s in classes]

    @pl.when(qi == 0)
    def _():
        qc_ref[...] = _query_cols(q0_ref)
        sg_ref[...] = _sigmoid(g0_ref[...]).T
        q_cmp0 = jnp.concatenate([qc_ref[...], acmp_ref[...]], axis=0)
        _cmp_finish(_cmp_scores(q_cmp0, cmp_ref, t0, classes[0][0]), sg_ref[...], cmpT_ref, ovT_ref, pen_ref, y1_ref, t0,
                    classes[0][1])

    qcols = qc_ref[...]
    sig_cur = sg_ref[...]
    q_cmp = jnp.concatenate([qcols, acmp_ref[...]], axis=0)
    pens = [pen_ref[:, g * TQ:(g + 1) * TQ] for g in range(N_KV) for _ in range(GROUP)]
    q_slc = jnp.concatenate([qcols, jnp.concatenate(pens, axis=1)], axis=0)
    y1 = y1_ref[...]

    m_ref[...] = jnp.full((1, R), NEG, f32)
    l_ref[...] = jnp.zeros((1, R), f32)
    acc_ref[...] = jnp.zeros((HEAD_DIM, R), f32)
    n_far = (qi + KT // TQ) // (KT // TQ) - 1
    first = (qi + 1) * TQ - n_far * KT

    def tile_row(i):
        return pl.multiple_of(first + i * KT, LANES)

    def produce(i, s_ref, mx_ref):
        r0 = tile_row(i)
        k_ext = jnp.concatenate([ks_ref[pl.ds(r0, KT), :], e_ref[pl.ds(r0, KT), :]], axis=1)
        s = _dot(k_ext, q_slc)
        s_ref[...] = s
        mx_ref[...] = jnp.max(s, axis=0, keepdims=True)

    def consume(i, s_ref, mx_ref, last):
        m_prev = m_ref[...]
        if last:
            s = jnp.concatenate([s_ref[0:KT - 2 * TQ], s_ref[KT - 2 * TQ:KT] + tab_ref[TQ:3 * TQ]], axis=0)
            m_new = jnp.maximum(m_prev, jnp.max(s, axis=0, keepdims=True))
        else:
            s = s_ref[...]
            m_new = jnp.maximum(m_prev, mx_ref[...])
        alpha = jnp.exp2(m_prev - m_new)
        p = jnp.exp2(s - m_new)
        p = p.astype(bf16)
        for g in range(N_KV):
            c = slice(g * GC, (g + 1) * GC)
            pv = _dot(vsT_ref[g, :, pl.ds(tile_row(i), KT)], p[:, c])
            l_ref[:, c] = alpha[:, c] * l_ref[:, c] + pv[HEAD_DIM:HEAD_DIM + 1]
            acc_ref[:, c] = acc_ref[:, c] * alpha[:, c] + pv[0:HEAD_DIM]
        m_ref[...] = m_new

    qn = jnp.minimum(qi + 1, nq - 1)

    def front(nc_use, nsel_use):
        tn = qn * TQ
        qc_next = _query_cols(qn_ref)
        sig_next = _sigmoid(gn_ref[...]).T
        qc_ref[...] = qc_next
        sg_ref[...] = sig_next
        s1 = _cmp_scores(jnp.concatenate([qc_next, acmp_ref[...]], axis=0), cmp_ref, tn, nc_use)
        w0 = pl.multiple_of(t0, LANES)
        s3 = _dot(kw_ref[pl.ds(w0, N_WIN), :], q_cmp)
        _cmp_finish(s1, sig_next, cmpT_ref, ovT_ref, pen_ref, y1_ref, tn, nsel_use)
        produce(0, sa_ref, mxa_ref)
        s3 = jnp.concatenate([s3[0:TQ] + tab_ref[0:TQ], s3[TQ:WINDOW - TQ], s3[WINDOW - TQ:] + tab_ref[TQ:3 * TQ]],
                             axis=0)
        p3 = jnp.exp2(s3 - jnp.max(s3, axis=0, keepdims=True)).astype(bf16)
        for g in range(N_KV):
            c = slice(g * GC, (g + 1) * GC)
            o3_ref[:, c] = _dot(vwT_ref[g, :, pl.ds(w0, N_WIN)], p3[:, c])

    for k, (nc_k, nsel_k) in enumerate(classes):
        lo = last_q[k - 1] if k else -1
        if k < N_CLASSES - 1:
            pl.when(jnp.logical_and(qn > lo, qn <= last_q[k]))(functools.partial(front, nc_k, nsel_k))
        else:
            pl.when(qn > lo)(functools.partial(front, nc_k, nsel_k))

    bufs = ((sa_ref, mxa_ref), (sb_ref, mxb_ref))

    def pipeline(base, count, last_at_end):
        for k in range(count):
            if k + 1 < count or not last_at_end:
                produce(base + k + 1, *bufs[(k + 1) % 2])
            consume(base + k, *bufs[k % 2], last_at_end and k == count - 1)

    def unrolled(j, c):
        pipeline(UNROLL * j, UNROLL, False)
        return c

    lax.fori_loop(0, n_far // UNROLL, unrolled, 0)

    def finish():
        inv2 = 1.0 / l_ref[...]
        inv3 = 1.0 / o3_ref[HEAD_DIM:HEAD_DIM + 1, :]
        sig_t = sig_cur
        outs = []
        for h in range(N_HEADS):
            c = slice(h * TQ, (h + 1) * TQ)
            g2 = sig_t[3 * h + 1:3 * h + 2, :] * inv2[:, c]
            g3 = sig_t[3 * h + 2:3 * h + 3, :] * inv3[:, c]
            outs.append(g2 * acc_ref[:, c] + g3 * o3_ref[0:HEAD_DIM, c])
        for p, y in enumerate(_merge_head_pairs(outs)):
            c = slice(p * LANES, (p + 1) * LANES)
            zb = z_ref[:, c].astype(f32)
            out_ref[:, c] = ((y + y1[:, c].astype(f32)) * zb * _sigmoid(zb)).astype(bf16)

    for r in range(UNROLL):
        def drain(r=r):
            pipeline(n_far - r, r + 1, True)
            finish()
        pl.when(n_far % UNROLL == r)(drain)


def _nsa(q, z, g, cmp, cmpT, ks, kw, vsT, vwT, e_pad, ovT, tabs, acmp, B, T):
    nq = T // TQ
    nc = T // CMP_STRIDE
    tp = T + KPAD
    tile = lambda b, i: (b * nq + i, 0)
    nxt = lambda b, i: (b * nq + jnp.minimum(i + 1, nq - 1), 0)
    head = lambda b, i: (b * nq, 0)
    const2 = lambda b, i: (0, 0)
    batch3 = lambda b, i: (b, 0, 0)
    return pl.pallas_call(
        _nsa_kernel,
        grid=(B, nq),
        in_specs=[pl.BlockSpec((TQ, D_NSA), head),
                  pl.BlockSpec((TQ, D_NSA), nxt),
                  pl.BlockSpec((TQ, D_NSA), tile),
                  pl.BlockSpec((TQ, LANES), head),
                  pl.BlockSpec((TQ, LANES), nxt),
                  pl.BlockSpec((None, nc, 2 * D_KV), batch3),
                  pl.BlockSpec((None, 2 * D_KV, nc), batch3),
                  pl.BlockSpec((NSB + SUM_ROWS, nc), const2),
                  pl.BlockSpec((None, tp, D_KV), batch3),
                  pl.BlockSpec((None, tp, 2 * D_KV), batch3),
                  pl.BlockSpec((None, N_KV, VROWS, tp), lambda b, i: (b, 0, 0, 0)),
                  pl.BlockSpec((None, N_KV, VROWS, tp), lambda b, i: (b, 0, 0, 0)),
                  pl.BlockSpec((tp, NSB), const2),
                  pl.BlockSpec((3 * TQ, R), const2),
                  pl.BlockSpec((LANES, R), const2)],
        out_specs=pl.BlockSpec((TQ, D_NSA), tile),
        out_shape=jax.ShapeDtypeStruct((B * T, D_NSA), bf16),
        scratch_shapes=[pltpu.VMEM((LANES, R), bf16), pltpu.VMEM((LANES, TQ), f32),
                        pltpu.VMEM((NSB, N_KV * TQ), bf16), pltpu.VMEM((TQ, D_NSA), bf16), pltpu.VMEM((VROWS, R), f32),
                        pltpu.VMEM((1, R), f32), pltpu.VMEM((1, R), f32), pltpu.VMEM((HEAD_DIM, R), f32),
                        pltpu.VMEM((KT, R), f32), pltpu.VMEM((KT, R), f32),
                        pltpu.VMEM((1, R), f32), pltpu.VMEM((1, R), f32)],
        compiler_params=pltpu.CompilerParams(dimension_semantics=("arbitrary", "arbitrary"),
                                             vmem_limit_bytes=VMEM_LIMIT),
        name="nsa",
    )(q, q, z, g, g, cmp, cmpT, ovT, ks, kw, vsT, vwT, e_pad, tabs, acmp)


def _outproj_kernel(alpha, ymix_ref, ynsa_ref, x_ref, w_ref, g_ref, b_ref, out_ref):
    y = jnp.concatenate([ymix_ref[...], ynsa_ref[...]], axis=1)
    r = alpha * x_ref[...] + _dot(y, w_ref[...])
    mu = jnp.mean(r, axis=-1, keepdims=True)
    d = r - mu
    var = jnp.mean(d * d, axis=-1, keepdims=True)
    out_ref[...] = d * lax.rsqrt(var + LN_EPS) * g_ref[...] + b_ref[...]


def _outproj(ymix, ynsa, x2, w_out, ln_g, ln_b, alpha, layer):
    bt = x2.shape[0]
    row = lambda i: (i, 0)
    wsel = lambda i: (layer, 0, 0)
    return pl.pallas_call(
        functools.partial(_outproj_kernel, alpha),
        grid=(bt // TT_OUT,),
        in_specs=[pl.BlockSpec((TT_OUT, D_POOL + D_CONV), row),
                  pl.BlockSpec((TT_OUT, D_NSA), row),
                  pl.BlockSpec((TT_OUT, D_MODEL), row),
                  pl.BlockSpec((None, D_MODEL, D_MODEL), wsel),
                  pl.BlockSpec((None, 1, D_MODEL), wsel),
                  pl.BlockSpec((None, 1, D_MODEL), wsel)],
        out_specs=pl.BlockSpec((TT_OUT, D_MODEL), row),
        out_shape=jax.ShapeDtypeStruct((bt, D_MODEL), f32),
        compiler_params=pltpu.CompilerParams(dimension_semantics=("arbitrary",),
                                             vmem_limit_bytes=VMEM_LIMIT),
        name="outproj",
    )(ymix, ynsa, x2, w_out, ln_g, ln_b)


def _bucket_np(d):
    d = np.asarray(d)
    max_exact = N_BUCKETS // 2
    nf = np.maximum(d, 1).astype(np.float32)
    large = max_exact + (np.log(nf / np.float32(max_exact)) / np.float32(math.log(MAX_DISTANCE / max_exact))
                         * np.float32(N_BUCKETS - max_exact)).astype(np.int32)
    large = np.minimum(large, N_BUCKETS - 1)
    return np.where(d < max_exact, d, large)


_FAR_DIST = 113
assert _bucket_np(np.arange(_FAR_DIST, 4 * WINDOW)).min() == N_BUCKETS - 1

_PAIR_PERM = np.concatenate([np.concatenate([np.arange(HEAD_DIM) + HEAD_DIM * p,
                                             np.arange(HEAD_DIM) + HEAD_DIM * (p + GROUP)]) for p in range(GROUP)])


def _bias_tables(rel_bias):
    nd = 2 * TQ
    onehot = jnp.asarray(np.eye(N_BUCKETS, dtype=np.float32)[_bucket_np(np.arange(nd))])
    tabp = ((jnp.dot(onehot, rel_bias, precision=lax.Precision.HIGHEST)
             - rel_bias[N_BUCKETS - 1:N_BUCKETS, :]) * LOG2E).T
    sl = np.arange(TQ)[:, None]
    tl = np.arange(TQ)[None, :]
    neg = lambda n: jnp.full((N_HEADS, n), NEG, f32)

    def skew(u, rows):
        period = u.shape[1]
        return jnp.tile(u, (1, rows))[:, :rows * (period - 1)].reshape(N_HEADS, rows, period - 1)

    def tile_layout(t):
        return t.transpose(1, 0, 2).reshape(TQ, R)

    diag = tile_layout(skew(jnp.concatenate([tabp[:, 0:TQ], neg(TQ)], axis=1), TQ)[:, :, :TQ])
    prev = tile_layout(skew(jnp.concatenate([tabp[:, TQ:nd], tabp[:, 0:TQ]], axis=1), TQ)[:, :, :TQ])
    edge = jnp.asarray(np.tile(np.where(sl > tl, 0.0, NEG).astype(np.float32), (1, N_HEADS)))
    tabs = jnp.concatenate([edge, prev, diag], axis=0).astype(f32)

    j = np.arange(16)[None, :]
    off = 8 * CMP_STRIDE - (CMP_BLOCK - 1)
    dc = np.arange(TQ)[:, None] - CMP_STRIDE * j + off
    okc = jnp.asarray(dc >= 0)[None]
    period = 4 * TQ
    u = jnp.concatenate([tabp[:, off:nd], jnp.zeros((N_HEADS, off), f32), neg(period - nd - off), tabp[:, 0:off]], axis=1)
    vc = skew(u, nd)[:, 0:16 * CMP_STRIDE:CMP_STRIDE, 0:TQ].transpose(0, 2, 1)
    vc = jnp.where(okc, vc, NEG)
    hi = vc.astype(bf16)
    lo = jnp.where(okc, vc - hi.astype(f32), 0.0).astype(bf16)
    fut = jnp.full((N_HEADS, TQ, 1), NEG, bf16)
    pad = jnp.zeros((N_HEADS, TQ, LANES - _PAD_FLAG_COL - 1), bf16)
    acmp = jnp.concatenate([hi, lo, fut, pad], axis=-1).reshape(R, LANES).T
    return tabs, acmp


def _static_tables(T):
    nc = T // CMP_STRIDE
    ns = T // SLC_BLOCK
    cstart = np.arange(nc)[None, :] * CMP_STRIDE
    sstart = np.arange(NSB)[:, None] * SLC_BLOCK
    ov = np.clip(np.minimum(cstart + CMP_BLOCK, sstart + SLC_BLOCK) - np.maximum(cstart, sstart), 0, None) / CMP_STRIDE
    ov[ns:, :] = 0
    ov[:, nc - 1] = 0
    ov = np.concatenate([ov, np.ones((SUM_ROWS, nc))], axis=0)
    e_pad = np.concatenate([np.ones((KPAD, NSB), bool),
                            np.arange(T)[:, None] // SLC_BLOCK == np.arange(NSB)[None, :]], axis=0)
    return jnp.asarray(ov, bf16), jnp.asarray(e_pad, bf16)


def _pair_order(w, axis):
    shp = w.shape
    w = w.reshape(shp[:axis] + (N_KV, GROUP, HEAD_DIM) + shp[axis + 1:])
    return jnp.swapaxes(w, axis, axis + 1).reshape(shp)


def _prep_weights(w_in, w_out, pool_w, pe_k, w1_k, w2_k, pe_v, w1_v, w2_v):
    depth = w_in.shape[0]
    sizes = (D_POOL, D_POOL, D_CONV, D_CONV, D_CONV, D_CONV, D_NSA, D_KV, D_KV, D_KV, D_KV, D_KV, D_KV,
             3 * N_HEADS, D_NSA)
    offs = np.cumsum((0,) + sizes)
    w_in = w_in.astype(bf16)
    col = lambda i: w_in[:, :, offs[i]:offs[i + 1]]
    wg = jnp.pad(col(13), ((0, 0), (0, 0), (0, LANES - 3 * N_HEADS)))
    w_all = jnp.concatenate([w_in[:, :, 0:MIX_W], _pair_order(col(6), 2), _pair_order(col(14), 2),
                             col(7), col(8), col(9), col(11), wg], axis=2)
    wvt = jnp.swapaxes(jnp.concatenate([col(10), col(12)], axis=2), 1, 2)
    w_out = w_out.astype(bf16)
    nm = D_POOL + D_CONV
    w_out_p = jnp.concatenate([w_out[:, 0:nm], _pair_order(w_out[:, nm:], 1)], axis=1)

    eye_g = jnp.eye(len(POOL_WINDOWS), dtype=bf16)
    pw_bd = jnp.einsum('zgcd,gh->zgchd', pool_w.astype(bf16), eye_g).reshape(depth, D_POOL, D_POOL)

    half = CMP_BLOCK // 2
    eye2 = jnp.eye(2, dtype=bf16)
    rw = half * 2 * D_KV
    cols = []
    for a in range(2):
        for kv, w1 in enumerate((w1_k, w1_v)):
            wsel = w1.astype(bf16).reshape(depth, 2, half, 1, 1, HEAD_DIM, CMP_HIDDEN)[:, a]
            for g in range(N_KV):
                blk = jnp.pad(wsel, ((0, 0), (0, 0), (kv, 1 - kv), (g, N_KV - 1 - g), (0, 0), (0, 0)))
                cols.append(blk.reshape(depth, rw, CMP_HIDDEN))
    w1_big = jnp.concatenate(cols, axis=2)
    pes = jnp.stack([pe_k, pe_v], axis=1).astype(bf16).reshape(depth, 2, 2, half, HEAD_DIM)
    pe_r = jnp.broadcast_to(pes.transpose(0, 2, 3, 1, 4)[:, :, :, :, None, :],
                            (depth, 2, half, 2, N_KV, HEAD_DIM)).reshape(depth, 2, half * 2 * D_KV)
    pe_r = jnp.pad(pe_r, ((0, 0), (0, 6), (0, 0)))
    w2s = jnp.stack([w2_k, w2_v], axis=1).astype(bf16)
    w2_big = jnp.einsum('zkhd,kK,gG->zkghKGd', w2s, eye2, eye2).reshape(depth, CH, 2 * D_KV)
    return w_all, wvt, w_out_p, pw_bd, w1_big, pe_r, w2_big, jnp.swapaxes(w2_big, 1, 2)


def kernel(x, w_in, w_out, pool_w, pool_scale, conv_w, cmp_pe_k, cmp_w1_k, cmp_w2_k, cmp_pe_v, cmp_w1_v, cmp_w2_v,
           rel_bias, ln_g, ln_b):
    B, T, D = x.shape
    depth = w_in.shape[0]
    assert D == D_MODEL and T % TT == 0 and T // SLC_BLOCK <= NSB and T // SLC_BLOCK >= N_SELECT
    assert KPAD == TT and (B * T) % TT_OUT == 0 and N_SELECT > N_LOCAL
    alpha = (2 * depth) ** 0.25
    tabs, acmp = _bias_tables(rel_bias)
    ovT, e_pad = _static_tables(T)
    w_all, wvt, w_out_p, pw_bd, w1_big, pe_r, w2_big, w2t_big = _prep_weights(
        w_in, w_out, pool_w, cmp_pe_k, cmp_w1_k, cmp_w2_k, cmp_pe_v, cmp_w1_v, cmp_w2_v)
    pool_scale = pool_scale.reshape(depth, 1, D_POOL)
    ln_g = ln_g.reshape(depth, 1, D_MODEL)
    ln_b = ln_b.reshape(depth, 1, D_MODEL)
    h = x.reshape(B * T, D)
    for l in range(depth):
        ymix, q, z, cmp_l, g, ks, kw, vsT, vwT = _inproj(h, w_all, wvt, pw_bd, pool_scale, conv_w, l, B, T)
        cmp, cmpT = _compress(cmp_l, pe_r, w1_big, w2_big, w2t_big, l, B)
        ynsa = _nsa(q, z, g, cmp, cmpT, ks, kw, vsT, vwT, e_pad, ovT, tabs, acmp, B, T)
        h = _outproj(ymix, ynsa, h, w_out_p, ln_g, ln_b, alpha, l)
    return h.reshape(B, T, D)
```

```python
import functools
import math

import numpy as np
import jax
import jax.numpy as jnp
from jax import lax
from jax.experimental import pallas as pl
from jax.experimental.pallas import tpu as pltpu

f32 = jnp.float32
bf16 = jnp.bfloat16

D_MODEL = 1024
D_POOL = 256
D_CONV = 256
D_NSA = 512
HEAD_DIM = 64
N_HEADS = 8
N_KV = 2
GROUP = 4
D_KV = 128
POOL_GROUP = 64
POOL_WINDOWS = (2, 4, 8, 16)
CONV_WIDTH = 3
CMP_BLOCK = 32
CMP_STRIDE = 16
CMP_HIDDEN = 128
SLC_BLOCK = 64
N_SELECT = 16
N_LOCAL = 2
WINDOW = 512
N_BUCKETS = 32
MAX_DISTANCE = 128
LN_EPS = 1e-5
NEG = -1e30

LANES = 128
TQ = 128
R = N_HEADS * TQ
NSB = 128
TT = 512
TT_OUT = 1024
UNROLL = 8
N_CLASSES = 4
HALO = 16
KT = 512
KPAD = KT
N_WIN = WINDOW + TQ
SUM_ROWS = 16
VROWS = HEAD_DIM + SUM_ROWS
GC = R // N_KV
LOG2E = math.log2(math.e)
CMP_BACK = 8
CMP_SLOTS = TQ // CMP_STRIDE + CMP_BACK
_PAD_FLAG_COL = 2 * CMP_SLOTS
MIX_W = 2 * D_POOL + 4 * D_CONV
VMEM_LIMIT = 56 * 1024 * 1024

_NT = (((1,), (1,)), ((), ()))


def _dot(a, b):
    return jnp.dot(a, b, preferred_element_type=f32)


def _dot_nt(a, b):
    return lax.dot_general(a, b, _NT, preferred_element_type=f32)


def _sigmoid(x):
    return 1.0 / (1.0 + jnp.exp(-x))


_C_MIX = (0, MIX_W)
_C_Q = (_C_MIX[1], _C_MIX[1] + D_NSA)
_C_Z = (_C_Q[1], _C_Q[1] + D_NSA)
_C_CMP = (_C_Z[1], _C_Z[1] + 2 * D_KV)
_C_K = (_C_CMP[1], _C_CMP[1] + 2 * D_KV)
_C_G = (_C_K[1], _C_K[1] + LANES)
W_ALL = _C_G[1]
CMP_ROWS = TT // CMP_STRIDE


def _local_mixers(ext, i, pw_ref, ps_ref, cw_ref):
    e = ext[:, 0:D_POOL]
    s2 = e + pltpu.roll(e, 1, axis=0)
    s4 = s2 + pltpu.roll(s2, 2, axis=0)
    s8 = s4 + pltpu.roll(s4, 4, axis=0)
    s16 = s8 + pltpu.roll(s8, 8, axis=0)
    lane = lax.broadcasted_iota(jnp.int32, (TT, D_POOL), 1)
    grp = lane // POOL_GROUP
    wsum = jnp.where(grp == 0, s2[HALO:], jnp.where(grp == 1, s4[HALO:], jnp.where(grp == 2, s8[HALO:], s16[HALO:])))
    win = jnp.left_shift(2, grp)
    pos = i * TT + lax.broadcasted_iota(jnp.int32, (TT, D_POOL), 0)
    cnt = jnp.minimum(pos + 1, win).astype(f32)
    v = e[HALO:]
    pooled = wsum / cnt - v
    y_pool = _dot(pooled.astype(bf16), pw_ref[...]) * ps_ref[...]
    zp = ext[HALO:, D_POOL:2 * D_POOL]
    y_pool = y_pool * (zp * _sigmoid(zp))

    o = 2 * D_POOL
    cb = ext[HALO:, o:o + D_CONV]
    u = ext[:, o + D_CONV:o + 2 * D_CONV] * ext[:, o + 2 * D_CONV:o + 3 * D_CONV]
    zc = ext[HALO:, o + 3 * D_CONV:o + 4 * D_CONV]
    conv = cw_ref[CONV_WIDTH - 1:CONV_WIDTH, :] * u[HALO:]
    for k in range(CONV_WIDTH - 1):
        conv = conv + cw_ref[k:k + 1, :] * pltpu.roll(u, CONV_WIDTH - 1 - k, axis=0)[HALO:]
    y_conv = cb * conv * (zc * _sigmoid(zc))
    return y_pool, y_conv


def _inproj_kernel(x_ref, w_ref, wvt_ref, pw_ref, ps_ref, cw_ref,
                   ymix_ref, q_ref, z_ref, cmp_ref, g_ref, ks_ref, kw_ref, vsT_ref, vwT_ref,
                   halo_ref, cscr_ref):
    step = pl.program_id(1)
    ones_rows = jnp.ones((SUM_ROWS, TT), bf16)

    @pl.when(step == 0)
    def _():
        ks_ref[...] = jnp.zeros((TT, D_KV), bf16)
        kw_ref[:, 0:D_KV] = jnp.zeros((TT, D_KV), bf16)
        lane = lax.broadcasted_iota(jnp.int32, (TT, D_KV), 1)
        kw_ref[:, D_KV:2 * D_KV] = jnp.where(lane == _PAD_FLAG_COL, 1.0, 0.0).astype(bf16)
        for g in range(N_KV):
            for ref in (vsT_ref, vwT_ref):
                ref[g, 0:HEAD_DIM] = jnp.zeros((HEAD_DIM, TT), bf16)
                ref[g, HEAD_DIM:VROWS] = ones_rows

    @pl.when(step > 0)
    def _():
        i = step - 1
        x = x_ref[...].astype(bf16)

        def proj(c):
            return _dot(x, w_ref[:, c[0]:c[1]])

        mix = proj(_C_MIX)
        halo = jnp.where(i > 0, halo_ref[...], 0.0)
        y_pool, y_conv = _local_mixers(jnp.concatenate([halo, mix], axis=0), i, pw_ref, ps_ref, cw_ref)
        halo_ref[...] = mix[TT - HALO:]
        ymix_ref[:, 0:D_POOL] = y_pool.astype(bf16)
        ymix_ref[:, D_POOL:D_POOL + D_CONV] = y_conv.astype(bf16)

        q_ref[...] = (proj(_C_Q) * (HEAD_DIM ** -0.5 * LOG2E)).astype(bf16)
        z_ref[...] = proj(_C_Z).astype(bf16)
        g_ref[...] = proj(_C_G)
        kk = proj(_C_K).astype(bf16)
        ks_ref[...] = kk[:, 0:D_KV]
        kw_ref[:, 0:D_KV] = kk[:, D_KV:2 * D_KV]
        kw_ref[:, D_KV:2 * D_KV] = jnp.zeros((TT, D_KV), bf16)
        vt = _dot_nt(wvt_ref[...], x)
        for g in range(N_KV):
            vsT_ref[g, 0:HEAD_DIM] = vt[g * HEAD_DIM:(g + 1) * HEAD_DIM].astype(bf16)
            vwT_ref[g, 0:HEAD_DIM] = vt[D_KV + g * HEAD_DIM:D_KV + (g + 1) * HEAD_DIM].astype(bf16)
            vsT_ref[g, HEAD_DIM:VROWS] = ones_rows
            vwT_ref[g, HEAD_DIM:VROWS] = ones_rows

        kvc = proj(_C_CMP)
        for c in range(2):
            cscr_ref[c] = kvc[:, c * LANES:(c + 1) * LANES]
            for l in range(CMP_STRIDE):
                cmp_ref[l, :, c * LANES:(c + 1) * LANES] = (
                    cscr_ref[c, pl.ds(l, CMP_ROWS, stride=CMP_STRIDE), :].astype(bf16))


def _inproj(x2, w_all, wvt, pw_bd, pool_scale, conv_w, layer, B, T):
    nt = T // TT
    bt = B * T
    row = lambda b, s: (b * nt + jnp.maximum(s - 1, 0), 0)
    wsel = lambda b, s: (layer, 0, 0)
    tp = T + KPAD
    out_shape = [jax.ShapeDtypeStruct((bt, D_POOL + D_CONV), bf16),
                 jax.ShapeDtypeStruct((bt, D_NSA), bf16),
                 jax.ShapeDtypeStruct((bt, D_NSA), bf16),
                 jax.ShapeDtypeStruct((CMP_STRIDE, bt // CMP_STRIDE, 2 * D_KV), bf16),
                 jax.ShapeDtypeStruct((bt, LANES), f32),
                 jax.ShapeDtypeStruct((B, tp, D_KV), bf16),
                 jax.ShapeDtypeStruct((B, tp, 2 * D_KV), bf16),
                 jax.ShapeDtypeStruct((B, N_KV, VROWS, tp), bf16),
                 jax.ShapeDtypeStruct((B, N_KV, VROWS, tp), bf16)]
    out_specs = [pl.BlockSpec((TT, D_POOL + D_CONV), row),
                 pl.BlockSpec((TT, D_NSA), row),
                 pl.BlockSpec((TT, D_NSA), row),
                 pl.BlockSpec((CMP_STRIDE, CMP_ROWS, 2 * D_KV), lambda b, s: (0, b * nt + jnp.maximum(s - 1, 0), 0)),
                 pl.BlockSpec((TT, LANES), row),
                 pl.BlockSpec((None, TT, D_KV), lambda b, s: (b, s, 0)),
                 pl.BlockSpec((None, TT, 2 * D_KV), lambda b, s: (b, s, 0)),
                 pl.BlockSpec((None, N_KV, VROWS, TT), lambda b, s: (b, 0, 0, s)),
                 pl.BlockSpec((None, N_KV, VROWS, TT), lambda b, s: (b, 0, 0, s))]
    return pl.pallas_call(
        _inproj_kernel,
        grid=(B, nt + 1),
        in_specs=[pl.BlockSpec((TT, D_MODEL), row),
                  pl.BlockSpec((None, D_MODEL, W_ALL), wsel),
                  pl.BlockSpec((None, 2 * D_KV, D_MODEL), wsel),
                  pl.BlockSpec((None, D_POOL, D_POOL), wsel),
                  pl.BlockSpec((None, 1, D_POOL), wsel),
                  pl.BlockSpec((None, CONV_WIDTH, D_CONV), wsel)],
        out_specs=out_specs,
        out_shape=out_shape,
        scratch_shapes=[pltpu.VMEM((HALO, MIX_W), f32), pltpu.VMEM((2, TT, LANES), f32)],
        compiler_params=pltpu.CompilerParams(dimension_semantics=("arbitrary", "arbitrary"),
                                             vmem_limit_bytes=VMEM_LIMIT),
        name="inproj",
    )(x2, w_all, wvt, pw_bd, pool_scale, conv_w)


CH = 4 * CMP_HIDDEN


def _compress_kernel(r_ref, pe_ref, w1_ref, w2_ref, w2t_ref, out_ref, outT_ref):
    nc = r_ref.shape[1]
    rows = jnp.concatenate([r_ref[l] for l in range(CMP_STRIDE)], axis=1)
    zz = _dot(rows, w1_ref[...])
    pb = _dot(pe_ref[...], w1_ref[...])
    z0 = zz[:, 0:CH] + pb[0:1, 0:CH]
    z1 = zz[:, CH:2 * CH] + pb[1:2, CH:2 * CH]
    h = z0 + pltpu.roll(z1, nc - 1, axis=0)
    h = (h * _sigmoid(h)).astype(bf16)
    out_ref[...] = _dot(h, w2_ref[...]).astype(bf16)
    outT_ref[...] = _dot_nt(w2t_ref[...], h).astype(bf16)


def _compress(cmp_l, pe_r, w1_big, w2_big, w2t_big, layer, B):
    nc = cmp_l.shape[1] // B
    rw = CMP_STRIDE * 2 * D_KV
    wsel = lambda b: (layer, 0, 0)
    return pl.pallas_call(
        _compress_kernel,
        grid=(B,),
        in_specs=[pl.BlockSpec((CMP_STRIDE, nc, 2 * D_KV), lambda b: (0, b, 0)),
                  pl.BlockSpec((None, 8, rw), wsel),
                  pl.BlockSpec((None, rw, 2 * CH), wsel),
                  pl.BlockSpec((None, CH, 2 * D_KV), wsel),
                  pl.BlockSpec((None, 2 * D_KV, CH), wsel)],
        out_specs=[pl.BlockSpec((None, nc, 2 * D_KV), lambda b: (b, 0, 0)),
                   pl.BlockSpec((None, 2 * D_KV, nc), lambda b: (b, 0, 0))],
        out_shape=[jax.ShapeDtypeStruct((B, nc, 2 * D_KV), bf16),
                   jax.ShapeDtypeStruct((B, 2 * D_KV, nc), bf16)],
        compiler_params=pltpu.CompilerParams(dimension_semantics=("arbitrary",),
                                             vmem_limit_bytes=VMEM_LIMIT),
        name="compress",
    )(cmp_l, pe_r, w1_big, w2_big, w2t_big)


def _select_penalty(imp_t, t0):
    shape = imp_t.shape
    jblk = lax.broadcasted_iota(jnp.int32, shape, 0)
    tq = lax.broadcasted_iota(jnp.int32, shape, 1) % TQ
    back = jnp.right_shift(t0 + tq, int(math.log2(SLC_BLOCK))) - jblk
    causal = back >= 0
    forced = jnp.logical_or(jblk == 0, jnp.logical_and(causal, back < N_LOCAL))
    score = jnp.where(forced, -jnp.inf, jnp.where(causal, imp_t, NEG))
    jf = jblk.astype(f32)
    for _ in range(N_SELECT - N_LOCAL - 1):
        best = jnp.max(score, axis=0, keepdims=True)
        first = jnp.min(jnp.where(score == best, jf, float(shape[0])), axis=0, keepdims=True)
        score = jnp.where(jf == first, -jnp.inf, score)
    return jnp.where(score == -jnp.inf, 0.0, NEG)


def _query_cols(q_ref):
    sub = lax.broadcasted_iota(jnp.int32, (LANES, TQ), 0)
    low = sub < HEAD_DIM
    pairs = [q_ref[:, LANES * p:LANES * (p + 1)].astype(f32).T for p in range(GROUP)]
    blocks = [(jnp.where(low, pairs[h % GROUP], 0.0) if h < GROUP else jnp.where(low, 0.0, pairs[h % GROUP])).astype(bf16)
              for h in range(N_HEADS)]
    return jnp.concatenate(blocks, axis=1)


def _merge_head_pairs(outs):
    return [jnp.concatenate([outs[p], outs[p + GROUP]], axis=0).T for p in range(GROUP)]


def _cmp_scores(q_cmp, cmp_ref, t0, nc):
    n_i = lax.broadcasted_iota(jnp.int32, (nc, LANES), 0)
    j_i = lax.broadcasted_iota(jnp.int32, (nc, LANES), 1)
    nstart = t0 // CMP_STRIDE - CMP_BACK
    in_window = jnp.logical_and(j_i < 2 * CMP_SLOTS, n_i == nstart + jnp.where(j_i < CMP_SLOTS, j_i, j_i - CMP_SLOTS))
    future = jnp.logical_and(j_i == _PAD_FLAG_COL, n_i >= nstart + CMP_SLOTS - 1)
    place_b = jnp.where(jnp.logical_or(in_window, future), 1.0, 0.0).astype(bf16)
    kc_ext = jnp.concatenate([cmp_ref[0:nc, 0:D_KV], place_b], axis=1)
    return _dot(kc_ext, q_cmp)


def _cmp_finish(s1, sig_t, cmpT_ref, ovT_ref, pen_ref, y1_ref, t0, nsel):
    nc = s1.shape[0]
    m1 = jnp.max(s1, axis=0, keepdims=True)
    p1 = jnp.exp2(s1 - m1)
    p1 = p1.astype(bf16)
    tcol = t0 + lax.broadcasted_iota(jnp.int32, (1, GC), 1) % TQ
    outs, sums = [], []
    for g in range(N_KV):
        lhs = jnp.concatenate([cmpT_ref[D_KV + g * HEAD_DIM:D_KV + (g + 1) * HEAD_DIM, 0:nc], ovT_ref[0:nsel, 0:nc],
                               ovT_ref[NSB:NSB + SUM_ROWS, 0:nc]], axis=0)
        both = _dot(lhs, p1[:, g * GC:(g + 1) * GC])
        l1 = both[HEAD_DIM + nsel:HEAD_DIM + nsel + 1]
        both = both * jnp.where(tcol >= CMP_BLOCK - 1, 1.0 / l1, 0.0)
        for r in range(GROUP):
            h = g * GROUP + r
            outs.append(sig_t[3 * h:3 * h + 1, :] * both[0:HEAD_DIM, r * TQ:(r + 1) * TQ])
        imp = both[HEAD_DIM:HEAD_DIM + nsel]
        acc = imp[:, 0:TQ]
        for r in range(1, GROUP):
            acc = acc + imp[:, r * TQ:(r + 1) * TQ]
        sums.append(acc)
    for p, y in enumerate(_merge_head_pairs(outs)):
        y1_ref[:, p * LANES:(p + 1) * LANES] = y.astype(bf16)
    imp_t = jnp.concatenate(sums, axis=1)
    pen_ref[0:nsel] = _select_penalty(imp_t, t0).astype(bf16)
    if nsel < NSB:
        pen_ref[nsel:NSB] = jnp.full((NSB - nsel, N_KV * TQ), NEG, bf16)


def _nsa_kernel(q0_ref, qn_ref, z_ref, g0_ref, gn_ref, cmp_ref, cmpT_ref, ovT_ref, ks_ref, kw_ref, vsT_ref, vwT_ref,
                e_ref, tab_ref, acmp_ref, out_ref,
                qc_ref, sg_ref, pen_ref, y1_ref, o3_ref, m_ref, l_ref, acc_ref, sa_ref, sb_ref, mxa_ref, mxb_ref):
    qi = pl.program_id(1)
    nq = pl.num_programs(1)
    t0 = qi * TQ
    nc = cmp_ref.shape[0]
    classes = [(nc * k // N_CLASSES, NSB * k // N_CLASSES) for k in range(1, N_CLASSES + 1)]
    last_q = [min(c // (TQ // CMP_STRIDE), s // (TQ // SLC_BLOCK)) - 1 for c, s in classes]

    @pl.when(qi == 0)
    def _():
        qc_ref[...] = _query_cols(q0_ref)
        sg_ref[...] = _sigmoid(g0_ref[...]).T
        q_cmp0 = jnp.concatenate([qc_ref[...], acmp_ref[...]], axis=0)
        _cmp_finish(_cmp_scores(q_cmp0, cmp_ref, t0, classes[0][0]), sg_ref[...], cmpT_ref, ovT_ref, pen_ref, y1_ref, t0,
                    classes[0][1])

    qcols = qc_ref[...]
    sig_cur = sg_ref[...]
    q_cmp = jnp.concatenate([qcols, acmp_ref[...]], axis=0)
    pens = [pen_ref[:, g * TQ:(g + 1) * TQ] for g in range(N_KV) for _ in range(GROUP)]
    q_slc = jnp.concatenate([qcols, jnp.concatenate(pens, axis=1)], axis=0)
    y1 = y1_ref[...]

    m_ref[...] = jnp.full((1, R), NEG, f32)
    l_ref[...] = jnp.zeros((1, R), f32)
    acc_ref[...] = jnp.zeros((HEAD_DIM, R), f32)
    n_far = (qi + KT // TQ) // (KT // TQ) - 1
    first = (qi + 1) * TQ - n_far * KT

    def tile_row(i):
        return pl.multiple_of(first + i * KT, LANES)

    def produce(i, s_ref, mx_ref):
        r0 = tile_row(i)
        k_ext = jnp.concatenate([ks_ref[pl.ds(r0, KT), :], e_ref[pl.ds(r0, KT), :]], axis=1)
        s = _dot(k_ext, q_slc)
        s_ref[...] = s
        mx_ref[...] = jnp.max(s, axis=0, keepdims=True)

    def consume(i, s_ref, mx_ref, last):
        m_prev = m_ref[...]
        if last:
            s = jnp.concatenate([s_ref[0:KT - 2 * TQ], s_ref[KT - 2 * TQ:KT] + tab_ref[TQ:3 * TQ]], axis=0)
            m_new = jnp.maximum(m_prev, jnp.max(s, axis=0, keepdims=True))
        else:
            s = s_ref[...]
            m_new = jnp.maximum(m_prev, mx_ref[...])
        alpha = jnp.exp2(m_prev - m_new)
        p = jnp.exp2(s - m_new)
        p = p.astype(bf16)
        for g in range(N_KV):
            c = slice(g * GC, (g + 1) * GC)
            pv = _dot(vsT_ref[g, :, pl.ds(tile_row(i), KT)], p[:, c])
            l_ref[:, c] = alpha[:, c] * l_ref[:, c] + pv[HEAD_DIM:HEAD_DIM + 1]
            acc_ref[:, c] = acc_ref[:, c] * alpha[:, c] + pv[0:HEAD_DIM]
        m_ref[...] = m_new

    qn = jnp.minimum(qi + 1, nq - 1)

    def front(nc_use, nsel_use):
        tn = qn * TQ
        qc_next = _query_cols(qn_ref)
        sig_next = _sigmoid(gn_ref[...]).T
        qc_ref[...] = qc_next
        sg_ref[...] = sig_next
        s1 = _cmp_scores(jnp.concatenate([qc_next, acmp_ref[...]], axis=0), cmp_ref, tn, nc_use)
        w0 = pl.multiple_of(t0, LANES)
        s3 = _dot(kw_ref[pl.ds(w0, N_WIN), :], q_cmp)
        _cmp_finish(s1, sig_next, cmpT_ref, ovT_ref, pen_ref, y1_ref, tn, nsel_use)
        produce(0, sa_ref, mxa_ref)
        s3 = jnp.concatenate([s3[0:TQ] + tab_ref[0:TQ], s3[TQ:WINDOW - TQ], s3[WINDOW - TQ:] + tab_ref[TQ:3 * TQ]],
                             axis=0)
        p3 = jnp.exp2(s3 - jnp.max(s3, axis=0, keepdims=True)).astype(bf16)
        for g in range(N_KV):
            c = slice(g * GC, (g + 1) * GC)
            o3_ref[:, c] = _dot(vwT_ref[g, :, pl.ds(w0, N_WIN)], p3[:, c])

    for k, (nc_k, nsel_k) in enumerate(classes):
        lo = last_q[k - 1] if k else -1
        if k < N_CLASSES - 1:
            pl.when(jnp.logical_and(qn > lo, qn <= last_q[k]))(functools.partial(front, nc_k, nsel_k))
        else:
            pl.when(qn > lo)(functools.partial(front, nc_k, nsel_k))

    bufs = ((sa_ref, mxa_ref), (sb_ref, mxb_ref))

    def pipeline(base, count, last_at_end):
        for k in range(count):
            if k + 1 < count or not last_at_end:
                produce(base + k + 1, *bufs[(k + 1) % 2])
            consume(base + k, *bufs[k % 2], last_at_end and k == count - 1)

    def unrolled(j, c):
        pipeline(UNROLL * j, UNROLL, False)
        return c

    lax.fori_loop(0, n_far // UNROLL, unrolled, 0)

    def finish():
        inv2 = 1.0 / l_ref[...]
        inv3 = 1.0 / o3_ref[HEAD_DIM:HEAD_DIM + 1, :]
        sig_t = sig_cur
        outs = []
        for h in range(N_HEADS):
            c = slice(h * TQ, (h + 1) * TQ)
            g2 = sig_t[3 * h + 1:3 * h + 2, :] * inv2[:, c]
            g3 = sig_t[3 * h + 2:3 * h + 3, :] * inv3[:, c]
            outs.append(g2 * acc_ref[:, c] + g3 * o3_ref[0:HEAD_DIM, c])
        for p, y in enumerate(_merge_head_pairs(outs)):
            c = slice(p * LANES, (p + 1) * LANES)
            zb = z_ref[:, c].astype(f32)
            out_ref[:, c] = ((y + y1[:, c].astype(f32)) * zb * _sigmoid(zb)).astype(bf16)

    for r in range(UNROLL):
        def drain(r=r):
            pipeline(n_far - r, r + 1, True)
            finish()
        pl.when(n_far % UNROLL == r)(drain)


def _nsa(q, z, g, cmp, cmpT, ks, kw, vsT, vwT, e_pad, ovT, tabs, acmp, B, T):
    nq = T // TQ
    nc = T // CMP_STRIDE
    tp = T + KPAD
    tile = lambda b, i: (b * nq + i, 0)
    nxt = lambda b, i: (b * nq + jnp.minimum(i + 1, nq - 1), 0)
    head = lambda b, i: (b * nq, 0)
    const2 = lambda b, i: (0, 0)
    batch3 = lambda b, i: (b, 0, 0)
    return pl.pallas_call(
        _nsa_kernel,
        grid=(B, nq),
        in_specs=[pl.BlockSpec((TQ, D_NSA), head),
                  pl.BlockSpec((TQ, D_NSA), nxt),
                  pl.BlockSpec((TQ, D_NSA), tile),
                  pl.BlockSpec((TQ, LANES), head),
                  pl.BlockSpec((TQ, LANES), nxt),
                  pl.BlockSpec((None, nc, 2 * D_KV), batch3),
                  pl.BlockSpec((None, 2 * D_KV, nc), batch3),
                  pl.BlockSpec((NSB + SUM_ROWS, nc), const2),
                  pl.BlockSpec((None, tp, D_KV), batch3),
                  pl.BlockSpec((None, tp, 2 * D_KV), batch3),
                  pl.BlockSpec((None, N_KV, VROWS, tp), lambda b, i: (b, 0, 0, 0)),
                  pl.BlockSpec((None, N_KV, VROWS, tp), lambda b, i: (b, 0, 0, 0)),
                  pl.BlockSpec((tp, NSB), const2),
                  pl.BlockSpec((3 * TQ, R), const2),
                  pl.BlockSpec((LANES, R), const2)],
        out_specs=pl.BlockSpec((TQ, D_NSA), tile),
        out_shape=jax.ShapeDtypeStruct((B * T, D_NSA), bf16),
        scratch_shapes=[pltpu.VMEM((LANES, R), bf16), pltpu.VMEM((LANES, TQ), f32),
                        pltpu.VMEM((NSB, N_KV * TQ), bf16), pltpu.VMEM((TQ, D_NSA), bf16), pltpu.VMEM((VROWS, R), f32),
                        pltpu.VMEM((1, R), f32), pltpu.VMEM((1, R), f32), pltpu.VMEM((HEAD_DIM, R), f32),
                        pltpu.VMEM((KT, R), f32), pltpu.VMEM((KT, R), f32),
                        pltpu.VMEM((1, R), f32), pltpu.VMEM((1, R), f32)],
        compiler_params=pltpu.CompilerParams(dimension_semantics=("arbitrary", "arbitrary"),
                                             vmem_limit_bytes=VMEM_LIMIT),
        name="nsa",
    )(q, q, z, g, g, cmp, cmpT, ovT, ks, kw, vsT, vwT, e_pad, tabs, acmp)


def _outproj_kernel(alpha, ymix_ref, ynsa_ref, x_ref, w_ref, g_ref, b_ref, out_ref):
    y = jnp.concatenate([ymix_ref[...], ynsa_ref[...]], axis=1)
    r = alpha * x_ref[...] + _dot(y, w_ref[...])
    mu = jnp.mean(r, axis=-1, keepdims=True)
    d = r - mu
    var = jnp.mean(d * d, axis=-1, keepdims=True)
    out_ref[...] = d * lax.rsqrt(var + LN_EPS) * g_ref[...] + b_ref[...]


def _outproj(ymix, ynsa, x2, w_out, ln_g, ln_b, alpha, layer):
    bt = x2.shape[0]
    row = lambda i: (i, 0)
    wsel = lambda i: (layer, 0, 0)
    return pl.pallas_call(
        functools.partial(_outproj_kernel, alpha),
        grid=(bt // TT_OUT,),
        in_specs=[pl.BlockSpec((TT_OUT, D_POOL + D_CONV), row),
                  pl.BlockSpec((TT_OUT, D_NSA), row),
                  pl.BlockSpec((TT_OUT, D_MODEL), row),
                  pl.BlockSpec((None, D_MODEL, D_MODEL), wsel),
                  pl.BlockSpec((None, 1, D_MODEL), wsel),
                  pl.BlockSpec((None, 1, D_MODEL), wsel)],
        out_specs=pl.BlockSpec((TT_OUT, D_MODEL), row),
        out_shape=jax.ShapeDtypeStruct((bt, D_MODEL), f32),
        compiler_params=pltpu.CompilerParams(dimension_semantics=("arbitrary",),
                                             vmem_limit_bytes=VMEM_LIMIT),
        name="outproj",
    )(ymix, ynsa, x2, w_out, ln_g, ln_b)


def _bucket_np(d):
    d = np.asarray(d)
    max_exact = N_BUCKETS // 2
    nf = np.maximum(d, 1).astype(np.float32)
    large = max_exact + (np.log(nf / np.float32(max_exact)) / np.float32(math.log(MAX_DISTANCE / max_exact))
                         * np.float32(N_BUCKETS - max_exact)).astype(np.int32)
    large = np.minimum(large, N_BUCKETS - 1)
    return np.where(d < max_exact, d, large)


_FAR_DIST = 113
assert _bucket_np(np.arange(_FAR_DIST, 4 * WINDOW)).min() == N_BUCKETS - 1

assert CMP_STRIDE * (CMP_BACK + 1) - (CMP_BLOCK - 1) >= _FAR_DIST


def _bias_tables(rel_bias):
    nd = 2 * TQ
    onehot = jnp.asarray(np.eye(N_BUCKETS, dtype=np.float32)[_bucket_np(np.arange(nd))])
    tabp = ((jnp.dot(onehot, rel_bias, precision=lax.Precision.HIGHEST)
             - rel_bias[N_BUCKETS - 1:N_BUCKETS, :]) * LOG2E).T
    sl = np.arange(TQ)[:, None]
    tl = np.arange(TQ)[None, :]
    neg = lambda n: jnp.full((N_HEADS, n), NEG, f32)

    def skew(u, rows):
        period = u.shape[1]
        return jnp.tile(u, (1, rows))[:, :rows * (period - 1)].reshape(N_HEADS, rows, period - 1)

    def tile_layout(t):
        return t.transpose(1, 0, 2).reshape(TQ, R)

    diag = tile_layout(skew(jnp.concatenate([tabp[:, 0:TQ], neg(TQ)], axis=1), TQ)[:, :, :TQ])
    prev = tile_layout(skew(jnp.concatenate([tabp[:, TQ:nd], tabp[:, 0:TQ]], axis=1), TQ)[:, :, :TQ])
    edge = jnp.asarray(np.tile(np.where(sl > tl, 0.0, NEG).astype(np.float32), (1, N_HEADS)))
    tabs = jnp.concatenate([edge, prev, diag], axis=0).astype(f32)

    j = np.arange(CMP_SLOTS)[None, :]
    off = CMP_BACK * CMP_STRIDE - (CMP_BLOCK - 1)
    dc = np.arange(TQ)[:, None] - CMP_STRIDE * j + off
    okc = jnp.asarray(dc >= 0)[None]
    period = 4 * TQ
    u = jnp.concatenate([tabp[:, off:nd], jnp.zeros((N_HEADS, off), f32), neg(period - nd - off), tabp[:, 0:off]], axis=1)
    vc = skew(u, nd)[:, 0:CMP_SLOTS * CMP_STRIDE:CMP_STRIDE, 0:TQ].transpose(0, 2, 1)
    vc = jnp.where(okc, vc, NEG)
    hi = vc.astype(bf16)
    lo = jnp.where(okc, vc - hi.astype(f32), 0.0).astype(bf16)
    fut = jnp.full((N_HEADS, TQ, 1), NEG, bf16)
    pad = jnp.zeros((N_HEADS, TQ, LANES - _PAD_FLAG_COL - 1), bf16)
    acmp = jnp.concatenate([hi, lo, fut, pad], axis=-1).reshape(R, LANES).T
    return tabs, acmp


def _static_tables(T):
    nc = T // CMP_STRIDE
    ns = T // SLC_BLOCK
    cstart = np.arange(nc)[None, :] * CMP_STRIDE
    sstart = np.arange(NSB)[:, None] * SLC_BLOCK
    ov = np.clip(np.minimum(cstart + CMP_BLOCK, sstart + SLC_BLOCK) - np.maximum(cstart, sstart), 0, None) / CMP_STRIDE
    ov[ns:, :] = 0
    ov[:, nc - 1] = 0
    ov = np.concatenate([ov, np.ones((SUM_ROWS, nc))], axis=0)
    e_pad = np.concatenate([np.ones((KPAD, NSB), bool),
                            np.arange(T)[:, None] // SLC_BLOCK == np.arange(NSB)[None, :]], axis=0)
    return jnp.asarray(ov, bf16), jnp.asarray(e_pad, bf16)


def _pair_order(w, axis):
    shp = w.shape
    w = w.reshape(shp[:axis] + (N_KV, GROUP, HEAD_DIM) + shp[axis + 1:])
    return jnp.swapaxes(w, axis, axis + 1).reshape(shp)


def _prep_weights(w_in, w_out, pool_w, pe_k, w1_k, w2_k, pe_v, w1_v, w2_v):
    depth = w_in.shape[0]
    sizes = (D_POOL, D_POOL, D_CONV, D_CONV, D_CONV, D_CONV, D_NSA, D_KV, D_KV, D_KV, D_KV, D_KV, D_KV,
             3 * N_HEADS, D_NSA)
    offs = np.cumsum((0,) + sizes)
    w_in = w_in.astype(bf16)
    col = lambda i: w_in[:, :, offs[i]:offs[i + 1]]
    wg = jnp.pad(col(13), ((0, 0), (0, 0), (0, LANES - 3 * N_HEADS)))
    w_all = jnp.concatenate([w_in[:, :, 0:MIX_W], _pair_order(col(6), 2), _pair_order(col(14), 2),
                             col(7), col(8), col(9), col(11), wg], axis=2)
    wvt = jnp.swapaxes(jnp.concatenate([col(10), col(12)], axis=2), 1, 2)
    w_out = w_out.astype(bf16)
    nm = D_POOL + D_CONV
    w_out_p = jnp.concatenate([w_out[:, 0:nm], _pair_order(w_out[:, nm:], 1)], axis=1)

    eye_g = jnp.eye(len(POOL_WINDOWS), dtype=bf16)
    pw_bd = jnp.einsum('zgcd,gh->zgchd', pool_w.astype(bf16), eye_g).reshape(depth, D_POOL, D_POOL)

    half = CMP_BLOCK // 2
    eye2 = jnp.eye(2, dtype=bf16)
    rw = half * 2 * D_KV
    cols = []
    for a in range(2):
        for kv, w1 in enumerate((w1_k, w1_v)):
            wsel = w1.astype(bf16).reshape(depth, 2, half, 1, 1, HEAD_DIM, CMP_HIDDEN)[:, a]
            for g in range(N_KV):
                blk = jnp.pad(wsel, ((0, 0), (0, 0), (kv, 1 - kv), (g, N_KV - 1 - g), (0, 0), (0, 0)))
                cols.append(blk.reshape(depth, rw, CMP_HIDDEN))
    w1_big = jnp.concatenate(cols, axis=2)
    pes = jnp.stack([pe_k, pe_v], axis=1).astype(bf16).reshape(depth, 2, 2, half, HEAD_DIM)
    pe_r = jnp.broadcast_to(pes.transpose(0, 2, 3, 1, 4)[:, :, :, :, None, :],
                            (depth, 2, half, 2, N_KV, HEAD_DIM)).reshape(depth, 2, half * 2 * D_KV)
    pe_r = jnp.pad(pe_r, ((0, 0), (0, 6), (0, 0)))
    w2s = jnp.stack([w2_k, w2_v], axis=1).astype(bf16)
    w2_big = jnp.einsum('zkhd,kK,gG->zkghKGd', w2s, eye2, eye2).reshape(depth, CH, 2 * D_KV)
    return w_all, wvt, w_out_p, pw_bd, w1_big, pe_r, w2_big, jnp.swapaxes(w2_big, 1, 2)


def kernel(x, w_in, w_out, pool_w, pool_scale, conv_w, cmp_pe_k, cmp_w1_k, cmp_w2_k, cmp_pe_v, cmp_w1_v, cmp_w2_v,
           rel_bias, ln_g, ln_b):
    B, T, D = x.shape
    depth = w_in.shape[0]
    assert D == D_MODEL and T % TT == 0 and T // SLC_BLOCK <= NSB and T // SLC_BLOCK >= N_SELECT
    assert KPAD == TT and (B * T) % TT_OUT == 0 and N_SELECT > N_LOCAL
    alpha = (2 * depth) ** 0.25
    tabs, acmp = _bias_tables(rel_bias)
    ovT, e_pad = _static_tables(T)
    w_all, wvt, w_out_p, pw_bd, w1_big, pe_r, w2_big, w2t_big = _prep_weights(
        w_in, w_out, pool_w, cmp_pe_k, cmp_w1_k, cmp_w2_k, cmp_pe_v, cmp_w1_v, cmp_w2_v)
    pool_scale = pool_scale.reshape(depth, 1, D_POOL)
    ln_g = ln_g.reshape(depth, 1, D_MODEL)
    ln_b = ln_b.reshape(depth, 1, D_MODEL)
    h = x.reshape(B * T, D)
    for l in range(depth):
        ymix, q, z, cmp_l, g, ks, kw, vsT, vwT = _inproj(h, w_all, wvt, pw_bd, pool_scale, conv_w, l, B, T)
        cmp, cmpT = _compress(cmp_l, pe_r, w1_big, w2_big, w2t_big, l, B)
        ynsa = _nsa(q, z, g, cmp, cmpT, ks, kw, vsT, vwT, e_pad, ovT, tabs, acmp, B, T)
        h = _outproj(ymix, ynsa, h, w_out_p, ln_g, ln_b, alpha, l)
    return h.reshape(B, T, D)
```

```python
import functools
import math

import numpy as np
import jax
import jax.numpy as jnp
from jax import lax
from jax.experimental import pallas as pl
from jax.experimental.pallas import tpu as pltpu

f32 = jnp.float32
bf16 = jnp.bfloat16

D_MODEL = 1024
D_POOL = 256
D_CONV = 256
D_NSA = 512
HEAD_DIM = 64
N_HEADS = 8
N_KV = 2
GROUP = 4
D_KV = 128
POOL_GROUP = 64
POOL_WINDOWS = (2, 4, 8, 16)
CONV_WIDTH = 3
CMP_BLOCK = 32
CMP_STRIDE = 16
CMP_HIDDEN = 128
SLC_BLOCK = 64
N_SELECT = 16
N_LOCAL = 2
WINDOW = 512
N_BUCKETS = 32
MAX_DISTANCE = 128
LN_EPS = 1e-5
NEG = -1e30

LANES = 128
TQ = 128
R = N_HEADS * TQ
NSB = 128
TT = 512
TT_OUT = 1024
UNROLL = 8
N_CLASSES = 4
HALO = 16
KT = 512
KPAD = KT
N_WIN = WINDOW + TQ
SUM_ROWS = 16
VROWS = HEAD_DIM + SUM_ROWS
GC = R // N_KV
LOG2E = math.log2(math.e)
CMP_BACK = 8
CMP_SLOTS = TQ // CMP_STRIDE + CMP_BACK
_PAD_FLAG_COL = 2 * CMP_SLOTS
MIX_W = 2 * D_POOL + 4 * D_CONV
VMEM_LIMIT = 56 * 1024 * 1024

_NT = (((1,), (1,)), ((), ()))


def _dot(a, b):
    return jnp.dot(a, b, preferred_element_type=f32)


def _dot_nt(a, b):
    return lax.dot_general(a, b, _NT, preferred_element_type=f32)


def _sigmoid(x):
    return 1.0 / (1.0 + jnp.exp(-x))


_C_MIX = (0, MIX_W)
_C_Q = (_C_MIX[1], _C_MIX[1] + D_NSA)
_C_Z = (_C_Q[1], _C_Q[1] + D_NSA)
_C_CMP = (_C_Z[1], _C_Z[1] + 2 * D_KV)
_C_K = (_C_CMP[1], _C_CMP[1] + 2 * D_KV)
_C_G = (_C_K[1], _C_K[1] + LANES)
W_ALL = _C_G[1]
CMP_ROWS = TT // CMP_STRIDE


def _local_mixers(ext, i, pw_ref, ps_ref, cw_ref):
    e = ext[:, 0:D_POOL]
    s2 = e + pltpu.roll(e, 1, axis=0)
    s4 = s2 + pltpu.roll(s2, 2, axis=0)
    s8 = s4 + pltpu.roll(s4, 4, axis=0)
    s16 = s8 + pltpu.roll(s8, 8, axis=0)
    lane = lax.broadcasted_iota(jnp.int32, (TT, D_POOL), 1)
    grp = lane // POOL_GROUP
    wsum = jnp.where(grp == 0, s2[HALO:], jnp.where(grp == 1, s4[HALO:], jnp.where(grp == 2, s8[HALO:], s16[HALO:])))
    win = jnp.left_shift(2, grp)
    pos = i * TT + lax.broadcasted_iota(jnp.int32, (TT, D_POOL), 0)
    cnt = jnp.minimum(pos + 1, win).astype(f32)
    v = e[HALO:]
    pooled = wsum / cnt - v
    y_pool = _dot(pooled.astype(bf16), pw_ref[...]) * ps_ref[...]
    zp = ext[HALO:, D_POOL:2 * D_POOL]
    y_pool = y_pool * (zp * _sigmoid(zp))

    o = 2 * D_POOL
    cb = ext[HALO:, o:o + D_CONV]
    u = ext[:, o + D_CONV:o + 2 * D_CONV] * ext[:, o + 2 * D_CONV:o + 3 * D_CONV]
    zc = ext[HALO:, o + 3 * D_CONV:o + 4 * D_CONV]
    conv = cw_ref[CONV_WIDTH - 1:CONV_WIDTH, :] * u[HALO:]
    for k in range(CONV_WIDTH - 1):
        conv = conv + cw_ref[k:k + 1, :] * pltpu.roll(u, CONV_WIDTH - 1 - k, axis=0)[HALO:]
    y_conv = cb * conv * (zc * _sigmoid(zc))
    return y_pool, y_conv


def _inproj_kernel(x_ref, w_ref, wvt_ref, pw_ref, ps_ref, cw_ref,
                   ymix_ref, q_ref, z_ref, cmp_ref, g_ref, ks_ref, kw_ref, vsT_ref, vwT_ref,
                   halo_ref, cscr_ref):
    step = pl.program_id(1)
    ones_rows = jnp.ones((SUM_ROWS, TT), bf16)

    @pl.when(step == 0)
    def _():
        ks_ref[...] = jnp.zeros((TT, D_KV), bf16)
        kw_ref[:, 0:D_KV] = jnp.zeros((TT, D_KV), bf16)
        lane = lax.broadcasted_iota(jnp.int32, (TT, D_KV), 1)
        kw_ref[:, D_KV:2 * D_KV] = jnp.where(lane == _PAD_FLAG_COL, 1.0, 0.0).astype(bf16)
        for g in range(N_KV):
            for ref in (vsT_ref, vwT_ref):
                ref[g, 0:HEAD_DIM] = jnp.zeros((HEAD_DIM, TT), bf16)
                ref[g, HEAD_DIM:VROWS] = ones_rows

    @pl.when(step > 0)
    def _():
        i = step - 1
        x = x_ref[...].astype(bf16)

        def proj(c):
            return _dot(x, w_ref[:, c[0]:c[1]])

        mix = proj(_C_MIX)
        halo = jnp.where(i > 0, halo_ref[...], 0.0)
        y_pool, y_conv = _local_mixers(jnp.concatenate([halo, mix], axis=0), i, pw_ref, ps_ref, cw_ref)
        halo_ref[...] = mix[TT - HALO:]
        ymix_ref[:, 0:D_POOL] = y_pool.astype(bf16)
        ymix_ref[:, D_POOL:D_POOL + D_CONV] = y_conv.astype(bf16)

        q_ref[...] = (proj(_C_Q) * (HEAD_DIM ** -0.5 * LOG2E)).astype(bf16)
        z_ref[...] = proj(_C_Z).astype(bf16)
        g_ref[...] = proj(_C_G)
        kk = proj(_C_K).astype(bf16)
        ks_ref[...] = kk[:, 0:D_KV]
        kw_ref[:, 0:D_KV] = kk[:, D_KV:2 * D_KV]
        kw_ref[:, D_KV:2 * D_KV] = jnp.zeros((TT, D_KV), bf16)
        vt = _dot_nt(wvt_ref[...], x)
        for g in range(N_KV):
            vsT_ref[g, 0:HEAD_DIM] = vt[g * HEAD_DIM:(g + 1) * HEAD_DIM].astype(bf16)
            vwT_ref[g, 0:HEAD_DIM] = vt[D_KV + g * HEAD_DIM:D_KV + (g + 1) * HEAD_DIM].astype(bf16)
            vsT_ref[g, HEAD_DIM:VROWS] = ones_rows
            vwT_ref[g, HEAD_DIM:VROWS] = ones_rows

        kvc = proj(_C_CMP)
        for c in range(2):
            cscr_ref[c] = kvc[:, c * LANES:(c + 1) * LANES]
            for l in range(CMP_STRIDE):
                cmp_ref[l, :, c * LANES:(c + 1) * LANES] = (
                    cscr_ref[c, pl.ds(l, CMP_ROWS, stride=CMP_STRIDE), :].astype(bf16))


def _inproj(x2, w_all, wvt, pw_bd, pool_scale, conv_w, layer, B, T):
    nt = T // TT
    bt = B * T
    row = lambda b, s: (b * nt + jnp.maximum(s - 1, 0), 0)
    wsel = lambda b, s: (layer, 0, 0)
    tp = T + KPAD
    out_shape = [jax.ShapeDtypeStruct((bt, D_POOL + D_CONV), bf16),
                 jax.ShapeDtypeStruct((bt, D_NSA), bf16),
                 jax.ShapeDtypeStruct((bt, D_NSA), bf16),
                 jax.ShapeDtypeStruct((CMP_STRIDE, bt // CMP_STRIDE, 2 * D_KV), bf16),
                 jax.ShapeDtypeStruct((bt, LANES), f32),
                 jax.ShapeDtypeStruct((B, tp, D_KV), bf16),
                 jax.ShapeDtypeStruct((B, tp, 2 * D_KV), bf16),
                 jax.ShapeDtypeStruct((B, N_KV, VROWS, tp), bf16),
                 jax.ShapeDtypeStruct((B, N_KV, VROWS, tp), bf16)]
    out_specs = [pl.BlockSpec((TT, D_POOL + D_CONV), row),
                 pl.BlockSpec((TT, D_NSA), row),
                 pl.BlockSpec((TT, D_NSA), row),
                 pl.BlockSpec((CMP_STRIDE, CMP_ROWS, 2 * D_KV), lambda b, s: (0, b * nt + jnp.maximum(s - 1, 0), 0)),
                 pl.BlockSpec((TT, LANES), row),
                 pl.BlockSpec((None, TT, D_KV), lambda b, s: (b, s, 0)),
                 pl.BlockSpec((None, TT, 2 * D_KV), lambda b, s: (b, s, 0)),
                 pl.BlockSpec((None, N_KV, VROWS, TT), lambda b, s: (b, 0, 0, s)),
                 pl.BlockSpec((None, N_KV, VROWS, TT), lambda b, s: (b, 0, 0, s))]
    return pl.pallas_call(
        _inproj_kernel,
        grid=(B, nt + 1),
        in_specs=[pl.BlockSpec((TT, D_MODEL), row),
                  pl.BlockSpec((None, D_MODEL, W_ALL), wsel),
                  pl.BlockSpec((None, 2 * D_KV, D_MODEL), wsel),
                  pl.BlockSpec((None, D_POOL, D_POOL), wsel),
                  pl.BlockSpec((None, 1, D_POOL), wsel),
                  pl.BlockSpec((None, CONV_WIDTH, D_CONV), wsel)],
        out_specs=out_specs,
        out_shape=out_shape,
        scratch_shapes=[pltpu.VMEM((HALO, MIX_W), f32), pltpu.VMEM((2, TT, LANES), f32)],
        compiler_params=pltpu.CompilerParams(dimension_semantics=("arbitrary", "arbitrary"),
                                             vmem_limit_bytes=VMEM_LIMIT),
        name="inproj",
    )(x2, w_all, wvt, pw_bd, pool_scale, conv_w)


CH = 4 * CMP_HIDDEN


def _compress_kernel(r_ref, pe_ref, w1_ref, w2_ref, w2t_ref, out_ref, outT_ref):
    nc = r_ref.shape[1]
    hk = CH // 2
    hs = []
    for kv in range(2):
        rows = jnp.concatenate([r_ref[l][:, kv * D_KV:(kv + 1) * D_KV] for l in range(CMP_STRIDE)], axis=1)
        zz = _dot(rows, w1_ref[kv])
        pb = _dot(pe_ref[kv], w1_ref[kv])
        z0 = zz[:, 0:hk] + pb[0:1, 0:hk]
        z1 = zz[:, hk:2 * hk] + pb[1:2, hk:2 * hk]
        hs.append(z0 + pltpu.roll(z1, nc - 1, axis=0))
    h = jnp.concatenate(hs, axis=1)
    h = (h * _sigmoid(h)).astype(bf16)
    out_ref[...] = _dot(h, w2_ref[...]).astype(bf16)
    outT_ref[...] = _dot_nt(w2t_ref[...], h).astype(bf16)


def _compress(cmp_l, pe_r, w1_big, w2_big, w2t_big, layer, B):
    nc = cmp_l.shape[1] // B
    rw = CMP_STRIDE * D_KV
    wsel = lambda b: (layer, 0, 0)
    wsel4 = lambda b: (layer, 0, 0, 0)
    return pl.pallas_call(
        _compress_kernel,
        grid=(B,),
        in_specs=[pl.BlockSpec((CMP_STRIDE, nc, 2 * D_KV), lambda b: (0, b, 0)),
                  pl.BlockSpec((None, 2, 8, rw), wsel4),
                  pl.BlockSpec((None, 2, rw, CH), wsel4),
                  pl.BlockSpec((None, CH, 2 * D_KV), wsel),
                  pl.BlockSpec((None, 2 * D_KV, CH), wsel)],
        out_specs=[pl.BlockSpec((None, nc, 2 * D_KV), lambda b: (b, 0, 0)),
                   pl.BlockSpec((None, 2 * D_KV, nc), lambda b: (b, 0, 0))],
        out_shape=[jax.ShapeDtypeStruct((B, nc, 2 * D_KV), bf16),
                   jax.ShapeDtypeStruct((B, 2 * D_KV, nc), bf16)],
        compiler_params=pltpu.CompilerParams(dimension_semantics=("arbitrary",),
                                             vmem_limit_bytes=VMEM_LIMIT),
        name="compress",
    )(cmp_l, pe_r, w1_big, w2_big, w2t_big)


def _select_penalty(imp_t, t0):
    shape = imp_t.shape
    jblk = lax.broadcasted_iota(jnp.int32, shape, 0)
    tq = lax.broadcasted_iota(jnp.int32, shape, 1) % TQ
    back = jnp.right_shift(t0 + tq, int(math.log2(SLC_BLOCK))) - jblk
    causal = back >= 0
    forced = jnp.logical_or(jblk == 0, jnp.logical_and(causal, back < N_LOCAL))
    score = jnp.where(forced, -jnp.inf, jnp.where(causal, imp_t, NEG))
    jf = jblk.astype(f32)
    for _ in range(N_SELECT - N_LOCAL - 1):
        best = jnp.max(score, axis=0, keepdims=True)
        first = jnp.min(jnp.where(score == best, jf, float(shape[0])), axis=0, keepdims=True)
        score = jnp.where(jf == first, -jnp.inf, score)
    return jnp.where(score == -jnp.inf, 0.0, NEG)


def _query_cols(q_ref):
    sub = lax.broadcasted_iota(jnp.int32, (LANES, TQ), 0)
    low = sub < HEAD_DIM
    pairs = [q_ref[:, LANES * p:LANES * (p + 1)].astype(f32).T for p in range(GROUP)]
    blocks = [(jnp.where(low, pairs[h % GROUP], 0.0) if h < GROUP else jnp.where(low, 0.0, pairs[h % GROUP])).astype(bf16)
              for h in range(N_HEADS)]
    return jnp.concatenate(blocks, axis=1)


def _merge_head_pairs(outs):
    return [jnp.concatenate([outs[p], outs[p + GROUP]], axis=0).T for p in range(GROUP)]


def _cmp_scores(q_cmp, cmp_ref, t0, nc):
    n_i = lax.broadcasted_iota(jnp.int32, (nc, LANES), 0)
    j_i = lax.broadcasted_iota(jnp.int32, (nc, LANES), 1)
    nstart = t0 // CMP_STRIDE - CMP_BACK
    in_window = jnp.logical_and(j_i < 2 * CMP_SLOTS, n_i == nstart + jnp.where(j_i < CMP_SLOTS, j_i, j_i - CMP_SLOTS))
    future = jnp.logical_and(j_i == _PAD_FLAG_COL, n_i >= nstart + CMP_SLOTS - 1)
    place_b = jnp.where(jnp.logical_or(in_window, future), 1.0, 0.0).astype(bf16)
    kc_ext = jnp.concatenate([cmp_ref[0:nc, 0:D_KV], place_b], axis=1)
    return _dot(kc_ext, q_cmp)


def _cmp_finish(s1, sig_t, cmpT_ref, ovT_ref, pen_ref, y1_ref, t0, nsel):
    nc = s1.shape[0]
    m1 = jnp.max(s1, axis=0, keepdims=True)
    p1 = jnp.exp2(s1 - m1)
    p1 = p1.astype(bf16)
    tcol = t0 + lax.broadcasted_iota(jnp.int32, (1, GC), 1) % TQ
    outs, sums = [], []
    for g in range(N_KV):
        lhs = jnp.concatenate([cmpT_ref[D_KV + g * HEAD_DIM:D_KV + (g + 1) * HEAD_DIM, 0:nc], ovT_ref[0:nsel, 0:nc],
                               ovT_ref[NSB:NSB + SUM_ROWS, 0:nc]], axis=0)
        both = _dot(lhs, p1[:, g * GC:(g + 1) * GC])
        l1 = both[HEAD_DIM + nsel:HEAD_DIM + nsel + 1]
        both = both * jnp.where(tcol >= CMP_BLOCK - 1, 1.0 / l1, 0.0)
        for r in range(GROUP):
            h = g * GROUP + r
            outs.append(sig_t[3 * h:3 * h + 1, :] * both[0:HEAD_DIM, r * TQ:(r + 1) * TQ])
        imp = both[HEAD_DIM:HEAD_DIM + nsel]
        acc = imp[:, 0:TQ]
        for r in range(1, GROUP):
            acc = acc + imp[:, r * TQ:(r + 1) * TQ]
        sums.append(acc)
    for p, y in enumerate(_merge_head_pairs(outs)):
        y1_ref[:, p * LANES:(p + 1) * LANES] = y.astype(bf16)
    imp_t = jnp.concatenate(sums, axis=1)
    pen_ref[0:nsel] = _select_penalty(imp_t, t0).astype(bf16)
    if nsel < NSB:
        pen_ref[nsel:NSB] = jnp.full((NSB - nsel, N_KV * TQ), NEG, bf16)


def _nsa_kernel(q0_ref, qn_ref, z_ref, g0_ref, gn_ref, cmp_ref, cmpT_ref, ovT_ref, ks_ref, kw_ref, vsT_ref, vwT_ref,
                e_ref, tab_ref, acmp_ref, out_ref,
                qc_ref, sg_ref, pen_ref, y1_ref, o3_ref, m_ref, l_ref, acc_ref, sa_ref, sb_ref, mxa_ref, mxb_ref):
    qi = pl.program_id(1)
    nq = pl.num_programs(1)
    t0 = qi * TQ
    nc = cmp_ref.shape[0]
    classes = [(nc * k // N_CLASSES, NSB * k // N_CLASSES) for k in range(1, N_CLASSES + 1)]
    last_q = [min(c // (TQ // CMP_STRIDE), s // (TQ // SLC_BLOCK)) - 1 for c, s in classes]

    @pl.when(qi == 0)
    def _():
        qc_ref[...] = _query_cols(q0_ref)
        sg_ref[...] = _sigmoid(g0_ref[...]).T
        q_cmp0 = jnp.concatenate([qc_ref[...], acmp_ref[...]], axis=0)
        _cmp_finish(_cmp_scores(q_cmp0, cmp_ref, t0, classes[0][0]), sg_ref[...], cmpT_ref, ovT_ref, pen_ref, y1_ref, t0,
                    classes[0][1])

    qcols = qc_ref[...]
    sig_cur = sg_ref[...]
    q_cmp = jnp.concatenate([qcols, acmp_ref[...]], axis=0)
    pens = [pen_ref[:, g * TQ:(g + 1) * TQ] for g in range(N_KV) for _ in range(GROUP)]
    q_slc = jnp.concatenate([qcols, jnp.concatenate(pens, axis=1)], axis=0)
    y1 = y1_ref[...]

    m_ref[...] = jnp.full((1, R), NEG, f32)
    l_ref[...] = jnp.zeros((1, R), f32)
    acc_ref[...] = jnp.zeros((HEAD_DIM, R), f32)
    n_far = (qi + KT // TQ) // (KT // TQ) - 1
    first = (qi + 1) * TQ - n_far * KT

    def tile_row(i):
        return pl.multiple_of(first + i * KT, LANES)

    def produce(i, s_ref, mx_ref):
        r0 = tile_row(i)
        k_ext = jnp.concatenate([ks_ref[pl.ds(r0, KT), :], e_ref[pl.ds(r0, KT), :]], axis=1)
        s = _dot(k_ext, q_slc)
        s_ref[...] = s
        mx_ref[...] = jnp.max(s, axis=0, keepdims=True)

    def consume(i, s_ref, mx_ref, last):
        m_prev = m_ref[...]
        if last:
            s = jnp.concatenate([s_ref[0:KT - 2 * TQ], s_ref[KT - 2 * TQ:KT] + tab_ref[TQ:3 * TQ]], axis=0)
            m_new = jnp.maximum(m_prev, jnp.max(s, axis=0, keepdims=True))
        else:
            s = s_ref[...]
            m_new = jnp.maximum(m_prev, mx_ref[...])
        alpha = jnp.exp2(m_prev - m_new)
        p = jnp.exp2(s - m_new)
        p = p.astype(bf16)
        for g in range(N_KV):
            c = slice(g * GC, (g + 1) * GC)
            pv = _dot(vsT_ref[g, :, pl.ds(tile_row(i), KT)], p[:, c])
            l_ref[:, c] = alpha[:, c] * l_ref[:, c] + pv[HEAD_DIM:HEAD_DIM + 1]
            acc_ref[:, c] = acc_ref[:, c] * alpha[:, c] + pv[0:HEAD_DIM]
        m_ref[...] = m_new

    qn = jnp.minimum(qi + 1, nq - 1)

    def front(nc_use, nsel_use):
        tn = qn * TQ
        qc_next = _query_cols(qn_ref)
        sig_next = _sigmoid(gn_ref[...]).T
        qc_ref[...] = qc_next
        sg_ref[...] = sig_next
        s1 = _cmp_scores(jnp.concatenate([qc_next, acmp_ref[...]], axis=0), cmp_ref, tn, nc_use)
        w0 = pl.multiple_of(t0, LANES)
        s3 = _dot(kw_ref[pl.ds(w0, N_WIN), :], q_cmp)
        _cmp_finish(s1, sig_next, cmpT_ref, ovT_ref, pen_ref, y1_ref, tn, nsel_use)
        produce(0, sa_ref, mxa_ref)
        s3 = jnp.concatenate([s3[0:TQ] + tab_ref[0:TQ], s3[TQ:WINDOW - TQ], s3[WINDOW - TQ:] + tab_ref[TQ:3 * TQ]],
                             axis=0)
        p3 = jnp.exp2(s3 - jnp.max(s3, axis=0, keepdims=True)).astype(bf16)
        for g in range(N_KV):
            c = slice(g * GC, (g + 1) * GC)
            o3_ref[:, c] = _dot(vwT_ref[g, :, pl.ds(w0, N_WIN)], p3[:, c])

    for k, (nc_k, nsel_k) in enumerate(classes):
        lo = last_q[k - 1] if k else -1
        if k < N_CLASSES - 1:
            pl.when(jnp.logical_and(qn > lo, qn <= last_q[k]))(functools.partial(front, nc_k, nsel_k))
        else:
            pl.when(qn > lo)(functools.partial(front, nc_k, nsel_k))

    bufs = ((sa_ref, mxa_ref), (sb_ref, mxb_ref))

    def pipeline(base, count, last_at_end):
        for k in range(count):
            if k + 1 < count or not last_at_end:
                produce(base + k + 1, *bufs[(k + 1) % 2])
            consume(base + k, *bufs[k % 2], last_at_end and k == count - 1)

    def unrolled(j, c):
        pipeline(UNROLL * j, UNROLL, False)
        return c

    lax.fori_loop(0, n_far // UNROLL, unrolled, 0)

    def finish():
        inv2 = 1.0 / l_ref[...]
        inv3 = 1.0 / o3_ref[HEAD_DIM:HEAD_DIM + 1, :]
        sig_t = sig_cur
        outs = []
        for h in range(N_HEADS):
            c = slice(h * TQ, (h + 1) * TQ)
            g2 = sig_t[3 * h + 1:3 * h + 2, :] * inv2[:, c]
            g3 = sig_t[3 * h + 2:3 * h + 3, :] * inv3[:, c]
            outs.append(g2 * acc_ref[:, c] + g3 * o3_ref[0:HEAD_DIM, c])
        for p, y in enumerate(_merge_head_pairs(outs)):
            c = slice(p * LANES, (p + 1) * LANES)
            zb = z_ref[:, c].astype(f32)
            out_ref[:, c] = ((y + y1[:, c].astype(f32)) * zb * _sigmoid(zb)).astype(bf16)

    for r in range(UNROLL):
        def drain(r=r):
            pipeline(n_far - r, r + 1, True)
            finish()
        pl.when(n_far % UNROLL == r)(drain)


def _nsa(q, z, g, cmp, cmpT, ks, kw, vsT, vwT, e_pad, ovT, tabs, acmp, B, T):
    nq = T // TQ
    nc = T // CMP_STRIDE
    tp = T + KPAD
    tile = lambda b, i: (b * nq + i, 0)
    nxt = lambda b, i: (b * nq + jnp.minimum(i + 1, nq - 1), 0)
    head = lambda b, i: (b * nq, 0)
    const2 = lambda b, i: (0, 0)
    batch3 = lambda b, i: (b, 0, 0)
    return pl.pallas_call(
        _nsa_kernel,
        grid=(B, nq),
        in_specs=[pl.BlockSpec((TQ, D_NSA), head),
                  pl.BlockSpec((TQ, D_NSA), nxt),
                  pl.BlockSpec((TQ, D_NSA), tile),
                  pl.BlockSpec((TQ, LANES), head),
                  pl.BlockSpec((TQ, LANES), nxt),
                  pl.BlockSpec((None, nc, 2 * D_KV), batch3),
                  pl.BlockSpec((None, 2 * D_KV, nc), batch3),
                  pl.BlockSpec((NSB + SUM_ROWS, nc), const2),
                  pl.BlockSpec((None, tp, D_KV), batch3),
                  pl.BlockSpec((None, tp, 2 * D_KV), batch3),
                  pl.BlockSpec((None, N_KV, VROWS, tp), lambda b, i: (b, 0, 0, 0)),
                  pl.BlockSpec((None, N_KV, VROWS, tp), lambda b, i: (b, 0, 0, 0)),
                  pl.BlockSpec((tp, NSB), const2),
                  pl.BlockSpec((3 * TQ, R), const2),
                  pl.BlockSpec((LANES, R), const2)],
        out_specs=pl.BlockSpec((TQ, D_NSA), tile),
        out_shape=jax.ShapeDtypeStruct((B * T, D_NSA), bf16),
        scratch_shapes=[pltpu.VMEM((LANES, R), bf16), pltpu.VMEM((LANES, TQ), f32),
                        pltpu.VMEM((NSB, N_KV * TQ), bf16), pltpu.VMEM((TQ, D_NSA), bf16), pltpu.VMEM((VROWS, R), f32),
                        pltpu.VMEM((1, R), f32), pltpu.VMEM((1, R), f32), pltpu.VMEM((HEAD_DIM, R), f32),
                        pltpu.VMEM((KT, R), f32), pltpu.VMEM((KT, R), f32),
                        pltpu.VMEM((1, R), f32), pltpu.VMEM((1, R), f32)],
        compiler_params=pltpu.CompilerParams(dimension_semantics=("arbitrary", "arbitrary"),
                                             vmem_limit_bytes=VMEM_LIMIT),
        name="nsa",
    )(q, q, z, g, g, cmp, cmpT, ovT, ks, kw, vsT, vwT, e_pad, tabs, acmp)


def _outproj_kernel(alpha, ymix_ref, ynsa_ref, x_ref, w_ref, g_ref, b_ref, out_ref):
    y = jnp.concatenate([ymix_ref[...], ynsa_ref[...]], axis=1)
    r = alpha * x_ref[...] + _dot(y, w_ref[...])
    mu = jnp.mean(r, axis=-1, keepdims=True)
    d = r - mu
    var = jnp.mean(d * d, axis=-1, keepdims=True)
    out_ref[...] = d * lax.rsqrt(var + LN_EPS) * g_ref[...] + b_ref[...]


def _outproj(ymix, ynsa, x2, w_out, ln_g, ln_b, alpha, layer):
    bt = x2.shape[0]
    row = lambda i: (i, 0)
    wsel = lambda i: (layer, 0, 0)
    return pl.pallas_call(
        functools.partial(_outproj_kernel, alpha),
        grid=(bt // TT_OUT,),
        in_specs=[pl.BlockSpec((TT_OUT, D_POOL + D_CONV), row),
                  pl.BlockSpec((TT_OUT, D_NSA), row),
                  pl.BlockSpec((TT_OUT, D_MODEL), row),
                  pl.BlockSpec((None, D_MODEL, D_MODEL), wsel),
                  pl.BlockSpec((None, 1, D_MODEL), wsel),
                  pl.BlockSpec((None, 1, D_MODEL), wsel)],
        out_specs=pl.BlockSpec((TT_OUT, D_MODEL), row),
        out_shape=jax.ShapeDtypeStruct((bt, D_MODEL), f32),
        compiler_params=pltpu.CompilerParams(dimension_semantics=("arbitrary",),
                                             vmem_limit_bytes=VMEM_LIMIT),
        name="outproj",
    )(ymix, ynsa, x2, w_out, ln_g, ln_b)


def _bucket_np(d):
    d = np.asarray(d)
    max_exact = N_BUCKETS // 2
    nf = np.maximum(d, 1).astype(np.float32)
    large = max_exact + (np.log(nf / np.float32(max_exact)) / np.float32(math.log(MAX_DISTANCE / max_exact))
                         * np.float32(N_BUCKETS - max_exact)).astype(np.int32)
    large = np.minimum(large, N_BUCKETS - 1)
    return np.where(d < max_exact, d, large)


_FAR_DIST = 113
assert _bucket_np(np.arange(_FAR_DIST, 4 * WINDOW)).min() == N_BUCKETS - 1

assert CMP_STRIDE * (CMP_BACK + 1) - (CMP_BLOCK - 1) >= _FAR_DIST


def _bias_tables(rel_bias):
    nd = 2 * TQ
    onehot = jnp.asarray(np.eye(N_BUCKETS, dtype=np.float32)[_bucket_np(np.arange(nd))])
    tabp = ((jnp.dot(onehot, rel_bias, precision=lax.Precision.HIGHEST)
             - rel_bias[N_BUCKETS - 1:N_BUCKETS, :]) * LOG2E).T
    sl = np.arange(TQ)[:, None]
    tl = np.arange(TQ)[None, :]
    neg = lambda n: jnp.full((N_HEADS, n), NEG, f32)

    def skew(u, rows):
        period = u.shape[1]
        return jnp.tile(u, (1, rows))[:, :rows * (period - 1)].reshape(N_HEADS, rows, period - 1)

    def tile_layout(t):
        return t.transpose(1, 0, 2).reshape(TQ, R)

    diag = tile_layout(skew(jnp.concatenate([tabp[:, 0:TQ], neg(TQ)], axis=1), TQ)[:, :, :TQ])
    prev = tile_layout(skew(jnp.concatenate([tabp[:, TQ:nd], tabp[:, 0:TQ]], axis=1), TQ)[:, :, :TQ])
    edge = jnp.asarray(np.tile(np.where(sl > tl, 0.0, NEG).astype(np.float32), (1, N_HEADS)))
    tabs = jnp.concatenate([edge, prev, diag], axis=0).astype(f32)

    j = np.arange(CMP_SLOTS)[None, :]
    off = CMP_BACK * CMP_STRIDE - (CMP_BLOCK - 1)
    dc = np.arange(TQ)[:, None] - CMP_STRIDE * j + off
    okc = jnp.asarray(dc >= 0)[None]
    period = 4 * TQ
    u = jnp.concatenate([tabp[:, off:nd], jnp.zeros((N_HEADS, off), f32), neg(period - nd - off), tabp[:, 0:off]], axis=1)
    vc = skew(u, nd)[:, 0:CMP_SLOTS * CMP_STRIDE:CMP_STRIDE, 0:TQ].transpose(0, 2, 1)
    vc = jnp.where(okc, vc, NEG)
    hi = vc.astype(bf16)
    lo = jnp.where(okc, vc - hi.astype(f32), 0.0).astype(bf16)
    fut = jnp.full((N_HEADS, TQ, 1), NEG, bf16)
    pad = jnp.zeros((N_HEADS, TQ, LANES - _PAD_FLAG_COL - 1), bf16)
    acmp = jnp.concatenate([hi, lo, fut, pad], axis=-1).reshape(R, LANES).T
    return tabs, acmp


def _static_tables(T):
    nc = T // CMP_STRIDE
    ns = T // SLC_BLOCK
    cstart = np.arange(nc)[None, :] * CMP_STRIDE
    sstart = np.arange(NSB)[:, None] * SLC_BLOCK
    ov = np.clip(np.minimum(cstart + CMP_BLOCK, sstart + SLC_BLOCK) - np.maximum(cstart, sstart), 0, None) / CMP_STRIDE
    ov[ns:, :] = 0
    ov[:, nc - 1] = 0
    ov = np.concatenate([ov, np.ones((SUM_ROWS, nc))], axis=0)
    e_pad = np.concatenate([np.ones((KPAD, NSB), bool),
                            np.arange(T)[:, None] // SLC_BLOCK == np.arange(NSB)[None, :]], axis=0)
    return jnp.asarray(ov, bf16), jnp.asarray(e_pad, bf16)


def _pair_order(w, axis):
    shp = w.shape
    w = w.reshape(shp[:axis] + (N_KV, GROUP, HEAD_DIM) + shp[axis + 1:])
    return jnp.swapaxes(w, axis, axis + 1).reshape(shp)


def _prep_weights(w_in, w_out, pool_w, pe_k, w1_k, w2_k, pe_v, w1_v, w2_v):
    depth = w_in.shape[0]
    sizes = (D_POOL, D_POOL, D_CONV, D_CONV, D_CONV, D_CONV, D_NSA, D_KV, D_KV, D_KV, D_KV, D_KV, D_KV,
             3 * N_HEADS, D_NSA)
    offs = np.cumsum((0,) + sizes)
    col = lambda i: w_in[:, :, offs[i]:offs[i + 1]]
    wg = jnp.pad(col(13), ((0, 0), (0, 0), (0, LANES - 3 * N_HEADS)))
    w_all = jnp.concatenate([w_in[:, :, 0:MIX_W], _pair_order(col(6), 2), _pair_order(col(14), 2),
                             col(7), col(8), col(9), col(11), wg], axis=2).astype(bf16)
    wvt = jnp.swapaxes(jnp.concatenate([col(10), col(12)], axis=2), 1, 2).astype(bf16)
    nm = D_POOL + D_CONV
    w_out_p = jnp.concatenate([w_out[:, 0:nm], _pair_order(w_out[:, nm:], 1)], axis=1).astype(bf16)

    eye_g = jnp.eye(len(POOL_WINDOWS), dtype=bf16)
    pw_bd = jnp.einsum('zgcd,gh->zgchd', pool_w.astype(bf16), eye_g).reshape(depth, D_POOL, D_POOL)

    half = CMP_BLOCK // 2
    eye2 = jnp.eye(2, dtype=bf16)
    rw = half * D_KV
    w1_kv = []
    for w1 in (w1_k, w1_v):
        cols = []
        for a in range(2):
            wsel = w1.astype(bf16).reshape(depth, 2, half, 1, HEAD_DIM, CMP_HIDDEN)[:, a]
            for g in range(N_KV):
                blk = jnp.pad(wsel, ((0, 0), (0, 0), (g, N_KV - 1 - g), (0, 0), (0, 0)))
                cols.append(blk.reshape(depth, rw, CMP_HIDDEN))
        w1_kv.append(jnp.concatenate(cols, axis=2))
    w1_big = jnp.stack(w1_kv, axis=1)
    pes = jnp.stack([pe_k, pe_v], axis=1).astype(bf16).reshape(depth, 2, 2, half, 1, HEAD_DIM)
    pe_r = jnp.broadcast_to(pes, (depth, 2, 2, half, N_KV, HEAD_DIM)).reshape(depth, 2, 2, rw)
    pe_r = jnp.pad(pe_r, ((0, 0), (0, 0), (0, 6), (0, 0)))
    w2s = jnp.stack([w2_k, w2_v], axis=1).astype(bf16)
    w2_big = jnp.einsum('zkhd,kK,gG->zkghKGd', w2s, eye2, eye2).reshape(depth, CH, 2 * D_KV)
    return w_all, wvt, w_out_p, pw_bd, w1_big, pe_r, w2_big, jnp.swapaxes(w2_big, 1, 2)


def kernel(x, w_in, w_out, pool_w, pool_scale, conv_w, cmp_pe_k, cmp_w1_k, cmp_w2_k, cmp_pe_v, cmp_w1_v, cmp_w2_v,
           rel_bias, ln_g, ln_b):
    B, T, D = x.shape
    depth = w_in.shape[0]
    assert D == D_MODEL and T % TT == 0 and T // SLC_BLOCK <= NSB and T // SLC_BLOCK >= N_SELECT
    assert KPAD == TT and (B * T) % TT_OUT == 0 and N_SELECT > N_LOCAL
    alpha = (2 * depth) ** 0.25
    tabs, acmp = _bias_tables(rel_bias)
    ovT, e_pad = _static_tables(T)
    w_all, wvt, w_out_p, pw_bd, w1_big, pe_r, w2_big, w2t_big = _prep_weights(
        w_in, w_out, pool_w, cmp_pe_k, cmp_w1_k, cmp_w2_k, cmp_pe_v, cmp_w1_v, cmp_w2_v)
    pool_scale = pool_scale.reshape(depth, 1, D_POOL)
    ln_g = ln_g.reshape(depth, 1, D_MODEL)
    ln_b = ln_b.reshape(depth, 1, D_MODEL)
    h = x.reshape(B * T, D)
    for l in range(depth):
        ymix, q, z, cmp_l, g, ks, kw, vsT, vwT = _inproj(h, w_all, wvt, pw_bd, pool_scale, conv_w, l, B, T)
        cmp, cmpT = _compress(cmp_l, pe_r, w1_big, w2_big, w2t_big, l, B)
        ynsa = _nsa(q, z, g, cmp, cmpT, ks, kw, vsT, vwT, e_pad, ovT, tabs, acmp, B, T)
        h = _outproj(ymix, ynsa, h, w_out_p, ln_g, ln_b, alpha, l)
    return h.reshape(B, T, D)
```

```python
import functools
import math

import numpy as np
import jax
import jax.numpy as jnp
from jax import lax
from jax.experimental import pallas as pl
from jax.experimental.pallas import tpu as pltpu

f32 = jnp.float32
bf16 = jnp.bfloat16

D_MODEL = 1024
D_POOL = 256
D_CONV = 256
D_NSA = 512
HEAD_DIM = 64
N_HEADS = 8
N_KV = 2
GROUP = 4
D_KV = 128
POOL_GROUP = 64
POOL_WINDOWS = (2, 4, 8, 16)
CONV_WIDTH = 3
CMP_BLOCK = 32
CMP_STRIDE = 16
CMP_HIDDEN = 128
SLC_BLOCK = 64
N_SELECT = 16
N_LOCAL = 2
WINDOW = 512
N_BUCKETS = 32
MAX_DISTANCE = 128
LN_EPS = 1e-5
NEG = -1e30

LANES = 128
TQ = 128
R = N_HEADS * TQ
NSB = 128
TT = 512
TT_OUT = 1024
UNROLL = 8
N_PICKS = N_SELECT - N_LOCAL - 1
NONCAUSAL_STEP = 1e24
N_CLASSES = 4
HALO = 16
KT = 512
KPAD = KT
N_WIN = WINDOW + TQ
SUM_ROWS = 16
VROWS = HEAD_DIM + SUM_ROWS
GC = R // N_KV
LOG2E = math.log2(math.e)
CMP_BACK = 8
CMP_SLOTS = TQ // CMP_STRIDE + CMP_BACK
_PAD_FLAG_COL = 2 * CMP_SLOTS
MIX_W = 2 * D_POOL + 4 * D_CONV
VMEM_LIMIT = 56 * 1024 * 1024

_NT = (((1,), (1,)), ((), ()))


def _dot(a, b):
    return jnp.dot(a, b, preferred_element_type=f32)


def _dot_nt(a, b):
    return lax.dot_general(a, b, _NT, preferred_element_type=f32)


def _sigmoid(x):
    return 1.0 / (1.0 + jnp.exp(-x))


_C_MIX = (0, MIX_W)
_C_Q = (_C_MIX[1], _C_MIX[1] + D_NSA)
_C_Z = (_C_Q[1], _C_Q[1] + D_NSA)
_C_CMP = (_C_Z[1], _C_Z[1] + 2 * D_KV)
_C_K = (_C_CMP[1], _C_CMP[1] + 2 * D_KV)
_C_G = (_C_K[1], _C_K[1] + LANES)
W_ALL = _C_G[1]
CMP_ROWS = TT // CMP_STRIDE


def _local_mixers(ext, i, pw_ref, ps_ref, cw_ref):
    e = ext[:, 0:D_POOL]
    s2 = e + pltpu.roll(e, 1, axis=0)
    s4 = s2 + pltpu.roll(s2, 2, axis=0)
    s8 = s4 + pltpu.roll(s4, 4, axis=0)
    s16 = s8 + pltpu.roll(s8, 8, axis=0)
    lane = lax.broadcasted_iota(jnp.int32, (TT, D_POOL), 1)
    grp = lane // POOL_GROUP
    wsum = jnp.where(grp == 0, s2[HALO:], jnp.where(grp == 1, s4[HALO:], jnp.where(grp == 2, s8[HALO:], s16[HALO:])))
    win = jnp.left_shift(2, grp)
    pos = i * TT + lax.broadcasted_iota(jnp.int32, (TT, D_POOL), 0)
    cnt = jnp.minimum(pos + 1, win).astype(f32)
    v = e[HALO:]
    pooled = wsum / cnt - v
    y_pool = _dot(pooled.astype(bf16), pw_ref[...]) * ps_ref[...]
    zp = ext[HALO:, D_POOL:2 * D_POOL]
    y_pool = y_pool * (zp * _sigmoid(zp))

    o = 2 * D_POOL
    cb = ext[HALO:, o:o + D_CONV]
    u = ext[:, o + D_CONV:o + 2 * D_CONV] * ext[:, o + 2 * D_CONV:o + 3 * D_CONV]
    zc = ext[HALO:, o + 3 * D_CONV:o + 4 * D_CONV]
    conv = cw_ref[CONV_WIDTH - 1:CONV_WIDTH, :] * u[HALO:]
    for k in range(CONV_WIDTH - 1):
        conv = conv + cw_ref[k:k + 1, :] * pltpu.roll(u, CONV_WIDTH - 1 - k, axis=0)[HALO:]
    y_conv = cb * conv * (zc * _sigmoid(zc))
    return y_pool, y_conv


def _inproj_kernel(x_ref, w_ref, wvt_ref, pw_ref, ps_ref, cw_ref,
                   ymix_ref, q_ref, z_ref, cmp_ref, g_ref, ks_ref, kw_ref, vsT_ref, vwT_ref,
                   halo_ref, cscr_ref):
    step = pl.program_id(1)
    ones_rows = jnp.ones((SUM_ROWS, TT), bf16)

    @pl.when(step == 0)
    def _():
        ks_ref[...] = jnp.zeros((TT, D_KV), bf16)
        kw_ref[:, 0:D_KV] = jnp.zeros((TT, D_KV), bf16)
        lane = lax.broadcasted_iota(jnp.int32, (TT, D_KV), 1)
        kw_ref[:, D_KV:2 * D_KV] = jnp.where(lane == _PAD_FLAG_COL, 1.0, 0.0).astype(bf16)
        for g in range(N_KV):
            for ref in (vsT_ref, vwT_ref):
                ref[g, 0:HEAD_DIM] = jnp.zeros((HEAD_DIM, TT), bf16)
                ref[g, HEAD_DIM:VROWS] = ones_rows

    @pl.when(step > 0)
    def _():
        i = step - 1
        x = x_ref[...].astype(bf16)

        def proj(c):
            return _dot(x, w_ref[:, c[0]:c[1]])

        mix = proj(_C_MIX)
        halo = jnp.where(i > 0, halo_ref[...], 0.0)
        y_pool, y_conv = _local_mixers(jnp.concatenate([halo, mix], axis=0), i, pw_ref, ps_ref, cw_ref)
        halo_ref[...] = mix[TT - HALO:]
        ymix_ref[:, 0:D_POOL] = y_pool.astype(bf16)
        ymix_ref[:, D_POOL:D_POOL + D_CONV] = y_conv.astype(bf16)

        q_ref[...] = (proj(_C_Q) * (HEAD_DIM ** -0.5 * LOG2E)).astype(bf16)
        z_ref[...] = proj(_C_Z).astype(bf16)
        g_ref[...] = proj(_C_G)
        kk = proj(_C_K).astype(bf16)
        ks_ref[...] = kk[:, 0:D_KV]
        kw_ref[:, 0:D_KV] = kk[:, D_KV:2 * D_KV]
        kw_ref[:, D_KV:2 * D_KV] = jnp.zeros((TT, D_KV), bf16)
        vt = _dot_nt(wvt_ref[...], x)
        for g in range(N_KV):
            vsT_ref[g, 0:HEAD_DIM] = vt[g * HEAD_DIM:(g + 1) * HEAD_DIM].astype(bf16)
            vwT_ref[g, 0:HEAD_DIM] = vt[D_KV + g * HEAD_DIM:D_KV + (g + 1) * HEAD_DIM].astype(bf16)
            vsT_ref[g, HEAD_DIM:VROWS] = ones_rows
            vwT_ref[g, HEAD_DIM:VROWS] = ones_rows

        kvc = proj(_C_CMP)
        for c in range(2):
            cscr_ref[c] = kvc[:, c * LANES:(c + 1) * LANES]
            for l in range(CMP_STRIDE):
                cmp_ref[l, :, c * LANES:(c + 1) * LANES] = (
                    cscr_ref[c, pl.ds(l, CMP_ROWS, stride=CMP_STRIDE), :].astype(bf16))


def _inproj(x2, w_all, wvt, pw_bd, pool_scale, conv_w, layer, B, T):
    nt = T // TT
    bt = B * T
    row = lambda b, s: (b * nt + jnp.maximum(s - 1, 0), 0)
    wsel = lambda b, s: (layer, 0, 0)
    tp = T + KPAD
    out_shape = [jax.ShapeDtypeStruct((bt, D_POOL + D_CONV), bf16),
                 jax.ShapeDtypeStruct((bt, D_NSA), bf16),
                 jax.ShapeDtypeStruct((bt, D_NSA), bf16),
                 jax.ShapeDtypeStruct((CMP_STRIDE, bt // CMP_STRIDE, 2 * D_KV), bf16),
                 jax.ShapeDtypeStruct((bt, LANES), f32),
                 jax.ShapeDtypeStruct((B, tp, D_KV), bf16),
                 jax.ShapeDtypeStruct((B, tp, 2 * D_KV), bf16),
                 jax.ShapeDtypeStruct((B, N_KV, VROWS, tp), bf16),
                 jax.ShapeDtypeStruct((B, N_KV, VROWS, tp), bf16)]
    out_specs = [pl.BlockSpec((TT, D_POOL + D_CONV), row),
                 pl.BlockSpec((TT, D_NSA), row),
                 pl.BlockSpec((TT, D_NSA), row),
                 pl.BlockSpec((CMP_STRIDE, CMP_ROWS, 2 * D_KV), lambda b, s: (0, b * nt + jnp.maximum(s - 1, 0), 0)),
                 pl.BlockSpec((TT, LANES), row),
                 pl.BlockSpec((None, TT, D_KV), lambda b, s: (b, s, 0)),
                 pl.BlockSpec((None, TT, 2 * D_KV), lambda b, s: (b, s, 0)),
                 pl.BlockSpec((None, N_KV, VROWS, TT), lambda b, s: (b, 0, 0, s)),
                 pl.BlockSpec((None, N_KV, VROWS, TT), lambda b, s: (b, 0, 0, s))]
    return pl.pallas_call(
        _inproj_kernel,
        grid=(B, nt + 1),
        in_specs=[pl.BlockSpec((TT, D_MODEL), row),
                  pl.BlockSpec((None, D_MODEL, W_ALL), wsel),
                  pl.BlockSpec((None, 2 * D_KV, D_MODEL), wsel),
                  pl.BlockSpec((None, D_POOL, D_POOL), wsel),
                  pl.BlockSpec((None, 1, D_POOL), wsel),
                  pl.BlockSpec((None, CONV_WIDTH, D_CONV), wsel)],
        out_specs=out_specs,
        out_shape=out_shape,
        scratch_shapes=[pltpu.VMEM((HALO, MIX_W), f32), pltpu.VMEM((2, TT, LANES), f32)],
        compiler_params=pltpu.CompilerParams(dimension_semantics=("arbitrary", "arbitrary"),
                                             vmem_limit_bytes=VMEM_LIMIT),
        name="inproj",
    )(x2, w_all, wvt, pw_bd, pool_scale, conv_w)


CH = 4 * CMP_HIDDEN


def _compress_kernel(r_ref, pe_ref, w1_ref, w2_ref, w2t_ref, out_ref, outT_ref):
    nc = r_ref.shape[1]
    hk = CH // 2
    hs = []
    for kv in range(2):
        rows = jnp.concatenate([r_ref[l][:, kv * D_KV:(kv + 1) * D_KV] for l in range(CMP_STRIDE)], axis=1)
        zz = _dot(rows, w1_ref[kv])
        pb = _dot(pe_ref[kv], w1_ref[kv])
        z0 = zz[:, 0:hk] + pb[0:1, 0:hk]
        z1 = zz[:, hk:2 * hk] + pb[1:2, hk:2 * hk]
        hs.append(z0 + pltpu.roll(z1, nc - 1, axis=0))
    h = jnp.concatenate(hs, axis=1)
    h = (h * _sigmoid(h)).astype(bf16)
    out_ref[...] = _dot(h, w2_ref[...]).astype(bf16)
    outT_ref[...] = _dot_nt(w2t_ref[...], h).astype(bf16)


def _compress(cmp_l, pe_r, w1_big, w2_big, w2t_big, layer, B):
    nc = cmp_l.shape[1] // B
    rw = CMP_STRIDE * D_KV
    wsel = lambda b: (layer, 0, 0)
    wsel4 = lambda b: (layer, 0, 0, 0)
    return pl.pallas_call(
        _compress_kernel,
        grid=(B,),
        in_specs=[pl.BlockSpec((CMP_STRIDE, nc, 2 * D_KV), lambda b: (0, b, 0)),
                  pl.BlockSpec((None, 2, 8, rw), wsel4),
                  pl.BlockSpec((None, 2, rw, CH), wsel4),
                  pl.BlockSpec((None, CH, 2 * D_KV), wsel),
                  pl.BlockSpec((None, 2 * D_KV, CH), wsel)],
        out_specs=[pl.BlockSpec((None, nc, 2 * D_KV), lambda b: (b, 0, 0)),
                   pl.BlockSpec((None, 2 * D_KV, nc), lambda b: (b, 0, 0))],
        out_shape=[jax.ShapeDtypeStruct((B, nc, 2 * D_KV), bf16),
                   jax.ShapeDtypeStruct((B, 2 * D_KV, nc), bf16)],
        compiler_params=pltpu.CompilerParams(dimension_semantics=("arbitrary",),
                                             vmem_limit_bytes=VMEM_LIMIT),
        name="compress",
    )(cmp_l, pe_r, w1_big, w2_big, w2t_big)


def _selection_scores(imp_t, t0):
    shape = imp_t.shape
    jblk = lax.broadcasted_iota(jnp.int32, shape, 0)
    tq = lax.broadcasted_iota(jnp.int32, shape, 1) % TQ
    back = jnp.right_shift(t0 + tq, int(math.log2(SLC_BLOCK))) - jblk
    causal = back >= 0
    forced = jnp.logical_or(jblk == 0, jnp.logical_and(causal, back < N_LOCAL))
    jf = jblk.astype(f32)
    score = jnp.where(forced, -jnp.inf, jnp.where(causal, imp_t, NEG - jf * NONCAUSAL_STEP))
    tblk = jnp.right_shift(t0 + lax.broadcasted_iota(jnp.int32, (1, shape[1]), 1) % TQ, int(math.log2(SLC_BLOCK)))
    return score, (jnp.minimum(tblk, N_LOCAL) + 1).astype(f32)


def _pick_exact(score):
    jf = lax.broadcasted_iota(jnp.int32, score.shape, 0).astype(f32)
    for _ in range(N_PICKS):
        best = jnp.max(score, axis=0, keepdims=True)
        first = jnp.min(jnp.where(score == best, jf, float(score.shape[0])), axis=0, keepdims=True)
        score = jnp.where(jf == first, -jnp.inf, score)
    return score


def _write_selection(imp_t, t0, nsel, pen_ref):
    score0, n_forced = _selection_scores(imp_t, t0)
    score = score0
    for _ in range(N_PICKS):
        score = jnp.where(score == jnp.max(score, axis=0, keepdims=True), -jnp.inf, score)
    taken = score == -jnp.inf
    pen_ref[0:nsel] = jnp.where(taken, 0.0, NEG).astype(bf16)
    if nsel < NSB:
        pen_ref[nsel:NSB] = jnp.full((NSB - nsel, N_KV * TQ), NEG, bf16)
    n_taken = jnp.sum(jnp.where(taken, 1.0, 0.0), axis=0, keepdims=True)
    tied = jnp.max(jnp.abs(n_taken - n_forced - N_PICKS))

    @pl.when(tied > 0.0)
    def _():
        pen_ref[0:nsel] = jnp.where(_pick_exact(score0) == -jnp.inf, 0.0, NEG).astype(bf16)


def _query_cols(q_ref):
    sub = lax.broadcasted_iota(jnp.int32, (LANES, TQ), 0)
    low = sub < HEAD_DIM
    pairs = [q_ref[:, LANES * p:LANES * (p + 1)].astype(f32).T for p in range(GROUP)]
    blocks = [(jnp.where(low, pairs[h % GROUP], 0.0) if h < GROUP else jnp.where(low, 0.0, pairs[h % GROUP])).astype(bf16)
              for h in range(N_HEADS)]
    return jnp.concatenate(blocks, axis=1)


def _merge_head_pairs(outs):
    return [jnp.concatenate([outs[p], outs[p + GROUP]], axis=0).T for p in range(GROUP)]


def _cmp_scores(q_cmp, cmp_ref, t0, nc):
    n_i = lax.broadcasted_iota(jnp.int32, (nc, LANES), 0)
    j_i = lax.broadcasted_iota(jnp.int32, (nc, LANES), 1)
    nstart = t0 // CMP_STRIDE - CMP_BACK
    in_window = jnp.logical_and(j_i < 2 * CMP_SLOTS, n_i == nstart + jnp.where(j_i < CMP_SLOTS, j_i, j_i - CMP_SLOTS))
    future = jnp.logical_and(j_i == _PAD_FLAG_COL, n_i >= nstart + CMP_SLOTS - 1)
    place_b = jnp.where(jnp.logical_or(in_window, future), 1.0, 0.0).astype(bf16)
    kc_ext = jnp.concatenate([cmp_ref[0:nc, 0:D_KV], place_b], axis=1)
    return _dot(kc_ext, q_cmp)


def _cmp_finish(s1, sig_t, cmpT_ref, ovT_ref, y1_ref, t0, nsel):
    nc = s1.shape[0]
    m1 = jnp.max(s1, axis=0, keepdims=True)
    p1 = jnp.exp2(s1 - m1)
    p1 = p1.astype(bf16)
    tcol = t0 + lax.broadcasted_iota(jnp.int32, (1, GC), 1) % TQ
    outs, sums = [], []
    for g in range(N_KV):
        lhs = jnp.concatenate([cmpT_ref[D_KV + g * HEAD_DIM:D_KV + (g + 1) * HEAD_DIM, 0:nc], ovT_ref[0:nsel, 0:nc],
                               ovT_ref[NSB:NSB + SUM_ROWS, 0:nc]], axis=0)
        both = _dot(lhs, p1[:, g * GC:(g + 1) * GC])
        l1 = both[HEAD_DIM + nsel:HEAD_DIM + nsel + 1]
        both = both * jnp.where(tcol >= CMP_BLOCK - 1, 1.0 / l1, 0.0)
        for r in range(GROUP):
            h = g * GROUP + r
            outs.append(sig_t[3 * h:3 * h + 1, :] * both[0:HEAD_DIM, r * TQ:(r + 1) * TQ])
        imp = both[HEAD_DIM:HEAD_DIM + nsel]
        acc = imp[:, 0:TQ]
        for r in range(1, GROUP):
            acc = acc + imp[:, r * TQ:(r + 1) * TQ]
        sums.append(acc)
    for p, y in enumerate(_merge_head_pairs(outs)):
        y1_ref[:, p * LANES:(p + 1) * LANES] = y.astype(bf16)
    return jnp.concatenate(sums, axis=1)


def _nsa_kernel(q0_ref, qn_ref, z_ref, g0_ref, gn_ref, cmp_ref, cmpT_ref, ovT_ref, ks_ref, kw_ref, vsT_ref, vwT_ref,
                e_ref, tab_ref, acmp_ref, out_ref,
                qc_ref, sg_ref, pen_ref, y1_ref, o3_ref, m_ref, l_ref, acc_ref, sa_ref, sb_ref, mxa_ref, mxb_ref):
    qi = pl.program_id(1)
    nq = pl.num_programs(1)
    t0 = qi * TQ
    nc = cmp_ref.shape[0]
    classes = [(nc * k // N_CLASSES, NSB * k // N_CLASSES) for k in range(1, N_CLASSES + 1)]
    last_q = [min(c // (TQ // CMP_STRIDE), s // (TQ // SLC_BLOCK)) - 1 for c, s in classes]

    @pl.when(qi == 0)
    def _():
        qc_ref[...] = _query_cols(q0_ref)
        sg_ref[...] = _sigmoid(g0_ref[...]).T
        q_cmp0 = jnp.concatenate([qc_ref[...], acmp_ref[...]], axis=0)
        imp0 = _cmp_finish(_cmp_scores(q_cmp0, cmp_ref, t0, classes[0][0]), sg_ref[...], cmpT_ref, ovT_ref, y1_ref, t0,
                           classes[0][1])
        _write_selection(imp0, t0, classes[0][1], pen_ref)

    qcols = qc_ref[...]
    sig_cur = sg_ref[...]
    q_cmp = jnp.concatenate([qcols, acmp_ref[...]], axis=0)
    pens = [pen_ref[:, g * TQ:(g + 1) * TQ] for g in range(N_KV) for _ in range(GROUP)]
    q_slc = jnp.concatenate([qcols, jnp.concatenate(pens, axis=1)], axis=0)
    y1 = y1_ref[...]

    m_ref[...] = jnp.full((1, R), NEG, f32)
    l_ref[...] = jnp.zeros((1, R), f32)
    acc_ref[...] = jnp.zeros((HEAD_DIM, R), f32)
    n_far = (qi + KT // TQ) // (KT // TQ) - 1
    first = (qi + 1) * TQ - n_far * KT

    def tile_row(i):
        return pl.multiple_of(first + i * KT, LANES)

    def produce(i, s_ref, mx_ref):
        r0 = tile_row(i)
        k_ext = jnp.concatenate([ks_ref[pl.ds(r0, KT), :], e_ref[pl.ds(r0, KT), :]], axis=1)
        s = _dot(k_ext, q_slc)
        s_ref[...] = s
        mx_ref[...] = jnp.max(s, axis=0, keepdims=True)

    def consume(i, s_ref, mx_ref, last):
        m_prev = m_ref[...]
        if last:
            s = jnp.concatenate([s_ref[0:KT - 2 * TQ], s_ref[KT - 2 * TQ:KT] + tab_ref[TQ:3 * TQ]], axis=0)
            m_new = jnp.maximum(m_prev, jnp.max(s, axis=0, keepdims=True))
        else:
            s = s_ref[...]
            m_new = jnp.maximum(m_prev, mx_ref[...])
        alpha = jnp.exp2(m_prev - m_new)
        p = jnp.exp2(s - m_new)
        p = p.astype(bf16)
        for g in range(N_KV):
            c = slice(g * GC, (g + 1) * GC)
            pv = _dot(vsT_ref[g, :, pl.ds(tile_row(i), KT)], p[:, c])
            l_ref[:, c] = alpha[:, c] * l_ref[:, c] + pv[HEAD_DIM:HEAD_DIM + 1]
            acc_ref[:, c] = acc_ref[:, c] * alpha[:, c] + pv[0:HEAD_DIM]
        m_ref[...] = m_new

    qn = jnp.minimum(qi + 1, nq - 1)

    def front(nc_use, nsel_use):
        tn = qn * TQ
        qc_next = _query_cols(qn_ref)
        sig_next = _sigmoid(gn_ref[...]).T
        qc_ref[...] = qc_next
        sg_ref[...] = sig_next
        s1 = _cmp_scores(jnp.concatenate([qc_next, acmp_ref[...]], axis=0), cmp_ref, tn, nc_use)
        w0 = pl.multiple_of(t0, LANES)
        s3 = _dot(kw_ref[pl.ds(w0, N_WIN), :], q_cmp)
        imp_next = _cmp_finish(s1, sig_next, cmpT_ref, ovT_ref, y1_ref, tn, nsel_use)
        produce(0, sa_ref, mxa_ref)
        s3 = jnp.concatenate([s3[0:TQ] + tab_ref[0:TQ], s3[TQ:WINDOW - TQ], s3[WINDOW - TQ:] + tab_ref[TQ:3 * TQ]],
                             axis=0)
        p3 = jnp.exp2(s3 - jnp.max(s3, axis=0, keepdims=True)).astype(bf16)
        for g in range(N_KV):
            c = slice(g * GC, (g + 1) * GC)
            o3_ref[:, c] = _dot(vwT_ref[g, :, pl.ds(w0, N_WIN)], p3[:, c])
        _write_selection(imp_next, tn, nsel_use, pen_ref)

    for k, (nc_k, nsel_k) in enumerate(classes):
        lo = last_q[k - 1] if k else -1
        if k < N_CLASSES - 1:
            pl.when(jnp.logical_and(qn > lo, qn <= last_q[k]))(functools.partial(front, nc_k, nsel_k))
        else:
            pl.when(qn > lo)(functools.partial(front, nc_k, nsel_k))

    bufs = ((sa_ref, mxa_ref), (sb_ref, mxb_ref))

    def pipeline(base, count, last_at_end):
        for k in range(count):
            if k + 1 < count or not last_at_end:
                produce(base + k + 1, *bufs[(k + 1) % 2])
            consume(base + k, *bufs[k % 2], last_at_end and k == count - 1)

    def unrolled(j, c):
        pipeline(UNROLL * j, UNROLL, False)
        return c

    lax.fori_loop(0, n_far // UNROLL, unrolled, 0)

    def finish():
        inv2 = 1.0 / l_ref[...]
        inv3 = 1.0 / o3_ref[HEAD_DIM:HEAD_DIM + 1, :]
        sig_t = sig_cur
        outs = []
        for h in range(N_HEADS):
            c = slice(h * TQ, (h + 1) * TQ)
            g2 = sig_t[3 * h + 1:3 * h + 2, :] * inv2[:, c]
            g3 = sig_t[3 * h + 2:3 * h + 3, :] * inv3[:, c]
            outs.append(g2 * acc_ref[:, c] + g3 * o3_ref[0:HEAD_DIM, c])
        for p, y in enumerate(_merge_head_pairs(outs)):
            c = slice(p * LANES, (p + 1) * LANES)
            zb = z_ref[:, c].astype(f32)
            out_ref[:, c] = ((y + y1[:, c].astype(f32)) * zb * _sigmoid(zb)).astype(bf16)

    for r in range(UNROLL):
        def drain(r=r):
            pipeline(n_far - r, r + 1, True)
            finish()
        pl.when(n_far % UNROLL == r)(drain)


def _nsa(q, z, g, cmp, cmpT, ks, kw, vsT, vwT, e_pad, ovT, tabs, acmp, B, T):
    nq = T // TQ
    nc = T // CMP_STRIDE
    tp = T + KPAD
    tile = lambda b, i: (b * nq + i, 0)
    nxt = lambda b, i: (b * nq + jnp.minimum(i + 1, nq - 1), 0)
    head = lambda b, i: (b * nq, 0)
    const2 = lambda b, i: (0, 0)
    batch3 = lambda b, i: (b, 0, 0)
    return pl.pallas_call(
        _nsa_kernel,
        grid=(B, nq),
        in_specs=[pl.BlockSpec((TQ, D_NSA), head),
                  pl.BlockSpec((TQ, D_NSA), nxt),
                  pl.BlockSpec((TQ, D_NSA), tile),
                  pl.BlockSpec((TQ, LANES), head),
                  pl.BlockSpec((TQ, LANES), nxt),
                  pl.BlockSpec((None, nc, 2 * D_KV), batch3),
                  pl.BlockSpec((None, 2 * D_KV, nc), batch3),
                  pl.BlockSpec((NSB + SUM_ROWS, nc), const2),
                  pl.BlockSpec((None, tp, D_KV), batch3),
                  pl.BlockSpec((None, tp, 2 * D_KV), batch3),
                  pl.BlockSpec((None, N_KV, VROWS, tp), lambda b, i: (b, 0, 0, 0)),
                  pl.BlockSpec((None, N_KV, VROWS, tp), lambda b, i: (b, 0, 0, 0)),
                  pl.BlockSpec((tp, NSB), const2),
                  pl.BlockSpec((3 * TQ, R), const2),
                  pl.BlockSpec((LANES, R), const2)],
        out_specs=pl.BlockSpec((TQ, D_NSA), tile),
        out_shape=jax.ShapeDtypeStruct((B * T, D_NSA), bf16),
        scratch_shapes=[pltpu.VMEM((LANES, R), bf16), pltpu.VMEM((LANES, TQ), f32),
                        pltpu.VMEM((NSB, N_KV * TQ), bf16), pltpu.VMEM((TQ, D_NSA), bf16), pltpu.VMEM((VROWS, R), f32),
                        pltpu.VMEM((1, R), f32), pltpu.VMEM((1, R), f32), pltpu.VMEM((HEAD_DIM, R), f32),
                        pltpu.VMEM((KT, R), f32), pltpu.VMEM((KT, R), f32),
                        pltpu.VMEM((1, R), f32), pltpu.VMEM((1, R), f32)],
        compiler_params=pltpu.CompilerParams(dimension_semantics=("arbitrary", "arbitrary"),
                                             vmem_limit_bytes=VMEM_LIMIT),
        name="nsa",
    )(q, q, z, g, g, cmp, cmpT, ovT, ks, kw, vsT, vwT, e_pad, tabs, acmp)


def _outproj_kernel(alpha, ymix_ref, ynsa_ref, x_ref, w_ref, g_ref, b_ref, out_ref):
    y = jnp.concatenate([ymix_ref[...], ynsa_ref[...]], axis=1)
    r = alpha * x_ref[...] + _dot(y, w_ref[...])
    mu = jnp.mean(r, axis=-1, keepdims=True)
    d = r - mu
    var = jnp.mean(d * d, axis=-1, keepdims=True)
    out_ref[...] = d * lax.rsqrt(var + LN_EPS) * g_ref[...] + b_ref[...]


def _outproj(ymix, ynsa, x2, w_out, ln_g, ln_b, alpha, layer):
    bt = x2.shape[0]
    row = lambda i: (i, 0)
    wsel = lambda i: (layer, 0, 0)
    return pl.pallas_call(
        functools.partial(_outproj_kernel, alpha),
        grid=(bt // TT_OUT,),
        in_specs=[pl.BlockSpec((TT_OUT, D_POOL + D_CONV), row),
                  pl.BlockSpec((TT_OUT, D_NSA), row),
                  pl.BlockSpec((TT_OUT, D_MODEL), row),
                  pl.BlockSpec((None, D_MODEL, D_MODEL), wsel),
                  pl.BlockSpec((None, 1, D_MODEL), wsel),
                  pl.BlockSpec((None, 1, D_MODEL), wsel)],
        out_specs=pl.BlockSpec((TT_OUT, D_MODEL), row),
        out_shape=jax.ShapeDtypeStruct((bt, D_MODEL), f32),
        compiler_params=pltpu.CompilerParams(dimension_semantics=("arbitrary",),
                                             vmem_limit_bytes=VMEM_LIMIT),
        name="outproj",
    )(ymix, ynsa, x2, w_out, ln_g, ln_b)


def _bucket_np(d):
    d = np.asarray(d)
    max_exact = N_BUCKETS // 2
    nf = np.maximum(d, 1).astype(np.float32)
    large = max_exact + (np.log(nf / np.float32(max_exact)) / np.float32(math.log(MAX_DISTANCE / max_exact))
                         * np.float32(N_BUCKETS - max_exact)).astype(np.int32)
    large = np.minimum(large, N_BUCKETS - 1)
    return np.where(d < max_exact, d, large)


_FAR_DIST = 113
assert _bucket_np(np.arange(_FAR_DIST, 4 * WINDOW)).min() == N_BUCKETS - 1

assert CMP_STRIDE * (CMP_BACK + 1) - (CMP_BLOCK - 1) >= _FAR_DIST


def _bias_tables(rel_bias):
    nd = 2 * TQ
    onehot = jnp.asarray(np.eye(N_BUCKETS, dtype=np.float32)[_bucket_np(np.arange(nd))])
    tabp = ((jnp.dot(onehot, rel_bias, precision=lax.Precision.HIGHEST)
             - rel_bias[N_BUCKETS - 1:N_BUCKETS, :]) * LOG2E).T
    sl = np.arange(TQ)[:, None]
    tl = np.arange(TQ)[None, :]
    neg = lambda n: jnp.full((N_HEADS, n), NEG, f32)

    def skew(u, rows):
        period = u.shape[1]
        return jnp.tile(u, (1, rows))[:, :rows * (period - 1)].reshape(N_HEADS, rows, period - 1)

    def tile_layout(t):
        return t.transpose(1, 0, 2).reshape(TQ, R)

    diag = tile_layout(skew(jnp.concatenate([tabp[:, 0:TQ], neg(TQ)], axis=1), TQ)[:, :, :TQ])
    prev = tile_layout(skew(jnp.concatenate([tabp[:, TQ:nd], tabp[:, 0:TQ]], axis=1), TQ)[:, :, :TQ])
    edge = jnp.asarray(np.tile(np.where(sl > tl, 0.0, NEG).astype(np.float32), (1, N_HEADS)))
    tabs = jnp.concatenate([edge, prev, diag], axis=0).astype(f32)

    j = np.arange(CMP_SLOTS)[None, :]
    off = CMP_BACK * CMP_STRIDE - (CMP_BLOCK - 1)
    dc = np.arange(TQ)[:, None] - CMP_STRIDE * j + off
    okc = jnp.asarray(dc >= 0)[None]
    period = 4 * TQ
    u = jnp.concatenate([tabp[:, off:nd], jnp.zeros((N_HEADS, off), f32), neg(period - nd - off), tabp[:, 0:off]], axis=1)
    vc = skew(u, nd)[:, 0:CMP_SLOTS * CMP_STRIDE:CMP_STRIDE, 0:TQ].transpose(0, 2, 1)
    vc = jnp.where(okc, vc, NEG)
    hi = vc.astype(bf16)
    lo = jnp.where(okc, vc - hi.astype(f32), 0.0).astype(bf16)
    fut = jnp.full((N_HEADS, TQ, 1), NEG, bf16)
    pad = jnp.zeros((N_HEADS, TQ, LANES - _PAD_FLAG_COL - 1), bf16)
    acmp = jnp.concatenate([hi, lo, fut, pad], axis=-1).reshape(R, LANES).T
    return tabs, acmp


def _static_tables(T):
    nc = T // CMP_STRIDE
    ns = T // SLC_BLOCK
    cstart = np.arange(nc)[None, :] * CMP_STRIDE
    sstart = np.arange(NSB)[:, None] * SLC_BLOCK
    ov = np.clip(np.minimum(cstart + CMP_BLOCK, sstart + SLC_BLOCK) - np.maximum(cstart, sstart), 0, None) / CMP_STRIDE
    ov[ns:, :] = 0
    ov[:, nc - 1] = 0
    ov = np.concatenate([ov, np.ones((SUM_ROWS, nc))], axis=0)
    e_pad = np.concatenate([np.ones((KPAD, NSB), bool),
                            np.arange(T)[:, None] // SLC_BLOCK == np.arange(NSB)[None, :]], axis=0)
    return jnp.asarray(ov, bf16), jnp.asarray(e_pad, bf16)


def _pair_order(w, axis):
    shp = w.shape
    w = w.reshape(shp[:axis] + (N_KV, GROUP, HEAD_DIM) + shp[axis + 1:])
    return jnp.swapaxes(w, axis, axis + 1).reshape(shp)


def _prep_weights(w_in, w_out, pool_w, pe_k, w1_k, w2_k, pe_v, w1_v, w2_v):
    depth = w_in.shape[0]
    sizes = (D_POOL, D_POOL, D_CONV, D_CONV, D_CONV, D_CONV, D_NSA, D_KV, D_KV, D_KV, D_KV, D_KV, D_KV,
             3 * N_HEADS, D_NSA)
    offs = np.cumsum((0,) + sizes)
    col = lambda i: w_in[:, :, offs[i]:offs[i + 1]]
    wg = jnp.pad(col(13), ((0, 0), (0, 0), (0, LANES - 3 * N_HEADS)))
    w_all = jnp.concatenate([w_in[:, :, 0:MIX_W], _pair_order(col(6), 2), _pair_order(col(14), 2),
                             col(7), col(8), col(9), col(11), wg], axis=2).astype(bf16)
    wvt = jnp.swapaxes(jnp.concatenate([col(10), col(12)], axis=2), 1, 2).astype(bf16)
    nm = D_POOL + D_CONV
    w_out_p = jnp.concatenate([w_out[:, 0:nm], _pair_order(w_out[:, nm:], 1)], axis=1).astype(bf16)

    eye_g = jnp.eye(len(POOL_WINDOWS), dtype=bf16)
    pw_bd = jnp.einsum('zgcd,gh->zgchd', pool_w.astype(bf16), eye_g).reshape(depth, D_POOL, D_POOL)

    half = CMP_BLOCK // 2
    eye2 = jnp.eye(2, dtype=bf16)
    rw = half * D_KV
    w1_kv = []
    for w1 in (w1_k, w1_v):
        cols = []
        for a in range(2):
            wsel = w1.astype(bf16).reshape(depth, 2, half, 1, HEAD_DIM, CMP_HIDDEN)[:, a]
            for g in range(N_KV):
                blk = jnp.pad(wsel, ((0, 0), (0, 0), (g, N_KV - 1 - g), (0, 0), (0, 0)))
                cols.append(blk.reshape(depth, rw, CMP_HIDDEN))
        w1_kv.append(jnp.concatenate(cols, axis=2))
    w1_big = jnp.stack(w1_kv, axis=1)
    pes = jnp.stack([pe_k, pe_v], axis=1).astype(bf16).reshape(depth, 2, 2, half, 1, HEAD_DIM)
    pe_r = jnp.broadcast_to(pes, (depth, 2, 2, half, N_KV, HEAD_DIM)).reshape(depth, 2, 2, rw)
    pe_r = jnp.pad(pe_r, ((0, 0), (0, 0), (0, 6), (0, 0)))
    w2s = jnp.stack([w2_k, w2_v], axis=1).astype(bf16)
    w2_big = jnp.einsum('zkhd,kK,gG->zkghKGd', w2s, eye2, eye2).reshape(depth, CH, 2 * D_KV)
    return w_all, wvt, w_out_p, pw_bd, w1_big, pe_r, w2_big, jnp.swapaxes(w2_big, 1, 2)


def kernel(x, w_in, w_out, pool_w, pool_scale, conv_w, cmp_pe_k, cmp_w1_k, cmp_w2_k, cmp_pe_v, cmp_w1_v, cmp_w2_v,
           rel_bias, ln_g, ln_b):
    B, T, D = x.shape
    depth = w_in.shape[0]
    assert D == D_MODEL and T % TT == 0 and T // SLC_BLOCK <= NSB and T // SLC_BLOCK >= N_SELECT
    assert KPAD == TT and (B * T) % TT_OUT == 0 and N_SELECT > N_LOCAL
    alpha = (2 * depth) ** 0.25
    tabs, acmp = _bias_tables(rel_bias)
    ovT, e_pad = _static_tables(T)
    w_all, wvt, w_out_p, pw_bd, w1_big, pe_r, w2_big, w2t_big = _prep_weights(
        w_in, w_out, pool_w, cmp_pe_k, cmp_w1_k, cmp_w2_k, cmp_pe_v, cmp_w1_v, cmp_w2_v)
    pool_scale = pool_scale.reshape(depth, 1, D_POOL)
    ln_g = ln_g.reshape(depth, 1, D_MODEL)
    ln_b = ln_b.reshape(depth, 1, D_MODEL)
    h = x.reshape(B * T, D)
    for l in range(depth):
        ymix, q, z, cmp_l, g, ks, kw, vsT, vwT = _inproj(h, w_all, wvt, pw_bd, pool_scale, conv_w, l, B, T)
        cmp, cmpT = _compress(cmp_l, pe_r, w1_big, w2_big, w2t_big, l, B)
        ynsa = _nsa(q, z, g, cmp, cmpT, ks, kw, vsT, vwT, e_pad, ovT, tabs, acmp, B, T)
        h = _outproj(ymix, ynsa, h, w_out_p, ln_g, ln_b, alpha, l)
    return h.reshape(B, T, D)
```

```python
import functools
import math

import numpy as np
import jax
import jax.numpy as jnp
from jax import lax
from jax.experimental import pallas as pl
from jax.experimental.pallas import tpu as pltpu

f32 = jnp.float32
bf16 = jnp.bfloat16

D_MODEL = 1024
D_POOL = 256
D_CONV = 256
D_NSA = 512
HEAD_DIM = 64
N_HEADS = 8
N_KV = 2
GROUP = 4
D_KV = 128
POOL_GROUP = 64
POOL_WINDOWS = (2, 4, 8, 16)
CONV_WIDTH = 3
CMP_BLOCK = 32
CMP_STRIDE = 16
CMP_HIDDEN = 128
SLC_BLOCK = 64
N_SELECT = 16
N_LOCAL = 2
WINDOW = 512
N_BUCKETS = 32
MAX_DISTANCE = 128
LN_EPS = 1e-5
NEG = -1e30

LANES = 128
TQ = 128
R = N_HEADS * TQ
NSB = 128
TT = 512
TT_OUT = 1024
OUT_CHUNK = 256
UNROLL = 8
N_PICKS = N_SELECT - N_LOCAL - 1
NONCAUSAL_STEP = 1e24
N_CLASSES = 4
HALO = 16
KT = 512
KPAD = KT
N_WIN = WINDOW + TQ
SUM_ROWS = 16
VROWS = HEAD_DIM + SUM_ROWS
GC = R // N_KV
LOG2E = math.log2(math.e)
CMP_BACK = 8
CMP_SLOTS = TQ // CMP_STRIDE + CMP_BACK
_PAD_FLAG_COL = 2 * CMP_SLOTS
MIX_W = 2 * D_POOL + 4 * D_CONV
VMEM_LIMIT = 56 * 1024 * 1024

_NT = (((1,), (1,)), ((), ()))


def _dot(a, b):
    return jnp.dot(a, b, preferred_element_type=f32)


def _dot_nt(a, b):
    return lax.dot_general(a, b, _NT, preferred_element_type=f32)


def _sigmoid(x):
    return 1.0 / (1.0 + jnp.exp(-x))


_C_MIX = (0, MIX_W)
_C_Q = (_C_MIX[1], _C_MIX[1] + D_NSA)
_C_Z = (_C_Q[1], _C_Q[1] + D_NSA)
_C_CMP = (_C_Z[1], _C_Z[1] + 2 * D_KV)
_C_K = (_C_CMP[1], _C_CMP[1] + 2 * D_KV)
_C_G = (_C_K[1], _C_K[1] + LANES)
W_ALL = _C_G[1]
CMP_ROWS = TT // CMP_STRIDE


def _local_mixers(ext, i, pw_ref, ps_ref, cw_ref):
    e = ext[:, 0:D_POOL]
    s2 = e + pltpu.roll(e, 1, axis=0)
    s4 = s2 + pltpu.roll(s2, 2, axis=0)
    s8 = s4 + pltpu.roll(s4, 4, axis=0)
    s16 = s8 + pltpu.roll(s8, 8, axis=0)
    lane = lax.broadcasted_iota(jnp.int32, (TT, D_POOL), 1)
    grp = lane // POOL_GROUP
    wsum = jnp.where(grp == 0, s2[HALO:], jnp.where(grp == 1, s4[HALO:], jnp.where(grp == 2, s8[HALO:], s16[HALO:])))
    win = jnp.left_shift(2, grp)
    pos = i * TT + lax.broadcasted_iota(jnp.int32, (TT, D_POOL), 0)
    cnt = jnp.minimum(pos + 1, win).astype(f32)
    v = e[HALO:]
    pooled = wsum / cnt - v
    y_pool = _dot(pooled.astype(bf16), pw_ref[...]) * ps_ref[...]
    zp = ext[HALO:, D_POOL:2 * D_POOL]
    y_pool = y_pool * (zp * _sigmoid(zp))

    o = 2 * D_POOL
    cb = ext[HALO:, o:o + D_CONV]
    u = ext[:, o + D_CONV:o + 2 * D_CONV] * ext[:, o + 2 * D_CONV:o + 3 * D_CONV]
    zc = ext[HALO:, o + 3 * D_CONV:o + 4 * D_CONV]
    conv = cw_ref[CONV_WIDTH - 1:CONV_WIDTH, :] * u[HALO:]
    for k in range(CONV_WIDTH - 1):
        conv = conv + cw_ref[k:k + 1, :] * pltpu.roll(u, CONV_WIDTH - 1 - k, axis=0)[HALO:]
    y_conv = cb * conv * (zc * _sigmoid(zc))
    return y_pool, y_conv


def _inproj_kernel(x_ref, w_ref, wvt_ref, pw_ref, ps_ref, cw_ref,
                   ymix_ref, q_ref, z_ref, cmp_ref, g_ref, ks_ref, kw_ref, vsT_ref, vwT_ref,
                   halo_ref, cscr_ref):
    step = pl.program_id(1)
    ones_rows = jnp.ones((SUM_ROWS, TT), bf16)

    @pl.when(step == 0)
    def _():
        ks_ref[...] = jnp.zeros((TT, D_KV), bf16)
        kw_ref[:, 0:D_KV] = jnp.zeros((TT, D_KV), bf16)
        lane = lax.broadcasted_iota(jnp.int32, (TT, D_KV), 1)
        kw_ref[:, D_KV:2 * D_KV] = jnp.where(lane == _PAD_FLAG_COL, 1.0, 0.0).astype(bf16)
        for g in range(N_KV):
            for ref in (vsT_ref, vwT_ref):
                ref[g, 0:HEAD_DIM] = jnp.zeros((HEAD_DIM, TT), bf16)
                ref[g, HEAD_DIM:VROWS] = ones_rows

    @pl.when(step > 0)
    def _():
        i = step - 1
        x = x_ref[...].astype(bf16)

        def proj(c):
            return _dot(x, w_ref[:, c[0]:c[1]])

        mix = proj(_C_MIX)
        halo = jnp.where(i > 0, halo_ref[...], 0.0)
        y_pool, y_conv = _local_mixers(jnp.concatenate([halo, mix], axis=0), i, pw_ref, ps_ref, cw_ref)
        halo_ref[...] = mix[TT - HALO:]
        ymix_ref[:, 0:D_POOL] = y_pool.astype(bf16)
        ymix_ref[:, D_POOL:D_POOL + D_CONV] = y_conv.astype(bf16)

        q_ref[...] = (proj(_C_Q) * (HEAD_DIM ** -0.5 * LOG2E)).astype(bf16)
        z_ref[...] = proj(_C_Z).astype(bf16)
        g_ref[...] = proj(_C_G)
        kk = proj(_C_K).astype(bf16)
        ks_ref[...] = kk[:, 0:D_KV]
        kw_ref[:, 0:D_KV] = kk[:, D_KV:2 * D_KV]
        kw_ref[:, D_KV:2 * D_KV] = jnp.zeros((TT, D_KV), bf16)
        vt = _dot_nt(wvt_ref[...], x)
        for g in range(N_KV):
            vsT_ref[g, 0:HEAD_DIM] = vt[g * HEAD_DIM:(g + 1) * HEAD_DIM].astype(bf16)
            vwT_ref[g, 0:HEAD_DIM] = vt[D_KV + g * HEAD_DIM:D_KV + (g + 1) * HEAD_DIM].astype(bf16)
            vsT_ref[g, HEAD_DIM:VROWS] = ones_rows
            vwT_ref[g, HEAD_DIM:VROWS] = ones_rows

        kvc = proj(_C_CMP)
        for c in range(2):
            cscr_ref[c] = kvc[:, c * LANES:(c + 1) * LANES]
            for l in range(CMP_STRIDE):
                cmp_ref[l, :, c * LANES:(c + 1) * LANES] = (
                    cscr_ref[c, pl.ds(l, CMP_ROWS, stride=CMP_STRIDE), :].astype(bf16))


def _inproj(x2, w_all, wvt, pw_bd, pool_scale, conv_w, layer, B, T):
    nt = T // TT
    bt = B * T
    row = lambda b, s: (b * nt + jnp.maximum(s - 1, 0), 0)
    wsel = lambda b, s: (layer, 0, 0)
    tp = T + KPAD
    out_shape = [jax.ShapeDtypeStruct((bt, D_POOL + D_CONV), bf16),
                 jax.ShapeDtypeStruct((bt, D_NSA), bf16),
                 jax.ShapeDtypeStruct((bt, D_NSA), bf16),
                 jax.ShapeDtypeStruct((CMP_STRIDE, bt // CMP_STRIDE, 2 * D_KV), bf16),
                 jax.ShapeDtypeStruct((bt, LANES), f32),
                 jax.ShapeDtypeStruct((B, tp, D_KV), bf16),
                 jax.ShapeDtypeStruct((B, tp, 2 * D_KV), bf16),
                 jax.ShapeDtypeStruct((B, N_KV, VROWS, tp), bf16),
                 jax.ShapeDtypeStruct((B, N_KV, VROWS, tp), bf16)]
    out_specs = [pl.BlockSpec((TT, D_POOL + D_CONV), row),
                 pl.BlockSpec((TT, D_NSA), row),
                 pl.BlockSpec((TT, D_NSA), row),
                 pl.BlockSpec((CMP_STRIDE, CMP_ROWS, 2 * D_KV), lambda b, s: (0, b * nt + jnp.maximum(s - 1, 0), 0)),
                 pl.BlockSpec((TT, LANES), row),
                 pl.BlockSpec((None, TT, D_KV), lambda b, s: (b, s, 0)),
                 pl.BlockSpec((None, TT, 2 * D_KV), lambda b, s: (b, s, 0)),
                 pl.BlockSpec((None, N_KV, VROWS, TT), lambda b, s: (b, 0, 0, s)),
                 pl.BlockSpec((None, N_KV, VROWS, TT), lambda b, s: (b, 0, 0, s))]
    return pl.pallas_call(
        _inproj_kernel,
        grid=(B, nt + 1),
        in_specs=[pl.BlockSpec((TT, D_MODEL), row),
                  pl.BlockSpec((None, D_MODEL, W_ALL), wsel),
                  pl.BlockSpec((None, 2 * D_KV, D_MODEL), wsel),
                  pl.BlockSpec((None, D_POOL, D_POOL), wsel),
                  pl.BlockSpec((None, 1, D_POOL), wsel),
                  pl.BlockSpec((None, CONV_WIDTH, D_CONV), wsel)],
        out_specs=out_specs,
        out_shape=out_shape,
        scratch_shapes=[pltpu.VMEM((HALO, MIX_W), f32), pltpu.VMEM((2, TT, LANES), f32)],
        compiler_params=pltpu.CompilerParams(dimension_semantics=("arbitrary", "arbitrary"),
                                             vmem_limit_bytes=VMEM_LIMIT),
        name="inproj",
    )(x2, w_all, wvt, pw_bd, pool_scale, conv_w)


CH = 4 * CMP_HIDDEN


def _compress_kernel(r_ref, pe_ref, w1_ref, w2_ref, w2t_ref, out_ref, outT_ref):
    nc = r_ref.shape[1]
    hk = CH // 2
    hs = []
    for kv in range(2):
        rows = jnp.concatenate([r_ref[l][:, kv * D_KV:(kv + 1) * D_KV] for l in range(CMP_STRIDE)], axis=1)
        zz = _dot(rows, w1_ref[kv])
        pb = _dot(pe_ref[kv], w1_ref[kv])
        z0 = zz[:, 0:hk] + pb[0:1, 0:hk]
        z1 = zz[:, hk:2 * hk] + pb[1:2, hk:2 * hk]
        hs.append(z0 + pltpu.roll(z1, nc - 1, axis=0))
    h = jnp.concatenate(hs, axis=1)
    h = (h * _sigmoid(h)).astype(bf16)
    out_ref[...] = _dot(h, w2_ref[...]).astype(bf16)
    outT_ref[...] = _dot_nt(w2t_ref[...], h).astype(bf16)


def _compress(cmp_l, pe_r, w1_big, w2_big, w2t_big, layer, B):
    nc = cmp_l.shape[1] // B
    rw = CMP_STRIDE * D_KV
    wsel = lambda b: (layer, 0, 0)
    wsel4 = lambda b: (layer, 0, 0, 0)
    return pl.pallas_call(
        _compress_kernel,
        grid=(B,),
        in_specs=[pl.BlockSpec((CMP_STRIDE, nc, 2 * D_KV), lambda b: (0, b, 0)),
                  pl.BlockSpec((None, 2, 8, rw), wsel4),
                  pl.BlockSpec((None, 2, rw, CH), wsel4),
                  pl.BlockSpec((None, CH, 2 * D_KV), wsel),
                  pl.BlockSpec((None, 2 * D_KV, CH), wsel)],
        out_specs=[pl.BlockSpec((None, nc, 2 * D_KV), lambda b: (b, 0, 0)),
                   pl.BlockSpec((None, 2 * D_KV, nc), lambda b: (b, 0, 0))],
        out_shape=[jax.ShapeDtypeStruct((B, nc, 2 * D_KV), bf16),
                   jax.ShapeDtypeStruct((B, 2 * D_KV, nc), bf16)],
        compiler_params=pltpu.CompilerParams(dimension_semantics=("arbitrary",),
                                             vmem_limit_bytes=VMEM_LIMIT),
        name="compress",
    )(cmp_l, pe_r, w1_big, w2_big, w2t_big)


def _selection_scores(imp_t, t0):
    shape = imp_t.shape
    jblk = lax.broadcasted_iota(jnp.int32, shape, 0)
    tq = lax.broadcasted_iota(jnp.int32, shape, 1) % TQ
    back = jnp.right_shift(t0 + tq, int(math.log2(SLC_BLOCK))) - jblk
    causal = back >= 0
    forced = jnp.logical_or(jblk == 0, jnp.logical_and(causal, back < N_LOCAL))
    jf = jblk.astype(f32)
    score = jnp.where(forced, -jnp.inf, jnp.where(causal, imp_t, NEG - jf * NONCAUSAL_STEP))
    tblk = jnp.right_shift(t0 + lax.broadcasted_iota(jnp.int32, (1, shape[1]), 1) % TQ, int(math.log2(SLC_BLOCK)))
    return score, (jnp.minimum(tblk, N_LOCAL) + 1).astype(f32)


def _pick_exact(score):
    jf = lax.broadcasted_iota(jnp.int32, score.shape, 0).astype(f32)
    for _ in range(N_PICKS):
        best = jnp.max(score, axis=0, keepdims=True)
        first = jnp.min(jnp.where(score == best, jf, float(score.shape[0])), axis=0, keepdims=True)
        score = jnp.where(jf == first, -jnp.inf, score)
    return score


def _write_selection(imp_t, t0, nsel, pen_ref):
    score0, n_forced = _selection_scores(imp_t, t0)
    score = score0
    for _ in range(N_PICKS):
        score = jnp.where(score == jnp.max(score, axis=0, keepdims=True), -jnp.inf, score)
    taken = score == -jnp.inf
    pen_ref[0:nsel] = jnp.where(taken, 0.0, NEG).astype(bf16)
    if nsel < NSB:
        pen_ref[nsel:NSB] = jnp.full((NSB - nsel, N_KV * TQ), NEG, bf16)
    n_taken = jnp.sum(jnp.where(taken, 1.0, 0.0), axis=0, keepdims=True)
    tied = jnp.max(jnp.abs(n_taken - n_forced - N_PICKS))

    @pl.when(tied > 0.0)
    def _():
        pen_ref[0:nsel] = jnp.where(_pick_exact(score0) == -jnp.inf, 0.0, NEG).astype(bf16)


def _query_cols(q_ref):
    sub = lax.broadcasted_iota(jnp.int32, (LANES, TQ), 0)
    low = sub < HEAD_DIM
    pairs = [q_ref[:, LANES * p:LANES * (p + 1)].astype(f32).T for p in range(GROUP)]
    blocks = [(jnp.where(low, pairs[h % GROUP], 0.0) if h < GROUP else jnp.where(low, 0.0, pairs[h % GROUP])).astype(bf16)
              for h in range(N_HEADS)]
    return jnp.concatenate(blocks, axis=1)


def _merge_head_pairs(outs):
    return [jnp.concatenate([outs[p], outs[p + GROUP]], axis=0).T for p in range(GROUP)]


def _cmp_scores(q_cmp, cmp_ref, t0, nc):
    n_i = lax.broadcasted_iota(jnp.int32, (nc, LANES), 0)
    j_i = lax.broadcasted_iota(jnp.int32, (nc, LANES), 1)
    nstart = t0 // CMP_STRIDE - CMP_BACK
    in_window = jnp.logical_and(j_i < 2 * CMP_SLOTS, n_i == nstart + jnp.where(j_i < CMP_SLOTS, j_i, j_i - CMP_SLOTS))
    future = jnp.logical_and(j_i == _PAD_FLAG_COL, n_i >= nstart + CMP_SLOTS - 1)
    place_b = jnp.where(jnp.logical_or(in_window, future), 1.0, 0.0).astype(bf16)
    kc_ext = jnp.concatenate([cmp_ref[0:nc, 0:D_KV], place_b], axis=1)
    return _dot(kc_ext, q_cmp)


def _cmp_finish(s1, sig_t, cmpT_ref, ovT_ref, y1_ref, t0, nsel):
    nc = s1.shape[0]
    m1 = jnp.max(s1, axis=0, keepdims=True)
    p1 = jnp.exp2(s1 - m1)
    p1 = p1.astype(bf16)
    tcol = t0 + lax.broadcasted_iota(jnp.int32, (1, GC), 1) % TQ
    outs, sums = [], []
    for g in range(N_KV):
        lhs = jnp.concatenate([cmpT_ref[D_KV + g * HEAD_DIM:D_KV + (g + 1) * HEAD_DIM, 0:nc], ovT_ref[0:nsel, 0:nc],
                               ovT_ref[NSB:NSB + SUM_ROWS, 0:nc]], axis=0)
        both = _dot(lhs, p1[:, g * GC:(g + 1) * GC])
        l1 = both[HEAD_DIM + nsel:HEAD_DIM + nsel + 1]
        both = both * jnp.where(tcol >= CMP_BLOCK - 1, 1.0 / l1, 0.0)
        for r in range(GROUP):
            h = g * GROUP + r
            outs.append(sig_t[3 * h:3 * h + 1, :] * both[0:HEAD_DIM, r * TQ:(r + 1) * TQ])
        imp = both[HEAD_DIM:HEAD_DIM + nsel]
        acc = imp[:, 0:TQ]
        for r in range(1, GROUP):
            acc = acc + imp[:, r * TQ:(r + 1) * TQ]
        sums.append(acc)
    for p, y in enumerate(_merge_head_pairs(outs)):
        y1_ref[:, p * LANES:(p + 1) * LANES] = y.astype(bf16)
    return jnp.concatenate(sums, axis=1)


def _nsa_kernel(q0_ref, qn_ref, z_ref, g0_ref, gn_ref, cmp_ref, cmpT_ref, ovT_ref, ks_ref, kw_ref, vsT_ref, vwT_ref,
                e_ref, tab_ref, acmp_ref, out_ref,
                qc_ref, sg_ref, pen_ref, y1_ref, o3_ref, m_ref, l_ref, acc_ref, sa_ref, sb_ref, mxa_ref, mxb_ref):
    qi = pl.program_id(1)
    nq = pl.num_programs(1)
    t0 = qi * TQ
    nc = cmp_ref.shape[0]
    classes = [(nc * k // N_CLASSES, NSB * k // N_CLASSES) for k in range(1, N_CLASSES + 1)]
    last_q = [min(c // (TQ // CMP_STRIDE), s // (TQ // SLC_BLOCK)) - 1 for c, s in classes]

    @pl.when(qi == 0)
    def _():
        qc_ref[...] = _query_cols(q0_ref)
        sg_ref[...] = _sigmoid(g0_ref[...]).T
        q_cmp0 = jnp.concatenate([qc_ref[...], acmp_ref[...]], axis=0)
        imp0 = _cmp_finish(_cmp_scores(q_cmp0, cmp_ref, t0, classes[0][0]), sg_ref[...], cmpT_ref, ovT_ref, y1_ref, t0,
                           classes[0][1])
        _write_selection(imp0, t0, classes[0][1], pen_ref)

    qcols = qc_ref[...]
    sig_cur = sg_ref[...]
    q_cmp = jnp.concatenate([qcols, acmp_ref[...]], axis=0)
    pens = [pen_ref[:, g * TQ:(g + 1) * TQ] for g in range(N_KV) for _ in range(GROUP)]
    q_slc = jnp.concatenate([qcols, jnp.concatenate(pens, axis=1)], axis=0)
    y1 = y1_ref[...]

    m_ref[...] = jnp.full((1, R), NEG, f32)
    l_ref[...] = jnp.zeros((1, R), f32)
    acc_ref[...] = jnp.zeros((HEAD_DIM, R), f32)
    n_far = (qi + KT // TQ) // (KT // TQ) - 1
    first = (qi + 1) * TQ - n_far * KT

    def tile_row(i):
        return pl.multiple_of(first + i * KT, LANES)

    def produce(i, s_ref, mx_ref):
        r0 = tile_row(i)
        k_ext = jnp.concatenate([ks_ref[pl.ds(r0, KT), :], e_ref[pl.ds(r0, KT), :]], axis=1)
        s = _dot(k_ext, q_slc)
        s_ref[...] = s
        mx_ref[...] = jnp.max(s, axis=0, keepdims=True)

    def consume(i, s_ref, mx_ref, last):
        m_prev = m_ref[...]
        if last:
            s = jnp.concatenate([s_ref[0:KT - 2 * TQ], s_ref[KT - 2 * TQ:KT] + tab_ref[TQ:3 * TQ]], axis=0)
            m_new = jnp.maximum(m_prev, jnp.max(s, axis=0, keepdims=True))
        else:
            s = s_ref[...]
            m_new = jnp.maximum(m_prev, mx_ref[...])
        alpha = jnp.exp2(m_prev - m_new)
        p = jnp.exp2(s - m_new)
        p = p.astype(bf16)
        for g in range(N_KV):
            c = slice(g * GC, (g + 1) * GC)
            pv = _dot(vsT_ref[g, :, pl.ds(tile_row(i), KT)], p[:, c])
            l_ref[:, c] = alpha[:, c] * l_ref[:, c] + pv[HEAD_DIM:HEAD_DIM + 1]
            acc_ref[:, c] = acc_ref[:, c] * alpha[:, c] + pv[0:HEAD_DIM]
        m_ref[...] = m_new

    qn = jnp.minimum(qi + 1, nq - 1)

    def front(nc_use, nsel_use):
        tn = qn * TQ
        qc_next = _query_cols(qn_ref)
        sig_next = _sigmoid(gn_ref[...]).T
        qc_ref[...] = qc_next
        sg_ref[...] = sig_next
        s1 = _cmp_scores(jnp.concatenate([qc_next, acmp_ref[...]], axis=0), cmp_ref, tn, nc_use)
        w0 = pl.multiple_of(t0, LANES)
        s3 = _dot(kw_ref[pl.ds(w0, N_WIN), :], q_cmp)
        imp_next = _cmp_finish(s1, sig_next, cmpT_ref, ovT_ref, y1_ref, tn, nsel_use)
        produce(0, sa_ref, mxa_ref)
        s3 = jnp.concatenate([s3[0:TQ] + tab_ref[0:TQ], s3[TQ:WINDOW - TQ], s3[WINDOW - TQ:] + tab_ref[TQ:3 * TQ]],
                             axis=0)
        p3 = jnp.exp2(s3 - jnp.max(s3, axis=0, keepdims=True)).astype(bf16)
        for g in range(N_KV):
            c = slice(g * GC, (g + 1) * GC)
            o3_ref[:, c] = _dot(vwT_ref[g, :, pl.ds(w0, N_WIN)], p3[:, c])
        _write_selection(imp_next, tn, nsel_use, pen_ref)

    for k, (nc_k, nsel_k) in enumerate(classes):
        lo = last_q[k - 1] if k else -1
        if k < N_CLASSES - 1:
            pl.when(jnp.logical_and(qn > lo, qn <= last_q[k]))(functools.partial(front, nc_k, nsel_k))
        else:
            pl.when(qn > lo)(functools.partial(front, nc_k, nsel_k))

    bufs = ((sa_ref, mxa_ref), (sb_ref, mxb_ref))

    def pipeline(base, count, last_at_end):
        for k in range(count):
            if k + 1 < count or not last_at_end:
                produce(base + k + 1, *bufs[(k + 1) % 2])
            consume(base + k, *bufs[k % 2], last_at_end and k == count - 1)

    def unrolled(j, c):
        pipeline(UNROLL * j, UNROLL, False)
        return c

    lax.fori_loop(0, n_far // UNROLL, unrolled, 0)

    def finish():
        inv2 = 1.0 / l_ref[...]
        inv3 = 1.0 / o3_ref[HEAD_DIM:HEAD_DIM + 1, :]
        sig_t = sig_cur
        outs = []
        for h in range(N_HEADS):
            c = slice(h * TQ, (h + 1) * TQ)
            g2 = sig_t[3 * h + 1:3 * h + 2, :] * inv2[:, c]
            g3 = sig_t[3 * h + 2:3 * h + 3, :] * inv3[:, c]
            outs.append(g2 * acc_ref[:, c] + g3 * o3_ref[0:HEAD_DIM, c])
        for p, y in enumerate(_merge_head_pairs(outs)):
            c = slice(p * LANES, (p + 1) * LANES)
            zb = z_ref[:, c].astype(f32)
            out_ref[:, c] = ((y + y1[:, c].astype(f32)) * zb * _sigmoid(zb)).astype(bf16)

    for r in range(UNROLL):
        def drain(r=r):
            pipeline(n_far - r, r + 1, True)
            finish()
        pl.when(n_far % UNROLL == r)(drain)


def _nsa(q, z, g, cmp, cmpT, ks, kw, vsT, vwT, e_pad, ovT, tabs, acmp, B, T):
    nq = T // TQ
    nc = T // CMP_STRIDE
    tp = T + KPAD
    tile = lambda b, i: (b * nq + i, 0)
    nxt = lambda b, i: (b * nq + jnp.minimum(i + 1, nq - 1), 0)
    head = lambda b, i: (b * nq, 0)
    const2 = lambda b, i: (0, 0)
    batch3 = lambda b, i: (b, 0, 0)
    return pl.pallas_call(
        _nsa_kernel,
        grid=(B, nq),
        in_specs=[pl.BlockSpec((TQ, D_NSA), head),
                  pl.BlockSpec((TQ, D_NSA), nxt),
                  pl.BlockSpec((TQ, D_NSA), tile),
                  pl.BlockSpec((TQ, LANES), head),
                  pl.BlockSpec((TQ, LANES), nxt),
                  pl.BlockSpec((None, nc, 2 * D_KV), batch3),
                  pl.BlockSpec((None, 2 * D_KV, nc), batch3),
                  pl.BlockSpec((NSB + SUM_ROWS, nc), const2),
                  pl.BlockSpec((None, tp, D_KV), batch3),
                  pl.BlockSpec((None, tp, 2 * D_KV), batch3),
                  pl.BlockSpec((None, N_KV, VROWS, tp), lambda b, i: (b, 0, 0, 0)),
                  pl.BlockSpec((None, N_KV, VROWS, tp), lambda b, i: (b, 0, 0, 0)),
                  pl.BlockSpec((tp, NSB), const2),
                  pl.BlockSpec((3 * TQ, R), const2),
                  pl.BlockSpec((LANES, R), const2)],
        out_specs=pl.BlockSpec((TQ, D_NSA), tile),
        out_shape=jax.ShapeDtypeStruct((B * T, D_NSA), bf16),
        scratch_shapes=[pltpu.VMEM((LANES, R), bf16), pltpu.VMEM((LANES, TQ), f32),
                        pltpu.VMEM((NSB, N_KV * TQ), bf16), pltpu.VMEM((TQ, D_NSA), bf16), pltpu.VMEM((VROWS, R), f32),
                        pltpu.VMEM((1, R), f32), pltpu.VMEM((1, R), f32), pltpu.VMEM((HEAD_DIM, R), f32),
                        pltpu.VMEM((KT, R), f32), pltpu.VMEM((KT, R), f32),
                        pltpu.VMEM((1, R), f32), pltpu.VMEM((1, R), f32)],
        compiler_params=pltpu.CompilerParams(dimension_semantics=("arbitrary", "arbitrary"),
                                             vmem_limit_bytes=VMEM_LIMIT),
        name="nsa",
    )(q, q, z, g, g, cmp, cmpT, ovT, ks, kw, vsT, vwT, e_pad, tabs, acmp)


def _outproj_kernel(alpha, ymix_ref, ynsa_ref, x_ref, w_ref, g_ref, b_ref, out_ref):
    for c in range(TT_OUT // OUT_CHUNK):
        rows = slice(c * OUT_CHUNK, (c + 1) * OUT_CHUNK)
        y = jnp.concatenate([ymix_ref[rows], ynsa_ref[rows]], axis=1)
        r = alpha * x_ref[rows] + _dot(y, w_ref[...])
        mu = jnp.mean(r, axis=-1, keepdims=True)
        d = r - mu
        var = jnp.mean(d * d, axis=-1, keepdims=True)
        out_ref[rows] = d * lax.rsqrt(var + LN_EPS) * g_ref[...] + b_ref[...]


def _outproj(ymix, ynsa, x2, w_out, ln_g, ln_b, alpha, layer):
    bt = x2.shape[0]
    row = lambda i: (i, 0)
    wsel = lambda i: (layer, 0, 0)
    return pl.pallas_call(
        functools.partial(_outproj_kernel, alpha),
        grid=(bt // TT_OUT,),
        in_specs=[pl.BlockSpec((TT_OUT, D_POOL + D_CONV), row),
                  pl.BlockSpec((TT_OUT, D_NSA), row),
                  pl.BlockSpec((TT_OUT, D_MODEL), row),
                  pl.BlockSpec((None, D_MODEL, D_MODEL), wsel),
                  pl.BlockSpec((None, 1, D_MODEL), wsel),
                  pl.BlockSpec((None, 1, D_MODEL), wsel)],
        out_specs=pl.BlockSpec((TT_OUT, D_MODEL), row),
        out_shape=jax.ShapeDtypeStruct((bt, D_MODEL), f32),
        compiler_params=pltpu.CompilerParams(dimension_semantics=("arbitrary",),
                                             vmem_limit_bytes=VMEM_LIMIT),
        name="outproj",
    )(ymix, ynsa, x2, w_out, ln_g, ln_b)


def _bucket_np(d):
    d = np.asarray(d)
    max_exact = N_BUCKETS // 2
    nf = np.maximum(d, 1).astype(np.float32)
    large = max_exact + (np.log(nf / np.float32(max_exact)) / np.float32(math.log(MAX_DISTANCE / max_exact))
                         * np.float32(N_BUCKETS - max_exact)).astype(np.int32)
    large = np.minimum(large, N_BUCKETS - 1)
    return np.where(d < max_exact, d, large)


_FAR_DIST = 113
assert _bucket_np(np.arange(_FAR_DIST, 4 * WINDOW)).min() == N_BUCKETS - 1

assert CMP_STRIDE * (CMP_BACK + 1) - (CMP_BLOCK - 1) >= _FAR_DIST


def _bias_tables(rel_bias):
    nd = 2 * TQ
    onehot = jnp.asarray(np.eye(N_BUCKETS, dtype=np.float32)[_bucket_np(np.arange(nd))])
    tabp = ((jnp.dot(onehot, rel_bias, precision=lax.Precision.HIGHEST)
             - rel_bias[N_BUCKETS - 1:N_BUCKETS, :]) * LOG2E).T
    sl = np.arange(TQ)[:, None]
    tl = np.arange(TQ)[None, :]
    neg = lambda n: jnp.full((N_HEADS, n), NEG, f32)

    def skew(u, rows):
        period = u.shape[1]
        return jnp.tile(u, (1, rows))[:, :rows * (period - 1)].reshape(N_HEADS, rows, period - 1)

    def tile_layout(t):
        return t.transpose(1, 0, 2).reshape(TQ, R)

    diag = tile_layout(skew(jnp.concatenate([tabp[:, 0:TQ], neg(TQ)], axis=1), TQ)[:, :, :TQ])
    prev = tile_layout(skew(jnp.concatenate([tabp[:, TQ:nd], tabp[:, 0:TQ]], axis=1), TQ)[:, :, :TQ])
    edge = jnp.asarray(np.tile(np.where(sl > tl, 0.0, NEG).astype(np.float32), (1, N_HEADS)))
    tabs = jnp.concatenate([edge, prev, diag], axis=0).astype(f32)

    j = np.arange(CMP_SLOTS)[None, :]
    off = CMP_BACK * CMP_STRIDE - (CMP_BLOCK - 1)
    dc = np.arange(TQ)[:, None] - CMP_STRIDE * j + off
    okc = jnp.asarray(dc >= 0)[None]
    period = 4 * TQ
    u = jnp.concatenate([tabp[:, off:nd], jnp.zeros((N_HEADS, off), f32), neg(period - nd - off), tabp[:, 0:off]], axis=1)
    vc = skew(u, nd)[:, 0:CMP_SLOTS * CMP_STRIDE:CMP_STRIDE, 0:TQ].transpose(0, 2, 1)
    vc = jnp.where(okc, vc, NEG)
    hi = vc.astype(bf16)
    lo = jnp.where(okc, vc - hi.astype(f32), 0.0).astype(bf16)
    fut = jnp.full((N_HEADS, TQ, 1), NEG, bf16)
    pad = jnp.zeros((N_HEADS, TQ, LANES - _PAD_FLAG_COL - 1), bf16)
    acmp = jnp.concatenate([hi, lo, fut, pad], axis=-1).reshape(R, LANES).T
    return tabs, acmp


def _static_tables(T):
    nc = T // CMP_STRIDE
    ns = T // SLC_BLOCK
    cstart = np.arange(nc)[None, :] * CMP_STRIDE
    sstart = np.arange(NSB)[:, None] * SLC_BLOCK
    ov = np.clip(np.minimum(cstart + CMP_BLOCK, sstart + SLC_BLOCK) - np.maximum(cstart, sstart), 0, None) / CMP_STRIDE
    ov[ns:, :] = 0
    ov[:, nc - 1] = 0
    ov = np.concatenate([ov, np.ones((SUM_ROWS, nc))], axis=0)
    e_pad = np.concatenate([np.ones((KPAD, NSB), bool),
                            np.arange(T)[:, None] // SLC_BLOCK == np.arange(NSB)[None, :]], axis=0)
    return jnp.asarray(ov, bf16), jnp.asarray(e_pad, bf16)


def _pair_order(w, axis):
    shp = w.shape
    w = w.reshape(shp[:axis] + (N_KV, GROUP, HEAD_DIM) + shp[axis + 1:])
    return jnp.swapaxes(w, axis, axis + 1).reshape(shp)


def _prep_weights(w_in, w_out, pool_w, pe_k, w1_k, w2_k, pe_v, w1_v, w2_v):
    depth = w_in.shape[0]
    sizes = (D_POOL, D_POOL, D_CONV, D_CONV, D_CONV, D_CONV, D_NSA, D_KV, D_KV, D_KV, D_KV, D_KV, D_KV,
             3 * N_HEADS, D_NSA)
    offs = np.cumsum((0,) + sizes)
    col = lambda i: w_in[:, :, offs[i]:offs[i + 1]]
    wg = jnp.pad(col(13), ((0, 0), (0, 0), (0, LANES - 3 * N_HEADS)))
    w_all = jnp.concatenate([w_in[:, :, 0:MIX_W], _pair_order(col(6), 2), _pair_order(col(14), 2),
                             col(7), col(8), col(9), col(11), wg], axis=2).astype(bf16)
    wvt = jnp.swapaxes(jnp.concatenate([col(10), col(12)], axis=2), 1, 2).astype(bf16)
    nm = D_POOL + D_CONV
    w_out_p = jnp.concatenate([w_out[:, 0:nm], _pair_order(w_out[:, nm:], 1)], axis=1).astype(bf16)

    eye_g = jnp.eye(len(POOL_WINDOWS), dtype=bf16)
    pw_bd = jnp.einsum('zgcd,gh->zgchd', pool_w.astype(bf16), eye_g).reshape(depth, D_POOL, D_POOL)

    half = CMP_BLOCK // 2
    eye2 = jnp.eye(2, dtype=bf16)
    rw = half * D_KV
    w1_kv = []
    for w1 in (w1_k, w1_v):
        cols = []
        for a in range(2):
            wsel = w1.astype(bf16).reshape(depth, 2, half, 1, HEAD_DIM, CMP_HIDDEN)[:, a]
            for g in range(N_KV):
                blk = jnp.pad(wsel, ((0, 0), (0, 0), (g, N_KV - 1 - g), (0, 0), (0, 0)))
                cols.append(blk.reshape(depth, rw, CMP_HIDDEN))
        w1_kv.append(jnp.concatenate(cols, axis=2))
    w1_big = jnp.stack(w1_kv, axis=1)
    pes = jnp.stack([pe_k, pe_v], axis=1).astype(bf16).reshape(depth, 2, 2, half, 1, HEAD_DIM)
    pe_r = jnp.broadcast_to(pes, (depth, 2, 2, half, N_KV, HEAD_DIM)).reshape(depth, 2, 2, rw)
    pe_r = jnp.pad(pe_r, ((0, 0), (0, 0), (0, 6), (0, 0)))
    w2s = jnp.stack([w2_k, w2_v], axis=1).astype(bf16)
    w2_big = jnp.einsum('zkhd,kK,gG->zkghKGd', w2s, eye2, eye2).reshape(depth, CH, 2 * D_KV)
    return w_all, wvt, w_out_p, pw_bd, w1_big, pe_r, w2_big, jnp.swapaxes(w2_big, 1, 2)


def kernel(x, w_in, w_out, pool_w, pool_scale, conv_w, cmp_pe_k, cmp_w1_k, cmp_w2_k, cmp_pe_v, cmp_w1_v, cmp_w2_v,
           rel_bias, ln_g, ln_b):
    B, T, D = x.shape
    depth = w_in.shape[0]
    assert D == D_MODEL and T % TT == 0 and T // SLC_BLOCK <= NSB and T // SLC_BLOCK >= N_SELECT
    assert KPAD == TT and (B * T) % TT_OUT == 0 and N_SELECT > N_LOCAL
    alpha = (2 * depth) ** 0.25
    tabs, acmp = _bias_tables(rel_bias)
    ovT, e_pad = _static_tables(T)
    w_all, wvt, w_out_p, pw_bd, w1_big, pe_r, w2_big, w2t_big = _prep_weights(
        w_in, w_out, pool_w, cmp_pe_k, cmp_w1_k, cmp_w2_k, cmp_pe_v, cmp_w1_v, cmp_w2_v)
    pool_scale = pool_scale.reshape(depth, 1, D_POOL)
    ln_g = ln_g.reshape(depth, 1, D_MODEL)
    ln_b = ln_b.reshape(depth, 1, D_MODEL)
    h = x.reshape(B * T, D)
    for l in range(depth):
        ymix, q, z, cmp_l, g, ks, kw, vsT, vwT = _inproj(h, w_all, wvt, pw_bd, pool_scale, conv_w, l, B, T)
        cmp, cmpT = _compress(cmp_l, pe_r, w1_big, w2_big, w2t_big, l, B)
        ynsa = _nsa(q, z, g, cmp, cmpT, ks, kw, vsT, vwT, e_pad, ovT, tabs, acmp, B, T)
        h = _outproj(ymix, ynsa, h, w_out_p, ln_g, ln_b, alpha, l)
    return h.reshape(B, T, D)
```

```python
import functools
import math

import numpy as np
import jax
import jax.numpy as jnp
from jax import lax
from jax.experimental import pallas as pl
from jax.experimental.pallas import tpu as pltpu

f32 = jnp.float32
bf16 = jnp.bfloat16

D_MODEL = 1024
D_POOL = 256
D_CONV = 256
D_NSA = 512
HEAD_DIM = 64
N_HEADS = 8
N_KV = 2
GROUP = 4
D_KV = 128
POOL_GROUP = 64
POOL_WINDOWS = (2, 4, 8, 16)
CONV_WIDTH = 3
CMP_BLOCK = 32
CMP_STRIDE = 16
CMP_HIDDEN = 128
SLC_BLOCK = 64
N_SELECT = 16
N_LOCAL = 2
WINDOW = 512
N_BUCKETS = 32
MAX_DISTANCE = 128
LN_EPS = 1e-5
NEG = -1e30

LANES = 128
TQ = 128
R = N_HEADS * TQ
NSB = 128
TT = 512
TT_OUT = 1024
UNROLL = 8
N_PICKS = N_SELECT - N_LOCAL - 1
NONCAUSAL_STEP = 1e24
N_CLASSES = 4
HALO = 16
KT = 512
KPAD = KT
N_WIN = WINDOW + TQ
SUM_ROWS = 16
VROWS = HEAD_DIM + SUM_ROWS
GC = R // N_KV
LOG2E = math.log2(math.e)
CMP_BACK = 8
CMP_SLOTS = TQ // CMP_STRIDE + CMP_BACK
_PAD_FLAG_COL = 2 * CMP_SLOTS
MIX_W = 2 * D_POOL + 4 * D_CONV
VMEM_LIMIT = 56 * 1024 * 1024

_NT = (((1,), (1,)), ((), ()))


def _dot(a, b):
    return jnp.dot(a, b, preferred_element_type=f32)


def _dot_nt(a, b):
    return lax.dot_general(a, b, _NT, preferred_element_type=f32)


def _sigmoid(x):
    return 1.0 / (1.0 + jnp.exp(-x))


_C_MIX = (0, MIX_W)
_C_Q = (_C_MIX[1], _C_MIX[1] + D_NSA)
_C_Z = (_C_Q[1], _C_Q[1] + D_NSA)
_C_CMP = (_C_Z[1], _C_Z[1] + 2 * D_KV)
_C_K = (_C_CMP[1], _C_CMP[1] + 2 * D_KV)
_C_G = (_C_K[1], _C_K[1] + LANES)
W_ALL = _C_G[1]
CMP_ROWS = TT // CMP_STRIDE


def _local_mixers(ext, i, pw_ref, ps_ref, cw_ref):
    e = ext[:, 0:D_POOL]
    s2 = e + pltpu.roll(e, 1, axis=0)
    s4 = s2 + pltpu.roll(s2, 2, axis=0)
    s8 = s4 + pltpu.roll(s4, 4, axis=0)
    s16 = s8 + pltpu.roll(s8, 8, axis=0)
    lane = lax.broadcasted_iota(jnp.int32, (TT, D_POOL), 1)
    grp = lane // POOL_GROUP
    wsum = jnp.where(grp == 0, s2[HALO:], jnp.where(grp == 1, s4[HALO:], jnp.where(grp == 2, s8[HALO:], s16[HALO:])))
    win = jnp.left_shift(2, grp)
    pos = i * TT + lax.broadcasted_iota(jnp.int32, (TT, D_POOL), 0)
    cnt = jnp.minimum(pos + 1, win).astype(f32)
    v = e[HALO:]
    pooled = wsum / cnt - v
    y_pool = _dot(pooled.astype(bf16), pw_ref[...]) * ps_ref[...]
    zp = ext[HALO:, D_POOL:2 * D_POOL]
    y_pool = y_pool * (zp * _sigmoid(zp))

    o = 2 * D_POOL
    cb = ext[HALO:, o:o + D_CONV]
    u = ext[:, o + D_CONV:o + 2 * D_CONV] * ext[:, o + 2 * D_CONV:o + 3 * D_CONV]
    zc = ext[HALO:, o + 3 * D_CONV:o + 4 * D_CONV]
    conv = cw_ref[CONV_WIDTH - 1:CONV_WIDTH, :] * u[HALO:]
    for k in range(CONV_WIDTH - 1):
        conv = conv + cw_ref[k:k + 1, :] * pltpu.roll(u, CONV_WIDTH - 1 - k, axis=0)[HALO:]
    y_conv = cb * conv * (zc * _sigmoid(zc))
    return y_pool, y_conv


def _inproj_kernel(x_ref, w_ref, wvt_ref, pw_ref, ps_ref, cw_ref,
                   ymix_ref, q_ref, z_ref, cmp_ref, g_ref, ks_ref, kw_ref, vsT_ref, vwT_ref,
                   halo_ref, cscr_ref):
    step = pl.program_id(1)
    ones_rows = jnp.ones((SUM_ROWS, TT), bf16)

    @pl.when(step == 0)
    def _():
        ks_ref[...] = jnp.zeros((TT, D_KV), bf16)
        kw_ref[:, 0:D_KV] = jnp.zeros((TT, D_KV), bf16)
        lane = lax.broadcasted_iota(jnp.int32, (TT, D_KV), 1)
        kw_ref[:, D_KV:2 * D_KV] = jnp.where(lane == _PAD_FLAG_COL, 1.0, 0.0).astype(bf16)
        for g in range(N_KV):
            for ref in (vsT_ref, vwT_ref):
                ref[g, 0:HEAD_DIM] = jnp.zeros((HEAD_DIM, TT), bf16)
                ref[g, HEAD_DIM:VROWS] = ones_rows

    @pl.when(step > 0)
    def _():
        i = step - 1
        x = x_ref[...].astype(bf16)

        def proj(c):
            return _dot(x, w_ref[:, c[0]:c[1]])

        mix = proj(_C_MIX)
        halo = jnp.where(i > 0, halo_ref[...], 0.0)
        y_pool, y_conv = _local_mixers(jnp.concatenate([halo, mix], axis=0), i, pw_ref, ps_ref, cw_ref)
        halo_ref[...] = mix[TT - HALO:]
        ymix_ref[:, 0:D_POOL] = y_pool.astype(bf16)
        ymix_ref[:, D_POOL:D_POOL + D_CONV] = y_conv.astype(bf16)

        kvc = proj(_C_CMP)
        for c in range(2):
            cscr_ref[c] = kvc[:, c * LANES:(c + 1) * LANES]
            for l in range(CMP_STRIDE):
                cmp_ref[l, :, c * LANES:(c + 1) * LANES] = (
                    cscr_ref[c, pl.ds(l, CMP_ROWS, stride=CMP_STRIDE), :].astype(bf16))

        q_ref[...] = (proj(_C_Q) * (HEAD_DIM ** -0.5 * LOG2E)).astype(bf16)
        z_ref[...] = proj(_C_Z).astype(bf16)
        g_ref[...] = proj(_C_G)
        kk = proj(_C_K).astype(bf16)
        ks_ref[...] = kk[:, 0:D_KV]
        kw_ref[:, 0:D_KV] = kk[:, D_KV:2 * D_KV]
        kw_ref[:, D_KV:2 * D_KV] = jnp.zeros((TT, D_KV), bf16)
        vt = _dot_nt(wvt_ref[...], x)
        for g in range(N_KV):
            vsT_ref[g, 0:HEAD_DIM] = vt[g * HEAD_DIM:(g + 1) * HEAD_DIM].astype(bf16)
            vwT_ref[g, 0:HEAD_DIM] = vt[D_KV + g * HEAD_DIM:D_KV + (g + 1) * HEAD_DIM].astype(bf16)
            vsT_ref[g, HEAD_DIM:VROWS] = ones_rows
            vwT_ref[g, HEAD_DIM:VROWS] = ones_rows


def _inproj(x2, w_all, wvt, pw_bd, pool_scale, conv_w, layer, B, T):
    nt = T // TT
    bt = B * T
    row = lambda b, s: (b * nt + jnp.maximum(s - 1, 0), 0)
    wsel = lambda b, s: (layer, 0, 0)
    tp = T + KPAD
    out_shape = [jax.ShapeDtypeStruct((bt, D_POOL + D_CONV), bf16),
                 jax.ShapeDtypeStruct((bt, D_NSA), bf16),
                 jax.ShapeDtypeStruct((bt, D_NSA), bf16),
                 jax.ShapeDtypeStruct((CMP_STRIDE, bt // CMP_STRIDE, 2 * D_KV), bf16),
                 jax.ShapeDtypeStruct((bt, LANES), f32),
                 jax.ShapeDtypeStruct((B, tp, D_KV), bf16),
                 jax.ShapeDtypeStruct((B, tp, 2 * D_KV), bf16),
                 jax.ShapeDtypeStruct((B, N_KV, VROWS, tp), bf16),
                 jax.ShapeDtypeStruct((B, N_KV, VROWS, tp), bf16)]
    out_specs = [pl.BlockSpec((TT, D_POOL + D_CONV), row),
                 pl.BlockSpec((TT, D_NSA), row),
                 pl.BlockSpec((TT, D_NSA), row),
                 pl.BlockSpec((CMP_STRIDE, CMP_ROWS, 2 * D_KV), lambda b, s: (0, b * nt + jnp.maximum(s - 1, 0), 0)),
                 pl.BlockSpec((TT, LANES), row),
                 pl.BlockSpec((None, TT, D_KV), lambda b, s: (b, s, 0)),
                 pl.BlockSpec((None, TT, 2 * D_KV), lambda b, s: (b, s, 0)),
                 pl.BlockSpec((None, N_KV, VROWS, TT), lambda b, s: (b, 0, 0, s)),
                 pl.BlockSpec((None, N_KV, VROWS, TT), lambda b, s: (b, 0, 0, s))]
    return pl.pallas_call(
        _inproj_kernel,
        grid=(B, nt + 1),
        in_specs=[pl.BlockSpec((TT, D_MODEL), row),
                  pl.BlockSpec((None, D_MODEL, W_ALL), wsel),
                  pl.BlockSpec((None, 2 * D_KV, D_MODEL), wsel),
                  pl.BlockSpec((None, D_POOL, D_POOL), wsel),
                  pl.BlockSpec((None, 1, D_POOL), wsel),
                  pl.BlockSpec((None, CONV_WIDTH, D_CONV), wsel)],
        out_specs=out_specs,
        out_shape=out_shape,
        scratch_shapes=[pltpu.VMEM((HALO, MIX_W), f32), pltpu.VMEM((2, TT, LANES), f32)],
        compiler_params=pltpu.CompilerParams(dimension_semantics=("arbitrary", "arbitrary"),
                                             vmem_limit_bytes=VMEM_LIMIT),
        name="inproj",
    )(x2, w_all, wvt, pw_bd, pool_scale, conv_w)


CH = 4 * CMP_HIDDEN


def _compress_kernel(r_ref, pe_ref, w1_ref, w2_ref, w2t_ref, out_ref, outT_ref):
    nc = r_ref.shape[1]
    hk = CH // 2
    hs = []
    for kv in range(2):
        rows = jnp.concatenate([r_ref[l][:, kv * D_KV:(kv + 1) * D_KV] for l in range(CMP_STRIDE)], axis=1)
        zz = _dot(rows, w1_ref[kv])
        pb = _dot(pe_ref[kv], w1_ref[kv])
        z0 = zz[:, 0:hk] + pb[0:1, 0:hk]
        z1 = zz[:, hk:2 * hk] + pb[1:2, hk:2 * hk]
        hs.append(z0 + pltpu.roll(z1, nc - 1, axis=0))
    h = jnp.concatenate(hs, axis=1)
    h = (h * _sigmoid(h)).astype(bf16)
    out_ref[...] = _dot(h, w2_ref[...]).astype(bf16)
    outT_ref[...] = _dot_nt(w2t_ref[...], h).astype(bf16)


def _compress(cmp_l, pe_r, w1_big, w2_big, w2t_big, layer, B):
    nc = cmp_l.shape[1] // B
    rw = CMP_STRIDE * D_KV
    wsel = lambda b: (layer, 0, 0)
    wsel4 = lambda b: (layer, 0, 0, 0)
    return pl.pallas_call(
        _compress_kernel,
        grid=(B,),
        in_specs=[pl.BlockSpec((CMP_STRIDE, nc, 2 * D_KV), lambda b: (0, b, 0)),
                  pl.BlockSpec((None, 2, 8, rw), wsel4),
                  pl.BlockSpec((None, 2, rw, CH), wsel4),
                  pl.BlockSpec((None, CH, 2 * D_KV), wsel),
                  pl.BlockSpec((None, 2 * D_KV, CH), wsel)],
        out_specs=[pl.BlockSpec((None, nc, 2 * D_KV), lambda b: (b, 0, 0)),
                   pl.BlockSpec((None, 2 * D_KV, nc), lambda b: (b, 0, 0))],
        out_shape=[jax.ShapeDtypeStruct((B, nc, 2 * D_KV), bf16),
                   jax.ShapeDtypeStruct((B, 2 * D_KV, nc), bf16)],
        compiler_params=pltpu.CompilerParams(dimension_semantics=("arbitrary",),
                                             vmem_limit_bytes=VMEM_LIMIT),
        name="compress",
    )(cmp_l, pe_r, w1_big, w2_big, w2t_big)


def _selection_scores(imp_t, t0):
    shape = imp_t.shape
    jblk = lax.broadcasted_iota(jnp.int32, shape, 0)
    tq = lax.broadcasted_iota(jnp.int32, shape, 1) % TQ
    back = jnp.right_shift(t0 + tq, int(math.log2(SLC_BLOCK))) - jblk
    causal = back >= 0
    forced = jnp.logical_or(jblk == 0, jnp.logical_and(causal, back < N_LOCAL))
    jf = jblk.astype(f32)
    score = jnp.where(forced, -jnp.inf, jnp.where(causal, imp_t, NEG - jf * NONCAUSAL_STEP))
    tblk = jnp.right_shift(t0 + lax.broadcasted_iota(jnp.int32, (1, shape[1]), 1) % TQ, int(math.log2(SLC_BLOCK)))
    return score, (jnp.minimum(tblk, N_LOCAL) + 1).astype(f32)


def _pick_exact(score):
    jf = lax.broadcasted_iota(jnp.int32, score.shape, 0).astype(f32)
    for _ in range(N_PICKS):
        best = jnp.max(score, axis=0, keepdims=True)
        first = jnp.min(jnp.where(score == best, jf, float(score.shape[0])), axis=0, keepdims=True)
        score = jnp.where(jf == first, -jnp.inf, score)
    return score


def _write_selection(imp_t, t0, nsel, pen_ref):
    score0, n_forced = _selection_scores(imp_t, t0)
    score = score0
    for _ in range(N_PICKS):
        score = jnp.where(score == jnp.max(score, axis=0, keepdims=True), -jnp.inf, score)
    taken = score == -jnp.inf
    pen_ref[0:nsel] = jnp.where(taken, 0.0, NEG).astype(bf16)
    if nsel < NSB:
        pen_ref[nsel:NSB] = jnp.full((NSB - nsel, N_KV * TQ), NEG, bf16)
    n_taken = jnp.sum(jnp.where(taken, 1.0, 0.0), axis=0, keepdims=True)
    tied = jnp.max(jnp.abs(n_taken - n_forced - N_PICKS))

    @pl.when(tied > 0.0)
    def _():
        pen_ref[0:nsel] = jnp.where(_pick_exact(score0) == -jnp.inf, 0.0, NEG).astype(bf16)


def _query_cols(q_ref):
    sub = lax.broadcasted_iota(jnp.int32, (LANES, TQ), 0)
    low = sub < HEAD_DIM
    pairs = [q_ref[:, LANES * p:LANES * (p + 1)].astype(f32).T for p in range(GROUP)]
    blocks = [(jnp.where(low, pairs[h % GROUP], 0.0) if h < GROUP else jnp.where(low, 0.0, pairs[h % GROUP])).astype(bf16)
              for h in range(N_HEADS)]
    return jnp.concatenate(blocks, axis=1)


def _merge_head_pairs(outs):
    return [jnp.concatenate([outs[p], outs[p + GROUP]], axis=0).T for p in range(GROUP)]


def _cmp_scores(q_cmp, cmp_ref, t0, nc):
    n_i = lax.broadcasted_iota(jnp.int32, (nc, LANES), 0)
    j_i = lax.broadcasted_iota(jnp.int32, (nc, LANES), 1)
    nstart = t0 // CMP_STRIDE - CMP_BACK
    in_window = jnp.logical_and(j_i < 2 * CMP_SLOTS, n_i == nstart + jnp.where(j_i < CMP_SLOTS, j_i, j_i - CMP_SLOTS))
    future = jnp.logical_and(j_i == _PAD_FLAG_COL, n_i >= nstart + CMP_SLOTS - 1)
    place_b = jnp.where(jnp.logical_or(in_window, future), 1.0, 0.0).astype(bf16)
    kc_ext = jnp.concatenate([cmp_ref[0:nc, 0:D_KV], place_b], axis=1)
    return _dot(kc_ext, q_cmp)


def _cmp_finish(s1, sig_t, cmpT_ref, ovT_ref, y1_ref, t0, nsel):
    nc = s1.shape[0]
    m1 = jnp.max(s1, axis=0, keepdims=True)
    p1 = jnp.exp2(s1 - m1)
    p1 = p1.astype(bf16)
    tcol = t0 + lax.broadcasted_iota(jnp.int32, (1, GC), 1) % TQ
    outs, sums = [], []
    for g in range(N_KV):
        lhs = jnp.concatenate([cmpT_ref[D_KV + g * HEAD_DIM:D_KV + (g + 1) * HEAD_DIM, 0:nc], ovT_ref[0:nsel, 0:nc],
                               ovT_ref[NSB:NSB + SUM_ROWS, 0:nc]], axis=0)
        both = _dot(lhs, p1[:, g * GC:(g + 1) * GC])
        l1 = both[HEAD_DIM + nsel:HEAD_DIM + nsel + 1]
        both = both * jnp.where(tcol >= CMP_BLOCK - 1, 1.0 / l1, 0.0)
        for r in range(GROUP):
            h = g * GROUP + r
            outs.append(sig_t[3 * h:3 * h + 1, :] * both[0:HEAD_DIM, r * TQ:(r + 1) * TQ])
        imp = both[HEAD_DIM:HEAD_DIM + nsel]
        acc = imp[:, 0:TQ]
        for r in range(1, GROUP):
            acc = acc + imp[:, r * TQ:(r + 1) * TQ]
        sums.append(acc)
    for p, y in enumerate(_merge_head_pairs(outs)):
        y1_ref[:, p * LANES:(p + 1) * LANES] = y.astype(bf16)
    return jnp.concatenate(sums, axis=1)


def _nsa_kernel(q0_ref, qn_ref, z_ref, g0_ref, gn_ref, cmp_ref, cmpT_ref, ovT_ref, ks_ref, kw_ref, vsT_ref, vwT_ref,
                e_ref, tab_ref, acmp_ref, out_ref,
                qc_ref, sg_ref, pen_ref, y1_ref, o3_ref, m_ref, l_ref, acc_ref, sa_ref, sb_ref, mxa_ref, mxb_ref):
    qi = pl.program_id(1)
    nq = pl.num_programs(1)
    t0 = qi * TQ
    nc = cmp_ref.shape[0]
    classes = [(nc * k // N_CLASSES, NSB * k // N_CLASSES) for k in range(1, N_CLASSES + 1)]
    last_q = [min(c // (TQ // CMP_STRIDE), s // (TQ // SLC_BLOCK)) - 1 for c, s in classes]

    @pl.when(qi == 0)
    def _():
        qc_ref[...] = _query_cols(q0_ref)
        sg_ref[...] = _sigmoid(g0_ref[...]).T
        q_cmp0 = jnp.concatenate([qc_ref[...], acmp_ref[...]], axis=0)
        imp0 = _cmp_finish(_cmp_scores(q_cmp0, cmp_ref, t0, classes[0][0]), sg_ref[...], cmpT_ref, ovT_ref, y1_ref, t0,
                           classes[0][1])
        _write_selection(imp0, t0, classes[0][1], pen_ref)

    qcols = qc_ref[...]
    sig_cur = sg_ref[...]
    q_cmp = jnp.concatenate([qcols, acmp_ref[...]], axis=0)
    pens = [pen_ref[:, g * TQ:(g + 1) * TQ] for g in range(N_KV) for _ in range(GROUP)]
    q_slc = jnp.concatenate([qcols, jnp.concatenate(pens, axis=1)], axis=0)
    y1 = y1_ref[...]

    m_ref[...] = jnp.full((1, R), NEG, f32)
    l_ref[...] = jnp.zeros((1, R), f32)
    acc_ref[...] = jnp.zeros((HEAD_DIM, R), f32)
    n_far = (qi + KT // TQ) // (KT // TQ) - 1
    first = (qi + 1) * TQ - n_far * KT

    def tile_row(i):
        return pl.multiple_of(first + i * KT, LANES)

    def produce(i, s_ref, mx_ref):
        r0 = tile_row(i)
        k_ext = jnp.concatenate([ks_ref[pl.ds(r0, KT), :], e_ref[pl.ds(r0, KT), :]], axis=1)
        s = _dot(k_ext, q_slc)
        s_ref[...] = s
        mx_ref[...] = jnp.max(s, axis=0, keepdims=True)

    def consume(i, s_ref, mx_ref, last):
        m_prev = m_ref[...]
        if last:
            s = jnp.concatenate([s_ref[0:KT - 2 * TQ], s_ref[KT - 2 * TQ:KT] + tab_ref[TQ:3 * TQ]], axis=0)
            m_new = jnp.maximum(m_prev, jnp.max(s, axis=0, keepdims=True))
        else:
            s = s_ref[...]
            m_new = jnp.maximum(m_prev, mx_ref[...])
        alpha = jnp.exp2(m_prev - m_new)
        p = jnp.exp2(s - m_new)
        p = p.astype(bf16)
        for g in range(N_KV):
            c = slice(g * GC, (g + 1) * GC)
            pv = _dot(vsT_ref[g, :, pl.ds(tile_row(i), KT)], p[:, c])
            l_ref[:, c] = alpha[:, c] * l_ref[:, c] + pv[HEAD_DIM:HEAD_DIM + 1]
            acc_ref[:, c] = acc_ref[:, c] * alpha[:, c] + pv[0:HEAD_DIM]
        m_ref[...] = m_new

    qn = jnp.minimum(qi + 1, nq - 1)

    def front(nc_use, nsel_use):
        tn = qn * TQ
        qc_next = _query_cols(qn_ref)
        sig_next = _sigmoid(gn_ref[...]).T
        qc_ref[...] = qc_next
        sg_ref[...] = sig_next
        s1 = _cmp_scores(jnp.concatenate([qc_next, acmp_ref[...]], axis=0), cmp_ref, tn, nc_use)
        w0 = pl.multiple_of(t0, LANES)
        s3 = _dot(kw_ref[pl.ds(w0, N_WIN), :], q_cmp)
        imp_next = _cmp_finish(s1, sig_next, cmpT_ref, ovT_ref, y1_ref, tn, nsel_use)
        produce(0, sa_ref, mxa_ref)
        s3 = jnp.concatenate([s3[0:TQ] + tab_ref[0:TQ], s3[TQ:WINDOW - TQ], s3[WINDOW - TQ:] + tab_ref[TQ:3 * TQ]],
                             axis=0)
        p3 = jnp.exp2(s3 - jnp.max(s3, axis=0, keepdims=True)).astype(bf16)
        for g in range(N_KV):
            c = slice(g * GC, (g + 1) * GC)
            o3_ref[:, c] = _dot(vwT_ref[g, :, pl.ds(w0, N_WIN)], p3[:, c])
        _write_selection(imp_next, tn, nsel_use, pen_ref)

    for k, (nc_k, nsel_k) in enumerate(classes):
        lo = last_q[k - 1] if k else -1
        if k < N_CLASSES - 1:
            pl.when(jnp.logical_and(qn > lo, qn <= last_q[k]))(functools.partial(front, nc_k, nsel_k))
        else:
            pl.when(qn > lo)(functools.partial(front, nc_k, nsel_k))

    bufs = ((sa_ref, mxa_ref), (sb_ref, mxb_ref))

    def pipeline(base, count, last_at_end):
        for k in range(count):
            if k + 1 < count or not last_at_end:
                produce(base + k + 1, *bufs[(k + 1) % 2])
            consume(base + k, *bufs[k % 2], last_at_end and k == count - 1)

    def unrolled(j, c):
        pipeline(UNROLL * j, UNROLL, False)
        return c

    lax.fori_loop(0, n_far // UNROLL, unrolled, 0)

    def finish():
        inv2 = 1.0 / l_ref[...]
        inv3 = 1.0 / o3_ref[HEAD_DIM:HEAD_DIM + 1, :]
        sig_t = sig_cur
        outs = []
        for h in range(N_HEADS):
            c = slice(h * TQ, (h + 1) * TQ)
            g2 = sig_t[3 * h + 1:3 * h + 2, :] * inv2[:, c]
            g3 = sig_t[3 * h + 2:3 * h + 3, :] * inv3[:, c]
            outs.append(g2 * acc_ref[:, c] + g3 * o3_ref[0:HEAD_DIM, c])
        for p, y in enumerate(_merge_head_pairs(outs)):
            c = slice(p * LANES, (p + 1) * LANES)
            zb = z_ref[:, c].astype(f32)
            out_ref[:, c] = ((y + y1[:, c].astype(f32)) * zb * _sigmoid(zb)).astype(bf16)

    for r in range(UNROLL):
        def drain(r=r):
            pipeline(n_far - r, r + 1, True)
            finish()
        pl.when(n_far % UNROLL == r)(drain)


def _nsa(q, z, g, cmp, cmpT, ks, kw, vsT, vwT, e_pad, ovT, tabs, acmp, B, T):
    nq = T // TQ
    nc = T // CMP_STRIDE
    tp = T + KPAD
    tile = lambda b, i: (b * nq + i, 0)
    nxt = lambda b, i: (b * nq + jnp.minimum(i + 1, nq - 1), 0)
    head = lambda b, i: (b * nq, 0)
    const2 = lambda b, i: (0, 0)
    batch3 = lambda b, i: (b, 0, 0)
    return pl.pallas_call(
        _nsa_kernel,
        grid=(B, nq),
        in_specs=[pl.BlockSpec((TQ, D_NSA), head),
                  pl.BlockSpec((TQ, D_NSA), nxt),
                  pl.BlockSpec((TQ, D_NSA), tile),
                  pl.BlockSpec((TQ, LANES), head),
                  pl.BlockSpec((TQ, LANES), nxt),
                  pl.BlockSpec((None, nc, 2 * D_KV), batch3),
                  pl.BlockSpec((None, 2 * D_KV, nc), batch3),
                  pl.BlockSpec((NSB + SUM_ROWS, nc), const2),
                  pl.BlockSpec((None, tp, D_KV), batch3),
                  pl.BlockSpec((None, tp, 2 * D_KV), batch3),
                  pl.BlockSpec((None, N_KV, VROWS, tp), lambda b, i: (b, 0, 0, 0)),
                  pl.BlockSpec((None, N_KV, VROWS, tp), lambda b, i: (b, 0, 0, 0)),
                  pl.BlockSpec((tp, NSB), const2),
                  pl.BlockSpec((3 * TQ, R), const2),
                  pl.BlockSpec((LANES, R), const2)],
        out_specs=pl.BlockSpec((TQ, D_NSA), tile),
        out_shape=jax.ShapeDtypeStruct((B * T, D_NSA), bf16),
        scratch_shapes=[pltpu.VMEM((LANES, R), bf16), pltpu.VMEM((LANES, TQ), f32),
                        pltpu.VMEM((NSB, N_KV * TQ), bf16), pltpu.VMEM((TQ, D_NSA), bf16), pltpu.VMEM((VROWS, R), f32),
                        pltpu.VMEM((1, R), f32), pltpu.VMEM((1, R), f32), pltpu.VMEM((HEAD_DIM, R), f32),
                        pltpu.VMEM((KT, R), f32), pltpu.VMEM((KT, R), f32),
                        pltpu.VMEM((1, R), f32), pltpu.VMEM((1, R), f32)],
        compiler_params=pltpu.CompilerParams(dimension_semantics=("arbitrary", "arbitrary"),
                                             vmem_limit_bytes=VMEM_LIMIT),
        name="nsa",
    )(q, q, z, g, g, cmp, cmpT, ovT, ks, kw, vsT, vwT, e_pad, tabs, acmp)


def _outproj_kernel(alpha, ymix_ref, ynsa_ref, x_ref, w_ref, g_ref, b_ref, out_ref):
    y = jnp.concatenate([ymix_ref[...], ynsa_ref[...]], axis=1)
    r = alpha * x_ref[...] + _dot(y, w_ref[...])
    mu = jnp.mean(r, axis=-1, keepdims=True)
    d = r - mu
    var = jnp.mean(d * d, axis=-1, keepdims=True)
    out_ref[...] = d * lax.rsqrt(var + LN_EPS) * g_ref[...] + b_ref[...]


def _outproj(ymix, ynsa, x2, w_out, ln_g, ln_b, alpha, layer):
    bt = x2.shape[0]
    row = lambda i: (i, 0)
    wsel = lambda i: (layer, 0, 0)
    return pl.pallas_call(
        functools.partial(_outproj_kernel, alpha),
        grid=(bt // TT_OUT,),
        in_specs=[pl.BlockSpec((TT_OUT, D_POOL + D_CONV), row),
                  pl.BlockSpec((TT_OUT, D_NSA), row),
                  pl.BlockSpec((TT_OUT, D_MODEL), row),
                  pl.BlockSpec((None, D_MODEL, D_MODEL), wsel),
                  pl.BlockSpec((None, 1, D_MODEL), wsel),
                  pl.BlockSpec((None, 1, D_MODEL), wsel)],
        out_specs=pl.BlockSpec((TT_OUT, D_MODEL), row),
        out_shape=jax.ShapeDtypeStruct((bt, D_MODEL), f32),
        compiler_params=pltpu.CompilerParams(dimension_semantics=("arbitrary",),
                                             vmem_limit_bytes=VMEM_LIMIT),
        name="outproj",
    )(ymix, ynsa, x2, w_out, ln_g, ln_b)


def _bucket_np(d):
    d = np.asarray(d)
    max_exact = N_BUCKETS // 2
    nf = np.maximum(d, 1).astype(np.float32)
    large = max_exact + (np.log(nf / np.float32(max_exact)) / np.float32(math.log(MAX_DISTANCE / max_exact))
                         * np.float32(N_BUCKETS - max_exact)).astype(np.int32)
    large = np.minimum(large, N_BUCKETS - 1)
    return np.where(d < max_exact, d, large)


_FAR_DIST = 113
assert _bucket_np(np.arange(_FAR_DIST, 4 * WINDOW)).min() == N_BUCKETS - 1

assert CMP_STRIDE * (CMP_BACK + 1) - (CMP_BLOCK - 1) >= _FAR_DIST


def _bias_tables(rel_bias):
    nd = 2 * TQ
    onehot = jnp.asarray(np.eye(N_BUCKETS, dtype=np.float32)[_bucket_np(np.arange(nd))])
    tabp = ((jnp.dot(onehot, rel_bias, precision=lax.Precision.HIGHEST)
             - rel_bias[N_BUCKETS - 1:N_BUCKETS, :]) * LOG2E).T
    sl = np.arange(TQ)[:, None]
    tl = np.arange(TQ)[None, :]
    neg = lambda n: jnp.full((N_HEADS, n), NEG, f32)

    def skew(u, rows):
        period = u.shape[1]
        return jnp.tile(u, (1, rows))[:, :rows * (period - 1)].reshape(N_HEADS, rows, period - 1)

    def tile_layout(t):
        return t.transpose(1, 0, 2).reshape(TQ, R)

    diag = tile_layout(skew(jnp.concatenate([tabp[:, 0:TQ], neg(TQ)], axis=1), TQ)[:, :, :TQ])
    prev = tile_layout(skew(jnp.concatenate([tabp[:, TQ:nd], tabp[:, 0:TQ]], axis=1), TQ)[:, :, :TQ])
    edge = jnp.asarray(np.tile(np.where(sl > tl, 0.0, NEG).astype(np.float32), (1, N_HEADS)))
    tabs = jnp.concatenate([edge, prev, diag], axis=0).astype(f32)

    j = np.arange(CMP_SLOTS)[None, :]
    off = CMP_BACK * CMP_STRIDE - (CMP_BLOCK - 1)
    dc = np.arange(TQ)[:, None] - CMP_STRIDE * j + off
    okc = jnp.asarray(dc >= 0)[None]
    period = 4 * TQ
    u = jnp.concatenate([tabp[:, off:nd], jnp.zeros((N_HEADS, off), f32), neg(period - nd - off), tabp[:, 0:off]], axis=1)
    vc = skew(u, nd)[:, 0:CMP_SLOTS * CMP_STRIDE:CMP_STRIDE, 0:TQ].transpose(0, 2, 1)
    vc = jnp.where(okc, vc, NEG)
    hi = vc.astype(bf16)
    lo = jnp.where(okc, vc - hi.astype(f32), 0.0).astype(bf16)
    fut = jnp.full((N_HEADS, TQ, 1), NEG, bf16)
    pad = jnp.zeros((N_HEADS, TQ, LANES - _PAD_FLAG_COL - 1), bf16)
    acmp = jnp.concatenate([hi, lo, fut, pad], axis=-1).reshape(R, LANES).T
    return tabs, acmp


def _static_tables(T):
    nc = T // CMP_STRIDE
    ns = T // SLC_BLOCK
    cstart = np.arange(nc)[None, :] * CMP_STRIDE
    sstart = np.arange(NSB)[:, None] * SLC_BLOCK
    ov = np.clip(np.minimum(cstart + CMP_BLOCK, sstart + SLC_BLOCK) - np.maximum(cstart, sstart), 0, None) / CMP_STRIDE
    ov[ns:, :] = 0
    ov[:, nc - 1] = 0
    ov = np.concatenate([ov, np.ones((SUM_ROWS, nc))], axis=0)
    e_pad = np.concatenate([np.ones((KPAD, NSB), bool),
                            np.arange(T)[:, None] // SLC_BLOCK == np.arange(NSB)[None, :]], axis=0)
    return jnp.asarray(ov, bf16), jnp.asarray(e_pad, bf16)


def _pair_order(w, axis):
    shp = w.shape
    w = w.reshape(shp[:axis] + (N_KV, GROUP, HEAD_DIM) + shp[axis + 1:])
    return jnp.swapaxes(w, axis, axis + 1).reshape(shp)


def _prep_weights(w_in, w_out, pool_w, pe_k, w1_k, w2_k, pe_v, w1_v, w2_v):
    depth = w_in.shape[0]
    sizes = (D_POOL, D_POOL, D_CONV, D_CONV, D_CONV, D_CONV, D_NSA, D_KV, D_KV, D_KV, D_KV, D_KV, D_KV,
             3 * N_HEADS, D_NSA)
    offs = np.cumsum((0,) + sizes)
    col = lambda i: w_in[:, :, offs[i]:offs[i + 1]]
    wg = jnp.pad(col(13), ((0, 0), (0, 0), (0, LANES - 3 * N_HEADS)))
    w_all = jnp.concatenate([w_in[:, :, 0:MIX_W], _pair_order(col(6), 2), _pair_order(col(14), 2),
                             col(7), col(8), col(9), col(11), wg], axis=2).astype(bf16)
    wvt = jnp.swapaxes(jnp.concatenate([col(10), col(12)], axis=2), 1, 2).astype(bf16)
    nm = D_POOL + D_CONV
    w_out_p = jnp.concatenate([w_out[:, 0:nm], _pair_order(w_out[:, nm:], 1)], axis=1).astype(bf16)

    eye_g = jnp.eye(len(POOL_WINDOWS), dtype=bf16)
    pw_bd = jnp.einsum('zgcd,gh->zgchd', pool_w.astype(bf16), eye_g).reshape(depth, D_POOL, D_POOL)

    half = CMP_BLOCK // 2
    eye2 = jnp.eye(2, dtype=bf16)
    rw = half * D_KV
    w1_kv = []
    for w1 in (w1_k, w1_v):
        cols = []
        for a in range(2):
            wsel = w1.astype(bf16).reshape(depth, 2, half, 1, HEAD_DIM, CMP_HIDDEN)[:, a]
            for g in range(N_KV):
                blk = jnp.pad(wsel, ((0, 0), (0, 0), (g, N_KV - 1 - g), (0, 0), (0, 0)))
                cols.append(blk.reshape(depth, rw, CMP_HIDDEN))
        w1_kv.append(jnp.concatenate(cols, axis=2))
    w1_big = jnp.stack(w1_kv, axis=1)
    pes = jnp.stack([pe_k, pe_v], axis=1).astype(bf16).reshape(depth, 2, 2, half, 1, HEAD_DIM)
    pe_r = jnp.broadcast_to(pes, (depth, 2, 2, half, N_KV, HEAD_DIM)).reshape(depth, 2, 2, rw)
    pe_r = jnp.pad(pe_r, ((0, 0), (0, 0), (0, 6), (0, 0)))
    w2s = jnp.stack([w2_k, w2_v], axis=1).astype(bf16)
    w2_big = jnp.einsum('zkhd,kK,gG->zkghKGd', w2s, eye2, eye2).reshape(depth, CH, 2 * D_KV)
    return w_all, wvt, w_out_p, pw_bd, w1_big, pe_r, w2_big, jnp.swapaxes(w2_big, 1, 2)


def kernel(x, w_in, w_out, pool_w, pool_scale, conv_w, cmp_pe_k, cmp_w1_k, cmp_w2_k, cmp_pe_v, cmp_w1_v, cmp_w2_v,
           rel_bias, ln_g, ln_b):
    B, T, D = x.shape
    depth = w_in.shape[0]
    assert D == D_MODEL and T % TT == 0 and T // SLC_BLOCK <= NSB and T // SLC_BLOCK >= N_SELECT
    assert KPAD == TT and (B * T) % TT_OUT == 0 and N_SELECT > N_LOCAL
    alpha = (2 * depth) ** 0.25
    tabs, acmp = _bias_tables(rel_bias)
    ovT, e_pad = _static_tables(T)
    w_all, wvt, w_out_p, pw_bd, w1_big, pe_r, w2_big, w2t_big = _prep_weights(
        w_in, w_out, pool_w, cmp_pe_k, cmp_w1_k, cmp_w2_k, cmp_pe_v, cmp_w1_v, cmp_w2_v)
    pool_scale = pool_scale.reshape(depth, 1, D_POOL)
    ln_g = ln_g.reshape(depth, 1, D_MODEL)
    ln_b = ln_b.reshape(depth, 1, D_MODEL)
    h = x.reshape(B * T, D)
    for l in range(depth):
        ymix, q, z, cmp_l, g, ks, kw, vsT, vwT = _inproj(h, w_all, wvt, pw_bd, pool_scale, conv_w, l, B, T)
        cmp, cmpT = _compress(cmp_l, pe_r, w1_big, w2_big, w2t_big, l, B)
        ynsa = _nsa(q, z, g, cmp, cmpT, ks, kw, vsT, vwT, e_pad, ovT, tabs, acmp, B, T)
        h = _outproj(ymix, ynsa, h, w_out_p, ln_g, ln_b, alpha, l)
    return h.reshape(B, T, D)
```
